```python
import math
import jax, jax.numpy as jnp
from jax import lax
import numpy as np

D_MODEL = 1024
BATCH = 8
SEQ = 2048
DEPTH = 4

CTX_LEN = 256
GRID_W = 64
LAYER_KINDS = ('attn', 'lru', 'conv')
N_MIXERS = len(LAYER_KINDS)
EPS = 1e-6
NEG_INF = -1e30
N_HEADS = 16
N_KV_HEADS = 4
HEAD_DIM = D_MODEL // N_HEADS
Q_DIM = N_HEADS * HEAD_DIM
KV_DIM = N_KV_HEADS * HEAD_DIM
WINDOW = 128
ROPE_BASE = 10000.0
LRU_WIDTH = D_MODEL
LRU_BLOCKS = 16
LRU_BLOCK = LRU_WIDTH // LRU_BLOCKS
LRU_C = 8.0
LRU_CONV = 4
CONV_WIDTH = D_MODEL
CONV_K = 3

kernel_name = 'hybrid_interleaved_dit_backbone'


def rmsnorm(x, g):
    xf = x.astype(jnp.float32)
    y = xf * lax.rsqrt(jnp.mean(xf * xf, axis=-1, keepdims=True) + EPS)
    return (y * g.astype(jnp.float32)).astype(x.dtype)


def modulation(cvec, w, b):
    m = jax.nn.silu(cvec) @ w + b
    return jnp.split(m, 3, axis=-1)


def axial_rope_tables(n_tokens):
    rows = n_tokens // GRID_W
    row = jnp.repeat(jnp.arange(rows), GRID_W).astype(jnp.float32)
    col = jnp.tile(jnp.arange(GRID_W), rows).astype(jnp.float32)
    half = HEAD_DIM // 2
    inv = 1.0 / (ROPE_BASE ** (jnp.arange(0, half, 2, dtype=jnp.float32) / half))
    ang = jnp.stack([row[:, None] * inv, col[:, None] * inv], axis=1)
    return jnp.cos(ang), jnp.sin(ang)


def apply_rope(x, cos, sin):
    shp = x.shape
    xr = x.astype(jnp.float32).reshape(shp[0], shp[1], -1, 2, 2, HEAD_DIM // 4)
    x1, x2 = xr[..., 0, :], xr[..., 1, :]
    c = cos[None, :, None]
    s = sin[None, :, None]
    out = jnp.stack([x1 * c - x2 * s, x2 * c + x1 * s], axis=-2)
    return out.reshape(shp).astype(x.dtype)


def dwconv(u, w, pad):
    C = u.shape[-1]
    return lax.conv_general_dilated(u, w[:, None, :].astype(u.dtype), window_strides=(1,),
                                    padding=[pad], dimension_numbers=('NWC', 'WIO', 'NWC'),
                                    feature_group_count=C)


def linear_scan(a, b):
    def comb(l, r):
        return l[0] * r[0], r[0] * l[1] + r[1]
    return lax.associative_scan(comb, (a, b), axis=1)[1]


def attn_mixer(hl, hc, w_in, w_out, sink, cos, sin, need_ctx):
    B, L, _ = hl.shape
    n_ctx = hc.shape[1]
    nb = L // WINDOW
    G = N_HEADS // N_KV_HEADS
    scale = HEAD_DIM ** -0.5
    q, k, v, g = jnp.split(hl @ w_in, [Q_DIM, Q_DIM + KV_DIM, Q_DIM + 2 * KV_DIM], axis=-1)
    q = apply_rope(q.reshape(B, L, N_KV_HEADS, G, HEAD_DIM), cos, sin) * scale
    k = apply_rope(k.reshape(B, L, N_KV_HEADS, HEAD_DIM), cos, sin)
    v = v.reshape(B, L, N_KV_HEADS, HEAD_DIM)
    kc, vc = jnp.split(hc @ w_in[:, Q_DIM:Q_DIM + 2 * KV_DIM], 2, axis=-1)
    kc = kc.reshape(B, n_ctx, N_KV_HEADS, HEAD_DIM)
    vc = vc.reshape(B, n_ctx, N_KV_HEADS, HEAD_DIM)
    sink_f = sink.astype(jnp.float32).reshape(N_KV_HEADS, G, 1, 1)

    def band(t):
        tp = jnp.pad(t, ((0, 0), (WINDOW, WINDOW), (0, 0), (0, 0)))
        tp = tp.reshape(B, nb + 2, WINDOW, N_KV_HEADS, HEAD_DIM)
        return jnp.concatenate([tp[:, :-2], tp[:, 1:-1], tp[:, 2:]], axis=2)

    kb, vb = band(k), band(v)
    qb = q.reshape(B, nb, WINDOW, N_KV_HEADS, G, HEAD_DIM)
    s_band = jnp.einsum('bnqkgd,bnskd->bnkgqs', qb, kb).astype(jnp.float32)
    qrel = jnp.arange(WINDOW)[:, None] + WINDOW
    srel = jnp.arange(3 * WINDOW)[None, :]
    kabs = jnp.arange(nb)[:, None, None] * WINDOW - WINDOW + srel[None]
    mask = (jnp.abs(qrel - srel) <= WINDOW)[None] & (kabs >= 0) & (kabs < L)
    s_band = jnp.where(mask[None, :, None, None], s_band, NEG_INF)
    s_ctx = jnp.einsum('bnqkgd,bckd->bnkgqc', qb, kc).astype(jnp.float32)
    m = jnp.maximum(jnp.maximum(s_band.max(-1, keepdims=True), s_ctx.max(-1, keepdims=True)), sink_f)
    e_band = jnp.exp(s_band - m)
    e_ctx = jnp.exp(s_ctx - m)
    denom = e_band.sum(-1, keepdims=True) + e_ctx.sum(-1, keepdims=True) + jnp.exp(sink_f - m)
    o = (jnp.einsum('bnkgqs,bnskd->bnqkgd', (e_band / denom).astype(v.dtype), vb)
         + jnp.einsum('bnkgqc,bckd->bnqkgd', (e_ctx / denom).astype(v.dtype), vc))
    yl = (o.reshape(B, L, Q_DIM) * jax.nn.silu(g)) @ w_out
    if not need_ctx:
        return yl, None
    qc = (hc @ w_in[:, :Q_DIM]).reshape(B, n_ctx, N_KV_HEADS, G, HEAD_DIM) * scale
    gc = hc @ w_in[:, Q_DIM + 2 * KV_DIM:]
    s = jnp.einsum('bqkgd,bckd->bkgqc', qc, kc).astype(jnp.float32)
    mc = jnp.maximum(s.max(-1, keepdims=True), sink_f)
    e = jnp.exp(s - mc)
    p = e / (e.sum(-1, keepdims=True) + jnp.exp(sink_f - mc))
    oc = jnp.einsum('bkgqc,bckd->bqkgd', p.astype(vc.dtype), vc)
    yc = (oc.reshape(B, n_ctx, Q_DIM) * jax.nn.silu(gc)) @ w_out
    return yl, yc


def rglru_direction(uc, ul, conv_w, conv_b, wa, ba, wx, bx, lam):
    def gates(xs):
        B, n, _ = xs.shape
        xb = xs.reshape(B, n, LRU_BLOCKS, LRU_BLOCK)
        r = jax.nn.sigmoid(jnp.einsum('bnhi,hij->bnhj', xb, wa).reshape(B, n, LRU_WIDTH) + ba)
        i = jax.nn.sigmoid(jnp.einsum('bnhi,hij->bnhj', xb, wx).reshape(B, n, LRU_WIDTH) + bx)
        log_a = -LRU_C * r.astype(jnp.float32) * jax.nn.softplus(-lam.astype(jnp.float32))
        a = jnp.exp(log_a)
        b = jnp.sqrt(-jnp.expm1(2.0 * log_a)) * (i * xs).astype(jnp.float32)
        return a, b

    xc = dwconv(uc, conv_w, (LRU_CONV - 1, 0)) + conv_b
    xl = dwconv(ul, conv_w, (LRU_CONV - 1, 0)) + conv_b
    ac, bc = gates(xc)
    hc = linear_scan(ac, bc)
    al, bl = gates(xl)
    bl = bl.at[:, 0].add(al[:, 0] * hc[:, -1])
    hl = linear_scan(al, bl)
    return hc, hl


def lru_mixer(hl, hc, w_in, w_out, conv_w, conv_b, gate_a_w, gate_a_b, gate_x_w, gate_x_b, lam, need_ctx):
    ul, gl = jnp.split(hl @ w_in, 2, axis=-1)
    if need_ctx:
        uc, gc = jnp.split(hc @ w_in, 2, axis=-1)
    else:
        uc = hc @ w_in[:, :LRU_WIDTH]
    hc_f, hl_f = rglru_direction(uc, ul, conv_w[0], conv_b[0], gate_a_w[0], gate_a_b[0],
                                 gate_x_w[0], gate_x_b[0], lam[0])
    hc_b, hl_b = rglru_direction(jnp.flip(uc, 1), jnp.flip(ul, 1), conv_w[1], conv_b[1], gate_a_w[1],
                                 gate_a_b[1], gate_x_w[1], gate_x_b[1], lam[1])
    yl = ((hl_f + jnp.flip(hl_b, 1)).astype(hl.dtype) * jax.nn.silu(gl)) @ w_out
    if not need_ctx:
        return yl, None
    yc = ((hc_f + jnp.flip(hc_b, 1)).astype(hc.dtype) * jax.nn.silu(gc)) @ w_out
    return yl, yc


def short_conv_branch(h, w_in, w_out, conv_w, conv_b):
    u, bg, cg, g = jnp.split(h @ w_in, 4, axis=-1)
    y = bg * (dwconv(cg * u, conv_w, (CONV_K // 2, CONV_K // 2)) + conv_b)
    return (y * jax.nn.silu(g)) @ w_out


def conv_mixer(hl, hc, w_in, w_out, conv_w, conv_b, need_ctx):
    yl = short_conv_branch(hl, w_in, w_out, conv_w, conv_b)
    yc = short_conv_branch(hc, w_in, w_out, conv_w, conv_b) if need_ctx else None
    return yl, yc


def setup_inputs(seed: int = 0) -> dict:
    key = jax.random.key(seed)
    keys = iter(jax.random.split(key, 128))
    D = D_MODEL

    def normal(shape, std):
        return jax.random.normal(next(keys), shape, jnp.float32) * std

    p = {
        'x': normal((BATCH, SEQ, D), 1.0),
        'c': normal((BATCH, D), 1.0),
        'ctx': normal((BATCH, CTX_LEN, D), 1.0),
        'c_ctx': normal((D,), 1.0),
    }
    for i in range(DEPTH):
        kind = LAYER_KINDS[i % N_MIXERS]
        pre = 'l%d_' % i
        p[pre + 'norm_g'] = 1.0 + normal((D,), 0.02)
        p[pre + 'mod_w'] = normal((D, 3 * D), 0.5 * D ** -0.5)
        p[pre + 'mod_b'] = normal((3 * D,), 0.02)
        if kind == 'attn':
            p[pre + 'w_in'] = normal((D, 2 * Q_DIM + 2 * KV_DIM), D ** -0.5)
            p[pre + 'w_out'] = normal((Q_DIM, D), Q_DIM ** -0.5)
            p[pre + 'sink'] = normal((N_HEADS,), 0.5)
        elif kind == 'lru':
            p[pre + 'w_in'] = normal((D, 2 * LRU_WIDTH), D ** -0.5)
            p[pre + 'w_out'] = normal((LRU_WIDTH, D), LRU_WIDTH ** -0.5)
            p[pre + 'conv_w'] = normal((2, LRU_CONV, LRU_WIDTH), LRU_CONV ** -0.5)
            p[pre + 'conv_b'] = normal((2, LRU_WIDTH), 0.02)
            p[pre + 'gate_a_w'] = normal((2, LRU_BLOCKS, LRU_BLOCK, LRU_BLOCK), LRU_BLOCK ** -0.5)
            p[pre + 'gate_a_b'] = normal((2, LRU_WIDTH), 0.02)
            p[pre + 'gate_x_w'] = normal((2, LRU_BLOCKS, LRU_BLOCK, LRU_BLOCK), LRU_BLOCK ** -0.5)
            p[pre + 'gate_x_b'] = normal((2, LRU_WIDTH), 0.02)
            u = jax.random.uniform(next(keys), (2, LRU_WIDTH), jnp.float32, minval=0.9, maxval=0.999)
            s = u ** (1.0 / LRU_C)
            p[pre + 'lambda'] = jnp.log(s) - jnp.log1p(-s)
        else:
            p[pre + 'w_in'] = normal((D, 4 * CONV_WIDTH), D ** -0.5)
            p[pre + 'w_out'] = normal((CONV_WIDTH, D), CONV_WIDTH ** -0.5)
            p[pre + 'conv_w'] = normal((CONV_K, CONV_WIDTH), CONV_K ** -0.5)
            p[pre + 'conv_b'] = normal((CONV_WIDTH,), 0.02)
    p['final_norm_g'] = 1.0 + normal((D,), 0.02)
    return p


def reference(x, c, ctx, c_ctx,
              l0_norm_g, l0_mod_w, l0_mod_b, l0_w_in, l0_w_out, l0_sink,
              l1_norm_g, l1_mod_w, l1_mod_b, l1_w_in, l1_w_out, l1_conv_w, l1_conv_b,
              l1_gate_a_w, l1_gate_a_b, l1_gate_x_w, l1_gate_x_b, l1_lambda,
              l2_norm_g, l2_mod_w, l2_mod_b, l2_w_in, l2_w_out, l2_conv_w, l2_conv_b,
              l3_norm_g, l3_mod_w, l3_mod_b, l3_w_in, l3_w_out, l3_sink,
              final_norm_g):
    layers = [
        (l0_norm_g, l0_mod_w, l0_mod_b, (l0_w_in, l0_w_out, l0_sink)),
        (l1_norm_g, l1_mod_w, l1_mod_b, (l1_w_in, l1_w_out, l1_conv_w, l1_conv_b, l1_gate_a_w,
                                         l1_gate_a_b, l1_gate_x_w, l1_gate_x_b, l1_lambda)),
        (l2_norm_g, l2_mod_w, l2_mod_b, (l2_w_in, l2_w_out, l2_conv_w, l2_conv_b)),
        (l3_norm_g, l3_mod_w, l3_mod_b, (l3_w_in, l3_w_out, l3_sink)),
    ]
    cos, sin = axial_rope_tables(x.shape[1])
    xl, xc = x, ctx
    for i in range(DEPTH):
        norm_g, mod_w, mod_b, mp = layers[i]
        kind = LAYER_KINDS[i % N_MIXERS]
        need_ctx = i < DEPTH - 1
        shift, scale, gate = modulation(c, mod_w, mod_b)
        shift_c, scale_c, gate_c = modulation(c_ctx, mod_w, mod_b)
        hl = rmsnorm(xl, norm_g) * (1.0 + scale[:, None]) + shift[:, None]
        hc = rmsnorm(xc, norm_g) * (1.0 + scale_c) + shift_c
        if kind == 'attn':
            yl, yc = attn_mixer(hl, hc, *mp, cos, sin, need_ctx)
        elif kind == 'lru':
            yl, yc = lru_mixer(hl, hc, *mp, need_ctx)
        else:
            yl, yc = conv_mixer(hl, hc, *mp, need_ctx)
        xl = xl + gate[:, None] * yl
        if need_ctx:
            xc = xc + gate_c * yc
    return rmsnorm(xl, final_norm_g)
```

```python
import functools
import math

import jax
import jax.numpy as jnp
from jax import lax
from jax.experimental import pallas as pl
from jax.experimental.pallas import tpu as pltpu

D_MODEL = 1024
BATCH = 8
SEQ = 2048
CTX_LEN = 256
TOK = CTX_LEN + SEQ
GRID_W = 64
EPS = 1e-6
NEG_INF = -1e30
N_HEADS = 16
N_KV_HEADS = 4
N_GROUPS = N_HEADS // N_KV_HEADS
HEAD_DIM = 64
Q_DIM = N_HEADS * HEAD_DIM
KV_DIM = N_KV_HEADS * HEAD_DIM
WINDOW = 128
ROPE_BASE = 10000.0
LRU_BLOCKS = 16
LRU_BLOCK = 64
LRU_C = 8.0
LRU_CONV = 4
CONV_K = 3

LANES = 128
SUBLANES = 8
MXU_DIM = 256
VMEM_LIMIT_BYTES = 56 * 1024 * 1024

ROW_TILE = 256
CTX_ROW_TILES = CTX_LEN // ROW_TILE
ROW_TILES = TOK // ROW_TILE
Q_TILE = WINDOW
CTX_Q_TILES = CTX_LEN // Q_TILE
Q_TILES = TOK // Q_TILE
BAND = 3 * WINDOW

LRU_TL = 256
LRU_WC = MXU_DIM
LRU_PITCH = LRU_TL + SUBLANES
LRU_SLABS = LRU_WC // LANES
LRU_CTX_BLKS = CTX_LEN // LRU_TL
LRU_BLKS = TOK // LRU_TL

F32 = jnp.float32
BF16 = jnp.bfloat16


def _params(*sem):
    return pltpu.CompilerParams(dimension_semantics=sem, vmem_limit_bytes=VMEM_LIMIT_BYTES)


def _sigmoid(x):
    return 1.0 / (1.0 + jnp.exp(-x))


def _silu(x):
    return x * _sigmoid(x)


def _mod_kernel(c_ref, w_ref, b_ref, o_ref):
    s = _silu(c_ref[...])
    w = w_ref[...]
    s_hi = s.astype(BF16)
    s_lo = (s - s_hi.astype(F32)).astype(BF16)
    w_hi = w.astype(BF16)
    w_lo = (w - w_hi.astype(F32)).astype(BF16)
    acc = jnp.dot(s_hi, w_hi, preferred_element_type=F32)
    acc += jnp.dot(s_hi, w_lo, preferred_element_type=F32)
    acc += jnp.dot(s_lo, w_hi, preferred_element_type=F32)
    o_ref[...] = acc + b_ref[...]


def _modulation(cc, mod_w, mod_b):
    rows = cc.shape[0]
    m = pl.pallas_call(
        _mod_kernel,
        grid=(3,),
        in_specs=[pl.BlockSpec((rows, D_MODEL), lambda j: (0, 0)),
                  pl.BlockSpec((D_MODEL, D_MODEL), lambda j: (0, j)),
                  pl.BlockSpec((1, D_MODEL), lambda j: (0, j))],
        out_specs=pl.BlockSpec((rows, D_MODEL), lambda j: (0, j)),
        out_shape=jax.ShapeDtypeStruct((rows, 3 * D_MODEL), F32),
        compiler_params=_params("arbitrary"),
        name="modulation",
    )(cc, mod_w, mod_b.reshape(1, 3 * D_MODEL))
    lat = m[:BATCH].reshape(BATCH, 1, 3, D_MODEL)
    ctx = jnp.broadcast_to(m[BATCH].reshape(1, 1, 3, D_MODEL), (BATCH, 1, 3, D_MODEL))
    return jnp.concatenate([ctx, lat], axis=1)


def _mod_spec(tiles_per_ctx, tile_offset=0):
    def idx(b, j):
        return (b, jnp.where(j + tile_offset < tiles_per_ctx, 0, 1), 0, 0)
    return pl.BlockSpec((1, 1, 3, D_MODEL), idx)


def _norm_mod(x, g, mod):
    y = x * lax.rsqrt(jnp.mean(x * x, axis=-1, keepdims=True) + EPS) * g
    return y * (1.0 + mod[1:2, :]) + mod[0:1, :]


def _in_attn_kernel(x_ref, mod_ref, g_ref, w_ref, rc_ref, ra_ref, rb_ref,
                    q_ref, k_ref, v_ref, gate_ref):
    h = _norm_mod(x_ref[0], g_ref[...], mod_ref[0, 0])
    z = jnp.dot(h.astype(BF16), w_ref[...], preferred_element_type=F32)
    rc, ra, rb = rc_ref[...], ra_ref[...], rb_ref[...]

    def rope(t):
        reps = t.shape[-1] // LANES
        n = t.shape[-1]
        return (t * jnp.tile(rc, (1, reps))
                + pltpu.roll(t, n - HEAD_DIM // 4, 1) * jnp.tile(ra, (1, reps))
                + pltpu.roll(t, HEAD_DIM // 4, 1) * jnp.tile(rb, (1, reps)))

    q_ref[0] = (rope(z[:, :Q_DIM]) * (HEAD_DIM ** -0.5)).astype(BF16)
    k_ref[0] = rope(z[:, Q_DIM:Q_DIM + KV_DIM]).astype(BF16)
    v_ref[0] = z[:, Q_DIM + KV_DIM:Q_DIM + 2 * KV_DIM].astype(BF16)
    gate_ref[0] = z[:, Q_DIM + 2 * KV_DIM:]


def _in_attn(xs, mod, norm_g, w_in, rope_tabs):
    n = w_in.shape[1]
    row = lambda width: pl.BlockSpec((1, ROW_TILE, width), lambda b, j: (b, j, 0))
    tab = pl.BlockSpec((ROW_TILE, LANES), lambda b, j: (j, 0))
    return pl.pallas_call(
        _in_attn_kernel,
        grid=(BATCH, ROW_TILES),
        in_specs=[row(D_MODEL), _mod_spec(CTX_ROW_TILES),
                  pl.BlockSpec((1, D_MODEL), lambda b, j: (0, 0)),
                  pl.BlockSpec((D_MODEL, n), lambda b, j: (0, 0)),
                  tab, tab, tab],
        out_specs=[row(Q_DIM), row(KV_DIM), row(KV_DIM), row(Q_DIM)],
        out_shape=[jax.ShapeDtypeStruct((BATCH, TOK, Q_DIM), BF16),
                   jax.ShapeDtypeStruct((BATCH, TOK, KV_DIM), BF16),
                   jax.ShapeDtypeStruct((BATCH, TOK, KV_DIM), BF16),
                   jax.ShapeDtypeStruct((BATCH, TOK, Q_DIM), F32)],
        compiler_params=_params("parallel", "arbitrary"),
        name="in_attn",
    )(xs, mod, norm_g.reshape(1, D_MODEL), w_in.astype(BF16), *rope_tabs)


def _in_plain_kernel(x_ref, mod_ref, g_ref, w_ref, *o_refs):
    h = _norm_mod(x_ref[0], g_ref[...], mod_ref[0, 0])
    z = jnp.dot(h.astype(BF16), w_ref[...], preferred_element_type=F32)
    width = z.shape[-1] // len(o_refs)
    for i, o_ref in enumerate(o_refs):
        o_ref[0] = z[:, i * width:(i + 1) * width]


def _in_plain(xs, mod, norm_g, w_in, n_split, name):
    n = w_in.shape[1]
    width = n // n_split
    row = lambda w: pl.BlockSpec((1, ROW_TILE, w), lambda b, j: (b, j, 0))
    return pl.pallas_call(
        _in_plain_kernel,
        grid=(BATCH, ROW_TILES),
        in_specs=[row(D_MODEL), _mod_spec(CTX_ROW_TILES),
                  pl.BlockSpec((1, D_MODEL), lambda b, j: (0, 0)),
                  pl.BlockSpec((D_MODEL, n), lambda b, j: (0, 0))],
        out_specs=[row(width)] * n_split,
        out_shape=[jax.ShapeDtypeStruct((BATCH, TOK, width), F32)] * n_split,
        compiler_params=_params("parallel", "arbitrary"),
        name=name,
    )(xs, mod, norm_g.reshape(1, D_MODEL), w_in.astype(BF16))


def _attn_kernel(sink_ref, q_ref, k_ref, v_ref, g_ref, o_ref, *, tile_offset):
    j = pl.program_id(1) + tile_offset
    q = q_ref[0]
    kc = k_ref[0, 0:CTX_LEN, :]
    vc = v_ref[0, 0:CTX_LEN, :]

    def heads_of(kh):
        return [kh * N_GROUPS + g for g in range(N_GROUPS)]

    def stacked_q(kh):
        return jnp.concatenate([q[:, h * HEAD_DIM:(h + 1) * HEAD_DIM] for h in heads_of(kh)], axis=0)

    def sink_col(kh):
        return jnp.concatenate([jnp.full((Q_TILE, 1), sink_ref[h], F32) for h in heads_of(kh)], axis=0)

    def head_cols(t, kh):
        return t[:, kh * HEAD_DIM:(kh + 1) * HEAD_DIM]

    def scores(qs, k):
        return lax.dot_general(qs, k, (((1,), (1,)), ((), ())), preferred_element_type=F32)

    def finish(outs):
        cols = []
        for kh in range(N_KV_HEADS):
            for g in range(N_GROUPS):
                cols.append(outs[kh][g * Q_TILE:(g + 1) * Q_TILE, :])
        o = jnp.concatenate(cols, axis=1)
        o_ref[0] = (o * _silu(g_ref[0])).astype(BF16)

    @pl.when(j >= CTX_Q_TILES)
    def _latent():
        own = j * Q_TILE
        start = pl.multiple_of(jnp.minimum(own - WINDOW, TOK - BAND), WINDOW)
        kb = k_ref[0, pl.ds(start, BAND), :]
        vb = v_ref[0, pl.ds(start, BAND), :]
        stacked = (N_GROUPS * Q_TILE, BAND)
        qpos = own + (lax.broadcasted_iota(jnp.int32, stacked, 0) & (Q_TILE - 1))
        kpos = start + lax.broadcasted_iota(jnp.int32, stacked, 1)
        mask = (jnp.abs(qpos - kpos) <= WINDOW) & (kpos >= CTX_LEN)
        outs = []
        for kh in range(N_KV_HEADS):
            qs = stacked_q(kh)
            s_band = jnp.where(mask, scores(qs, head_cols(kb, kh)), NEG_INF)
            s_ctx = scores(qs, head_cols(kc, kh))
            sk = sink_col(kh)
            m = jnp.maximum(jnp.maximum(s_band.max(-1, keepdims=True), s_ctx.max(-1, keepdims=True)), sk)
            e_band = jnp.exp(s_band - m)
            e_ctx = jnp.exp(s_ctx - m)
            denom = e_band.sum(-1, keepdims=True) + e_ctx.sum(-1, keepdims=True) + jnp.exp(sk - m)
            o = (jnp.dot(e_band.astype(BF16), head_cols(vb, kh), preferred_element_type=F32)
                 + jnp.dot(e_ctx.astype(BF16), head_cols(vc, kh), preferred_element_type=F32))
            outs.append(o / denom)
        finish(outs)

    @pl.when(j < CTX_Q_TILES)
    def _context():
        outs = []
        for kh in range(N_KV_HEADS):
            s = scores(stacked_q(kh), head_cols(kc, kh))
            sk = sink_col(kh)
            m = jnp.maximum(s.max(-1, keepdims=True), sk)
            e = jnp.exp(s - m)
            denom = e.sum(-1, keepdims=True) + jnp.exp(sk - m)
            o = jnp.dot(e.astype(BF16), head_cols(vc, kh), preferred_element_type=F32)
            outs.append(o / denom)
        finish(outs)


def _attention(q, k, v, gate, sink, need_ctx):
    tile_offset = 0 if need_ctx else CTX_Q_TILES
    tile = lambda width: pl.BlockSpec((1, Q_TILE, width), lambda b, j: (b, j + tile_offset, 0))
    full = pl.BlockSpec((1, TOK, KV_DIM), lambda b, j: (b, 0, 0))
    return pl.pallas_call(
        functools.partial(_attn_kernel, tile_offset=tile_offset),
        grid=(BATCH, Q_TILES - tile_offset),
        in_specs=[pl.BlockSpec(memory_space=pltpu.SMEM), tile(Q_DIM), full, full, tile(Q_DIM)],
        out_specs=pl.BlockSpec((1, Q_TILE, Q_DIM), lambda b, j: (b, j, 0)),
        out_shape=jax.ShapeDtypeStruct((BATCH, TOK - tile_offset * Q_TILE, Q_DIM), BF16),
        compiler_params=_params("parallel", "arbitrary"),
        name="attention",
    )(sink, q, k, v, gate)


def _lru_kernel(uf_ref, ub_ref, cw_ref, cb_ref, wa_ref, ba_ref, wx_ref, bx_ref, lam_ref,
                hf_ref, hb_ref, a_sc, b_sc, halo_sc, h_sc):
    step = pl.program_id(1)
    tl, wc, pitch = LRU_TL, LRU_WC, LRU_PITCH

    @pl.when((step == 0) | (step == LRU_CTX_BLKS))
    def _segment_start():
        halo_sc[...] = jnp.zeros_like(halo_sc)

    @pl.when(step == 0)
    def _sequence_start():
        h_sc[...] = jnp.zeros_like(h_sc)

    def coefficients(d, u_ref):
        u = u_ref[...]
        halo = halo_sc[d]
        w = cw_ref[d]
        if d == 0:
            ext = jnp.concatenate([halo, u], axis=1)
            offs = [SUBLANES - (LRU_CONV - 1) + kk for kk in range(LRU_CONV)]
            halo_sc[d] = u[:, tl - SUBLANES:, :]
        else:
            ext = jnp.concatenate([u, halo], axis=1)
            offs = [LRU_CONV - 1 - kk for kk in range(LRU_CONV)]
            halo_sc[d] = u[:, :SUBLANES, :]
        x = cb_ref[d][None]
        for kk in range(LRU_CONV):
            x = x + w[kk:kk + 1, :][None] * ext[:, offs[kk]:offs[kk] + tl, :]
        x = x.reshape(BATCH * tl, wc)
        xb = x.astype(BF16)
        r = _sigmoid(jnp.dot(xb, wa_ref[d, 0], preferred_element_type=F32) + ba_ref[d])
        i = _sigmoid(jnp.dot(xb, wx_ref[d, 0], preferred_element_type=F32) + bx_ref[d])
        nl = -lam_ref[d]
        softplus = jnp.maximum(nl, 0.0) + jnp.log1p(jnp.exp(-jnp.abs(nl)))
        log_a = (-LRU_C * r) * softplus
        a = jnp.exp(log_a)
        b = jnp.sqrt(jnp.tanh(-log_a) * (a * a + 1.0)) * (i * x)
        for bi in range(BATCH):
            for s in range(LRU_SLABS):
                rows = slice(bi * tl, (bi + 1) * tl)
                lanes = slice(s * LANES, (s + 1) * LANES)
                a_sc[d, s, bi * pitch:bi * pitch + tl, :] = a[rows, lanes]
                b_sc[d, s, bi * pitch:bi * pitch + tl, :] = b[rows, lanes]

    coefficients(0, uf_ref)
    coefficients(1, ub_ref)

    def scan_step(t, carry):
        new = []
        for d in range(2):
            tt = t if d == 0 else tl - 1 - t
            for s in range(LRU_SLABS):
                rows = pl.ds(tt, BATCH, stride=pitch)
                h = a_sc[d, s, rows, :] * carry[d * LRU_SLABS + s] + b_sc[d, s, rows, :]
                b_sc[d, s, rows, :] = h
                new.append(h)
        return tuple(new)

    init = tuple(h_sc[d, s] for d in range(2) for s in range(LRU_SLABS))
    final = lax.fori_loop(0, tl, scan_step, init, unroll=8)
    for d in range(2):
        for s in range(LRU_SLABS):
            h_sc[d, s] = final[d * LRU_SLABS + s]

    for d, o_ref in ((0, hf_ref), (1, hb_ref)):
        for bi in range(BATCH):
            for s in range(LRU_SLABS):
                o_ref[bi, :, s * LANES:(s + 1) * LANES] = b_sc[d, s, bi * pitch:bi * pitch + tl, :]


def _block_diag_chunks(w):
    per = MXU_DIM // LRU_BLOCK
    w = w.reshape(2, LRU_BLOCKS // per, per, LRU_BLOCK, LRU_BLOCK)
    eye = jnp.eye(per, dtype=w.dtype)
    bd = jnp.einsum('dcpij,pq->dcpiqj', w, eye)
    return bd.reshape(2, LRU_BLOCKS // per, MXU_DIM, MXU_DIM).astype(BF16)


def _lru_scan(u, conv_w, conv_b, gate_a_w, gate_a_b, gate_x_w, gate_x_b, lam):
    def fwd_blk(w, s):
        return (0, s, w)

    def bwd_blk(w, s):
        blk = jnp.where(s < LRU_CTX_BLKS, LRU_CTX_BLKS - 1 - s, LRU_BLKS - 1 - (s - LRU_CTX_BLKS))
        return (0, blk, w)

    blk = (BATCH, LRU_TL, LRU_WC)
    vec = pl.BlockSpec((2, 1, LRU_WC), lambda w, s: (0, 0, w))
    gate = pl.BlockSpec((2, LRU_WC // MXU_DIM, MXU_DIM, MXU_DIM), lambda w, s: (0, w, 0, 0))
    width = u.shape[-1]
    out = jax.ShapeDtypeStruct((BATCH, TOK, width), F32)
    return pl.pallas_call(
        _lru_kernel,
        grid=(width // LRU_WC, LRU_BLKS),
        in_specs=[pl.BlockSpec(blk, fwd_blk), pl.BlockSpec(blk, bwd_blk),
                  pl.BlockSpec((2, LRU_CONV, LRU_WC), lambda w, s: (0, 0, w)), vec,
                  gate, vec, gate, vec, vec],
        out_specs=[pl.BlockSpec(blk, fwd_blk), pl.BlockSpec(blk, bwd_blk)],
        out_shape=[out, out],
        scratch_shapes=[pltpu.VMEM((2, LRU_SLABS, BATCH * LRU_PITCH, LANES), F32),
                        pltpu.VMEM((2, LRU_SLABS, BATCH * LRU_PITCH, LANES), F32),
                        pltpu.VMEM((2, BATCH, SUBLANES, LRU_WC), F32),
                        pltpu.VMEM((2, LRU_SLABS, BATCH, LANES), F32)],
        compiler_params=_params("parallel", "arbitrary"),
        name="lru_scan",
    )(u, u, conv_w, conv_b.reshape(2, 1, width), _block_diag_chunks(gate_a_w),
      gate_a_b.reshape(2, 1, width), _block_diag_chunks(gate_x_w), gate_x_b.reshape(2, 1, width),
      lam.reshape(2, 1, width))


def _residual(x_ref, mod_ref, act, w_ref):
    y = jnp.dot(act.astype(BF16), w_ref[...], preferred_element_type=F32)
    return x_ref[0] + mod_ref[0, 0][2:3, :] * y


def _final_norm(x, g):
    return x * lax.rsqrt(jnp.mean(x * x, axis=-1, keepdims=True) + EPS) * g


def _out_attn_kernel(a_ref, w_ref, x_ref, mod_ref, fg_ref, o_ref, *, final):
    out = _residual(x_ref, mod_ref, a_ref[0], w_ref)
    o_ref[0] = _final_norm(out, fg_ref[...]) if final else out


def _out_lru_kernel(hf_ref, hb_ref, g_ref, w_ref, x_ref, mod_ref, o_ref):
    act = (hf_ref[0] + hb_ref[0]) * _silu(g_ref[0])
    o_ref[0] = _residual(x_ref, mod_ref, act, w_ref)


def _out_conv_kernel(u_ref, bg_ref, cg_ref, g_ref, up_ref, cp_ref, un_ref, cn_ref, cw_ref, cb_ref,
                     w_ref, x_ref, mod_ref, o_ref):
    j = pl.program_id(1)
    p = cg_ref[0] * u_ref[0]
    first = (j == 0) | (j == CTX_ROW_TILES)
    last = (j == CTX_ROW_TILES - 1) | (j == ROW_TILES - 1)
    p_prev = jnp.where(first, 0.0, cp_ref[0, SUBLANES - 1:, :] * up_ref[0, SUBLANES - 1:, :])
    p_next = jnp.where(last, 0.0, cn_ref[0, :1, :] * un_ref[0, :1, :])
    before = jnp.concatenate([p_prev, p[:-1]], axis=0)
    after = jnp.concatenate([p[1:], p_next], axis=0)
    w = cw_ref[...]
    conv = w[0:1] * before + w[1:2] * p + w[2:3] * after + cb_ref[...]
    act = bg_ref[0] * conv * _silu(g_ref[0])
    o_ref[0] = _residual(x_ref, mod_ref, act, w_ref)


def _row_spec(width, tile_offset=0):
    return pl.BlockSpec((1, ROW_TILE, width), lambda b, j: (b, j + tile_offset, 0))


_W_OUT_SPEC = pl.BlockSpec((D_MODEL, D_MODEL), lambda b, j: (0, 0))


def _out_attn(act, w_out, xs, mod, final_g):
    final = final_g is not None
    off = CTX_ROW_TILES if final else 0
    fg = (final_g if final else jnp.ones((D_MODEL,), F32)).reshape(1, D_MODEL)
    rows = SEQ if final else TOK
    return pl.pallas_call(
        functools.partial(_out_attn_kernel, final=final),
        grid=(BATCH, ROW_TILES - off),
        in_specs=[_row_spec(D_MODEL), _W_OUT_SPEC, _row_spec(D_MODEL, off),
                  _mod_spec(CTX_ROW_TILES, off), pl.BlockSpec((1, D_MODEL), lambda b, j: (0, 0))],
        out_specs=_row_spec(D_MODEL),
        out_shape=jax.ShapeDtypeStruct((BATCH, rows, D_MODEL), F32),
        compiler_params=_params("parallel", "arbitrary"),
        name="out_attn_final" if final else "out_attn",
    )(act, w_out.astype(BF16), xs, mod, fg)


def _out_lru(hf, hb, gate, w_out, xs, mod):
    return pl.pallas_call(
        _out_lru_kernel,
        grid=(BATCH, ROW_TILES),
        in_specs=[_row_spec(D_MODEL)] * 3 + [_W_OUT_SPEC, _row_spec(D_MODEL), _mod_spec(CTX_ROW_TILES)],
        out_specs=_row_spec(D_MODEL),
        out_shape=jax.ShapeDtypeStruct((BATCH, TOK, D_MODEL), F32),
        compiler_params=_params("parallel", "arbitrary"),
        name="out_lru",
    )(hf, hb, gate, w_out.astype(BF16), xs, mod)


def _out_conv(u, bg, cg, gate, conv_w, conv_b, w_out, xs, mod):
    per = ROW_TILE // SUBLANES
    n8 = TOK // SUBLANES
    prev = pl.BlockSpec((1, SUBLANES, D_MODEL), lambda b, j: (b, jnp.maximum(j * per - 1, 0), 0))
    nxt = pl.BlockSpec((1, SUBLANES, D_MODEL), lambda b, j: (b, jnp.minimum((j + 1) * per, n8 - 1), 0))
    return pl.pallas_call(
        _out_conv_kernel,
        grid=(BATCH, ROW_TILES),
        in_specs=[_row_spec(D_MODEL)] * 4 + [prev, prev, nxt, nxt,
                  pl.BlockSpec((CONV_K, D_MODEL), lambda b, j: (0, 0)),
                  pl.BlockSpec((1, D_MODEL), lambda b, j: (0, 0)),
                  _W_OUT_SPEC, _row_spec(D_MODEL), _mod_spec(CTX_ROW_TILES)],
        out_specs=_row_spec(D_MODEL),
        out_shape=jax.ShapeDtypeStruct((BATCH, TOK, D_MODEL), F32),
        compiler_params=_params("parallel", "arbitrary"),
        name="out_conv",
    )(u, bg, cg, gate, u, cg, u, cg, conv_w, conv_b.reshape(1, D_MODEL), w_out.astype(BF16), xs, mod)


def _rope_tables():
    quarter = HEAD_DIM // 4
    pos = jnp.arange(SEQ)
    row = (pos // GRID_W).astype(F32)
    col = (pos % GRID_W).astype(F32)
    half = HEAD_DIM // 2
    inv = 1.0 / (ROPE_BASE ** (jnp.arange(0, half, 2, dtype=F32) / half))
    zero = jnp.zeros((SEQ, quarter), F32)
    parts_c, parts_a, parts_b = [], [], []
    for axis_pos in (row, col):
        ang = axis_pos[:, None] * inv
        c, s = jnp.cos(ang), jnp.sin(ang)
        parts_c += [c, c]
        parts_a += [-s, zero]
        parts_b += [zero, s]

    def table(parts, ctx_value):
        head = jnp.concatenate(parts, axis=1)
        head = jnp.concatenate([jnp.full((CTX_LEN, HEAD_DIM), ctx_value, F32), head], axis=0)
        return jnp.tile(head, (1, LANES // HEAD_DIM))

    return table(parts_c, 1.0), table(parts_a, 0.0), table(parts_b, 0.0)


def kernel(x, c, ctx, c_ctx, l0_norm_g, l0_mod_w, l0_mod_b, l0_w_in, l0_w_out, l0_sink, l1_norm_g, l1_mod_w, l1_mod_b, l1_w_in, l1_w_out, l1_conv_w, l1_conv_b, l1_gate_a_w, l1_gate_a_b, l1_gate_x_w, l1_gate_x_b, l1_lambda, l2_norm_g, l2_mod_w, l2_mod_b, l2_w_in, l2_w_out, l2_conv_w, l2_conv_b, l3_norm_g, l3_mod_w, l3_mod_b, l3_w_in, l3_w_out, l3_sink, final_norm_g):
    xs = jnp.concatenate([ctx, x], axis=1)
    cc = jnp.concatenate([c, c_ctx[None], jnp.zeros((2 * SUBLANES - BATCH - 1, D_MODEL), F32)], axis=0)
    rope_tabs = _rope_tables()

    mod = _modulation(cc, l0_mod_w, l0_mod_b)
    q, k, v, gate = _in_attn(xs, mod, l0_norm_g, l0_w_in, rope_tabs)
    act = _attention(q, k, v, gate, l0_sink, need_ctx=True)
    xs = _out_attn(act, l0_w_out, xs, mod, None)

    mod = _modulation(cc, l1_mod_w, l1_mod_b)
    u, gate = _in_plain(xs, mod, l1_norm_g, l1_w_in, 2, "in_lru")
    hf, hb = _lru_scan(u, l1_conv_w, l1_conv_b, l1_gate_a_w, l1_gate_a_b, l1_gate_x_w, l1_gate_x_b,
                       l1_lambda)
    xs = _out_lru(hf, hb, gate, l1_w_out, xs, mod)

    mod = _modulation(cc, l2_mod_w, l2_mod_b)
    u, bg, cg, gate = _in_plain(xs, mod, l2_norm_g, l2_w_in, 4, "in_conv")
    xs = _out_conv(u, bg, cg, gate, l2_conv_w, l2_conv_b, l2_w_out, xs, mod)

    mod = _modulation(cc, l3_mod_w, l3_mod_b)
    q, k, v, gate = _in_attn(xs, mod, l3_norm_g, l3_w_in, rope_tabs)
    act = _attention(q, k, v, gate, l3_sink, need_ctx=False)
    return _out_attn(act, l3_w_out, xs, mod, final_norm_g)
```

```python
import functools
import math

import jax
import jax.numpy as jnp
from jax import lax
from jax.experimental import pallas as pl
from jax.experimental.pallas import tpu as pltpu

D_MODEL = 1024
BATCH = 8
SEQ = 2048
CTX_LEN = 256
TOK = CTX_LEN + SEQ
GRID_W = 64
EPS = 1e-6
NEG_INF = -1e30
N_HEADS = 16
N_KV_HEADS = 4
N_GROUPS = N_HEADS // N_KV_HEADS
HEAD_DIM = 64
Q_DIM = N_HEADS * HEAD_DIM
KV_DIM = N_KV_HEADS * HEAD_DIM
WINDOW = 128
ROPE_BASE = 10000.0
LRU_BLOCKS = 16
LRU_BLOCK = 64
LRU_C = 8.0
LRU_CONV = 4
CONV_K = 3

LANES = 128
SUBLANES = 8
MXU_DIM = 256
VMEM_LIMIT_BYTES = 56 * 1024 * 1024

ROW_TILE = 256
CTX_ROW_TILES = CTX_LEN // ROW_TILE
ROW_TILES = TOK // ROW_TILE
Q_TILE = WINDOW
CTX_Q_TILES = CTX_LEN // Q_TILE
Q_TILES = TOK // Q_TILE
BAND = 3 * WINDOW

LRU_TL = 256
LRU_WC = MXU_DIM
LRU_PITCH = LRU_TL + SUBLANES
LRU_SLABS = LRU_WC // LANES
LRU_CTX_BLKS = CTX_LEN // LRU_TL
LRU_BLKS = TOK // LRU_TL

F32 = jnp.float32
BF16 = jnp.bfloat16


def _params(*sem):
    return pltpu.CompilerParams(dimension_semantics=sem, vmem_limit_bytes=VMEM_LIMIT_BYTES)


def _sigmoid(x):
    return 1.0 / (1.0 + jnp.exp(-x))


def _silu(x):
    return x * _sigmoid(x)


def _mod_kernel(c_ref, w_ref, b_ref, o_ref):
    s = _silu(c_ref[...])
    w = w_ref[...]
    s_hi = s.astype(BF16)
    s_lo = (s - s_hi.astype(F32)).astype(BF16)
    w_hi = w.astype(BF16)
    w_lo = (w - w_hi.astype(F32)).astype(BF16)
    acc = jnp.dot(s_hi, w_hi, preferred_element_type=F32)
    acc += jnp.dot(s_hi, w_lo, preferred_element_type=F32)
    acc += jnp.dot(s_lo, w_hi, preferred_element_type=F32)
    o_ref[...] = acc + b_ref[...]


def _modulation(cc, mod_w, mod_b):
    rows = cc.shape[0]
    m = pl.pallas_call(
        _mod_kernel,
        grid=(3,),
        in_specs=[pl.BlockSpec((rows, D_MODEL), lambda j: (0, 0)),
                  pl.BlockSpec((D_MODEL, D_MODEL), lambda j: (0, j)),
                  pl.BlockSpec((1, D_MODEL), lambda j: (0, j))],
        out_specs=pl.BlockSpec((rows, D_MODEL), lambda j: (0, j)),
        out_shape=jax.ShapeDtypeStruct((rows, 3 * D_MODEL), F32),
        compiler_params=_params("arbitrary"),
        name="modulation",
    )(cc, mod_w, mod_b.reshape(1, 3 * D_MODEL))
    lat = m[:BATCH].reshape(BATCH, 1, 3, D_MODEL)
    ctx = jnp.broadcast_to(m[BATCH].reshape(1, 1, 3, D_MODEL), (BATCH, 1, 3, D_MODEL))
    return jnp.concatenate([ctx, lat], axis=1)


def _mod_spec(tiles_per_ctx, tile_offset=0):
    def idx(b, j):
        return (b, jnp.where(j + tile_offset < tiles_per_ctx, 0, 1), 0, 0)
    return pl.BlockSpec((1, 1, 3, D_MODEL), idx)


def _norm_mod(x, g, mod):
    y = x * lax.rsqrt(jnp.mean(x * x, axis=-1, keepdims=True) + EPS) * g
    return y * (1.0 + mod[1:2, :]) + mod[0:1, :]


def _in_attn_kernel(x_ref, mod_ref, g_ref, w_ref, rc_ref, ra_ref, rb_ref,
                    qt_ref, k_ref, vt_ref, gate_ref):
    h = _norm_mod(x_ref[0], g_ref[...], mod_ref[0, 0])
    z = jnp.dot(h.astype(BF16), w_ref[...], preferred_element_type=F32)
    rc, ra, rb = rc_ref[...], ra_ref[...], rb_ref[...]

    def rope(t):
        reps = t.shape[-1] // LANES
        n = t.shape[-1]
        return (t * jnp.tile(rc, (1, reps))
                + pltpu.roll(t, n - HEAD_DIM // 4, 1) * jnp.tile(ra, (1, reps))
                + pltpu.roll(t, HEAD_DIM // 4, 1) * jnp.tile(rb, (1, reps)))

    qt_ref[0] = (rope(z[:, :Q_DIM]) * (HEAD_DIM ** -0.5)).T.astype(BF16)
    k = rope(z[:, Q_DIM:Q_DIM + KV_DIM]).astype(BF16)
    for kh in range(N_KV_HEADS):
        k_ref[0, kh] = k[:, kh * HEAD_DIM:(kh + 1) * HEAD_DIM]
    vt_ref[0] = z[:, Q_DIM + KV_DIM:Q_DIM + 2 * KV_DIM].T.astype(BF16)
    gate_ref[0] = z[:, Q_DIM + 2 * KV_DIM:]


def _in_attn(xs, mod, norm_g, w_in, rope_tabs):
    n = w_in.shape[1]
    row = lambda width: pl.BlockSpec((1, ROW_TILE, width), lambda b, j: (b, j, 0))
    col = lambda height: pl.BlockSpec((1, height, ROW_TILE), lambda b, j: (b, 0, j))
    tab = pl.BlockSpec((ROW_TILE, LANES), lambda b, j: (j, 0))
    return pl.pallas_call(
        _in_attn_kernel,
        grid=(BATCH, ROW_TILES),
        in_specs=[row(D_MODEL), _mod_spec(CTX_ROW_TILES),
                  pl.BlockSpec((1, D_MODEL), lambda b, j: (0, 0)),
                  pl.BlockSpec((D_MODEL, n), lambda b, j: (0, 0)),
                  tab, tab, tab],
        out_specs=[col(Q_DIM),
                   pl.BlockSpec((1, N_KV_HEADS, ROW_TILE, HEAD_DIM), lambda b, j: (b, 0, j, 0)),
                   col(KV_DIM), row(Q_DIM)],
        out_shape=[jax.ShapeDtypeStruct((BATCH, Q_DIM, TOK), BF16),
                   jax.ShapeDtypeStruct((BATCH, N_KV_HEADS, TOK, HEAD_DIM), BF16),
                   jax.ShapeDtypeStruct((BATCH, KV_DIM, TOK), BF16),
                   jax.ShapeDtypeStruct((BATCH, TOK, Q_DIM), F32)],
        compiler_params=_params("parallel", "arbitrary"),
        name="in_attn",
    )(xs, mod, norm_g.reshape(1, D_MODEL), w_in.astype(BF16), *rope_tabs)


def _in_plain_kernel(x_ref, mod_ref, g_ref, w_ref, *o_refs):
    h = _norm_mod(x_ref[0], g_ref[...], mod_ref[0, 0])
    z = jnp.dot(h.astype(BF16), w_ref[...], preferred_element_type=F32)
    width = z.shape[-1] // len(o_refs)
    for i, o_ref in enumerate(o_refs):
        o_ref[0] = z[:, i * width:(i + 1) * width]


def _in_plain(xs, mod, norm_g, w_in, n_split, name):
    n = w_in.shape[1]
    width = n // n_split
    row = lambda w: pl.BlockSpec((1, ROW_TILE, w), lambda b, j: (b, j, 0))
    return pl.pallas_call(
        _in_plain_kernel,
        grid=(BATCH, ROW_TILES),
        in_specs=[row(D_MODEL), _mod_spec(CTX_ROW_TILES),
                  pl.BlockSpec((1, D_MODEL), lambda b, j: (0, 0)),
                  pl.BlockSpec((D_MODEL, n), lambda b, j: (0, 0))],
        out_specs=[row(width)] * n_split,
        out_shape=[jax.ShapeDtypeStruct((BATCH, TOK, width), F32)] * n_split,
        compiler_params=_params("parallel", "arbitrary"),
        name=name,
    )(xs, mod, norm_g.reshape(1, D_MODEL), w_in.astype(BF16))


def _attn_kernel(sink_ref, qt_ref, k_ref, vt_ref, ot_ref, s_sc, bias_sc, *, tile_offset):
    j = pl.program_id(1) + tile_offset
    lanes = N_GROUPS * Q_TILE

    def head_group(kh, key_starts, n_masked):
        heads = [kh * N_GROUPS + g for g in range(N_GROUPS)]
        qt = jnp.concatenate([qt_ref[0, h * HEAD_DIM:(h + 1) * HEAD_DIM, :] for h in heads], axis=1)
        sink = jnp.concatenate([jnp.full((1, Q_TILE), sink_ref[h], F32) for h in heads], axis=1)
        keys = jnp.concatenate([k_ref[0, kh, pl.ds(ks, WINDOW), :] for ks in key_starts], axis=0)
        s_all = jnp.dot(keys, qt, preferred_element_type=F32)
        m8 = jnp.broadcast_to(sink, (SUBLANES, lanes))
        for c in range(len(key_starts)):
            s = s_all[c * WINDOW:(c + 1) * WINDOW, :]
            if c < n_masked:
                s = s + bias_sc[c]
            s_sc[kh, c * WINDOW:(c + 1) * WINDOW, :] = s
            m8 = jnp.maximum(m8, s.reshape(WINDOW // SUBLANES, SUBLANES, lanes).max(axis=0))
        m = m8.max(axis=0, keepdims=True)
        acc = jnp.zeros((HEAD_DIM, lanes), F32)
        l8 = jnp.zeros((SUBLANES, lanes), F32)
        for c, ks in enumerate(key_starts):
            p = jnp.exp(s_sc[kh, c * WINDOW:(c + 1) * WINDOW, :] - m)
            l8 = l8 + p.reshape(WINDOW // SUBLANES, SUBLANES, lanes).sum(axis=0)
            vt = vt_ref[0, kh * HEAD_DIM:(kh + 1) * HEAD_DIM, pl.ds(ks, WINDOW)]
            acc = acc + jnp.dot(vt, p.astype(BF16), preferred_element_type=F32)
        denom = l8.sum(axis=0, keepdims=True) + jnp.exp(sink - m)
        o = acc / denom
        for g, h in enumerate(heads):
            ot_ref[0, h * HEAD_DIM:(h + 1) * HEAD_DIM, :] = o[:, g * Q_TILE:(g + 1) * Q_TILE].astype(BF16)

    ctx_starts = [c * WINDOW for c in range(CTX_LEN // WINDOW)]

    @pl.when(j >= CTX_Q_TILES)
    def _latent():
        own = j * Q_TILE
        start = pl.multiple_of(jnp.minimum(own - WINDOW, TOK - BAND), WINDOW)
        band_starts = [pl.multiple_of(start + c * WINDOW, WINDOW) for c in range(BAND // WINDOW)]
        qpos = own + (lax.broadcasted_iota(jnp.int32, (WINDOW, lanes), 1) & (Q_TILE - 1))
        for c in range(BAND // WINDOW):
            kpos = start + c * WINDOW + lax.broadcasted_iota(jnp.int32, (WINDOW, lanes), 0)
            valid = (jnp.abs(qpos - kpos) <= WINDOW) & (kpos >= CTX_LEN)
            bias_sc[c] = jnp.where(valid, 0.0, NEG_INF)
        for kh in range(N_KV_HEADS):
            head_group(kh, band_starts + ctx_starts, BAND // WINDOW)

    @pl.when(j < CTX_Q_TILES)
    def _context():
        for kh in range(N_KV_HEADS):
            head_group(kh, ctx_starts, 0)


def _attention(qt, k, vt, sink, need_ctx):
    tile_offset = 0 if need_ctx else CTX_Q_TILES
    n_keys = BAND + CTX_LEN
    return pl.pallas_call(
        functools.partial(_attn_kernel, tile_offset=tile_offset),
        grid=(BATCH, Q_TILES - tile_offset),
        in_specs=[pl.BlockSpec(memory_space=pltpu.SMEM),
                  pl.BlockSpec((1, Q_DIM, Q_TILE), lambda b, j: (b, 0, j + tile_offset)),
                  pl.BlockSpec((1, N_KV_HEADS, TOK, HEAD_DIM), lambda b, j: (b, 0, 0, 0)),
                  pl.BlockSpec((1, KV_DIM, TOK), lambda b, j: (b, 0, 0))],
        out_specs=pl.BlockSpec((1, Q_DIM, Q_TILE), lambda b, j: (b, 0, j)),
        out_shape=jax.ShapeDtypeStruct((BATCH, Q_DIM, TOK - tile_offset * Q_TILE), BF16),
        scratch_shapes=[pltpu.VMEM((N_KV_HEADS, n_keys, N_GROUPS * Q_TILE), F32),
                        pltpu.VMEM((BAND // WINDOW, WINDOW, N_GROUPS * Q_TILE), F32)],
        compiler_params=_params("parallel", "arbitrary"),
        name="attention",
    )(sink, qt, k, vt)


def _lru_kernel(uf_ref, ub_ref, cw_ref, cb_ref, wa_ref, ba_ref, wx_ref, bx_ref, lam_ref,
                hf_ref, hb_ref, a_sc, b_sc, halo_sc, h_sc):
    step = pl.program_id(1)
    tl, wc, pitch = LRU_TL, LRU_WC, LRU_PITCH

    @pl.when((step == 0) | (step == LRU_CTX_BLKS))
    def _segment_start():
        halo_sc[...] = jnp.zeros_like(halo_sc)

    @pl.when(step == 0)
    def _sequence_start():
        h_sc[...] = jnp.zeros_like(h_sc)

    def coefficients(d, u_ref):
        u = u_ref[...]
        halo = halo_sc[d]
        w = cw_ref[d]
        if d == 0:
            ext = jnp.concatenate([halo, u], axis=1)
            offs = [SUBLANES - (LRU_CONV - 1) + kk for kk in range(LRU_CONV)]
            halo_sc[d] = u[:, tl - SUBLANES:, :]
        else:
            ext = jnp.concatenate([u, halo], axis=1)
            offs = [LRU_CONV - 1 - kk for kk in range(LRU_CONV)]
            halo_sc[d] = u[:, :SUBLANES, :]
        x = cb_ref[d][None]
        for kk in range(LRU_CONV):
            x = x + w[kk:kk + 1, :][None] * ext[:, offs[kk]:offs[kk] + tl, :]
        x = x.reshape(BATCH * tl, wc)
        xb = x.astype(BF16)
        r = _sigmoid(jnp.dot(xb, wa_ref[d, 0], preferred_element_type=F32) + ba_ref[d])
        i = _sigmoid(jnp.dot(xb, wx_ref[d, 0], preferred_element_type=F32) + bx_ref[d])
        nl = -lam_ref[d]
        softplus = jnp.maximum(nl, 0.0) + jnp.log1p(jnp.exp(-jnp.abs(nl)))
        log_a = (-LRU_C * r) * softplus
        a = jnp.exp(log_a)
        b = jnp.sqrt(jnp.tanh(-log_a) * (a * a + 1.0)) * (i * x)
        for bi in range(BATCH):
            for s in range(LRU_SLABS):
                rows = slice(bi * tl, (bi + 1) * tl)
                lanes = slice(s * LANES, (s + 1) * LANES)
                a_sc[d, s, bi * pitch:bi * pitch + tl, :] = a[rows, lanes]
                b_sc[d, s, bi * pitch:bi * pitch + tl, :] = b[rows, lanes]

    coefficients(0, uf_ref)
    coefficients(1, ub_ref)

    def scan_step(t, carry):
        new = []
        for d in range(2):
            tt = t if d == 0 else tl - 1 - t
            for s in range(LRU_SLABS):
                rows = pl.ds(tt, BATCH, stride=pitch)
                h = a_sc[d, s, rows, :] * carry[d * LRU_SLABS + s] + b_sc[d, s, rows, :]
                b_sc[d, s, rows, :] = h
                new.append(h)
        return tuple(new)

    init = tuple(h_sc[d, s] for d in range(2) for s in range(LRU_SLABS))
    final = lax.fori_loop(0, tl, scan_step, init, unroll=8)
    for d in range(2):
        for s in range(LRU_SLABS):
            h_sc[d, s] = final[d * LRU_SLABS + s]

    for d, o_ref in ((0, hf_ref), (1, hb_ref)):
        for bi in range(BATCH):
            for s in range(LRU_SLABS):
                o_ref[bi, :, s * LANES:(s + 1) * LANES] = b_sc[d, s, bi * pitch:bi * pitch + tl, :]


def _block_diag_chunks(w):
    per = MXU_DIM // LRU_BLOCK
    w = w.reshape(2, LRU_BLOCKS // per, per, LRU_BLOCK, LRU_BLOCK)
    eye = jnp.eye(per, dtype=w.dtype)
    bd = jnp.einsum('dcpij,pq->dcpiqj', w, eye)
    return bd.reshape(2, LRU_BLOCKS // per, MXU_DIM, MXU_DIM).astype(BF16)


def _lru_scan(u, conv_w, conv_b, gate_a_w, gate_a_b, gate_x_w, gate_x_b, lam):
    def fwd_blk(w, s):
        return (0, s, w)

    def bwd_blk(w, s):
        blk = jnp.where(s < LRU_CTX_BLKS, LRU_CTX_BLKS - 1 - s, LRU_BLKS - 1 - (s - LRU_CTX_BLKS))
        return (0, blk, w)

    blk = (BATCH, LRU_TL, LRU_WC)
    vec = pl.BlockSpec((2, 1, LRU_WC), lambda w, s: (0, 0, w))
    gate = pl.BlockSpec((2, LRU_WC // MXU_DIM, MXU_DIM, MXU_DIM), lambda w, s: (0, w, 0, 0))
    width = u.shape[-1]
    out = jax.ShapeDtypeStruct((BATCH, TOK, width), F32)
    return pl.pallas_call(
        _lru_kernel,
        grid=(width // LRU_WC, LRU_BLKS),
        in_specs=[pl.BlockSpec(blk, fwd_blk), pl.BlockSpec(blk, bwd_blk),
                  pl.BlockSpec((2, LRU_CONV, LRU_WC), lambda w, s: (0, 0, w)), vec,
                  gate, vec, gate, vec, vec],
        out_specs=[pl.BlockSpec(blk, fwd_blk), pl.BlockSpec(blk, bwd_blk)],
        out_shape=[out, out],
        scratch_shapes=[pltpu.VMEM((2, LRU_SLABS, BATCH * LRU_PITCH, LANES), F32),
                        pltpu.VMEM((2, LRU_SLABS, BATCH * LRU_PITCH, LANES), F32),
                        pltpu.VMEM((2, BATCH, SUBLANES, LRU_WC), F32),
                        pltpu.VMEM((2, LRU_SLABS, BATCH, LANES), F32)],
        compiler_params=_params("parallel", "arbitrary"),
        name="lru_scan",
    )(u, u, conv_w, conv_b.reshape(2, 1, width), _block_diag_chunks(gate_a_w),
      gate_a_b.reshape(2, 1, width), _block_diag_chunks(gate_x_w), gate_x_b.reshape(2, 1, width),
      lam.reshape(2, 1, width))


def _residual(x_ref, mod_ref, act, w_ref):
    y = jnp.dot(act.astype(BF16), w_ref[...], preferred_element_type=F32)
    return x_ref[0] + mod_ref[0, 0][2:3, :] * y


def _final_norm(x, g):
    return x * lax.rsqrt(jnp.mean(x * x, axis=-1, keepdims=True) + EPS) * g


def _out_attn_kernel(ot_ref, g_ref, w_ref, x_ref, mod_ref, fg_ref, o_ref, *, final):
    act = ot_ref[0].astype(F32).T * _silu(g_ref[0])
    out = _residual(x_ref, mod_ref, act, w_ref)
    o_ref[0] = _final_norm(out, fg_ref[...]) if final else out


def _out_lru_kernel(hf_ref, hb_ref, g_ref, w_ref, x_ref, mod_ref, o_ref):
    act = (hf_ref[0] + hb_ref[0]) * _silu(g_ref[0])
    o_ref[0] = _residual(x_ref, mod_ref, act, w_ref)


def _out_conv_kernel(u_ref, bg_ref, cg_ref, g_ref, up_ref, cp_ref, un_ref, cn_ref, cw_ref, cb_ref,
                     w_ref, x_ref, mod_ref, o_ref):
    j = pl.program_id(1)
    p = cg_ref[0] * u_ref[0]
    first = (j == 0) | (j == CTX_ROW_TILES)
    last = (j == CTX_ROW_TILES - 1) | (j == ROW_TILES - 1)
    p_prev = jnp.where(first, 0.0, cp_ref[0, SUBLANES - 1:, :] * up_ref[0, SUBLANES - 1:, :])
    p_next = jnp.where(last, 0.0, cn_ref[0, :1, :] * un_ref[0, :1, :])
    before = jnp.concatenate([p_prev, p[:-1]], axis=0)
    after = jnp.concatenate([p[1:], p_next], axis=0)
    w = cw_ref[...]
    conv = w[0:1] * before + w[1:2] * p + w[2:3] * after + cb_ref[...]
    act = bg_ref[0] * conv * _silu(g_ref[0])
    o_ref[0] = _residual(x_ref, mod_ref, act, w_ref)


def _row_spec(width, tile_offset=0):
    return pl.BlockSpec((1, ROW_TILE, width), lambda b, j: (b, j + tile_offset, 0))


_W_OUT_SPEC = pl.BlockSpec((D_MODEL, D_MODEL), lambda b, j: (0, 0))


def _out_attn(ot, gate, w_out, xs, mod, final_g):
    final = final_g is not None
    off = CTX_ROW_TILES if final else 0
    fg = (final_g if final else jnp.ones((D_MODEL,), F32)).reshape(1, D_MODEL)
    rows = SEQ if final else TOK
    return pl.pallas_call(
        functools.partial(_out_attn_kernel, final=final),
        grid=(BATCH, ROW_TILES - off),
        in_specs=[pl.BlockSpec((1, Q_DIM, ROW_TILE), lambda b, j: (b, 0, j)), _row_spec(Q_DIM, off),
                  _W_OUT_SPEC, _row_spec(D_MODEL, off), _mod_spec(CTX_ROW_TILES, off),
                  pl.BlockSpec((1, D_MODEL), lambda b, j: (0, 0))],
        out_specs=_row_spec(D_MODEL),
        out_shape=jax.ShapeDtypeStruct((BATCH, rows, D_MODEL), F32),
        compiler_params=_params("parallel", "arbitrary"),
        name="out_attn_final" if final else "out_attn",
    )(ot, gate, w_out.astype(BF16), xs, mod, fg)


def _out_lru(hf, hb, gate, w_out, xs, mod):
    return pl.pallas_call(
        _out_lru_kernel,
        grid=(BATCH, ROW_TILES),
        in_specs=[_row_spec(D_MODEL)] * 3 + [_W_OUT_SPEC, _row_spec(D_MODEL), _mod_spec(CTX_ROW_TILES)],
        out_specs=_row_spec(D_MODEL),
        out_shape=jax.ShapeDtypeStruct((BATCH, TOK, D_MODEL), F32),
        compiler_params=_params("parallel", "arbitrary"),
        name="out_lru",
    )(hf, hb, gate, w_out.astype(BF16), xs, mod)


def _out_conv(u, bg, cg, gate, conv_w, conv_b, w_out, xs, mod):
    per = ROW_TILE // SUBLANES
    n8 = TOK // SUBLANES
    prev = pl.BlockSpec((1, SUBLANES, D_MODEL), lambda b, j: (b, jnp.maximum(j * per - 1, 0), 0))
    nxt = pl.BlockSpec((1, SUBLANES, D_MODEL), lambda b, j: (b, jnp.minimum((j + 1) * per, n8 - 1), 0))
    return pl.pallas_call(
        _out_conv_kernel,
        grid=(BATCH, ROW_TILES),
        in_specs=[_row_spec(D_MODEL)] * 4 + [prev, prev, nxt, nxt,
                  pl.BlockSpec((CONV_K, D_MODEL), lambda b, j: (0, 0)),
                  pl.BlockSpec((1, D_MODEL), lambda b, j: (0, 0)),
                  _W_OUT_SPEC, _row_spec(D_MODEL), _mod_spec(CTX_ROW_TILES)],
        out_specs=_row_spec(D_MODEL),
        out_shape=jax.ShapeDtypeStruct((BATCH, TOK, D_MODEL), F32),
        compiler_params=_params("parallel", "arbitrary"),
        name="out_conv",
    )(u, bg, cg, gate, u, cg, u, cg, conv_w, conv_b.reshape(1, D_MODEL), w_out.astype(BF16), xs, mod)


def _rope_tables():
    quarter = HEAD_DIM // 4
    pos = jnp.arange(SEQ)
    row = (pos // GRID_W).astype(F32)
    col = (pos % GRID_W).astype(F32)
    half = HEAD_DIM // 2
    inv = 1.0 / (ROPE_BASE ** (jnp.arange(0, half, 2, dtype=F32) / half))
    zero = jnp.zeros((SEQ, quarter), F32)
    parts_c, parts_a, parts_b = [], [], []
    for axis_pos in (row, col):
        ang = axis_pos[:, None] * inv
        c, s = jnp.cos(ang), jnp.sin(ang)
        parts_c += [c, c]
        parts_a += [-s, zero]
        parts_b += [zero, s]

    def table(parts, ctx_value):
        head = jnp.concatenate(parts, axis=1)
        head = jnp.concatenate([jnp.full((CTX_LEN, HEAD_DIM), ctx_value, F32), head], axis=0)
        return jnp.tile(head, (1, LANES // HEAD_DIM))

    return table(parts_c, 1.0), table(parts_a, 0.0), table(parts_b, 0.0)


def kernel(x, c, ctx, c_ctx, l0_norm_g, l0_mod_w, l0_mod_b, l0_w_in, l0_w_out, l0_sink, l1_norm_g, l1_mod_w, l1_mod_b, l1_w_in, l1_w_out, l1_conv_w, l1_conv_b, l1_gate_a_w, l1_gate_a_b, l1_gate_x_w, l1_gate_x_b, l1_lambda, l2_norm_g, l2_mod_w, l2_mod_b, l2_w_in, l2_w_out, l2_conv_w, l2_conv_b, l3_norm_g, l3_mod_w, l3_mod_b, l3_w_in, l3_w_out, l3_sink, final_norm_g):
    xs = jnp.concatenate([ctx, x], axis=1)
    cc = jnp.concatenate([c, c_ctx[None], jnp.zeros((2 * SUBLANES - BATCH - 1, D_MODEL), F32)], axis=0)
    rope_tabs = _rope_tables()

    mod = _modulation(cc, l0_mod_w, l0_mod_b)
    qt, k, vt, gate = _in_attn(xs, mod, l0_norm_g, l0_w_in, rope_tabs)
    ot = _attention(qt, k, vt, l0_sink, need_ctx=True)
    xs = _out_attn(ot, gate, l0_w_out, xs, mod, None)

    mod = _modulation(cc, l1_mod_w, l1_mod_b)
    u, gate = _in_plain(xs, mod, l1_norm_g, l1_w_in, 2, "in_lru")
    hf, hb = _lru_scan(u, l1_conv_w, l1_conv_b, l1_gate_a_w, l1_gate_a_b, l1_gate_x_w, l1_gate_x_b,
                       l1_lambda)
    xs = _out_lru(hf, hb, gate, l1_w_out, xs, mod)

    mod = _modulation(cc, l2_mod_w, l2_mod_b)
    u, bg, cg, gate = _in_plain(xs, mod, l2_norm_g, l2_w_in, 4, "in_conv")
    xs = _out_conv(u, bg, cg, gate, l2_conv_w, l2_conv_b, l2_w_out, xs, mod)

    mod = _modulation(cc, l3_mod_w, l3_mod_b)
    qt, k, vt, gate = _in_attn(xs, mod, l3_norm_g, l3_w_in, rope_tabs)
    ot = _attention(qt, k, vt, l3_sink, need_ctx=False)
    return _out_attn(ot, gate, l3_w_out, xs, mod, final_norm_g)
```

```python
import functools
import math

import jax
import jax.numpy as jnp
from jax import lax
from jax.experimental import pallas as pl
from jax.experimental.pallas import tpu as pltpu

D_MODEL = 1024
BATCH = 8
SEQ = 2048
CTX_LEN = 256
TOK = CTX_LEN + SEQ
GRID_W = 64
EPS = 1e-6
NEG_INF = -1e30
N_HEADS = 16
N_KV_HEADS = 4
N_GROUPS = N_HEADS // N_KV_HEADS
HEAD_DIM = 64
Q_DIM = N_HEADS * HEAD_DIM
KV_DIM = N_KV_HEADS * HEAD_DIM
WINDOW = 128
ROPE_BASE = 10000.0
LRU_BLOCKS = 16
LRU_BLOCK = 64
LRU_C = 8.0
LRU_CONV = 4
CONV_K = 3

LANES = 128
SUBLANES = 8
BF16_SUBLANES = 16
MXU_DIM = 256
VMEM_LIMIT_BYTES = 56 * 1024 * 1024

ROW_TILE = 256
CTX_ROW_TILES = CTX_LEN // ROW_TILE
ROW_TILES = TOK // ROW_TILE
Q_TILE = WINDOW
CTX_Q_TILES = CTX_LEN // Q_TILE
Q_TILES = TOK // Q_TILE
BAND = 3 * WINDOW
MAX_ROWS = 32

LRU_TL = 256
LRU_WC = MXU_DIM
LRU_PITCH = LRU_TL + SUBLANES
LRU_SLABS = LRU_WC // LANES
LRU_CTX_BLKS = CTX_LEN // LRU_TL
LRU_BLKS = TOK // LRU_TL

F32 = jnp.float32
BF16 = jnp.bfloat16
F32_TINY = float(jnp.finfo(jnp.float32).tiny)
LOG2E = math.log2(math.e)


def _params(*sem):
    return pltpu.CompilerParams(dimension_semantics=sem, vmem_limit_bytes=VMEM_LIMIT_BYTES)


def _sigmoid(x):
    return 0.5 * jnp.tanh(0.5 * x) + 0.5


def _silu(x):
    return x * _sigmoid(x)


def _mod_kernel(c_ref, w_ref, b_ref, o_ref):
    s = _silu(c_ref[...])
    w = w_ref[...]
    s_hi = s.astype(BF16)
    s_lo = (s - s_hi.astype(F32)).astype(BF16)
    w_hi = w.astype(BF16)
    w_lo = (w - w_hi.astype(F32)).astype(BF16)
    acc = jnp.dot(s_hi, w_hi, preferred_element_type=F32)
    acc += jnp.dot(s_hi, w_lo, preferred_element_type=F32)
    acc += jnp.dot(s_lo, w_hi, preferred_element_type=F32)
    o_ref[...] = acc + b_ref[...]


def _modulation(cc, mod_w, mod_b):
    rows = cc.shape[0]
    m = pl.pallas_call(
        _mod_kernel,
        grid=(3,),
        in_specs=[pl.BlockSpec((rows, D_MODEL), lambda j: (0, 0)),
                  pl.BlockSpec((D_MODEL, D_MODEL), lambda j: (0, j)),
                  pl.BlockSpec((1, D_MODEL), lambda j: (0, j))],
        out_specs=pl.BlockSpec((rows, D_MODEL), lambda j: (0, j)),
        out_shape=jax.ShapeDtypeStruct((rows, 3 * D_MODEL), F32),
        compiler_params=_params("arbitrary"),
        name="modulation",
    )(cc, mod_w, mod_b.reshape(1, 3 * D_MODEL))
    lat = m[:BATCH].reshape(BATCH, 1, 3, D_MODEL)
    ctx = jnp.broadcast_to(m[BATCH].reshape(1, 1, 3, D_MODEL), (BATCH, 1, 3, D_MODEL))
    return jnp.concatenate([ctx, lat], axis=1)


def _mod_spec(tiles_per_ctx, tile_offset=0):
    def idx(b, j):
        return (b, jnp.where(j + tile_offset < tiles_per_ctx, 0, 1), 0, 0)
    return pl.BlockSpec((1, 1, 3, D_MODEL), idx)


def _norm_mod(x, g, mod):
    y = x * lax.rsqrt(jnp.mean(x * x, axis=-1, keepdims=True) + EPS) * g
    return y * (1.0 + mod[1:2, :]) + mod[0:1, :]


def _in_attn_kernel(x_ref, mod_ref, g_ref, w_ref, rc_ref, ra_ref, rb_ref,
                    qt_ref, k_ref, vt_ref, gate_ref):
    h = _norm_mod(x_ref[0], g_ref[...], mod_ref[0, 0])
    z = jnp.dot(h.astype(BF16), w_ref[...], preferred_element_type=F32)
    rc, ra, rb = rc_ref[...], ra_ref[...], rb_ref[...]

    def rope(t):
        reps = t.shape[-1] // LANES
        n = t.shape[-1]
        return (t * jnp.tile(rc, (1, reps))
                + pltpu.roll(t, n - HEAD_DIM // 4, 1) * jnp.tile(ra, (1, reps))
                + pltpu.roll(t, HEAD_DIM // 4, 1) * jnp.tile(rb, (1, reps)))

    qt_ref[0] = (rope(z[:, :Q_DIM]) * (LOG2E * HEAD_DIM ** -0.5)).T.astype(BF16)
    k = rope(z[:, Q_DIM:Q_DIM + KV_DIM]).astype(BF16)
    for kh in range(N_KV_HEADS):
        k_ref[0, kh] = k[:, kh * HEAD_DIM:(kh + 1) * HEAD_DIM]
    vt_ref[0] = z[:, Q_DIM + KV_DIM:Q_DIM + 2 * KV_DIM].T.astype(BF16)
    gate_ref[0] = z[:, Q_DIM + 2 * KV_DIM:]


def _in_attn(xs, mod, norm_g, w_in, rope_tabs):
    n = w_in.shape[1]
    row = lambda width: pl.BlockSpec((1, ROW_TILE, width), lambda b, j: (b, j, 0))
    col = lambda height: pl.BlockSpec((1, height, ROW_TILE), lambda b, j: (b, 0, j))
    tab = pl.BlockSpec((ROW_TILE, LANES), lambda b, j: (j, 0))
    return pl.pallas_call(
        _in_attn_kernel,
        grid=(BATCH, ROW_TILES),
        in_specs=[row(D_MODEL), _mod_spec(CTX_ROW_TILES),
                  pl.BlockSpec((1, D_MODEL), lambda b, j: (0, 0)),
                  pl.BlockSpec((D_MODEL, n), lambda b, j: (0, 0)),
                  tab, tab, tab],
        out_specs=[col(Q_DIM),
                   pl.BlockSpec((1, N_KV_HEADS, ROW_TILE, HEAD_DIM), lambda b, j: (b, 0, j, 0)),
                   col(KV_DIM), row(Q_DIM)],
        out_shape=[jax.ShapeDtypeStruct((BATCH, Q_DIM, TOK), BF16),
                   jax.ShapeDtypeStruct((BATCH, N_KV_HEADS, TOK, HEAD_DIM), BF16),
                   jax.ShapeDtypeStruct((BATCH, KV_DIM, TOK), BF16),
                   jax.ShapeDtypeStruct((BATCH, TOK, Q_DIM), F32)],
        compiler_params=_params("parallel", "arbitrary"),
        name="in_attn",
    )(xs, mod, norm_g.reshape(1, D_MODEL), w_in.astype(BF16), *rope_tabs)


def _in_plain_kernel(x_ref, mod_ref, g_ref, w_ref, *o_refs):
    h = _norm_mod(x_ref[0], g_ref[...], mod_ref[0, 0])
    z = jnp.dot(h.astype(BF16), w_ref[...], preferred_element_type=F32)
    width = z.shape[-1] // len(o_refs)
    for i, o_ref in enumerate(o_refs):
        o_ref[0] = z[:, i * width:(i + 1) * width].astype(o_ref.dtype)


def _in_plain(xs, mod, norm_g, w_in, dtypes, name):
    n = w_in.shape[1]
    n_split = len(dtypes)
    width = n // n_split
    row = lambda w: pl.BlockSpec((1, ROW_TILE, w), lambda b, j: (b, j, 0))
    return pl.pallas_call(
        _in_plain_kernel,
        grid=(BATCH, ROW_TILES),
        in_specs=[row(D_MODEL), _mod_spec(CTX_ROW_TILES),
                  pl.BlockSpec((1, D_MODEL), lambda b, j: (0, 0)),
                  pl.BlockSpec((D_MODEL, n), lambda b, j: (0, 0))],
        out_specs=[row(width)] * n_split,
        out_shape=[jax.ShapeDtypeStruct((BATCH, TOK, width), dt) for dt in dtypes],
        compiler_params=_params("parallel", "arbitrary"),
        name=name,
    )(xs, mod, norm_g.reshape(1, D_MODEL), w_in.astype(BF16))


def _attn_kernel(sink_ref, qt_ref, k_ref, vt_ref, ot_ref, s_sc, bias_sc, *, tile_offset):
    j = pl.program_id(1) + tile_offset
    lanes = N_GROUPS * Q_TILE
    ones = jnp.ones((BF16_SUBLANES, WINDOW), BF16)

    @pl.when(pl.program_id(1) == 0)
    def _band_masks():
        key = lax.broadcasted_iota(jnp.int32, (WINDOW, lanes), 0)
        query = lax.broadcasted_iota(jnp.int32, (WINDOW, lanes), 1) & (Q_TILE - 1)
        bias_sc[0] = jnp.where(key >= query, 0.0, NEG_INF)
        bias_sc[1] = jnp.where(key <= query, 0.0, NEG_INF)

    def heads_of(kh):
        return [kh * N_GROUPS + g for g in range(N_GROUPS)]

    def sink_row(kh):
        return jnp.concatenate([jnp.full((1, Q_TILE), sink_ref[h] * LOG2E, F32) for h in heads_of(kh)], axis=1)

    def score_steps(kh, chunks, state):
        qt = jnp.concatenate([qt_ref[0, h * HEAD_DIM:(h + 1) * HEAD_DIM, :] for h in heads_of(kh)], axis=1)
        mx = jnp.broadcast_to(sink_row(kh), (MAX_ROWS, lanes))
        for c, (ks, bias) in enumerate(chunks):
            s = jnp.dot(k_ref[0, kh, pl.ds(ks, WINDOW), :], qt, preferred_element_type=F32)
            if bias is not None:
                s = s + bias_sc[bias]
            s_sc[kh, c * WINDOW:(c + 1) * WINDOW, :] = s
            mx = jnp.maximum(mx, s.reshape(WINDOW // MAX_ROWS, MAX_ROWS, lanes).max(axis=0))
            yield
        state[kh] = mx.max(axis=0, keepdims=True)

    def value_steps(kh, chunks, state):
        m = state[kh]
        acc = jnp.zeros((HEAD_DIM + BF16_SUBLANES, lanes), F32)
        for c, (ks, _) in enumerate(chunks):
            p = jnp.exp2(s_sc[kh, c * WINDOW:(c + 1) * WINDOW, :] - m).astype(BF16)
            vt = vt_ref[0, kh * HEAD_DIM:(kh + 1) * HEAD_DIM, pl.ds(ks, WINDOW)]
            acc = acc + jnp.dot(jnp.concatenate([vt, ones], axis=0), p, preferred_element_type=F32)
            yield
        denom = acc[HEAD_DIM:HEAD_DIM + 1, :] + jnp.exp2(sink_row(kh) - m)
        o = acc[:HEAD_DIM, :] / denom
        for g, h in enumerate(heads_of(kh)):
            ot_ref[0, h * HEAD_DIM:(h + 1) * HEAD_DIM, :] = o[:, g * Q_TILE:(g + 1) * Q_TILE].astype(BF16)

    def tile(chunks):
        state = {}
        pending = None
        for kh in range(N_KV_HEADS + 1):
            stages = []
            if kh < N_KV_HEADS:
                stages.append(score_steps(kh, chunks, state))
            if pending is not None:
                stages.append(pending)
            while stages:
                stages = [st for st in stages if next(st, StopIteration) is not StopIteration]
            pending = value_steps(kh, chunks, state) if kh < N_KV_HEADS else None

    own = j * Q_TILE
    ctx_chunks = [(c * WINDOW, None) for c in range(CTX_LEN // WINDOW)]
    prev_chunk = (pl.multiple_of(own - WINDOW, WINDOW), 0)
    own_chunk = (pl.multiple_of(own, WINDOW), None)
    next_chunk = (pl.multiple_of(own + WINDOW, WINDOW), 1)

    pl.when(j == CTX_Q_TILES)(lambda: tile([own_chunk, next_chunk] + ctx_chunks))
    pl.when((j > CTX_Q_TILES) & (j < Q_TILES - 1))(lambda: tile([prev_chunk, own_chunk, next_chunk] + ctx_chunks))
    pl.when(j == Q_TILES - 1)(lambda: tile([prev_chunk, own_chunk] + ctx_chunks))
    if tile_offset == 0:
        pl.when(j < CTX_Q_TILES)(lambda: tile(ctx_chunks))


def _attention(qt, k, vt, sink, need_ctx):
    tile_offset = 0 if need_ctx else CTX_Q_TILES
    n_keys = BAND + CTX_LEN
    return pl.pallas_call(
        functools.partial(_attn_kernel, tile_offset=tile_offset),
        grid=(BATCH, Q_TILES - tile_offset),
        in_specs=[pl.BlockSpec(memory_space=pltpu.SMEM),
                  pl.BlockSpec((1, Q_DIM, Q_TILE), lambda b, j: (b, 0, j + tile_offset)),
                  pl.BlockSpec((1, N_KV_HEADS, TOK, HEAD_DIM), lambda b, j: (b, 0, 0, 0)),
                  pl.BlockSpec((1, KV_DIM, TOK), lambda b, j: (b, 0, 0))],
        out_specs=pl.BlockSpec((1, Q_DIM, Q_TILE), lambda b, j: (b, 0, j)),
        out_shape=jax.ShapeDtypeStruct((BATCH, Q_DIM, TOK - tile_offset * Q_TILE), BF16),
        scratch_shapes=[pltpu.VMEM((N_KV_HEADS, n_keys, N_GROUPS * Q_TILE), F32),
                        pltpu.VMEM((2, WINDOW, N_GROUPS * Q_TILE), F32)],
        compiler_params=_params("parallel", "arbitrary"),
        name="attention",
    )(sink, qt, k, vt)


def _lru_kernel(uf_ref, ub_ref, cw_ref, cb_ref, wa_ref, ba_ref, wx_ref, bx_ref, lam_ref,
                hf_ref, hb_ref, u_sc, a_sc, b_sc, y_sc, h_sc):
    step = pl.program_id(1)
    tl, pitch, halo = LRU_TL, LRU_PITCH, SUBLANES
    segment_start = (step == 0) | (step == LRU_CTX_BLKS)

    @pl.when(segment_start)
    def _zero_halo():
        u_sc[0, :, 0:halo, :] = jnp.zeros((BATCH * LRU_SLABS, halo, LANES), F32)
        u_sc[1, :, halo + tl:, :] = jnp.zeros((BATCH * LRU_SLABS, halo, LANES), F32)

    @pl.when(jnp.logical_not(segment_start))
    def _carry_halo():
        u_sc[0, :, 0:halo, :] = u_sc[0, :, tl:tl + halo, :]
        u_sc[1, :, halo + tl:, :] = u_sc[1, :, halo:2 * halo, :]

    @pl.when(step == 0)
    def _sequence_start():
        h_sc[...] = jnp.zeros_like(h_sc)

    for d, u_ref in ((0, uf_ref), (1, ub_ref)):
        for bi in range(BATCH):
            for s in range(LRU_SLABS):
                u_sc[d, bi * LRU_SLABS + s, halo:halo + tl, :] = u_ref[bi, :, s * LANES:(s + 1) * LANES]

    for d in range(2):
        nl = -lam_ref[d]
        half_decay = (0.5 * LRU_C) * (jnp.maximum(nl, 0.0) + jnp.log1p(jnp.exp(-jnp.abs(nl))))
        for bi in range(BATCH):
            cols = []
            for s in range(LRU_SLABS):
                lanes = slice(s * LANES, (s + 1) * LANES)
                x = cb_ref[d][:, lanes]
                for kk in range(LRU_CONV):
                    off = halo + (kk - (LRU_CONV - 1) if d == 0 else (LRU_CONV - 1) - kk)
                    rows = pl.ds(off, tl, stride=1)
                    x = x + cw_ref[d][kk:kk + 1, lanes] * u_sc[d, bi * LRU_SLABS + s, rows, :]
                cols.append(x)
            x = jnp.concatenate(cols, axis=1)
            xb = x.astype(BF16)
            tr = jnp.tanh(jnp.dot(xb, wa_ref[d, 0], preferred_element_type=F32) + ba_ref[d])
            ti = jnp.tanh(jnp.dot(xb, wx_ref[d, 0], preferred_element_type=F32) + bx_ref[d])
            neg_log_a = tr * half_decay + half_decay
            half_x = 0.5 * x
            ix = ti * half_x + half_x
            a = jnp.exp(-neg_log_a)
            var = jnp.tanh(neg_log_a) * (a * a + 1.0)
            b = (var * lax.rsqrt(jnp.maximum(var, F32_TINY))) * ix
            for s in range(LRU_SLABS):
                lanes = slice(s * LANES, (s + 1) * LANES)
                a_sc[d, s, bi * pitch:bi * pitch + tl, :] = a[:, lanes]
                b_sc[d, s, bi * pitch:bi * pitch + tl, :] = b[:, lanes]

    def scan_step(t, carry):
        new = []
        for d in range(2):
            tt = t if d == 0 else tl - 1 - t
            for s in range(LRU_SLABS):
                rows = pl.ds(tt, BATCH, stride=pitch)
                h = a_sc[d, s, rows, :] * carry[d * LRU_SLABS + s] + b_sc[d, s, rows, :]
                y_sc[d, s, rows, :] = h
                new.append(h)
        return tuple(new)

    init = tuple(h_sc[d, s] for d in range(2) for s in range(LRU_SLABS))
    final = lax.fori_loop(0, tl, scan_step, init, unroll=8)
    for d in range(2):
        for s in range(LRU_SLABS):
            h_sc[d, s] = final[d * LRU_SLABS + s]

    for d, o_ref in ((0, hf_ref), (1, hb_ref)):
        for bi in range(BATCH):
            for s in range(LRU_SLABS):
                o_ref[bi, :, s * LANES:(s + 1) * LANES] = (
                    y_sc[d, s, bi * pitch:bi * pitch + tl, :].astype(o_ref.dtype))


def _block_diag_chunks(w):
    per = MXU_DIM // LRU_BLOCK
    w = w.reshape(2, LRU_BLOCKS // per, per, LRU_BLOCK, LRU_BLOCK)
    eye = jnp.eye(per, dtype=w.dtype)
    bd = jnp.einsum('dcpij,pq->dcpiqj', w, eye)
    return bd.reshape(2, LRU_BLOCKS // per, MXU_DIM, MXU_DIM).astype(BF16)


def _lru_scan(u, conv_w, conv_b, gate_a_w, gate_a_b, gate_x_w, gate_x_b, lam):
    def fwd_blk(w, s):
        return (0, s, w)

    def bwd_blk(w, s):
        blk = jnp.where(s < LRU_CTX_BLKS, LRU_CTX_BLKS - 1 - s, LRU_BLKS - 1 - (s - LRU_CTX_BLKS))
        return (0, blk, w)

    blk = (BATCH, LRU_TL, LRU_WC)
    vec = pl.BlockSpec((2, 1, LRU_WC), lambda w, s: (0, 0, w))
    gate = pl.BlockSpec((2, LRU_WC // MXU_DIM, MXU_DIM, MXU_DIM), lambda w, s: (0, w, 0, 0))
    width = u.shape[-1]
    out = jax.ShapeDtypeStruct((BATCH, TOK, width), BF16)
    scan_buf = pltpu.VMEM((2, LRU_SLABS, BATCH * LRU_PITCH, LANES), F32)
    return pl.pallas_call(
        _lru_kernel,
        grid=(width // LRU_WC, LRU_BLKS),
        in_specs=[pl.BlockSpec(blk, fwd_blk), pl.BlockSpec(blk, bwd_blk),
                  pl.BlockSpec((2, LRU_CONV, LRU_WC), lambda w, s: (0, 0, w)), vec,
                  gate, vec, gate, vec, vec],
        out_specs=[pl.BlockSpec(blk, fwd_blk), pl.BlockSpec(blk, bwd_blk)],
        out_shape=[out, out],
        scratch_shapes=[pltpu.VMEM((2, BATCH * LRU_SLABS, LRU_TL + 2 * SUBLANES, LANES), F32),
                        scan_buf, scan_buf, scan_buf,
                        pltpu.VMEM((2, LRU_SLABS, BATCH, LANES), F32)],
        compiler_params=_params("parallel", "arbitrary"),
        name="lru_scan",
    )(u, u, conv_w, conv_b.reshape(2, 1, width),
      _block_diag_chunks(0.5 * gate_a_w), 0.5 * gate_a_b.reshape(2, 1, width),
      _block_diag_chunks(0.5 * gate_x_w), 0.5 * gate_x_b.reshape(2, 1, width),
      lam.reshape(2, 1, width))


def _residual(x_ref, mod_ref, act, w_ref):
    y = jnp.dot(act.astype(BF16), w_ref[...], preferred_element_type=F32)
    return x_ref[0] + mod_ref[0, 0][2:3, :] * y


def _final_norm(x, g):
    return x * lax.rsqrt(jnp.mean(x * x, axis=-1, keepdims=True) + EPS) * g


def _out_attn_kernel(ot_ref, g_ref, w_ref, x_ref, mod_ref, fg_ref, o_ref, *, final):
    act = ot_ref[0].astype(F32).T * _silu(g_ref[0])
    out = _residual(x_ref, mod_ref, act, w_ref)
    o_ref[0] = _final_norm(out, fg_ref[...]) if final else out


def _out_lru_kernel(hf_ref, hb_ref, g_ref, w_ref, x_ref, mod_ref, o_ref):
    act = (hf_ref[0].astype(F32) + hb_ref[0].astype(F32)) * _silu(g_ref[0].astype(F32))
    o_ref[0] = _residual(x_ref, mod_ref, act, w_ref)


def _out_conv_kernel(u_ref, bg_ref, cg_ref, g_ref, up_ref, cp_ref, un_ref, cn_ref, cw_ref, cb_ref,
                     w_ref, x_ref, mod_ref, o_ref):
    j = pl.program_id(1)
    p = cg_ref[0] * u_ref[0]
    first = (j == 0) | (j == CTX_ROW_TILES)
    last = (j == CTX_ROW_TILES - 1) | (j == ROW_TILES - 1)
    p_prev = jnp.where(first, 0.0, cp_ref[0, SUBLANES - 1:, :] * up_ref[0, SUBLANES - 1:, :])
    p_next = jnp.where(last, 0.0, cn_ref[0, :1, :] * un_ref[0, :1, :])
    before = jnp.concatenate([p_prev, p[:-1]], axis=0)
    after = jnp.concatenate([p[1:], p_next], axis=0)
    w = cw_ref[...]
    conv = w[0:1] * before + w[1:2] * p + w[2:3] * after + cb_ref[...]
    act = bg_ref[0] * conv * _silu(g_ref[0])
    o_ref[0] = _residual(x_ref, mod_ref, act, w_ref)


def _row_spec(width, tile_offset=0):
    return pl.BlockSpec((1, ROW_TILE, width), lambda b, j: (b, j + tile_offset, 0))


_W_OUT_SPEC = pl.BlockSpec((D_MODEL, D_MODEL), lambda b, j: (0, 0))


def _out_attn(ot, gate, w_out, xs, mod, final_g):
    final = final_g is not None
    off = CTX_ROW_TILES if final else 0
    fg = (final_g if final else jnp.ones((D_MODEL,), F32)).reshape(1, D_MODEL)
    rows = SEQ if final else TOK
    return pl.pallas_call(
        functools.partial(_out_attn_kernel, final=final),
        grid=(BATCH, ROW_TILES - off),
        in_specs=[pl.BlockSpec((1, Q_DIM, ROW_TILE), lambda b, j: (b, 0, j)), _row_spec(Q_DIM, off),
                  _W_OUT_SPEC, _row_spec(D_MODEL, off), _mod_spec(CTX_ROW_TILES, off),
                  pl.BlockSpec((1, D_MODEL), lambda b, j: (0, 0))],
        out_specs=_row_spec(D_MODEL),
        out_shape=jax.ShapeDtypeStruct((BATCH, rows, D_MODEL), F32),
        compiler_params=_params("parallel", "arbitrary"),
        name="out_attn_final" if final else "out_attn",
    )(ot, gate, w_out.astype(BF16), xs, mod, fg)


def _out_lru(hf, hb, gate, w_out, xs, mod):
    return pl.pallas_call(
        _out_lru_kernel,
        grid=(BATCH, ROW_TILES),
        in_specs=[_row_spec(D_MODEL)] * 3 + [_W_OUT_SPEC, _row_spec(D_MODEL), _mod_spec(CTX_ROW_TILES)],
        out_specs=_row_spec(D_MODEL),
        out_shape=jax.ShapeDtypeStruct((BATCH, TOK, D_MODEL), F32),
        compiler_params=_params("parallel", "arbitrary"),
        name="out_lru",
    )(hf, hb, gate, w_out.astype(BF16), xs, mod)


def _out_conv(u, bg, cg, gate, conv_w, conv_b, w_out, xs, mod):
    per = ROW_TILE // SUBLANES
    n8 = TOK // SUBLANES
    prev = pl.BlockSpec((1, SUBLANES, D_MODEL), lambda b, j: (b, jnp.maximum(j * per - 1, 0), 0))
    nxt = pl.BlockSpec((1, SUBLANES, D_MODEL), lambda b, j: (b, jnp.minimum((j + 1) * per, n8 - 1), 0))
    return pl.pallas_call(
        _out_conv_kernel,
        grid=(BATCH, ROW_TILES),
        in_specs=[_row_spec(D_MODEL)] * 4 + [prev, prev, nxt, nxt,
                  pl.BlockSpec((CONV_K, D_MODEL), lambda b, j: (0, 0)),
                  pl.BlockSpec((1, D_MODEL), lambda b, j: (0, 0)),
                  _W_OUT_SPEC, _row_spec(D_MODEL), _mod_spec(CTX_ROW_TILES)],
        out_specs=_row_spec(D_MODEL),
        out_shape=jax.ShapeDtypeStruct((BATCH, TOK, D_MODEL), F32),
        compiler_params=_params("parallel", "arbitrary"),
        name="out_conv",
    )(u, bg, cg, gate, u, cg, u, cg, conv_w, conv_b.reshape(1, D_MODEL), w_out.astype(BF16), xs, mod)


def _rope_tables():
    quarter = HEAD_DIM // 4
    pos = jnp.arange(SEQ)
    row = (pos // GRID_W).astype(F32)
    col = (pos % GRID_W).astype(F32)
    half = HEAD_DIM // 2
    inv = 1.0 / (ROPE_BASE ** (jnp.arange(0, half, 2, dtype=F32) / half))
    zero = jnp.zeros((SEQ, quarter), F32)
    parts_c, parts_a, parts_b = [], [], []
    for axis_pos in (row, col):
        ang = axis_pos[:, None] * inv
        c, s = jnp.cos(ang), jnp.sin(ang)
        parts_c += [c, c]
        parts_a += [-s, zero]
        parts_b += [zero, s]

    def table(parts, ctx_value):
        head = jnp.concatenate(parts, axis=1)
        head = jnp.concatenate([jnp.full((CTX_LEN, HEAD_DIM), ctx_value, F32), head], axis=0)
        return jnp.tile(head, (1, LANES // HEAD_DIM))

    return table(parts_c, 1.0), table(parts_a, 0.0), table(parts_b, 0.0)


def kernel(x, c, ctx, c_ctx, l0_norm_g, l0_mod_w, l0_mod_b, l0_w_in, l0_w_out, l0_sink, l1_norm_g, l1_mod_w, l1_mod_b, l1_w_in, l1_w_out, l1_conv_w, l1_conv_b, l1_gate_a_w, l1_gate_a_b, l1_gate_x_w, l1_gate_x_b, l1_lambda, l2_norm_g, l2_mod_w, l2_mod_b, l2_w_in, l2_w_out, l2_conv_w, l2_conv_b, l3_norm_g, l3_mod_w, l3_mod_b, l3_w_in, l3_w_out, l3_sink, final_norm_g):
    xs = jnp.concatenate([ctx, x], axis=1)
    cc = jnp.concatenate([c, c_ctx[None], jnp.zeros((2 * SUBLANES - BATCH - 1, D_MODEL), F32)], axis=0)
    rope_tabs = _rope_tables()

    mod = _modulation(cc, l0_mod_w, l0_mod_b)
    qt, k, vt, gate = _in_attn(xs, mod, l0_norm_g, l0_w_in, rope_tabs)
    ot = _attention(qt, k, vt, l0_sink, need_ctx=True)
    xs = _out_attn(ot, gate, l0_w_out, xs, mod, None)

    mod = _modulation(cc, l1_mod_w, l1_mod_b)
    u, gate = _in_plain(xs, mod, l1_norm_g, l1_w_in, (F32, BF16), "in_lru")
    hf, hb = _lru_scan(u, l1_conv_w, l1_conv_b, l1_gate_a_w, l1_gate_a_b, l1_gate_x_w, l1_gate_x_b,
                       l1_lambda)
    xs = _out_lru(hf, hb, gate, l1_w_out, xs, mod)

    mod = _modulation(cc, l2_mod_w, l2_mod_b)
    u, bg, cg, gate = _in_plain(xs, mod, l2_norm_g, l2_w_in, (F32,) * 4, "in_conv")
    xs = _out_conv(u, bg, cg, gate, l2_conv_w, l2_conv_b, l2_w_out, xs, mod)

    mod = _modulation(cc, l3_mod_w, l3_mod_b)
    qt, k, vt, gate = _in_attn(xs, mod, l3_norm_g, l3_w_in, rope_tabs)
    ot = _attention(qt, k, vt, l3_sink, need_ctx=False)
    return _out_attn(ot, gate, l3_w_out, xs, mod, final_norm_g)
```

```python
import functools
import math

import jax
import jax.numpy as jnp
from jax import lax
from jax.experimental import pallas as pl
from jax.experimental.pallas import tpu as pltpu

D_MODEL = 1024
BATCH = 8
SEQ = 2048
CTX_LEN = 256
TOK = CTX_LEN + SEQ
GRID_W = 64
EPS = 1e-6
NEG_INF = -1e30
N_HEADS = 16
N_KV_HEADS = 4
N_GROUPS = N_HEADS // N_KV_HEADS
HEAD_DIM = 64
Q_DIM = N_HEADS * HEAD_DIM
KV_DIM = N_KV_HEADS * HEAD_DIM
WINDOW = 128
ROPE_BASE = 10000.0
LRU_BLOCKS = 16
LRU_BLOCK = 64
LRU_C = 8.0
LRU_CONV = 4
CONV_K = 3

LANES = 128
SUBLANES = 8
BF16_SUBLANES = 16
MXU_DIM = 256
VMEM_LIMIT_BYTES = 56 * 1024 * 1024

ROW_TILE = 256
CTX_ROW_TILES = CTX_LEN // ROW_TILE
ROW_TILES = TOK // ROW_TILE
Q_TILE = WINDOW
CTX_Q_TILES = CTX_LEN // Q_TILE
Q_TILES = TOK // Q_TILE
BAND = 3 * WINDOW
MAX_ROWS = 32
CONV_CHUNK = MXU_DIM

LRU_TL = 256
LRU_WC = MXU_DIM
LRU_PITCH = LRU_TL + SUBLANES
LRU_SLABS = LRU_WC // LANES
LRU_CTX_BLKS = CTX_LEN // LRU_TL
LRU_BLKS = TOK // LRU_TL

F32 = jnp.float32
BF16 = jnp.bfloat16
F32_TINY = float(jnp.finfo(jnp.float32).tiny)
LOG2E = math.log2(math.e)


def _params(*sem):
    return pltpu.CompilerParams(dimension_semantics=sem, vmem_limit_bytes=VMEM_LIMIT_BYTES)


def _sigmoid(x):
    return 0.5 * jnp.tanh(0.5 * x) + 0.5


def _silu(x):
    return x * _sigmoid(x)


def _mod_kernel(c_ref, w_ref, b_ref, o_ref):
    s = _silu(c_ref[...])
    w = w_ref[...]
    s_hi = s.astype(BF16)
    s_lo = (s - s_hi.astype(F32)).astype(BF16)
    w_hi = w.astype(BF16)
    w_lo = (w - w_hi.astype(F32)).astype(BF16)
    acc = jnp.dot(s_hi, w_hi, preferred_element_type=F32)
    acc += jnp.dot(s_hi, w_lo, preferred_element_type=F32)
    acc += jnp.dot(s_lo, w_hi, preferred_element_type=F32)
    o_ref[...] = acc + b_ref[...]


def _modulation(cc, mod_w, mod_b):
    rows = cc.shape[0]
    m = pl.pallas_call(
        _mod_kernel,
        grid=(3,),
        in_specs=[pl.BlockSpec((rows, D_MODEL), lambda j: (0, 0)),
                  pl.BlockSpec((D_MODEL, D_MODEL), lambda j: (0, j)),
                  pl.BlockSpec((1, D_MODEL), lambda j: (0, j))],
        out_specs=pl.BlockSpec((rows, D_MODEL), lambda j: (0, j)),
        out_shape=jax.ShapeDtypeStruct((rows, 3 * D_MODEL), F32),
        compiler_params=_params("arbitrary"),
        name="modulation",
    )(cc, mod_w, mod_b.reshape(1, 3 * D_MODEL))
    lat = m[:BATCH].reshape(BATCH, 1, 3, D_MODEL)
    ctx = jnp.broadcast_to(m[BATCH].reshape(1, 1, 3, D_MODEL), (BATCH, 1, 3, D_MODEL))
    return jnp.concatenate([ctx, lat], axis=1)


def _mod_spec(tiles_per_ctx, tile_offset=0):
    def idx(b, j):
        return (b, jnp.where(j + tile_offset < tiles_per_ctx, 0, 1), 0, 0)
    return pl.BlockSpec((1, 1, 3, D_MODEL), idx)


def _norm_mod(x, g, mod):
    y = x * lax.rsqrt(jnp.mean(x * x, axis=-1, keepdims=True) + EPS) * g
    return y * (1.0 + mod[1:2, :]) + mod[0:1, :]


def _in_attn_kernel(x_ref, mod_ref, g_ref, w_ref, rc_ref, ra_ref, rb_ref,
                    qt_ref, k_ref, vt_ref, gate_ref):
    h = _norm_mod(x_ref[0], g_ref[...], mod_ref[0, 0])
    z = jnp.dot(h.astype(BF16), w_ref[...], preferred_element_type=F32)
    rc, ra, rb = rc_ref[...], ra_ref[...], rb_ref[...]

    def rope(t):
        reps = t.shape[-1] // LANES
        n = t.shape[-1]
        return (t * jnp.tile(rc, (1, reps))
                + pltpu.roll(t, n - HEAD_DIM // 4, 1) * jnp.tile(ra, (1, reps))
                + pltpu.roll(t, HEAD_DIM // 4, 1) * jnp.tile(rb, (1, reps)))

    qt_ref[0] = (rope(z[:, :Q_DIM]) * (LOG2E * HEAD_DIM ** -0.5)).T.astype(BF16)
    k = rope(z[:, Q_DIM:Q_DIM + KV_DIM]).astype(BF16)
    for kh in range(N_KV_HEADS):
        k_ref[0, kh] = k[:, kh * HEAD_DIM:(kh + 1) * HEAD_DIM]
    vt_ref[0] = z[:, Q_DIM + KV_DIM:Q_DIM + 2 * KV_DIM].T.astype(BF16)
    gate_ref[0] = z[:, Q_DIM + 2 * KV_DIM:]


def _in_attn(xs, mod, norm_g, w_in, rope_tabs):
    n = w_in.shape[1]
    row = lambda width: pl.BlockSpec((1, ROW_TILE, width), lambda b, j: (b, j, 0))
    col = lambda height: pl.BlockSpec((1, height, ROW_TILE), lambda b, j: (b, 0, j))
    tab = pl.BlockSpec((ROW_TILE, LANES), lambda b, j: (j, 0))
    return pl.pallas_call(
        _in_attn_kernel,
        grid=(BATCH, ROW_TILES),
        in_specs=[row(D_MODEL), _mod_spec(CTX_ROW_TILES),
                  pl.BlockSpec((1, D_MODEL), lambda b, j: (0, 0)),
                  pl.BlockSpec((D_MODEL, n), lambda b, j: (0, 0)),
                  tab, tab, tab],
        out_specs=[col(Q_DIM),
                   pl.BlockSpec((1, N_KV_HEADS, ROW_TILE, HEAD_DIM), lambda b, j: (b, 0, j, 0)),
                   col(KV_DIM), row(Q_DIM)],
        out_shape=[jax.ShapeDtypeStruct((BATCH, Q_DIM, TOK), BF16),
                   jax.ShapeDtypeStruct((BATCH, N_KV_HEADS, TOK, HEAD_DIM), BF16),
                   jax.ShapeDtypeStruct((BATCH, KV_DIM, TOK), BF16),
                   jax.ShapeDtypeStruct((BATCH, TOK, Q_DIM), F32)],
        compiler_params=_params("parallel", "arbitrary"),
        name="in_attn",
    )(xs, mod, norm_g.reshape(1, D_MODEL), w_in.astype(BF16), *rope_tabs)


def _in_plain_kernel(x_ref, mod_ref, g_ref, w_ref, *o_refs):
    h = _norm_mod(x_ref[0], g_ref[...], mod_ref[0, 0])
    z = jnp.dot(h.astype(BF16), w_ref[...], preferred_element_type=F32)
    width = z.shape[-1] // len(o_refs)
    for i, o_ref in enumerate(o_refs):
        o_ref[0] = z[:, i * width:(i + 1) * width].astype(o_ref.dtype)


def _in_plain(xs, mod, norm_g, w_in, dtypes, name):
    n = w_in.shape[1]
    n_split = len(dtypes)
    width = n // n_split
    row = lambda w: pl.BlockSpec((1, ROW_TILE, w), lambda b, j: (b, j, 0))
    return pl.pallas_call(
        _in_plain_kernel,
        grid=(BATCH, ROW_TILES),
        in_specs=[row(D_MODEL), _mod_spec(CTX_ROW_TILES),
                  pl.BlockSpec((1, D_MODEL), lambda b, j: (0, 0)),
                  pl.BlockSpec((D_MODEL, n), lambda b, j: (0, 0))],
        out_specs=[row(width)] * n_split,
        out_shape=[jax.ShapeDtypeStruct((BATCH, TOK, width), dt) for dt in dtypes],
        compiler_params=_params("parallel", "arbitrary"),
        name=name,
    )(xs, mod, norm_g.reshape(1, D_MODEL), w_in.astype(BF16))


def _attn_kernel(sink_ref, qt_ref, k_ref, vt_ref, ot_ref, s_sc, bias_sc, *, tile_offset):
    j = pl.program_id(1) + tile_offset
    lanes = N_GROUPS * Q_TILE
    ones = jnp.ones((BF16_SUBLANES, WINDOW), BF16)

    @pl.when(pl.program_id(1) == 0)
    def _band_masks():
        key = lax.broadcasted_iota(jnp.int32, (WINDOW, lanes), 0)
        query = lax.broadcasted_iota(jnp.int32, (WINDOW, lanes), 1) & (Q_TILE - 1)
        bias_sc[0] = jnp.where(key >= query, 0.0, NEG_INF)
        bias_sc[1] = jnp.where(key <= query, 0.0, NEG_INF)

    def heads_of(kh):
        return [kh * N_GROUPS + g for g in range(N_GROUPS)]

    def sink_row(kh):
        return jnp.concatenate([jnp.full((1, Q_TILE), sink_ref[h] * LOG2E, F32) for h in heads_of(kh)], axis=1)

    def score_steps(kh, chunks, state):
        qt = jnp.concatenate([qt_ref[0, h * HEAD_DIM:(h + 1) * HEAD_DIM, :] for h in heads_of(kh)], axis=1)
        mx = jnp.broadcast_to(sink_row(kh), (MAX_ROWS, lanes))
        for c, (ks, bias) in enumerate(chunks):
            s = jnp.dot(k_ref[0, kh, pl.ds(ks, WINDOW), :], qt, preferred_element_type=F32)
            if bias is not None:
                s = s + bias_sc[bias]
            s_sc[kh, c * WINDOW:(c + 1) * WINDOW, :] = s
            mx = jnp.maximum(mx, s.reshape(WINDOW // MAX_ROWS, MAX_ROWS, lanes).max(axis=0))
            yield
        state[kh] = mx.max(axis=0, keepdims=True)

    def value_steps(kh, chunks, state):
        m = state[kh]
        acc = jnp.zeros((HEAD_DIM + BF16_SUBLANES, lanes), F32)
        for c, (ks, _) in enumerate(chunks):
            p = jnp.exp2(s_sc[kh, c * WINDOW:(c + 1) * WINDOW, :] - m).astype(BF16)
            vt = vt_ref[0, kh * HEAD_DIM:(kh + 1) * HEAD_DIM, pl.ds(ks, WINDOW)]
            acc = acc + jnp.dot(jnp.concatenate([vt, ones], axis=0), p, preferred_element_type=F32)
            yield
        denom = acc[HEAD_DIM:HEAD_DIM + 1, :] + jnp.exp2(sink_row(kh) - m)
        o = acc[:HEAD_DIM, :] / denom
        for g, h in enumerate(heads_of(kh)):
            ot_ref[0, h * HEAD_DIM:(h + 1) * HEAD_DIM, :] = o[:, g * Q_TILE:(g + 1) * Q_TILE].astype(BF16)

    def tile(chunks):
        state = {}
        pending = None
        for kh in range(N_KV_HEADS + 1):
            stages = []
            if kh < N_KV_HEADS:
                stages.append(score_steps(kh, chunks, state))
            if pending is not None:
                stages.append(pending)
            while stages:
                stages = [st for st in stages if next(st, StopIteration) is not StopIteration]
            pending = value_steps(kh, chunks, state) if kh < N_KV_HEADS else None

    own = j * Q_TILE
    ctx_chunks = [(c * WINDOW, None) for c in range(CTX_LEN // WINDOW)]
    prev_chunk = (pl.multiple_of(own - WINDOW, WINDOW), 0)
    own_chunk = (pl.multiple_of(own, WINDOW), None)
    next_chunk = (pl.multiple_of(own + WINDOW, WINDOW), 1)

    pl.when(j == CTX_Q_TILES)(lambda: tile([own_chunk, next_chunk] + ctx_chunks))
    pl.when((j > CTX_Q_TILES) & (j < Q_TILES - 1))(lambda: tile([prev_chunk, own_chunk, next_chunk] + ctx_chunks))
    pl.when(j == Q_TILES - 1)(lambda: tile([prev_chunk, own_chunk] + ctx_chunks))
    if tile_offset == 0:
        pl.when(j < CTX_Q_TILES)(lambda: tile(ctx_chunks))


def _attention(qt, k, vt, sink, need_ctx):
    tile_offset = 0 if need_ctx else CTX_Q_TILES
    n_keys = BAND + CTX_LEN
    return pl.pallas_call(
        functools.partial(_attn_kernel, tile_offset=tile_offset),
        grid=(BATCH, Q_TILES - tile_offset),
        in_specs=[pl.BlockSpec(memory_space=pltpu.SMEM),
                  pl.BlockSpec((1, Q_DIM, Q_TILE), lambda b, j: (b, 0, j + tile_offset)),
                  pl.BlockSpec((1, N_KV_HEADS, TOK, HEAD_DIM), lambda b, j: (b, 0, 0, 0)),
                  pl.BlockSpec((1, KV_DIM, TOK), lambda b, j: (b, 0, 0))],
        out_specs=pl.BlockSpec((1, Q_DIM, Q_TILE), lambda b, j: (b, 0, j)),
        out_shape=jax.ShapeDtypeStruct((BATCH, Q_DIM, TOK - tile_offset * Q_TILE), BF16),
        scratch_shapes=[pltpu.VMEM((N_KV_HEADS, n_keys, N_GROUPS * Q_TILE), F32),
                        pltpu.VMEM((2, WINDOW, N_GROUPS * Q_TILE), F32)],
        compiler_params=_params("parallel", "arbitrary"),
        name="attention",
    )(sink, qt, k, vt)


def _lru_kernel(uf_ref, ub_ref, cw_ref, cb_ref, wa_ref, ba_ref, wx_ref, bx_ref, lam_ref,
                hf_ref, hb_ref, u_sc, a_sc, b_sc, y_sc, h_sc):
    step = pl.program_id(1)
    tl, pitch, halo = LRU_TL, LRU_PITCH, SUBLANES
    segment_start = (step == 0) | (step == LRU_CTX_BLKS)

    @pl.when(segment_start)
    def _zero_halo():
        u_sc[0, :, 0:halo, :] = jnp.zeros((BATCH * LRU_SLABS, halo, LANES), F32)
        u_sc[1, :, halo + tl:, :] = jnp.zeros((BATCH * LRU_SLABS, halo, LANES), F32)

    @pl.when(jnp.logical_not(segment_start))
    def _carry_halo():
        u_sc[0, :, 0:halo, :] = u_sc[0, :, tl:tl + halo, :]
        u_sc[1, :, halo + tl:, :] = u_sc[1, :, halo:2 * halo, :]

    @pl.when(step == 0)
    def _sequence_start():
        h_sc[...] = jnp.zeros_like(h_sc)

    for d, u_ref in ((0, uf_ref), (1, ub_ref)):
        for bi in range(BATCH):
            for s in range(LRU_SLABS):
                u_sc[d, bi * LRU_SLABS + s, halo:halo + tl, :] = u_ref[bi, :, s * LANES:(s + 1) * LANES]

    for d in range(2):
        nl = -lam_ref[d]
        half_decay = (0.5 * LRU_C) * (jnp.maximum(nl, 0.0) + jnp.log1p(jnp.exp(-jnp.abs(nl))))
        for bi in range(BATCH):
            cols = []
            for s in range(LRU_SLABS):
                lanes = slice(s * LANES, (s + 1) * LANES)
                x = cb_ref[d][:, lanes]
                for kk in range(LRU_CONV):
                    off = halo + (kk - (LRU_CONV - 1) if d == 0 else (LRU_CONV - 1) - kk)
                    rows = pl.ds(off, tl, stride=1)
                    x = x + cw_ref[d][kk:kk + 1, lanes] * u_sc[d, bi * LRU_SLABS + s, rows, :]
                cols.append(x)
            x = jnp.concatenate(cols, axis=1)
            xb = x.astype(BF16)
            tr = jnp.tanh(jnp.dot(xb, wa_ref[d, 0], preferred_element_type=F32) + ba_ref[d])
            ti = jnp.tanh(jnp.dot(xb, wx_ref[d, 0], preferred_element_type=F32) + bx_ref[d])
            neg_log_a = tr * half_decay + half_decay
            half_x = 0.5 * x
            ix = ti * half_x + half_x
            a = jnp.exp(-neg_log_a)
            var = jnp.tanh(neg_log_a) * (a * a + 1.0)
            b = (var * lax.rsqrt(jnp.maximum(var, F32_TINY))) * ix
            for s in range(LRU_SLABS):
                lanes = slice(s * LANES, (s + 1) * LANES)
                a_sc[d, s, bi * pitch:bi * pitch + tl, :] = a[:, lanes]
                b_sc[d, s, bi * pitch:bi * pitch + tl, :] = b[:, lanes]

    def scan_step(t, carry):
        new = []
        for d in range(2):
            tt = t if d == 0 else tl - 1 - t
            for s in range(LRU_SLABS):
                rows = pl.ds(tt, BATCH, stride=pitch)
                h = a_sc[d, s, rows, :] * carry[d * LRU_SLABS + s] + b_sc[d, s, rows, :]
                y_sc[d, s, rows, :] = h
                new.append(h)
        return tuple(new)

    init = tuple(h_sc[d, s] for d in range(2) for s in range(LRU_SLABS))
    final = lax.fori_loop(0, tl, scan_step, init, unroll=8)
    for d in range(2):
        for s in range(LRU_SLABS):
            h_sc[d, s] = final[d * LRU_SLABS + s]

    for d, o_ref in ((0, hf_ref), (1, hb_ref)):
        for bi in range(BATCH):
            for s in range(LRU_SLABS):
                o_ref[bi, :, s * LANES:(s + 1) * LANES] = (
                    y_sc[d, s, bi * pitch:bi * pitch + tl, :].astype(o_ref.dtype))


def _block_diag_chunks(w):
    per = MXU_DIM // LRU_BLOCK
    w = w.reshape(2, LRU_BLOCKS // per, per, LRU_BLOCK, LRU_BLOCK)
    eye = jnp.eye(per, dtype=w.dtype)
    bd = jnp.einsum('dcpij,pq->dcpiqj', w, eye)
    return bd.reshape(2, LRU_BLOCKS // per, MXU_DIM, MXU_DIM).astype(BF16)


def _lru_scan(u, conv_w, conv_b, gate_a_w, gate_a_b, gate_x_w, gate_x_b, lam):
    def fwd_blk(w, s):
        return (0, s, w)

    def bwd_blk(w, s):
        blk = jnp.where(s < LRU_CTX_BLKS, LRU_CTX_BLKS - 1 - s, LRU_BLKS - 1 - (s - LRU_CTX_BLKS))
        return (0, blk, w)

    blk = (BATCH, LRU_TL, LRU_WC)
    vec = pl.BlockSpec((2, 1, LRU_WC), lambda w, s: (0, 0, w))
    gate = pl.BlockSpec((2, LRU_WC // MXU_DIM, MXU_DIM, MXU_DIM), lambda w, s: (0, w, 0, 0))
    width = u.shape[-1]
    out = jax.ShapeDtypeStruct((BATCH, TOK, width), BF16)
    scan_buf = pltpu.VMEM((2, LRU_SLABS, BATCH * LRU_PITCH, LANES), F32)
    return pl.pallas_call(
        _lru_kernel,
        grid=(width // LRU_WC, LRU_BLKS),
        in_specs=[pl.BlockSpec(blk, fwd_blk), pl.BlockSpec(blk, bwd_blk),
                  pl.BlockSpec((2, LRU_CONV, LRU_WC), lambda w, s: (0, 0, w)), vec,
                  gate, vec, gate, vec, vec],
        out_specs=[pl.BlockSpec(blk, fwd_blk), pl.BlockSpec(blk, bwd_blk)],
        out_shape=[out, out],
        scratch_shapes=[pltpu.VMEM((2, BATCH * LRU_SLABS, LRU_TL + 2 * SUBLANES, LANES), F32),
                        scan_buf, scan_buf, scan_buf,
                        pltpu.VMEM((2, LRU_SLABS, BATCH, LANES), F32)],
        compiler_params=_params("parallel", "arbitrary"),
        name="lru_scan",
    )(u, u, conv_w, conv_b.reshape(2, 1, width),
      _block_diag_chunks(0.5 * gate_a_w), 0.5 * gate_a_b.reshape(2, 1, width),
      _block_diag_chunks(0.5 * gate_x_w), 0.5 * gate_x_b.reshape(2, 1, width),
      lam.reshape(2, 1, width))


def _residual(x_ref, mod_ref, act, w_ref):
    y = jnp.dot(act.astype(BF16), w_ref[...], preferred_element_type=F32)
    return x_ref[0] + mod_ref[0, 0][2:3, :] * y


def _final_norm(x, g):
    return x * lax.rsqrt(jnp.mean(x * x, axis=-1, keepdims=True) + EPS) * g


def _out_attn_kernel(ot_ref, g_ref, w_ref, x_ref, mod_ref, fg_ref, o_ref, *, final):
    act = ot_ref[0].astype(F32).T * _silu(g_ref[0])
    out = _residual(x_ref, mod_ref, act, w_ref)
    o_ref[0] = _final_norm(out, fg_ref[...]) if final else out


def _out_lru_kernel(hf_ref, hb_ref, g_ref, w_ref, x_ref, mod_ref, o_ref):
    act = (hf_ref[0].astype(F32) + hb_ref[0].astype(F32)) * _silu(g_ref[0].astype(F32))
    o_ref[0] = _residual(x_ref, mod_ref, act, w_ref)


def _conv_layer_kernel(x_ref, xp_ref, xn_ref, mod_ref, g_ref, wi_ref, cw_ref, cb_ref, wo_ref,
                       o_ref, p_sc):
    j = pl.program_id(1)
    halo, cc = SUBLANES, CONV_CHUNK
    rows = ROW_TILE + 2 * halo
    x = jnp.concatenate([xp_ref[0], x_ref[0], xn_ref[0]], axis=0)
    h = _norm_mod(x, g_ref[...], mod_ref[0, 0]).astype(BF16)
    keep_prev = jnp.where((j == 0) | (j == CTX_ROW_TILES), 0.0, 1.0)
    keep_next = jnp.where((j == CTX_ROW_TILES - 1) | (j == ROW_TILES - 1), 0.0, 1.0)
    main = slice(halo, halo + ROW_TILE)
    acc = jnp.zeros((ROW_TILE, D_MODEL), F32)
    for c in range(D_MODEL // cc):
        z = jnp.dot(h, wi_ref[:, c * 4 * cc:(c + 1) * 4 * cc], preferred_element_type=F32)
        u, bg, cg, g = (z[:, i * cc:(i + 1) * cc] for i in range(4))
        p = cg * u
        chans = slice(c * cc, (c + 1) * cc)
        conv = cw_ref[1:2, chans] * p[main] + cb_ref[:, chans]
        for s in range(cc // LANES):
            lanes = slice(s * LANES, (s + 1) * LANES)
            p_sc[c, s, 0:halo, :] = p[0:halo, lanes] * keep_prev
            p_sc[c, s, main, :] = p[main, lanes]
            p_sc[c, s, halo + ROW_TILE:rows, :] = p[halo + ROW_TILE:rows, lanes] * keep_next
        before = jnp.concatenate([p_sc[c, s, pl.ds(halo - 1, ROW_TILE, stride=1), :]
                                  for s in range(cc // LANES)], axis=1)
        after = jnp.concatenate([p_sc[c, s, pl.ds(halo + 1, ROW_TILE, stride=1), :]
                                 for s in range(cc // LANES)], axis=1)
        conv = conv + cw_ref[0:1, chans] * before + cw_ref[2:3, chans] * after
        act = bg[main] * conv * _silu(g[main])
        acc = acc + jnp.dot(act.astype(BF16), wo_ref[chans, :], preferred_element_type=F32)
    o_ref[0] = x_ref[0] + mod_ref[0, 0][2:3, :] * acc


def _row_spec(width, tile_offset=0):
    return pl.BlockSpec((1, ROW_TILE, width), lambda b, j: (b, j + tile_offset, 0))


_W_OUT_SPEC = pl.BlockSpec((D_MODEL, D_MODEL), lambda b, j: (0, 0))


def _out_attn(ot, gate, w_out, xs, mod, final_g):
    final = final_g is not None
    off = CTX_ROW_TILES if final else 0
    fg = (final_g if final else jnp.ones((D_MODEL,), F32)).reshape(1, D_MODEL)
    rows = SEQ if final else TOK
    return pl.pallas_call(
        functools.partial(_out_attn_kernel, final=final),
        grid=(BATCH, ROW_TILES - off),
        in_specs=[pl.BlockSpec((1, Q_DIM, ROW_TILE), lambda b, j: (b, 0, j)), _row_spec(Q_DIM, off),
                  _W_OUT_SPEC, _row_spec(D_MODEL, off), _mod_spec(CTX_ROW_TILES, off),
                  pl.BlockSpec((1, D_MODEL), lambda b, j: (0, 0))],
        out_specs=_row_spec(D_MODEL),
        out_shape=jax.ShapeDtypeStruct((BATCH, rows, D_MODEL), F32),
        compiler_params=_params("parallel", "arbitrary"),
        name="out_attn_final" if final else "out_attn",
    )(ot, gate, w_out.astype(BF16), xs, mod, fg)


def _out_lru(hf, hb, gate, w_out, xs, mod):
    return pl.pallas_call(
        _out_lru_kernel,
        grid=(BATCH, ROW_TILES),
        in_specs=[_row_spec(D_MODEL)] * 3 + [_W_OUT_SPEC, _row_spec(D_MODEL), _mod_spec(CTX_ROW_TILES)],
        out_specs=_row_spec(D_MODEL),
        out_shape=jax.ShapeDtypeStruct((BATCH, TOK, D_MODEL), F32),
        compiler_params=_params("parallel", "arbitrary"),
        name="out_lru",
    )(hf, hb, gate, w_out.astype(BF16), xs, mod)


def _conv_layer(xs, mod, norm_g, w_in, conv_w, conv_b, w_out):
    per = ROW_TILE // SUBLANES
    n8 = TOK // SUBLANES
    prev = pl.BlockSpec((1, SUBLANES, D_MODEL), lambda b, j: (b, jnp.maximum(j * per - 1, 0), 0))
    nxt = pl.BlockSpec((1, SUBLANES, D_MODEL), lambda b, j: (b, jnp.minimum((j + 1) * per, n8 - 1), 0))
    n_chunks = D_MODEL // CONV_CHUNK
    wi = w_in.reshape(D_MODEL, 4, n_chunks, CONV_CHUNK).transpose(0, 2, 1, 3).reshape(D_MODEL, 4 * D_MODEL)
    const = lambda shape: pl.BlockSpec(shape, lambda b, j: (0, 0))
    return pl.pallas_call(
        _conv_layer_kernel,
        grid=(BATCH, ROW_TILES),
        in_specs=[_row_spec(D_MODEL), prev, nxt, _mod_spec(CTX_ROW_TILES), const((1, D_MODEL)),
                  const((D_MODEL, 4 * D_MODEL)), const((CONV_K, D_MODEL)), const((1, D_MODEL)),
                  _W_OUT_SPEC],
        out_specs=_row_spec(D_MODEL),
        out_shape=jax.ShapeDtypeStruct((BATCH, TOK, D_MODEL), F32),
        scratch_shapes=[pltpu.VMEM((n_chunks, CONV_CHUNK // LANES, ROW_TILE + 2 * SUBLANES, LANES), F32)],
        compiler_params=_params("parallel", "arbitrary"),
        name="conv_layer",
    )(xs, xs, xs, mod, norm_g.reshape(1, D_MODEL), wi.astype(BF16), conv_w, conv_b.reshape(1, D_MODEL),
      w_out.astype(BF16))


def _rope_tables():
    quarter = HEAD_DIM // 4
    pos = jnp.arange(SEQ)
    row = (pos // GRID_W).astype(F32)
    col = (pos % GRID_W).astype(F32)
    half = HEAD_DIM // 2
    inv = 1.0 / (ROPE_BASE ** (jnp.arange(0, half, 2, dtype=F32) / half))
    zero = jnp.zeros((SEQ, quarter), F32)
    parts_c, parts_a, parts_b = [], [], []
    for axis_pos in (row, col):
        ang = axis_pos[:, None] * inv
        c, s = jnp.cos(ang), jnp.sin(ang)
        parts_c += [c, c]
        parts_a += [-s, zero]
        parts_b += [zero, s]

    def table(parts, ctx_value):
        head = jnp.concatenate(parts, axis=1)
        head = jnp.concatenate([jnp.full((CTX_LEN, HEAD_DIM), ctx_value, F32), head], axis=0)
        return jnp.tile(head, (1, LANES // HEAD_DIM))

    return table(parts_c, 1.0), table(parts_a, 0.0), table(parts_b, 0.0)


def kernel(x, c, ctx, c_ctx, l0_norm_g, l0_mod_w, l0_mod_b, l0_w_in, l0_w_out, l0_sink, l1_norm_g, l1_mod_w, l1_mod_b, l1_w_in, l1_w_out, l1_conv_w, l1_conv_b, l1_gate_a_w, l1_gate_a_b, l1_gate_x_w, l1_gate_x_b, l1_lambda, l2_norm_g, l2_mod_w, l2_mod_b, l2_w_in, l2_w_out, l2_conv_w, l2_conv_b, l3_norm_g, l3_mod_w, l3_mod_b, l3_w_in, l3_w_out, l3_sink, final_norm_g):
    xs = jnp.concatenate([ctx, x], axis=1)
    cc = jnp.concatenate([c, c_ctx[None], jnp.zeros((2 * SUBLANES - BATCH - 1, D_MODEL), F32)], axis=0)
    rope_tabs = _rope_tables()

    mod = _modulation(cc, l0_mod_w, l0_mod_b)
    qt, k, vt, gate = _in_attn(xs, mod, l0_norm_g, l0_w_in, rope_tabs)
    ot = _attention(qt, k, vt, l0_sink, need_ctx=True)
    xs = _out_attn(ot, gate, l0_w_out, xs, mod, None)

    mod = _modulation(cc, l1_mod_w, l1_mod_b)
    u, gate = _in_plain(xs, mod, l1_norm_g, l1_w_in, (F32, BF16), "in_lru")
    hf, hb = _lru_scan(u, l1_conv_w, l1_conv_b, l1_gate_a_w, l1_gate_a_b, l1_gate_x_w, l1_gate_x_b,
                       l1_lambda)
    xs = _out_lru(hf, hb, gate, l1_w_out, xs, mod)

    mod = _modulation(cc, l2_mod_w, l2_mod_b)
    xs = _conv_layer(xs, mod, l2_norm_g, l2_w_in, l2_conv_w, l2_conv_b, l2_w_out)

    mod = _modulation(cc, l3_mod_w, l3_mod_b)
    qt, k, vt, gate = _in_attn(xs, mod, l3_norm_g, l3_w_in, rope_tabs)
    ot = _attention(qt, k, vt, l3_sink, need_ctx=False)
    return _out_attn(ot, gate, l3_w_out, xs, mod, final_norm_g)
```

```python
import functools
import math

import jax
import jax.numpy as jnp
from jax import lax
from jax.experimental import pallas as pl
from jax.experimental.pallas import tpu as pltpu

D_MODEL = 1024
BATCH = 8
SEQ = 2048
CTX_LEN = 256
TOK = CTX_LEN + SEQ
GRID_W = 64
EPS = 1e-6
NEG_INF = -1e30
N_HEADS = 16
N_KV_HEADS = 4
N_GROUPS = N_HEADS // N_KV_HEADS
HEAD_DIM = 64
Q_DIM = N_HEADS * HEAD_DIM
KV_DIM = N_KV_HEADS * HEAD_DIM
WINDOW = 128
ROPE_BASE = 10000.0
LRU_BLOCKS = 16
LRU_BLOCK = 64
LRU_C = 8.0
LRU_CONV = 4
CONV_K = 3

LANES = 128
SUBLANES = 8
BF16_SUBLANES = 16
MXU_DIM = 256
VMEM_LIMIT_BYTES = 56 * 1024 * 1024

ROW_TILE = 256
ROW_BATCH = 2
CTX_ROW_TILES = CTX_LEN // ROW_TILE
ROW_TILES = TOK // ROW_TILE
Q_TILE = WINDOW
CTX_Q_TILES = CTX_LEN // Q_TILE
Q_TILES = TOK // Q_TILE
BAND = 3 * WINDOW
MAX_ROWS = 32
CONV_CHUNK = MXU_DIM

LRU_TL = 256
LRU_WC = MXU_DIM
LRU_PITCH = LRU_TL + SUBLANES
LRU_SLABS = LRU_WC // LANES
LRU_CTX_BLKS = CTX_LEN // LRU_TL
LRU_BLKS = TOK // LRU_TL

F32 = jnp.float32
BF16 = jnp.bfloat16
F32_TINY = float(jnp.finfo(jnp.float32).tiny)
LOG2E = math.log2(math.e)


def _params(*sem):
    return pltpu.CompilerParams(dimension_semantics=sem, vmem_limit_bytes=VMEM_LIMIT_BYTES)


def _sigmoid(x):
    return 0.5 * jnp.tanh(0.5 * x) + 0.5


def _silu(x):
    return x * _sigmoid(x)


def _mod_kernel(c_ref, w_ref, b_ref, o_ref):
    s = _silu(c_ref[...])
    w = w_ref[...]
    s_hi = s.astype(BF16)
    s_lo = (s - s_hi.astype(F32)).astype(BF16)
    w_hi = w.astype(BF16)
    w_lo = (w - w_hi.astype(F32)).astype(BF16)
    acc = jnp.dot(s_hi, w_hi, preferred_element_type=F32)
    acc += jnp.dot(s_hi, w_lo, preferred_element_type=F32)
    acc += jnp.dot(s_lo, w_hi, preferred_element_type=F32)
    o_ref[...] = acc + b_ref[...]


def _modulation(cc, mod_w, mod_b):
    rows = cc.shape[0]
    m = pl.pallas_call(
        _mod_kernel,
        grid=(3,),
        in_specs=[pl.BlockSpec((rows, D_MODEL), lambda j: (0, 0)),
                  pl.BlockSpec((D_MODEL, D_MODEL), lambda j: (0, j)),
                  pl.BlockSpec((1, D_MODEL), lambda j: (0, j))],
        out_specs=pl.BlockSpec((rows, D_MODEL), lambda j: (0, j)),
        out_shape=jax.ShapeDtypeStruct((rows, 3 * D_MODEL), F32),
        compiler_params=_params("arbitrary"),
        name="modulation",
    )(cc, mod_w, mod_b.reshape(1, 3 * D_MODEL))
    lat = m[:BATCH].reshape(BATCH, 1, 3, D_MODEL)
    ctx = jnp.broadcast_to(m[BATCH].reshape(1, 1, 3, D_MODEL), (BATCH, 1, 3, D_MODEL))
    return jnp.concatenate([ctx, lat], axis=1)


def _mod_spec(tile_offset=0):
    def idx(b, j):
        return (b, jnp.where(j + tile_offset < CTX_ROW_TILES, 0, 1), 0, 0)
    return pl.BlockSpec((ROW_BATCH, 1, 3, D_MODEL), idx)


def _row_spec(width, tile_offset=0):
    return pl.BlockSpec((ROW_BATCH, ROW_TILE, width), lambda b, j: (b, j + tile_offset, 0))


def _const_spec(shape):
    return pl.BlockSpec(shape, lambda b, j: (0,) * len(shape))


_ROW_GRID = (BATCH // ROW_BATCH, ROW_TILES)


def _norm_mod(x, g, mod):
    y = x * lax.rsqrt(jnp.mean(x * x, axis=-1, keepdims=True) + EPS) * g
    return y * (1.0 + mod[1:2, :]) + mod[0:1, :]


def _norm_mod_rows(x_ref, g_ref, mod_ref):
    tiles = [_norm_mod(x_ref[bb], g_ref[...], mod_ref[bb, 0]) for bb in range(ROW_BATCH)]
    return jnp.concatenate(tiles, axis=0).astype(BF16)


def _tile_rows(t, bb):
    return t[bb * ROW_TILE:(bb + 1) * ROW_TILE]


def _in_attn_kernel(x_ref, mod_ref, g_ref, w_ref, rc_ref, ra_ref, rb_ref,
                    qt_ref, k_ref, vt_ref, gate_ref):
    z_all = jnp.dot(_norm_mod_rows(x_ref, g_ref, mod_ref), w_ref[...], preferred_element_type=F32)
    rc, ra, rb = rc_ref[...], ra_ref[...], rb_ref[...]

    def rope(t):
        reps = t.shape[-1] // LANES
        n = t.shape[-1]
        return (t * jnp.tile(rc, (1, reps))
                + pltpu.roll(t, n - HEAD_DIM // 4, 1) * jnp.tile(ra, (1, reps))
                + pltpu.roll(t, HEAD_DIM // 4, 1) * jnp.tile(rb, (1, reps)))

    for bb in range(ROW_BATCH):
        z = _tile_rows(z_all, bb)
        qt_ref[bb] = (rope(z[:, :Q_DIM]) * (LOG2E * HEAD_DIM ** -0.5)).T.astype(BF16)
        k = rope(z[:, Q_DIM:Q_DIM + KV_DIM]).astype(BF16)
        for kh in range(N_KV_HEADS):
            k_ref[bb, kh] = k[:, kh * HEAD_DIM:(kh + 1) * HEAD_DIM]
        vt_ref[bb] = z[:, Q_DIM + KV_DIM:Q_DIM + 2 * KV_DIM].T.astype(BF16)
        gate_ref[bb] = z[:, Q_DIM + 2 * KV_DIM:]


def _in_attn(xs, mod, norm_g, w_in, rope_tabs):
    n = w_in.shape[1]
    col = lambda height: pl.BlockSpec((ROW_BATCH, height, ROW_TILE), lambda b, j: (b, 0, j))
    tab = pl.BlockSpec((ROW_TILE, LANES), lambda b, j: (j, 0))
    return pl.pallas_call(
        _in_attn_kernel,
        grid=_ROW_GRID,
        in_specs=[_row_spec(D_MODEL), _mod_spec(), _const_spec((1, D_MODEL)), _const_spec((D_MODEL, n)),
                  tab, tab, tab],
        out_specs=[col(Q_DIM),
                   pl.BlockSpec((ROW_BATCH, N_KV_HEADS, ROW_TILE, HEAD_DIM), lambda b, j: (b, 0, j, 0)),
                   col(KV_DIM), _row_spec(Q_DIM)],
        out_shape=[jax.ShapeDtypeStruct((BATCH, Q_DIM, TOK), BF16),
                   jax.ShapeDtypeStruct((BATCH, N_KV_HEADS, TOK, HEAD_DIM), BF16),
                   jax.ShapeDtypeStruct((BATCH, KV_DIM, TOK), BF16),
                   jax.ShapeDtypeStruct((BATCH, TOK, Q_DIM), F32)],
        compiler_params=_params("parallel", "arbitrary"),
        name="in_attn",
    )(xs, mod, norm_g.reshape(1, D_MODEL), w_in.astype(BF16), *rope_tabs)


def _in_plain_kernel(x_ref, mod_ref, g_ref, w_ref, *o_refs):
    z = jnp.dot(_norm_mod_rows(x_ref, g_ref, mod_ref), w_ref[...], preferred_element_type=F32)
    width = z.shape[-1] // len(o_refs)
    for i, o_ref in enumerate(o_refs):
        for bb in range(ROW_BATCH):
            o_ref[bb] = _tile_rows(z, bb)[:, i * width:(i + 1) * width].astype(o_ref.dtype)


def _in_plain(xs, mod, norm_g, w_in, dtypes, name):
    n = w_in.shape[1]
    width = n // len(dtypes)
    return pl.pallas_call(
        _in_plain_kernel,
        grid=_ROW_GRID,
        in_specs=[_row_spec(D_MODEL), _mod_spec(), _const_spec((1, D_MODEL)), _const_spec((D_MODEL, n))],
        out_specs=[_row_spec(width)] * len(dtypes),
        out_shape=[jax.ShapeDtypeStruct((BATCH, TOK, width), dt) for dt in dtypes],
        compiler_params=_params("parallel", "arbitrary"),
        name=name,
    )(xs, mod, norm_g.reshape(1, D_MODEL), w_in.astype(BF16))


def _attn_kernel(sink_ref, qt_ref, k_ref, vt_ref, ot_ref, s_sc, bias_sc, *, first_tile):
    lanes = N_GROUPS * Q_TILE
    ones = jnp.ones((BF16_SUBLANES, WINDOW), BF16)

    key = lax.broadcasted_iota(jnp.int32, (WINDOW, lanes), 0)
    query = lax.broadcasted_iota(jnp.int32, (WINDOW, lanes), 1) & (Q_TILE - 1)
    bias_sc[0] = jnp.where(key >= query, 0.0, NEG_INF)
    bias_sc[1] = jnp.where(key <= query, 0.0, NEG_INF)

    def aligned(start):
        return start if isinstance(start, int) else pl.multiple_of(start, WINDOW)

    def heads_of(kh):
        return [kh * N_GROUPS + g for g in range(N_GROUPS)]

    def sink_row(kh):
        return jnp.concatenate([jnp.full((1, Q_TILE), sink_ref[h] * LOG2E, F32) for h in heads_of(kh)], axis=1)

    def score_steps(j, kh, chunks, state):
        cols = pl.ds(aligned(j * Q_TILE), Q_TILE)
        qt = jnp.concatenate([qt_ref[0, h * HEAD_DIM:(h + 1) * HEAD_DIM, cols] for h in heads_of(kh)], axis=1)
        mx = jnp.broadcast_to(sink_row(kh), (MAX_ROWS, lanes))
        for c, (ks, bias) in enumerate(chunks):
            s = jnp.dot(k_ref[0, kh, pl.ds(ks, WINDOW), :], qt, preferred_element_type=F32)
            if bias is not None:
                s = s + bias_sc[bias]
            s_sc[kh, c * WINDOW:(c + 1) * WINDOW, :] = s
            mx = jnp.maximum(mx, s.reshape(WINDOW // MAX_ROWS, MAX_ROWS, lanes).max(axis=0))
            yield
        state[kh] = mx.max(axis=0, keepdims=True)

    def value_steps(j, kh, chunks, state):
        m = state[kh]
        acc = jnp.zeros((HEAD_DIM + BF16_SUBLANES, lanes), F32)
        for c, (ks, _) in enumerate(chunks):
            p = jnp.exp2(s_sc[kh, c * WINDOW:(c + 1) * WINDOW, :] - m).astype(BF16)
            vt = vt_ref[0, kh * HEAD_DIM:(kh + 1) * HEAD_DIM, pl.ds(ks, WINDOW)]
            acc = acc + jnp.dot(jnp.concatenate([vt, ones], axis=0), p, preferred_element_type=F32)
            yield
        denom = acc[HEAD_DIM:HEAD_DIM + 1, :] + jnp.exp2(sink_row(kh) - m)
        o = acc[:HEAD_DIM, :] / denom
        cols = pl.ds(aligned((j - first_tile) * Q_TILE), Q_TILE)
        for g, h in enumerate(heads_of(kh)):
            ot_ref[0, h * HEAD_DIM:(h + 1) * HEAD_DIM, cols] = o[:, g * Q_TILE:(g + 1) * Q_TILE].astype(BF16)

    def tile(j, chunks):
        state = {}
        pending = None
        for kh in range(N_KV_HEADS + 1):
            stages = []
            if kh < N_KV_HEADS:
                stages.append(score_steps(j, kh, chunks, state))
            if pending is not None:
                stages.append(pending)
            while stages:
                stages = [st for st in stages if next(st, StopIteration) is not StopIteration]
            pending = value_steps(j, kh, chunks, state) if kh < N_KV_HEADS else None

    ctx_chunks = [(c * WINDOW, None) for c in range(CTX_LEN // WINDOW)]

    def band_chunks(j, has_prev, has_next):
        own = j * Q_TILE
        chunks = [(aligned(own - WINDOW), 0)] if has_prev else []
        chunks.append((aligned(own), None))
        if has_next:
            chunks.append((aligned(own + WINDOW), 1))
        return chunks + ctx_chunks

    if first_tile == 0:
        for j in range(CTX_Q_TILES):
            tile(j, ctx_chunks)
    tile(CTX_Q_TILES, band_chunks(CTX_Q_TILES, False, True))

    def middle(j, carry):
        tile(j, band_chunks(j, True, True))
        return carry

    lax.fori_loop(CTX_Q_TILES + 1, Q_TILES - 1, middle, 0)
    tile(Q_TILES - 1, band_chunks(Q_TILES - 1, True, False))


def _attention(qt, k, vt, sink, need_ctx):
    first_tile = 0 if need_ctx else CTX_Q_TILES
    n_keys = BAND + CTX_LEN
    out_cols = TOK - first_tile * Q_TILE
    return pl.pallas_call(
        functools.partial(_attn_kernel, first_tile=first_tile),
        grid=(BATCH,),
        in_specs=[pl.BlockSpec(memory_space=pltpu.SMEM),
                  pl.BlockSpec((1, Q_DIM, TOK), lambda b: (b, 0, 0)),
                  pl.BlockSpec((1, N_KV_HEADS, TOK, HEAD_DIM), lambda b: (b, 0, 0, 0)),
                  pl.BlockSpec((1, KV_DIM, TOK), lambda b: (b, 0, 0))],
        out_specs=pl.BlockSpec((1, Q_DIM, out_cols), lambda b: (b, 0, 0)),
        out_shape=jax.ShapeDtypeStruct((BATCH, Q_DIM, out_cols), BF16),
        scratch_shapes=[pltpu.VMEM((N_KV_HEADS, n_keys, N_GROUPS * Q_TILE), F32),
                        pltpu.VMEM((2, WINDOW, N_GROUPS * Q_TILE), F32)],
        compiler_params=_params("arbitrary"),
        name="attention",
    )(sink, qt, k, vt)


def _lru_kernel(uf_ref, ub_ref, cw_ref, cb_ref, wa_ref, ba_ref, wx_ref, bx_ref, lam_ref,
                hf_ref, hb_ref, u_sc, a_sc, b_sc, y_sc, h_sc):
    step = pl.program_id(1)
    tl, pitch, halo = LRU_TL, LRU_PITCH, SUBLANES
    segment_start = (step == 0) | (step == LRU_CTX_BLKS)

    @pl.when(segment_start)
    def _zero_halo():
        u_sc[0, :, 0:halo, :] = jnp.zeros((BATCH * LRU_SLABS, halo, LANES), F32)
        u_sc[1, :, halo + tl:, :] = jnp.zeros((BATCH * LRU_SLABS, halo, LANES), F32)

    @pl.when(jnp.logical_not(segment_start))
    def _carry_halo():
        u_sc[0, :, 0:halo, :] = u_sc[0, :, tl:tl + halo, :]
        u_sc[1, :, halo + tl:, :] = u_sc[1, :, halo:2 * halo, :]

    @pl.when(step == 0)
    def _sequence_start():
        h_sc[...] = jnp.zeros_like(h_sc)

    for d, u_ref in ((0, uf_ref), (1, ub_ref)):
        for bi in range(BATCH):
            for s in range(LRU_SLABS):
                u_sc[d, bi * LRU_SLABS + s, halo:halo + tl, :] = u_ref[bi, :, s * LANES:(s + 1) * LANES]

    for d in range(2):
        nl = -lam_ref[d]
        half_decay = (0.5 * LRU_C) * (jnp.maximum(nl, 0.0) + jnp.log1p(jnp.exp(-jnp.abs(nl))))
        for bi in range(BATCH):
            cols = []
            for s in range(LRU_SLABS):
                lanes = slice(s * LANES, (s + 1) * LANES)
                x = cb_ref[d][:, lanes]
                for kk in range(LRU_CONV):
                    off = halo + (kk - (LRU_CONV - 1) if d == 0 else (LRU_CONV - 1) - kk)
                    rows = pl.ds(off, tl, stride=1)
                    x = x + cw_ref[d][kk:kk + 1, lanes] * u_sc[d, bi * LRU_SLABS + s, rows, :]
                cols.append(x)
            x = jnp.concatenate(cols, axis=1)
            xb = x.astype(BF16)
            tr = jnp.tanh(jnp.dot(xb, wa_ref[d, 0], preferred_element_type=F32) + ba_ref[d])
            ti = jnp.tanh(jnp.dot(xb, wx_ref[d, 0], preferred_element_type=F32) + bx_ref[d])
            neg_log_a = tr * half_decay + half_decay
            half_x = 0.5 * x
            ix = ti * half_x + half_x
            a = jnp.exp(-neg_log_a)
            var = jnp.tanh(neg_log_a) * (a * a + 1.0)
            b = (var * lax.rsqrt(jnp.maximum(var, F32_TINY))) * ix
            for s in range(LRU_SLABS):
                lanes = slice(s * LANES, (s + 1) * LANES)
                a_sc[d, s, bi * pitch:bi * pitch + tl, :] = a[:, lanes]
                b_sc[d, s, bi * pitch:bi * pitch + tl, :] = b[:, lanes]

    def scan_step(t, carry):
        new = []
        for d in range(2):
            tt = t if d == 0 else tl - 1 - t
            for s in range(LRU_SLABS):
                rows = pl.ds(tt, BATCH, stride=pitch)
                h = a_sc[d, s, rows, :] * carry[d * LRU_SLABS + s] + b_sc[d, s, rows, :]
                y_sc[d, s, rows, :] = h
                new.append(h)
        return tuple(new)

    init = tuple(h_sc[d, s] for d in range(2) for s in range(LRU_SLABS))
    final = lax.fori_loop(0, tl, scan_step, init, unroll=8)
    for d in range(2):
        for s in range(LRU_SLABS):
            h_sc[d, s] = final[d * LRU_SLABS + s]

    for d, o_ref in ((0, hf_ref), (1, hb_ref)):
        for bi in range(BATCH):
            for s in range(LRU_SLABS):
                o_ref[bi, :, s * LANES:(s + 1) * LANES] = (
                    y_sc[d, s, bi * pitch:bi * pitch + tl, :].astype(o_ref.dtype))


def _block_diag_chunks(w):
    per = MXU_DIM // LRU_BLOCK
    w = w.reshape(2, LRU_BLOCKS // per, per, LRU_BLOCK, LRU_BLOCK)
    eye = jnp.eye(per, dtype=w.dtype)
    bd = jnp.einsum('dcpij,pq->dcpiqj', w, eye)
    return bd.reshape(2, LRU_BLOCKS // per, MXU_DIM, MXU_DIM).astype(BF16)


def _lru_scan(u, conv_w, conv_b, gate_a_w, gate_a_b, gate_x_w, gate_x_b, lam):
    def fwd_blk(w, s):
        return (0, s, w)

    def bwd_blk(w, s):
        blk = jnp.where(s < LRU_CTX_BLKS, LRU_CTX_BLKS - 1 - s, LRU_BLKS - 1 - (s - LRU_CTX_BLKS))
        return (0, blk, w)

    blk = (BATCH, LRU_TL, LRU_WC)
    vec = pl.BlockSpec((2, 1, LRU_WC), lambda w, s: (0, 0, w))
    gate = pl.BlockSpec((2, LRU_WC // MXU_DIM, MXU_DIM, MXU_DIM), lambda w, s: (0, w, 0, 0))
    width = u.shape[-1]
    out = jax.ShapeDtypeStruct((BATCH, TOK, width), BF16)
    scan_buf = pltpu.VMEM((2, LRU_SLABS, BATCH * LRU_PITCH, LANES), F32)
    return pl.pallas_call(
        _lru_kernel,
        grid=(width // LRU_WC, LRU_BLKS),
        in_specs=[pl.BlockSpec(blk, fwd_blk), pl.BlockSpec(blk, bwd_blk),
                  pl.BlockSpec((2, LRU_CONV, LRU_WC), lambda w, s: (0, 0, w)), vec,
                  gate, vec, gate, vec, vec],
        out_specs=[pl.BlockSpec(blk, fwd_blk), pl.BlockSpec(blk, bwd_blk)],
        out_shape=[out, out],
        scratch_shapes=[pltpu.VMEM((2, BATCH * LRU_SLABS, LRU_TL + 2 * SUBLANES, LANES), F32),
                        scan_buf, scan_buf, scan_buf,
                        pltpu.VMEM((2, LRU_SLABS, BATCH, LANES), F32)],
        compiler_params=_params("parallel", "arbitrary"),
        name="lru_scan",
    )(u, u, conv_w, conv_b.reshape(2, 1, width),
      _block_diag_chunks(0.5 * gate_a_w), 0.5 * gate_a_b.reshape(2, 1, width),
      _block_diag_chunks(0.5 * gate_x_w), 0.5 * gate_x_b.reshape(2, 1, width),
      lam.reshape(2, 1, width))


def _residual_rows(x_ref, mod_ref, acts, w_ref, o_ref, post=None):
    y = jnp.dot(jnp.concatenate(acts, axis=0).astype(BF16), w_ref[...], preferred_element_type=F32)
    for bb in range(ROW_BATCH):
        out = x_ref[bb] + mod_ref[bb, 0][2:3, :] * _tile_rows(y, bb)
        o_ref[bb] = out if post is None else post(out)


def _out_attn_kernel(ot_ref, g_ref, w_ref, x_ref, mod_ref, fg_ref, o_ref, *, final):
    acts = [ot_ref[bb].astype(F32).T * _silu(g_ref[bb]) for bb in range(ROW_BATCH)]

    def final_norm(x):
        return x * lax.rsqrt(jnp.mean(x * x, axis=-1, keepdims=True) + EPS) * fg_ref[...]

    _residual_rows(x_ref, mod_ref, acts, w_ref, o_ref, final_norm if final else None)


def _out_lru_kernel(hf_ref, hb_ref, g_ref, w_ref, x_ref, mod_ref, o_ref):
    acts = [(hf_ref[bb].astype(F32) + hb_ref[bb].astype(F32)) * _silu(g_ref[bb].astype(F32))
            for bb in range(ROW_BATCH)]
    _residual_rows(x_ref, mod_ref, acts, w_ref, o_ref)


def _conv_layer_kernel(x_ref, xp_ref, xn_ref, mod_ref, g_ref, wu_ref, wb_ref, wc_ref, wg_ref,
                       cw_ref, cb_ref, wo_ref, o_ref, p_sc):
    j = pl.program_id(1)
    halo, cc = SUBLANES, CONV_CHUNK
    rows = ROW_TILE + 2 * halo
    tiles = [_norm_mod(jnp.concatenate([xp_ref[bb], x_ref[bb], xn_ref[bb]], axis=0), g_ref[...],
                       mod_ref[bb, 0]) for bb in range(ROW_BATCH)]
    h = jnp.concatenate(tiles, axis=0).astype(BF16)
    keep_prev = jnp.where((j == 0) | (j == CTX_ROW_TILES), 0.0, 1.0)
    keep_next = jnp.where((j == CTX_ROW_TILES - 1) | (j == ROW_TILES - 1), 0.0, 1.0)
    acc = jnp.zeros((ROW_BATCH * ROW_TILE, D_MODEL), F32)
    for c in range(D_MODEL // cc):
        chans = slice(c * cc, (c + 1) * cc)
        u, bg, cg, g = (jnp.dot(h, w_ref[:, chans], preferred_element_type=F32)
                        for w_ref in (wu_ref, wb_ref, wc_ref, wg_ref))
        p = cg * u
        acts = []
        for bb in range(ROW_BATCH):
            top = bb * rows
            main = slice(top + halo, top + halo + ROW_TILE)
            for s in range(cc // LANES):
                lanes = slice(s * LANES, (s + 1) * LANES)
                p_sc[bb, c, s, 0:halo, :] = p[top:top + halo, lanes] * keep_prev
                p_sc[bb, c, s, halo:halo + ROW_TILE, :] = p[main, lanes]
                p_sc[bb, c, s, halo + ROW_TILE:rows, :] = p[top + halo + ROW_TILE:top + rows, lanes] * keep_next
            before = jnp.concatenate([p_sc[bb, c, s, pl.ds(halo - 1, ROW_TILE, stride=1), :]
                                      for s in range(cc // LANES)], axis=1)
            after = jnp.concatenate([p_sc[bb, c, s, pl.ds(halo + 1, ROW_TILE, stride=1), :]
                                     for s in range(cc // LANES)], axis=1)
            conv = (cw_ref[0:1, chans] * before + cw_ref[1:2, chans] * p[main] + cw_ref[2:3, chans] * after
                    + cb_ref[:, chans])
            acts.append(bg[main] * conv * _silu(g[main]))
        acc = acc + jnp.dot(jnp.concatenate(acts, axis=0).astype(BF16), wo_ref[chans, :],
                            preferred_element_type=F32)
    for bb in range(ROW_BATCH):
        o_ref[bb] = x_ref[bb] + mod_ref[bb, 0][2:3, :] * _tile_rows(acc, bb)


_W_OUT_SPEC = _const_spec((D_MODEL, D_MODEL))


def _out_attn(ot, gate, w_out, xs, mod, final_g):
    final = final_g is not None
    off = CTX_ROW_TILES if final else 0
    fg = (final_g if final else jnp.ones((D_MODEL,), F32)).reshape(1, D_MODEL)
    rows = SEQ if final else TOK
    return pl.pallas_call(
        functools.partial(_out_attn_kernel, final=final),
        grid=(BATCH // ROW_BATCH, ROW_TILES - off),
        in_specs=[pl.BlockSpec((ROW_BATCH, Q_DIM, ROW_TILE), lambda b, j: (b, 0, j)), _row_spec(Q_DIM, off),
                  _W_OUT_SPEC, _row_spec(D_MODEL, off), _mod_spec(off), _const_spec((1, D_MODEL))],
        out_specs=_row_spec(D_MODEL),
        out_shape=jax.ShapeDtypeStruct((BATCH, rows, D_MODEL), F32),
        compiler_params=_params("parallel", "arbitrary"),
        name="out_attn_final" if final else "out_attn",
    )(ot, gate, w_out.astype(BF16), xs, mod, fg)


def _out_lru(hf, hb, gate, w_out, xs, mod):
    return pl.pallas_call(
        _out_lru_kernel,
        grid=_ROW_GRID,
        in_specs=[_row_spec(D_MODEL)] * 3 + [_W_OUT_SPEC, _row_spec(D_MODEL), _mod_spec()],
        out_specs=_row_spec(D_MODEL),
        out_shape=jax.ShapeDtypeStruct((BATCH, TOK, D_MODEL), F32),
        compiler_params=_params("parallel", "arbitrary"),
        name="out_lru",
    )(hf, hb, gate, w_out.astype(BF16), xs, mod)


def _conv_layer(xs, mod, norm_g, w_in, conv_w, conv_b, w_out):
    per = ROW_TILE // SUBLANES
    n8 = TOK // SUBLANES
    halo = lambda idx: pl.BlockSpec((ROW_BATCH, SUBLANES, D_MODEL), idx)
    prev = halo(lambda b, j: (b, jnp.maximum(j * per - 1, 0), 0))
    nxt = halo(lambda b, j: (b, jnp.minimum((j + 1) * per, n8 - 1), 0))
    wi = w_in.astype(BF16)
    w_blocks = [pl.BlockSpec((D_MODEL, D_MODEL), functools.partial(lambda i, b, j: (0, i), i)) for i in range(4)]
    return pl.pallas_call(
        _conv_layer_kernel,
        grid=_ROW_GRID,
        in_specs=[_row_spec(D_MODEL), prev, nxt, _mod_spec(), _const_spec((1, D_MODEL))] + w_blocks
                 + [_const_spec((CONV_K, D_MODEL)), _const_spec((1, D_MODEL)), _W_OUT_SPEC],
        out_specs=_row_spec(D_MODEL),
        out_shape=jax.ShapeDtypeStruct((BATCH, TOK, D_MODEL), F32),
        scratch_shapes=[pltpu.VMEM((ROW_BATCH, D_MODEL // CONV_CHUNK, CONV_CHUNK // LANES,
                                    ROW_TILE + 2 * SUBLANES, LANES), F32)],
        compiler_params=_params("parallel", "arbitrary"),
        name="conv_layer",
    )(xs, xs, xs, mod, norm_g.reshape(1, D_MODEL), wi, wi, wi, wi, conv_w, conv_b.reshape(1, D_MODEL),
      w_out.astype(BF16))


def _rope_tables():
    quarter = HEAD_DIM // 4
    pos = jnp.arange(SEQ)
    row = (pos // GRID_W).astype(F32)
    col = (pos % GRID_W).astype(F32)
    half = HEAD_DIM // 2
    inv = 1.0 / (ROPE_BASE ** (jnp.arange(0, half, 2, dtype=F32) / half))
    zero = jnp.zeros((SEQ, quarter), F32)
    parts_c, parts_a, parts_b = [], [], []
    for axis_pos in (row, col):
        ang = axis_pos[:, None] * inv
        c, s = jnp.cos(ang), jnp.sin(ang)
        parts_c += [c, c]
        parts_a += [-s, zero]
        parts_b += [zero, s]

    def table(parts, ctx_value):
        head = jnp.concatenate(parts, axis=1)
        head = jnp.concatenate([jnp.full((CTX_LEN, HEAD_DIM), ctx_value, F32), head], axis=0)
        return jnp.tile(head, (1, LANES // HEAD_DIM))

    return table(parts_c, 1.0), table(parts_a, 0.0), table(parts_b, 0.0)


def kernel(x, c, ctx, c_ctx, l0_norm_g, l0_mod_w, l0_mod_b, l0_w_in, l0_w_out, l0_sink, l1_norm_g, l1_mod_w, l1_mod_b, l1_w_in, l1_w_out, l1_conv_w, l1_conv_b, l1_gate_a_w, l1_gate_a_b, l1_gate_x_w, l1_gate_x_b, l1_lambda, l2_norm_g, l2_mod_w, l2_mod_b, l2_w_in, l2_w_out, l2_conv_w, l2_conv_b, l3_norm_g, l3_mod_w, l3_mod_b, l3_w_in, l3_w_out, l3_sink, final_norm_g):
    xs = jnp.concatenate([ctx, x], axis=1)
    cc = jnp.concatenate([c, c_ctx[None], jnp.zeros((2 * SUBLANES - BATCH - 1, D_MODEL), F32)], axis=0)
    rope_tabs = _rope_tables()

    mod = _modulation(cc, l0_mod_w, l0_mod_b)
    qt, k, vt, gate = _in_attn(xs, mod, l0_norm_g, l0_w_in, rope_tabs)
    ot = _attention(qt, k, vt, l0_sink, need_ctx=True)
    xs = _out_attn(ot, gate, l0_w_out, xs, mod, None)

    mod = _modulation(cc, l1_mod_w, l1_mod_b)
    u, gate = _in_plain(xs, mod, l1_norm_g, l1_w_in, (F32, BF16), "in_lru")
    hf, hb = _lru_scan(u, l1_conv_w, l1_conv_b, l1_gate_a_w, l1_gate_a_b, l1_gate_x_w, l1_gate_x_b,
                       l1_lambda)
    xs = _out_lru(hf, hb, gate, l1_w_out, xs, mod)

    mod = _modulation(cc, l2_mod_w, l2_mod_b)
    xs = _conv_layer(xs, mod, l2_norm_g, l2_w_in, l2_conv_w, l2_conv_b, l2_w_out)

    mod = _modulation(cc, l3_mod_w, l3_mod_b)
    qt, k, vt, gate = _in_attn(xs, mod, l3_norm_g, l3_w_in, rope_tabs)
    ot = _attention(qt, k, vt, l3_sink, need_ctx=False)
    return _out_attn(ot, gate, l3_w_out, xs, mod, final_norm_g)
```

```python
import functools
import math

import jax
import jax.numpy as jnp
from jax import lax
from jax.experimental import pallas as pl
from jax.experimental.pallas import tpu as pltpu

D_MODEL = 1024
BATCH = 8
SEQ = 2048
CTX_LEN = 256
TOK = CTX_LEN + SEQ
GRID_W = 64
EPS = 1e-6
NEG_INF = -1e30
N_HEADS = 16
N_KV_HEADS = 4
N_GROUPS = N_HEADS // N_KV_HEADS
HEAD_DIM = 64
Q_DIM = N_HEADS * HEAD_DIM
KV_DIM = N_KV_HEADS * HEAD_DIM
WINDOW = 128
ROPE_BASE = 10000.0
LRU_BLOCKS = 16
LRU_BLOCK = 64
LRU_C = 8.0
LRU_CONV = 4
CONV_K = 3

LANES = 128
SUBLANES = 8
BF16_SUBLANES = 16
MXU_DIM = 256
VMEM_LIMIT_BYTES = 56 * 1024 * 1024

ROW_TILE = 256
ROW_BATCH = 2
CTX_ROW_TILES = CTX_LEN // ROW_TILE
ROW_TILES = TOK // ROW_TILE
Q_TILE = WINDOW
CTX_Q_TILES = CTX_LEN // Q_TILE
Q_TILES = TOK // Q_TILE
BAND = 3 * WINDOW
MAX_ROWS = 32
CONV_CHUNK = MXU_DIM

LRU_TL = 256
LRU_WC = MXU_DIM
LRU_PITCH = LRU_TL + SUBLANES
LRU_SLABS = LRU_WC // LANES
LRU_CTX_BLKS = CTX_LEN // LRU_TL
LRU_BLKS = TOK // LRU_TL

F32 = jnp.float32
BF16 = jnp.bfloat16
F32_TINY = float(jnp.finfo(jnp.float32).tiny)
LOG2E = math.log2(math.e)


def _params(*sem):
    return pltpu.CompilerParams(dimension_semantics=sem, vmem_limit_bytes=VMEM_LIMIT_BYTES)


def _sigmoid(x):
    return 0.5 * jnp.tanh(0.5 * x) + 0.5


def _silu(x):
    return x * _sigmoid(x)


def _mod_kernel(c_ref, b_ref, *refs):
    w_refs, o_ref = refs[:-1], refs[-1]
    layer = pl.program_id(0) // 3
    s = _silu(c_ref[...])
    s_hi = s.astype(BF16)
    s_lo = (s - s_hi.astype(F32)).astype(BF16)
    for l, w_ref in enumerate(w_refs):
        @pl.when(layer == l)
        def _layer(w_ref=w_ref):
            w = w_ref[...]
            w_hi = w.astype(BF16)
            w_lo = (w - w_hi.astype(F32)).astype(BF16)
            acc = jnp.dot(s_hi, w_hi, preferred_element_type=F32)
            acc += jnp.dot(s_hi, w_lo, preferred_element_type=F32)
            acc += jnp.dot(s_lo, w_hi, preferred_element_type=F32)
            m = acc + b_ref[0]
            o_ref[0, :, 0, :] = jnp.broadcast_to(m[BATCH:BATCH + 1], (BATCH, D_MODEL))
            o_ref[0, :, 1, :] = m[:BATCH]


def _modulation(c, c_ctx, mod_ws, mod_bs):
    n = len(mod_ws)
    cc = jnp.concatenate([c, c_ctx[None], jnp.zeros((2 * SUBLANES - BATCH - 1, D_MODEL), F32)], axis=0)
    rows = cc.shape[0]
    w_spec = lambda l: pl.BlockSpec((D_MODEL, D_MODEL), lambda s: (0, jnp.clip(s - 3 * l, 0, 2)))
    m = pl.pallas_call(
        _mod_kernel,
        grid=(3 * n,),
        in_specs=[pl.BlockSpec((rows, D_MODEL), lambda s: (0, 0)),
                  pl.BlockSpec((1, 1, D_MODEL), lambda s: (s // 3, 0, s % 3))] + [w_spec(l) for l in range(n)],
        out_specs=pl.BlockSpec((1, BATCH, 2, D_MODEL), lambda s: (s // 3, 0, 0, s % 3)),
        out_shape=jax.ShapeDtypeStruct((n, BATCH, 2, 3 * D_MODEL), F32),
        compiler_params=_params("arbitrary"),
        name="modulation",
    )(cc, jnp.stack(mod_bs).reshape(n, 1, 3 * D_MODEL), *mod_ws)
    return m.reshape(n, BATCH, 2, 3, D_MODEL)


def _mod_spec(tile_offset=0):
    def idx(b, j):
        return (b, jnp.where(j + tile_offset < CTX_ROW_TILES, 0, 1), 0, 0)
    return pl.BlockSpec((ROW_BATCH, 1, 3, D_MODEL), idx)


def _row_spec(width, tile_offset=0):
    return pl.BlockSpec((ROW_BATCH, ROW_TILE, width), lambda b, j: (b, j + tile_offset, 0))


def _const_spec(shape, index=None):
    index = index or (lambda b, j: (0,) * len(shape))
    return pl.BlockSpec(shape, index, pipeline_mode=pl.Buffered(1))


def _stream_specs(split):
    if not split:
        return [_row_spec(D_MODEL)]
    blk = (ROW_BATCH, ROW_TILE, D_MODEL)
    return [pl.BlockSpec(blk, lambda b, j: (b, 0, 0)),
            pl.BlockSpec(blk, lambda b, j: (b, jnp.maximum(j - CTX_ROW_TILES, 0), 0))]


def _stream_tile(x_refs, bb):
    if len(x_refs) == 1:
        return x_refs[0][bb]
    ctx_ref, lat_ref = x_refs
    return jnp.where(pl.program_id(1) < CTX_ROW_TILES, ctx_ref[bb], lat_ref[bb])


def _cast_once(w_ref, wb_sc):
    @pl.when((pl.program_id(0) == 0) & (pl.program_id(1) == 0))
    def _cast():
        wb_sc[...] = w_ref[...].astype(BF16)


_ROW_GRID = (BATCH // ROW_BATCH, ROW_TILES)


def _norm_mod(x, g, mod):
    y = x * lax.rsqrt(jnp.mean(x * x, axis=-1, keepdims=True) + EPS) * g
    return y * (1.0 + mod[1:2, :]) + mod[0:1, :]


def _norm_mod_rows(x_refs, g_ref, mod_ref):
    tiles = [_norm_mod(_stream_tile(x_refs, bb), g_ref[...], mod_ref[bb, 0]) for bb in range(ROW_BATCH)]
    return jnp.concatenate(tiles, axis=0).astype(BF16)


def _tile_rows(t, bb):
    return t[bb * ROW_TILE:(bb + 1) * ROW_TILE]


def _in_attn_kernel(*refs, n_x):
    x_refs = refs[:n_x]
    mod_ref, g_ref, w_ref, rc_ref, ra_ref, rb_ref, qt_ref, k_ref, vt_ref, gate_ref, wb_sc = refs[n_x:]
    _cast_once(w_ref, wb_sc)
    z_all = jnp.dot(_norm_mod_rows(x_refs, g_ref, mod_ref), wb_sc[...], preferred_element_type=F32)
    rc, ra, rb = rc_ref[...], ra_ref[...], rb_ref[...]

    def rope(t):
        reps = t.shape[-1] // LANES
        n = t.shape[-1]
        return (t * jnp.tile(rc, (1, reps))
                + pltpu.roll(t, n - HEAD_DIM // 4, 1) * jnp.tile(ra, (1, reps))
                + pltpu.roll(t, HEAD_DIM // 4, 1) * jnp.tile(rb, (1, reps)))

    for bb in range(ROW_BATCH):
        z = _tile_rows(z_all, bb)
        qt_ref[bb] = (rope(z[:, :Q_DIM]) * (LOG2E * HEAD_DIM ** -0.5)).T.astype(BF16)
        k = rope(z[:, Q_DIM:Q_DIM + KV_DIM]).astype(BF16)
        for kh in range(N_KV_HEADS):
            k_ref[bb, kh] = k[:, kh * HEAD_DIM:(kh + 1) * HEAD_DIM]
        vt_ref[bb] = z[:, Q_DIM + KV_DIM:Q_DIM + 2 * KV_DIM].T.astype(BF16)
        gate_ref[bb] = z[:, Q_DIM + 2 * KV_DIM:]


def _in_attn(streams, mod, norm_g, w_in, rope_tabs):
    n = w_in.shape[1]
    col = lambda height: pl.BlockSpec((ROW_BATCH, height, ROW_TILE), lambda b, j: (b, 0, j))
    tab = pl.BlockSpec((ROW_TILE, LANES), lambda b, j: (j, 0))
    return pl.pallas_call(
        functools.partial(_in_attn_kernel, n_x=len(streams)),
        grid=_ROW_GRID,
        in_specs=_stream_specs(len(streams) > 1)
                 + [_mod_spec(), _const_spec((1, D_MODEL)), _const_spec((D_MODEL, n)), tab, tab, tab],
        out_specs=[col(Q_DIM),
                   pl.BlockSpec((ROW_BATCH, N_KV_HEADS, ROW_TILE, HEAD_DIM), lambda b, j: (b, 0, j, 0)),
                   col(KV_DIM), _row_spec(Q_DIM)],
        out_shape=[jax.ShapeDtypeStruct((BATCH, Q_DIM, TOK), BF16),
                   jax.ShapeDtypeStruct((BATCH, N_KV_HEADS, TOK, HEAD_DIM), BF16),
                   jax.ShapeDtypeStruct((BATCH, KV_DIM, TOK), BF16),
                   jax.ShapeDtypeStruct((BATCH, TOK, Q_DIM), F32)],
        scratch_shapes=[pltpu.VMEM((D_MODEL, n), BF16)],
        compiler_params=_params("arbitrary", "arbitrary"),
        name="in_attn",
    )(*streams, mod, norm_g.reshape(1, D_MODEL), w_in, *rope_tabs)


def _in_plain_kernel(x_ref, mod_ref, g_ref, w_ref, *refs):
    o_refs, wb_sc = refs[:-1], refs[-1]
    _cast_once(w_ref, wb_sc)
    z = jnp.dot(_norm_mod_rows((x_ref,), g_ref, mod_ref), wb_sc[...], preferred_element_type=F32)
    width = z.shape[-1] // len(o_refs)
    for i, o_ref in enumerate(o_refs):
        for bb in range(ROW_BATCH):
            o_ref[bb] = _tile_rows(z, bb)[:, i * width:(i + 1) * width].astype(o_ref.dtype)


def _in_plain(xs, mod, norm_g, w_in, dtypes, name):
    n = w_in.shape[1]
    width = n // len(dtypes)
    return pl.pallas_call(
        _in_plain_kernel,
        grid=_ROW_GRID,
        in_specs=[_row_spec(D_MODEL), _mod_spec(), _const_spec((1, D_MODEL)), _const_spec((D_MODEL, n))],
        out_specs=[_row_spec(width)] * len(dtypes),
        out_shape=[jax.ShapeDtypeStruct((BATCH, TOK, width), dt) for dt in dtypes],
        scratch_shapes=[pltpu.VMEM((D_MODEL, n), BF16)],
        compiler_params=_params("arbitrary", "arbitrary"),
        name=name,
    )(xs, mod, norm_g.reshape(1, D_MODEL), w_in)


def _attn_kernel(sink_ref, qt_ref, k_ref, vt_ref, ot_ref, s_sc, bias_sc, *, first_tile):
    lanes = N_GROUPS * Q_TILE
    ones = jnp.ones((BF16_SUBLANES, WINDOW), BF16)

    key = lax.broadcasted_iota(jnp.int32, (WINDOW, lanes), 0)
    query = lax.broadcasted_iota(jnp.int32, (WINDOW, lanes), 1) & (Q_TILE - 1)
    bias_sc[0] = jnp.where(key >= query, 0.0, NEG_INF)
    bias_sc[1] = jnp.where(key <= query, 0.0, NEG_INF)

    def aligned(start):
        return start if isinstance(start, int) else pl.multiple_of(start, WINDOW)

    def heads_of(kh):
        return [kh * N_GROUPS + g for g in range(N_GROUPS)]

    def sink_row(kh):
        return jnp.concatenate([jnp.full((1, Q_TILE), sink_ref[h] * LOG2E, F32) for h in heads_of(kh)], axis=1)

    def score_steps(j, kh, chunks, state):
        cols = pl.ds(aligned(j * Q_TILE), Q_TILE)
        qt = jnp.concatenate([qt_ref[0, h * HEAD_DIM:(h + 1) * HEAD_DIM, cols] for h in heads_of(kh)], axis=1)
        mx = jnp.broadcast_to(sink_row(kh), (MAX_ROWS, lanes))
        for c, (ks, bias) in enumerate(chunks):
            s = jnp.dot(k_ref[0, kh, pl.ds(ks, WINDOW), :], qt, preferred_element_type=F32)
            if bias is not None:
                s = s + bias_sc[bias]
            s_sc[kh, c * WINDOW:(c + 1) * WINDOW, :] = s
            mx = jnp.maximum(mx, s.reshape(WINDOW // MAX_ROWS, MAX_ROWS, lanes).max(axis=0))
            yield
        state[kh] = mx.max(axis=0, keepdims=True)

    def value_steps(j, kh, chunks, state):
        m = state[kh]
        acc = jnp.zeros((HEAD_DIM + BF16_SUBLANES, lanes), F32)
        for c, (ks, _) in enumerate(chunks):
            p = jnp.exp2(s_sc[kh, c * WINDOW:(c + 1) * WINDOW, :] - m).astype(BF16)
            vt = vt_ref[0, kh * HEAD_DIM:(kh + 1) * HEAD_DIM, pl.ds(ks, WINDOW)]
            acc = acc + jnp.dot(jnp.concatenate([vt, ones], axis=0), p, preferred_element_type=F32)
            yield
        denom = acc[HEAD_DIM:HEAD_DIM + 1, :] + jnp.exp2(sink_row(kh) - m)
        o = acc[:HEAD_DIM, :] / denom
        cols = pl.ds(aligned((j - first_tile) * Q_TILE), Q_TILE)
        for g, h in enumerate(heads_of(kh)):
            ot_ref[0, h * HEAD_DIM:(h + 1) * HEAD_DIM, cols] = o[:, g * Q_TILE:(g + 1) * Q_TILE].astype(BF16)

    def tile(j, chunks):
        state = {}
        pending = None
        for kh in range(N_KV_HEADS + 1):
            stages = []
            if kh < N_KV_HEADS:
                stages.append(score_steps(j, kh, chunks, state))
            if pending is not None:
                stages.append(pending)
            while stages:
                stages = [st for st in stages if next(st, StopIteration) is not StopIteration]
            pending = value_steps(j, kh, chunks, state) if kh < N_KV_HEADS else None

    ctx_chunks = [(c * WINDOW, None) for c in range(CTX_LEN // WINDOW)]

    def band_chunks(j, has_prev, has_next):
        own = j * Q_TILE
        chunks = [(aligned(own - WINDOW), 0)] if has_prev else []
        chunks.append((aligned(own), None))
        if has_next:
            chunks.append((aligned(own + WINDOW), 1))
        return chunks + ctx_chunks

    if first_tile == 0:
        for j in range(CTX_Q_TILES):
            tile(j, ctx_chunks)
    tile(CTX_Q_TILES, band_chunks(CTX_Q_TILES, False, True))

    def middle(j, carry):
        tile(j, band_chunks(j, True, True))
        return carry

    lax.fori_loop(CTX_Q_TILES + 1, Q_TILES - 1, middle, 0)
    tile(Q_TILES - 1, band_chunks(Q_TILES - 1, True, False))


def _attention(qt, k, vt, sink, need_ctx):
    first_tile = 0 if need_ctx else CTX_Q_TILES
    n_keys = BAND + CTX_LEN
    out_cols = TOK - first_tile * Q_TILE
    return pl.pallas_call(
        functools.partial(_attn_kernel, first_tile=first_tile),
        grid=(BATCH,),
        in_specs=[pl.BlockSpec(memory_space=pltpu.SMEM),
                  pl.BlockSpec((1, Q_DIM, TOK), lambda b: (b, 0, 0)),
                  pl.BlockSpec((1, N_KV_HEADS, TOK, HEAD_DIM), lambda b: (b, 0, 0, 0)),
                  pl.BlockSpec((1, KV_DIM, TOK), lambda b: (b, 0, 0))],
        out_specs=pl.BlockSpec((1, Q_DIM, out_cols), lambda b: (b, 0, 0)),
        out_shape=jax.ShapeDtypeStruct((BATCH, Q_DIM, out_cols), BF16),
        scratch_shapes=[pltpu.VMEM((N_KV_HEADS, n_keys, N_GROUPS * Q_TILE), F32),
                        pltpu.VMEM((2, WINDOW, N_GROUPS * Q_TILE), F32)],
        compiler_params=_params("arbitrary"),
        name="attention",
    )(sink, qt, k, vt)


def _lru_kernel(uf_ref, ub_ref, cw_ref, cb_ref, wa_ref, ba_ref, wx_ref, bx_ref, lam_ref,
                hf_ref, hb_ref, u_sc, a_sc, b_sc, y_sc, h_sc):
    step = pl.program_id(1)
    tl, pitch, halo = LRU_TL, LRU_PITCH, SUBLANES
    segment_start = (step == 0) | (step == LRU_CTX_BLKS)

    @pl.when(segment_start)
    def _zero_halo():
        u_sc[0, :, 0:halo, :] = jnp.zeros((BATCH * LRU_SLABS, halo, LANES), F32)
        u_sc[1, :, halo + tl:, :] = jnp.zeros((BATCH * LRU_SLABS, halo, LANES), F32)

    @pl.when(jnp.logical_not(segment_start))
    def _carry_halo():
        u_sc[0, :, 0:halo, :] = u_sc[0, :, tl:tl + halo, :]
        u_sc[1, :, halo + tl:, :] = u_sc[1, :, halo:2 * halo, :]

    @pl.when(step == 0)
    def _sequence_start():
        h_sc[...] = jnp.zeros_like(h_sc)

    for d, u_ref in ((0, uf_ref), (1, ub_ref)):
        for bi in range(BATCH):
            for s in range(LRU_SLABS):
                u_sc[d, bi * LRU_SLABS + s, halo:halo + tl, :] = u_ref[bi, :, s * LANES:(s + 1) * LANES]

    for d in range(2):
        nl = -lam_ref[d]
        half_decay = (0.5 * LRU_C) * (jnp.maximum(nl, 0.0) + jnp.log1p(jnp.exp(-jnp.abs(nl))))
        for bi in range(BATCH):
            cols = []
            for s in range(LRU_SLABS):
                lanes = slice(s * LANES, (s + 1) * LANES)
                x = cb_ref[d][:, lanes]
                for kk in range(LRU_CONV):
                    off = halo + (kk - (LRU_CONV - 1) if d == 0 else (LRU_CONV - 1) - kk)
                    rows = pl.ds(off, tl, stride=1)
                    x = x + cw_ref[d][kk:kk + 1, lanes] * u_sc[d, bi * LRU_SLABS + s, rows, :]
                cols.append(x)
            x = jnp.concatenate(cols, axis=1)
            xb = x.astype(BF16)
            tr = jnp.tanh(jnp.dot(xb, wa_ref[d, 0], preferred_element_type=F32) + ba_ref[d])
            ti = jnp.tanh(jnp.dot(xb, wx_ref[d, 0], preferred_element_type=F32) + bx_ref[d])
            neg_log_a = tr * half_decay + half_decay
            half_x = 0.5 * x
            ix = ti * half_x + half_x
            a = jnp.exp(-neg_log_a)
            var = jnp.tanh(neg_log_a) * (a * a + 1.0)
            b = (var * lax.rsqrt(jnp.maximum(var, F32_TINY))) * ix
            for s in range(LRU_SLABS):
                lanes = slice(s * LANES, (s + 1) * LANES)
                a_sc[d, s, bi * pitch:bi * pitch + tl, :] = a[:, lanes]
                b_sc[d, s, bi * pitch:bi * pitch + tl, :] = b[:, lanes]

    def scan_step(t, carry):
        new = []
        for d in range(2):
            tt = t if d == 0 else tl - 1 - t
            for s in range(LRU_SLABS):
                rows = pl.ds(tt, BATCH, stride=pitch)
                h = a_sc[d, s, rows, :] * carry[d * LRU_SLABS + s] + b_sc[d, s, rows, :]
                y_sc[d, s, rows, :] = h
                new.append(h)
        return tuple(new)

    init = tuple(h_sc[d, s] for d in range(2) for s in range(LRU_SLABS))
    final = lax.fori_loop(0, tl, scan_step, init, unroll=8)
    for d in range(2):
        for s in range(LRU_SLABS):
            h_sc[d, s] = final[d * LRU_SLABS + s]

    for d, o_ref in ((0, hf_ref), (1, hb_ref)):
        for bi in range(BATCH):
            for s in range(LRU_SLABS):
                o_ref[bi, :, s * LANES:(s + 1) * LANES] = (
                    y_sc[d, s, bi * pitch:bi * pitch + tl, :].astype(o_ref.dtype))


def _block_diag_chunks(w):
    per = MXU_DIM // LRU_BLOCK
    w = w.reshape(2, LRU_BLOCKS // per, per, LRU_BLOCK, LRU_BLOCK)
    eye = jnp.eye(per, dtype=w.dtype)
    bd = jnp.einsum('dcpij,pq->dcpiqj', w, eye)
    return bd.reshape(2, LRU_BLOCKS // per, MXU_DIM, MXU_DIM).astype(BF16)


def _lru_scan(u, conv_w, conv_b, gate_a_w, gate_a_b, gate_x_w, gate_x_b, lam):
    def fwd_blk(w, s):
        return (0, s, w)

    def bwd_blk(w, s):
        blk = jnp.where(s < LRU_CTX_BLKS, LRU_CTX_BLKS - 1 - s, LRU_BLKS - 1 - (s - LRU_CTX_BLKS))
        return (0, blk, w)

    blk = (BATCH, LRU_TL, LRU_WC)
    vec = pl.BlockSpec((2, 1, LRU_WC), lambda w, s: (0, 0, w))
    gate = pl.BlockSpec((2, LRU_WC // MXU_DIM, MXU_DIM, MXU_DIM), lambda w, s: (0, w, 0, 0))
    width = u.shape[-1]
    out = jax.ShapeDtypeStruct((BATCH, TOK, width), BF16)
    scan_buf = pltpu.VMEM((2, LRU_SLABS, BATCH * LRU_PITCH, LANES), F32)
    return pl.pallas_call(
        _lru_kernel,
        grid=(width // LRU_WC, LRU_BLKS),
        in_specs=[pl.BlockSpec(blk, fwd_blk), pl.BlockSpec(blk, bwd_blk),
                  pl.BlockSpec((2, LRU_CONV, LRU_WC), lambda w, s: (0, 0, w)), vec,
                  gate, vec, gate, vec, vec],
        out_specs=[pl.BlockSpec(blk, fwd_blk), pl.BlockSpec(blk, bwd_blk)],
        out_shape=[out, out],
        scratch_shapes=[pltpu.VMEM((2, BATCH * LRU_SLABS, LRU_TL + 2 * SUBLANES, LANES), F32),
                        scan_buf, scan_buf, scan_buf,
                        pltpu.VMEM((2, LRU_SLABS, BATCH, LANES), F32)],
        compiler_params=_params("parallel", "arbitrary"),
        name="lru_scan",
    )(u, u, conv_w, conv_b.reshape(2, 1, width),
      _block_diag_chunks(0.5 * gate_a_w), 0.5 * gate_a_b.reshape(2, 1, width),
      _block_diag_chunks(0.5 * gate_x_w), 0.5 * gate_x_b.reshape(2, 1, width),
      lam.reshape(2, 1, width))


def _residual_rows(x_refs, mod_ref, acts, w_ref, wb_sc, o_ref, post=None):
    _cast_once(w_ref, wb_sc)
    y = jnp.dot(jnp.concatenate(acts, axis=0).astype(BF16), wb_sc[...], preferred_element_type=F32)
    for bb in range(ROW_BATCH):
        out = _stream_tile(x_refs, bb) + mod_ref[bb, 0][2:3, :] * _tile_rows(y, bb)
        o_ref[bb] = out if post is None else post(out)


def _out_attn_kernel(ot_ref, g_ref, w_ref, *refs, n_x, final):
    x_refs = refs[:n_x]
    mod_ref, fg_ref, o_ref, wb_sc = refs[n_x:]
    acts = [ot_ref[bb].astype(F32).T * _silu(g_ref[bb]) for bb in range(ROW_BATCH)]

    def final_norm(x):
        return x * lax.rsqrt(jnp.mean(x * x, axis=-1, keepdims=True) + EPS) * fg_ref[...]

    _residual_rows(x_refs, mod_ref, acts, w_ref, wb_sc, o_ref, final_norm if final else None)


def _out_lru_kernel(hf_ref, hb_ref, g_ref, w_ref, x_ref, mod_ref, o_ref, wb_sc):
    acts = [(hf_ref[bb].astype(F32) + hb_ref[bb].astype(F32)) * _silu(g_ref[bb].astype(F32))
            for bb in range(ROW_BATCH)]
    _residual_rows((x_ref,), mod_ref, acts, w_ref, wb_sc, o_ref)


def _conv_layer_kernel(x_ref, xp_ref, xn_ref, mod_ref, g_ref, wu_ref, wb_ref, wc_ref, wg_ref,
                       cw_ref, cb_ref, wo_ref, o_ref, p_sc, wi_sc, wo_sc):
    j = pl.program_id(1)
    halo, cc = SUBLANES, CONV_CHUNK
    rows = ROW_TILE + 2 * halo
    for i, w_ref in enumerate((wu_ref, wb_ref, wc_ref, wg_ref)):
        _cast_once(w_ref, wi_sc.at[i])
    _cast_once(wo_ref, wo_sc)
    tiles = [_norm_mod(jnp.concatenate([xp_ref[bb], x_ref[bb], xn_ref[bb]], axis=0), g_ref[...],
                       mod_ref[bb, 0]) for bb in range(ROW_BATCH)]
    h = jnp.concatenate(tiles, axis=0).astype(BF16)
    keep_prev = jnp.where((j == 0) | (j == CTX_ROW_TILES), 0.0, 1.0)
    keep_next = jnp.where((j == CTX_ROW_TILES - 1) | (j == ROW_TILES - 1), 0.0, 1.0)
    acc = jnp.zeros((ROW_BATCH * ROW_TILE, D_MODEL), F32)
    for c in range(D_MODEL // cc):
        chans = slice(c * cc, (c + 1) * cc)
        u, bg, cg, g = (jnp.dot(h, wi_sc[i, :, chans], preferred_element_type=F32) for i in range(4))
        p = cg * u
        acts = []
        for bb in range(ROW_BATCH):
            top = bb * rows
            main = slice(top + halo, top + halo + ROW_TILE)
            for s in range(cc // LANES):
                lanes = slice(s * LANES, (s + 1) * LANES)
                p_sc[bb, c, s, 0:halo, :] = p[top:top + halo, lanes] * keep_prev
                p_sc[bb, c, s, halo:halo + ROW_TILE, :] = p[main, lanes]
                p_sc[bb, c, s, halo + ROW_TILE:rows, :] = p[top + halo + ROW_TILE:top + rows, lanes] * keep_next
            before = jnp.concatenate([p_sc[bb, c, s, pl.ds(halo - 1, ROW_TILE, stride=1), :]
                                      for s in range(cc // LANES)], axis=1)
            after = jnp.concatenate([p_sc[bb, c, s, pl.ds(halo + 1, ROW_TILE, stride=1), :]
                                     for s in range(cc // LANES)], axis=1)
            conv = (cw_ref[0:1, chans] * before + cw_ref[1:2, chans] * p[main] + cw_ref[2:3, chans] * after
                    + cb_ref[:, chans])
            acts.append(bg[main] * conv * _silu(g[main]))
        acc = acc + jnp.dot(jnp.concatenate(acts, axis=0).astype(BF16), wo_sc[chans, :],
                            preferred_element_type=F32)
    for bb in range(ROW_BATCH):
        o_ref[bb] = x_ref[bb] + mod_ref[bb, 0][2:3, :] * _tile_rows(acc, bb)


_W_OUT_SPEC = _const_spec((D_MODEL, D_MODEL))


def _out_attn(ot, gate, w_out, streams, mod, final_g):
    final = final_g is not None
    off = CTX_ROW_TILES if final else 0
    fg = (final_g if final else jnp.ones((D_MODEL,), F32)).reshape(1, D_MODEL)
    rows = SEQ if final else TOK
    x_specs = [_row_spec(D_MODEL, off)] if final else _stream_specs(len(streams) > 1)
    return pl.pallas_call(
        functools.partial(_out_attn_kernel, n_x=len(streams), final=final),
        grid=(BATCH // ROW_BATCH, ROW_TILES - off),
        in_specs=[pl.BlockSpec((ROW_BATCH, Q_DIM, ROW_TILE), lambda b, j: (b, 0, j)), _row_spec(Q_DIM, off),
                  _W_OUT_SPEC] + x_specs + [_mod_spec(off), _const_spec((1, D_MODEL))],
        out_specs=_row_spec(D_MODEL),
        out_shape=jax.ShapeDtypeStruct((BATCH, rows, D_MODEL), F32),
        scratch_shapes=[pltpu.VMEM((D_MODEL, D_MODEL), BF16)],
        compiler_params=_params("arbitrary", "arbitrary"),
        name="out_attn_final" if final else "out_attn",
    )(ot, gate, w_out, *streams, mod, fg)


def _out_lru(hf, hb, gate, w_out, xs, mod):
    return pl.pallas_call(
        _out_lru_kernel,
        grid=_ROW_GRID,
        in_specs=[_row_spec(D_MODEL)] * 3 + [_W_OUT_SPEC, _row_spec(D_MODEL), _mod_spec()],
        out_specs=_row_spec(D_MODEL),
        out_shape=jax.ShapeDtypeStruct((BATCH, TOK, D_MODEL), F32),
        scratch_shapes=[pltpu.VMEM((D_MODEL, D_MODEL), BF16)],
        compiler_params=_params("arbitrary", "arbitrary"),
        name="out_lru",
    )(hf, hb, gate, w_out, xs, mod)


def _conv_layer(xs, mod, norm_g, w_in, conv_w, conv_b, w_out):
    per = ROW_TILE // SUBLANES
    n8 = TOK // SUBLANES
    halo = lambda idx: pl.BlockSpec((ROW_BATCH, SUBLANES, D_MODEL), idx)
    prev = halo(lambda b, j: (b, jnp.maximum(j * per - 1, 0), 0))
    nxt = halo(lambda b, j: (b, jnp.minimum((j + 1) * per, n8 - 1), 0))
    w_blocks = [_const_spec((D_MODEL, D_MODEL), functools.partial(lambda i, b, j: (0, i), i)) for i in range(4)]
    return pl.pallas_call(
        _conv_layer_kernel,
        grid=_ROW_GRID,
        in_specs=[_row_spec(D_MODEL), prev, nxt, _mod_spec(), _const_spec((1, D_MODEL))] + w_blocks
                 + [_const_spec((CONV_K, D_MODEL)), _const_spec((1, D_MODEL)), _W_OUT_SPEC],
        out_specs=_row_spec(D_MODEL),
        out_shape=jax.ShapeDtypeStruct((BATCH, TOK, D_MODEL), F32),
        scratch_shapes=[pltpu.VMEM((ROW_BATCH, D_MODEL // CONV_CHUNK, CONV_CHUNK // LANES,
                                    ROW_TILE + 2 * SUBLANES, LANES), F32),
                        pltpu.VMEM((4, D_MODEL, D_MODEL), BF16), pltpu.VMEM((D_MODEL, D_MODEL), BF16)],
        compiler_params=_params("arbitrary", "arbitrary"),
        name="conv_layer",
    )(xs, xs, xs, mod, norm_g.reshape(1, D_MODEL), w_in, w_in, w_in, w_in, conv_w,
      conv_b.reshape(1, D_MODEL), w_out)


def _rope_tables():
    quarter = HEAD_DIM // 4
    pos = jnp.arange(SEQ)
    row = (pos // GRID_W).astype(F32)
    col = (pos % GRID_W).astype(F32)
    half = HEAD_DIM // 2
    inv = 1.0 / (ROPE_BASE ** (jnp.arange(0, half, 2, dtype=F32) / half))
    zero = jnp.zeros((SEQ, quarter), F32)
    parts_c, parts_a, parts_b = [], [], []
    for axis_pos in (row, col):
        ang = axis_pos[:, None] * inv
        c, s = jnp.cos(ang), jnp.sin(ang)
        parts_c += [c, c]
        parts_a += [-s, zero]
        parts_b += [zero, s]

    def table(parts, ctx_value):
        head = jnp.concatenate(parts, axis=1)
        head = jnp.concatenate([jnp.full((CTX_LEN, HEAD_DIM), ctx_value, F32), head], axis=0)
        return jnp.tile(head, (1, LANES // HEAD_DIM))

    return table(parts_c, 1.0), table(parts_a, 0.0), table(parts_b, 0.0)


def kernel(x, c, ctx, c_ctx, l0_norm_g, l0_mod_w, l0_mod_b, l0_w_in, l0_w_out, l0_sink, l1_norm_g, l1_mod_w, l1_mod_b, l1_w_in, l1_w_out, l1_conv_w, l1_conv_b, l1_gate_a_w, l1_gate_a_b, l1_gate_x_w, l1_gate_x_b, l1_lambda, l2_norm_g, l2_mod_w, l2_mod_b, l2_w_in, l2_w_out, l2_conv_w, l2_conv_b, l3_norm_g, l3_mod_w, l3_mod_b, l3_w_in, l3_w_out, l3_sink, final_norm_g):
    mods = _modulation(c, c_ctx, (l0_mod_w, l1_mod_w, l2_mod_w, l3_mod_w),
                       (l0_mod_b, l1_mod_b, l2_mod_b, l3_mod_b))
    rope_tabs = _rope_tables()

    qt, k, vt, gate = _in_attn((ctx, x), mods[0], l0_norm_g, l0_w_in, rope_tabs)
    ot = _attention(qt, k, vt, l0_sink, need_ctx=True)
    xs = _out_attn(ot, gate, l0_w_out, (ctx, x), mods[0], None)

    u, gate = _in_plain(xs, mods[1], l1_norm_g, l1_w_in, (F32, BF16), "in_lru")
    hf, hb = _lru_scan(u, l1_conv_w, l1_conv_b, l1_gate_a_w, l1_gate_a_b, l1_gate_x_w, l1_gate_x_b,
                       l1_lambda)
    xs = _out_lru(hf, hb, gate, l1_w_out, xs, mods[1])

    xs = _conv_layer(xs, mods[2], l2_norm_g, l2_w_in, l2_conv_w, l2_conv_b, l2_w_out)

    qt, k, vt, gate = _in_attn((xs,), mods[3], l3_norm_g, l3_w_in, rope_tabs)
    ot = _attention(qt, k, vt, l3_sink, need_ctx=False)
    return _out_attn(ot, gate, l3_w_out, (xs,), mods[3], final_norm_g)
```

```python
import functools
import math

import jax
import jax.numpy as jnp
from jax import lax
from jax.experimental import pallas as pl
from jax.experimental.pallas import tpu as pltpu

D_MODEL = 1024
BATCH = 8
SEQ = 2048
CTX_LEN = 256
TOK = CTX_LEN + SEQ
GRID_W = 64
EPS = 1e-6
NEG_INF = -1e30
N_HEADS = 16
N_KV_HEADS = 4
N_GROUPS = N_HEADS // N_KV_HEADS
HEAD_DIM = 64
Q_DIM = N_HEADS * HEAD_DIM
KV_DIM = N_KV_HEADS * HEAD_DIM
WINDOW = 128
ROPE_BASE = 10000.0
LRU_BLOCKS = 16
LRU_BLOCK = 64
LRU_C = 8.0
LRU_CONV = 4
CONV_K = 3

LANES = 128
SUBLANES = 8
BF16_SUBLANES = 16
MXU_DIM = 256
VMEM_LIMIT_BYTES = 56 * 1024 * 1024

ROW_TILE = 256
ROW_BATCH = 2
CTX_ROW_TILES = CTX_LEN // ROW_TILE
ROW_TILES = TOK // ROW_TILE
Q_TILE = WINDOW
CTX_Q_TILES = CTX_LEN // Q_TILE
Q_TILES = TOK // Q_TILE
BAND = 3 * WINDOW
MAX_ROWS = 32
CONV_CHUNK = MXU_DIM

LRU_TL = 256
LRU_WC = MXU_DIM
LRU_PITCH = LRU_TL + SUBLANES
LRU_SLABS = LRU_WC // LANES
LRU_CTX_BLKS = CTX_LEN // LRU_TL
LRU_BLKS = TOK // LRU_TL

F32 = jnp.float32
BF16 = jnp.bfloat16
F32_TINY = float(jnp.finfo(jnp.float32).tiny)
LOG2E = math.log2(math.e)


def _params(*sem):
    return pltpu.CompilerParams(dimension_semantics=sem, vmem_limit_bytes=VMEM_LIMIT_BYTES)


def _sigmoid(x):
    return 0.5 * jnp.tanh(0.5 * x) + 0.5


def _silu(x):
    return x * _sigmoid(x)


def _mod_kernel(c_ref, b_ref, *refs):
    w_refs, o_ref = refs[:-1], refs[-1]
    layer = pl.program_id(0) // 3
    s = _silu(c_ref[...])
    s_hi = s.astype(BF16)
    s_lo = (s - s_hi.astype(F32)).astype(BF16)
    for l, w_ref in enumerate(w_refs):
        @pl.when(layer == l)
        def _layer(w_ref=w_ref):
            w = w_ref[...]
            w_hi = w.astype(BF16)
            w_lo = (w - w_hi.astype(F32)).astype(BF16)
            acc = jnp.dot(s_hi, w_hi, preferred_element_type=F32)
            acc += jnp.dot(s_hi, w_lo, preferred_element_type=F32)
            acc += jnp.dot(s_lo, w_hi, preferred_element_type=F32)
            m = acc + b_ref[0]
            o_ref[0, :, 0, :] = jnp.broadcast_to(m[BATCH:BATCH + 1], (BATCH, D_MODEL))
            o_ref[0, :, 1, :] = m[:BATCH]


def _modulation(c, c_ctx, mod_ws, mod_bs):
    n = len(mod_ws)
    cc = jnp.concatenate([c, c_ctx[None], jnp.zeros((2 * SUBLANES - BATCH - 1, D_MODEL), F32)], axis=0)
    rows = cc.shape[0]
    w_spec = lambda l: pl.BlockSpec((D_MODEL, D_MODEL), lambda s: (0, jnp.clip(s - 3 * l, 0, 2)))
    m = pl.pallas_call(
        _mod_kernel,
        grid=(3 * n,),
        in_specs=[pl.BlockSpec((rows, D_MODEL), lambda s: (0, 0)),
                  pl.BlockSpec((1, 1, D_MODEL), lambda s: (s // 3, 0, s % 3))] + [w_spec(l) for l in range(n)],
        out_specs=pl.BlockSpec((1, BATCH, 2, D_MODEL), lambda s: (s // 3, 0, 0, s % 3)),
        out_shape=jax.ShapeDtypeStruct((n, BATCH, 2, 3 * D_MODEL), F32),
        compiler_params=_params("arbitrary"),
        name="modulation",
    )(cc, jnp.stack(mod_bs).reshape(n, 1, 3 * D_MODEL), *mod_ws)
    return m.reshape(n, BATCH, 2, 3, D_MODEL)


def _mod_spec(tile_offset=0):
    def idx(b, j):
        return (b, jnp.where(j + tile_offset < CTX_ROW_TILES, 0, 1), 0, 0)
    return pl.BlockSpec((ROW_BATCH, 1, 3, D_MODEL), idx)


def _row_spec(width, tile_offset=0):
    return pl.BlockSpec((ROW_BATCH, ROW_TILE, width), lambda b, j: (b, j + tile_offset, 0))


def _const_spec(shape, index=None):
    index = index or (lambda b, j: (0,) * len(shape))
    return pl.BlockSpec(shape, index, pipeline_mode=pl.Buffered(1))


def _stream_specs(split):
    if not split:
        return [_row_spec(D_MODEL)]
    blk = (ROW_BATCH, ROW_TILE, D_MODEL)
    return [pl.BlockSpec(blk, lambda b, j: (b, 0, 0)),
            pl.BlockSpec(blk, lambda b, j: (b, jnp.maximum(j - CTX_ROW_TILES, 0), 0))]


def _stream_tile(x_refs, bb):
    if len(x_refs) == 1:
        return x_refs[0][bb]
    ctx_ref, lat_ref = x_refs
    return jnp.where(pl.program_id(1) < CTX_ROW_TILES, ctx_ref[bb], lat_ref[bb])


def _cast_once(w_ref, wb_sc):
    @pl.when((pl.program_id(0) == 0) & (pl.program_id(1) == 0))
    def _cast():
        wb_sc[...] = w_ref[...].astype(BF16)


_ROW_GRID = (BATCH // ROW_BATCH, ROW_TILES)


def _norm_mod(x, g, mod):
    y = x * lax.rsqrt(jnp.mean(x * x, axis=-1, keepdims=True) + EPS) * g
    return y * (1.0 + mod[1:2, :]) + mod[0:1, :]


def _norm_mod_rows(x_refs, g_ref, mod_ref):
    tiles = [_norm_mod(_stream_tile(x_refs, bb), g_ref[...], mod_ref[bb, 0]) for bb in range(ROW_BATCH)]
    return jnp.concatenate(tiles, axis=0).astype(BF16)


def _tile_rows(t, bb):
    return t[bb * ROW_TILE:(bb + 1) * ROW_TILE]


def _in_attn_kernel(*refs, n_x):
    x_refs = refs[:n_x]
    mod_ref, g_ref, w_ref, rc_ref, ra_ref, rb_ref, qt_ref, k_ref, vt_ref, gate_ref, wb_sc = refs[n_x:]
    _cast_once(w_ref, wb_sc)
    z_all = jnp.dot(_norm_mod_rows(x_refs, g_ref, mod_ref), wb_sc[...], preferred_element_type=F32)
    rc, ra, rb = rc_ref[...], ra_ref[...], rb_ref[...]

    def rope(t):
        reps = t.shape[-1] // LANES
        n = t.shape[-1]
        return (t * jnp.tile(rc, (1, reps))
                + pltpu.roll(t, n - HEAD_DIM // 4, 1) * jnp.tile(ra, (1, reps))
                + pltpu.roll(t, HEAD_DIM // 4, 1) * jnp.tile(rb, (1, reps)))

    for bb in range(ROW_BATCH):
        z = _tile_rows(z_all, bb)
        qt_ref[bb] = (rope(z[:, :Q_DIM]) * (LOG2E * HEAD_DIM ** -0.5)).T.astype(BF16)
        k = rope(z[:, Q_DIM:Q_DIM + KV_DIM]).astype(BF16)
        for kh in range(N_KV_HEADS):
            k_ref[bb, kh] = k[:, kh * HEAD_DIM:(kh + 1) * HEAD_DIM]
        vt_ref[bb] = z[:, Q_DIM + KV_DIM:Q_DIM + 2 * KV_DIM].T.astype(BF16)
        gate_ref[bb] = z[:, Q_DIM + 2 * KV_DIM:].astype(gate_ref.dtype)


def _in_attn(streams, mod, norm_g, w_in, rope_tabs):
    n = w_in.shape[1]
    col = lambda height: pl.BlockSpec((ROW_BATCH, height, ROW_TILE), lambda b, j: (b, 0, j))
    tab = pl.BlockSpec((ROW_TILE, LANES), lambda b, j: (j, 0))
    return pl.pallas_call(
        functools.partial(_in_attn_kernel, n_x=len(streams)),
        grid=_ROW_GRID,
        in_specs=_stream_specs(len(streams) > 1)
                 + [_mod_spec(), _const_spec((1, D_MODEL)), _const_spec((D_MODEL, n)), tab, tab, tab],
        out_specs=[col(Q_DIM),
                   pl.BlockSpec((ROW_BATCH, N_KV_HEADS, ROW_TILE, HEAD_DIM), lambda b, j: (b, 0, j, 0)),
                   col(KV_DIM), _row_spec(Q_DIM)],
        out_shape=[jax.ShapeDtypeStruct((BATCH, Q_DIM, TOK), BF16),
                   jax.ShapeDtypeStruct((BATCH, N_KV_HEADS, TOK, HEAD_DIM), BF16),
                   jax.ShapeDtypeStruct((BATCH, KV_DIM, TOK), BF16),
                   jax.ShapeDtypeStruct((BATCH, TOK, Q_DIM), BF16)],
        scratch_shapes=[pltpu.VMEM((D_MODEL, n), BF16)],
        compiler_params=_params("arbitrary", "arbitrary"),
        name="in_attn",
    )(*streams, mod, norm_g.reshape(1, D_MODEL), w_in, *rope_tabs)


def _in_plain_kernel(x_ref, mod_ref, g_ref, w_ref, *refs):
    o_refs, wb_sc = refs[:-1], refs[-1]
    _cast_once(w_ref, wb_sc)
    z = jnp.dot(_norm_mod_rows((x_ref,), g_ref, mod_ref), wb_sc[...], preferred_element_type=F32)
    width = z.shape[-1] // len(o_refs)
    for i, o_ref in enumerate(o_refs):
        for bb in range(ROW_BATCH):
            o_ref[bb] = _tile_rows(z, bb)[:, i * width:(i + 1) * width].astype(o_ref.dtype)


def _in_plain(xs, mod, norm_g, w_in, dtypes, name):
    n = w_in.shape[1]
    width = n // len(dtypes)
    return pl.pallas_call(
        _in_plain_kernel,
        grid=_ROW_GRID,
        in_specs=[_row_spec(D_MODEL), _mod_spec(), _const_spec((1, D_MODEL)), _const_spec((D_MODEL, n))],
        out_specs=[_row_spec(width)] * len(dtypes),
        out_shape=[jax.ShapeDtypeStruct((BATCH, TOK, width), dt) for dt in dtypes],
        scratch_shapes=[pltpu.VMEM((D_MODEL, n), BF16)],
        compiler_params=_params("arbitrary", "arbitrary"),
        name=name,
    )(xs, mod, norm_g.reshape(1, D_MODEL), w_in)


def _attn_kernel(sink_ref, qt_ref, k_ref, vt_ref, ot_ref, s_sc, bias_sc, *, first_tile):
    lanes = N_GROUPS * Q_TILE
    ones = jnp.ones((BF16_SUBLANES, WINDOW), BF16)

    key = lax.broadcasted_iota(jnp.int32, (WINDOW, lanes), 0)
    query = lax.broadcasted_iota(jnp.int32, (WINDOW, lanes), 1) & (Q_TILE - 1)
    bias_sc[0] = jnp.where(key >= query, 0.0, NEG_INF)
    bias_sc[1] = jnp.where(key <= query, 0.0, NEG_INF)
    bias_sc[2] = jnp.full((WINDOW, lanes), NEG_INF, F32)

    def aligned(start):
        return start if isinstance(start, int) else pl.multiple_of(start, WINDOW)

    def heads_of(kh):
        return [kh * N_GROUPS + g for g in range(N_GROUPS)]

    def sink_row(kh):
        return jnp.concatenate([jnp.full((1, Q_TILE), sink_ref[h] * LOG2E, F32) for h in heads_of(kh)], axis=1)

    def score_steps(j, kh, chunks, m_box):
        cols = pl.ds(aligned(j * Q_TILE), Q_TILE)
        qt = jnp.concatenate([qt_ref[0, h * HEAD_DIM:(h + 1) * HEAD_DIM, cols] for h in heads_of(kh)], axis=1)
        mx = jnp.broadcast_to(sink_row(kh), (MAX_ROWS, lanes))
        for c, (ks, bias) in enumerate(chunks):
            s = jnp.dot(k_ref[0, kh, pl.ds(ks, WINDOW), :], qt, preferred_element_type=F32)
            if bias is not None:
                s = s + bias_sc[bias]
            s_sc[kh, c * WINDOW:(c + 1) * WINDOW, :] = s
            mx = jnp.maximum(mx, s.reshape(WINDOW // MAX_ROWS, MAX_ROWS, lanes).max(axis=0))
            yield
        m_box.append(mx.max(axis=0, keepdims=True))

    def value_steps(j, kh, chunks, m_box):
        m = m_box[0]
        acc = jnp.zeros((HEAD_DIM + BF16_SUBLANES, lanes), F32)
        for c, (ks, _) in enumerate(chunks):
            p = jnp.exp2(s_sc[kh, c * WINDOW:(c + 1) * WINDOW, :] - m).astype(BF16)
            vt = vt_ref[0, kh * HEAD_DIM:(kh + 1) * HEAD_DIM, pl.ds(ks, WINDOW)]
            acc = acc + jnp.dot(jnp.concatenate([vt, ones], axis=0), p, preferred_element_type=F32)
            yield
        denom = acc[HEAD_DIM:HEAD_DIM + 1, :] + jnp.exp2(sink_row(kh) - m)
        o = acc[:HEAD_DIM, :] / denom
        cols = pl.ds(aligned((j - first_tile) * Q_TILE), Q_TILE)
        for g, h in enumerate(heads_of(kh)):
            ot_ref[0, h * HEAD_DIM:(h + 1) * HEAD_DIM, cols] = o[:, g * Q_TILE:(g + 1) * Q_TILE].astype(BF16)

    def interleave(*stages):
        stages = list(stages)
        while stages:
            stages = [st for st in stages if next(st, StopIteration) is not StopIteration]

    ctx_chunks = [(c * WINDOW, None) for c in range(CTX_LEN // WINDOW)]

    def latent_chunks(j):
        own = j * Q_TILE
        if isinstance(j, int):
            prev_bias = 2 if j == CTX_Q_TILES else 0
            next_bias = 2 if j == Q_TILES - 1 else 1
            next_start = min(own + WINDOW, TOK - WINDOW)
        else:
            prev_bias = jnp.where(j == CTX_Q_TILES, 2, 0)
            next_bias = jnp.where(j == Q_TILES - 1, 2, 1)
            next_start = jnp.minimum(own + WINDOW, TOK - WINDOW)
        return [(aligned(own - WINDOW), prev_bias), (aligned(own), None),
                (aligned(next_start), next_bias)] + ctx_chunks

    if first_tile == 0:
        units = [(j, kh, ctx_chunks) for j in range(CTX_Q_TILES) for kh in range(N_KV_HEADS)]
        boxes = [[] for _ in units]
        interleave(score_steps(*units[0], boxes[0]))
        for i, unit in enumerate(units):
            stages = [score_steps(*units[i + 1], boxes[i + 1])] if i + 1 < len(units) else []
            interleave(*stages, value_steps(*unit, boxes[i]))

    def latent_tile(j, m_head0, next_j):
        chunks = latent_chunks(j)
        box = [m_head0]
        for kh in range(N_KV_HEADS):
            nxt = []
            stages = []
            if kh + 1 < N_KV_HEADS:
                stages.append(score_steps(j, kh + 1, chunks, nxt))
            elif next_j is not None:
                stages.append(score_steps(next_j, 0, latent_chunks(next_j), nxt))
            interleave(*stages, value_steps(j, kh, chunks, box))
            box = nxt
        return box[0] if box else None

    first_box = []
    interleave(score_steps(CTX_Q_TILES, 0, latent_chunks(CTX_Q_TILES), first_box))
    m_last = lax.fori_loop(CTX_Q_TILES, Q_TILES - 1, lambda j, m: latent_tile(j, m, j + 1), first_box[0])
    latent_tile(Q_TILES - 1, m_last, None)


def _attention(qt, k, vt, sink, need_ctx):
    first_tile = 0 if need_ctx else CTX_Q_TILES
    n_keys = BAND + CTX_LEN
    out_cols = TOK - first_tile * Q_TILE
    return pl.pallas_call(
        functools.partial(_attn_kernel, first_tile=first_tile),
        grid=(BATCH,),
        in_specs=[pl.BlockSpec(memory_space=pltpu.SMEM),
                  pl.BlockSpec((1, Q_DIM, TOK), lambda b: (b, 0, 0)),
                  pl.BlockSpec((1, N_KV_HEADS, TOK, HEAD_DIM), lambda b: (b, 0, 0, 0)),
                  pl.BlockSpec((1, KV_DIM, TOK), lambda b: (b, 0, 0))],
        out_specs=pl.BlockSpec((1, Q_DIM, out_cols), lambda b: (b, 0, 0)),
        out_shape=jax.ShapeDtypeStruct((BATCH, Q_DIM, out_cols), BF16),
        scratch_shapes=[pltpu.VMEM((N_KV_HEADS, n_keys, N_GROUPS * Q_TILE), F32),
                        pltpu.VMEM((3, WINDOW, N_GROUPS * Q_TILE), F32)],
        compiler_params=_params("arbitrary"),
        name="attention",
    )(sink, qt, k, vt)


def _lru_kernel(uf_ref, ub_ref, cw_ref, cb_ref, wa_ref, ba_ref, wx_ref, bx_ref, lam_ref,
                hf_ref, hb_ref, u_sc, a_sc, b_sc, y_sc, h_sc):
    step = pl.program_id(1)
    tl, pitch, halo = LRU_TL, LRU_PITCH, SUBLANES
    segment_start = (step == 0) | (step == LRU_CTX_BLKS)

    @pl.when(segment_start)
    def _zero_halo():
        u_sc[0, :, 0:halo, :] = jnp.zeros((BATCH * LRU_SLABS, halo, LANES), F32)
        u_sc[1, :, halo + tl:, :] = jnp.zeros((BATCH * LRU_SLABS, halo, LANES), F32)

    @pl.when(jnp.logical_not(segment_start))
    def _carry_halo():
        u_sc[0, :, 0:halo, :] = u_sc[0, :, tl:tl + halo, :]
        u_sc[1, :, halo + tl:, :] = u_sc[1, :, halo:2 * halo, :]

    @pl.when(step == 0)
    def _sequence_start():
        h_sc[...] = jnp.zeros_like(h_sc)

    for d, u_ref in ((0, uf_ref), (1, ub_ref)):
        for bi in range(BATCH):
            for s in range(LRU_SLABS):
                u_sc[d, bi * LRU_SLABS + s, halo:halo + tl, :] = u_ref[bi, :, s * LANES:(s + 1) * LANES]

    for d in range(2):
        nl = -lam_ref[d]
        half_decay = (0.5 * LRU_C) * (jnp.maximum(nl, 0.0) + jnp.log1p(jnp.exp(-jnp.abs(nl))))
        for bi in range(BATCH):
            cols = []
            for s in range(LRU_SLABS):
                lanes = slice(s * LANES, (s + 1) * LANES)
                half_x = 0.5 * cb_ref[d][:, lanes]
                for kk in range(LRU_CONV):
                    off = halo + (kk - (LRU_CONV - 1) if d == 0 else (LRU_CONV - 1) - kk)
                    rows = pl.ds(off, tl, stride=1)
                    tap = 0.5 * cw_ref[d][kk:kk + 1, lanes]
                    half_x = half_x + tap * u_sc[d, bi * LRU_SLABS + s, rows, :]
                cols.append(half_x)
            half_x = jnp.concatenate(cols, axis=1)
            xb = half_x.astype(BF16)
            tr = jnp.tanh(jnp.dot(xb, wa_ref[d, 0], preferred_element_type=F32) + ba_ref[d])
            ti = jnp.tanh(jnp.dot(xb, wx_ref[d, 0], preferred_element_type=F32) + bx_ref[d])
            neg_log_a = tr * half_decay + half_decay
            ix = ti * half_x + half_x
            a = jnp.exp(-neg_log_a)
            var = jnp.tanh(neg_log_a) * (a * a + 1.0)
            b = (var * lax.rsqrt(jnp.maximum(var, F32_TINY))) * ix
            for s in range(LRU_SLABS):
                lanes = slice(s * LANES, (s + 1) * LANES)
                a_sc[d, s, bi * pitch:bi * pitch + tl, :] = a[:, lanes]
                b_sc[d, s, bi * pitch:bi * pitch + tl, :] = b[:, lanes]

    def scan_step(t, carry):
        new = []
        for d in range(2):
            tt = t if d == 0 else tl - 1 - t
            for s in range(LRU_SLABS):
                rows = pl.ds(tt, BATCH, stride=pitch)
                h = a_sc[d, s, rows, :] * carry[d * LRU_SLABS + s] + b_sc[d, s, rows, :]
                y_sc[d, s, rows, :] = h
                new.append(h)
        return tuple(new)

    init = tuple(h_sc[d, s] for d in range(2) for s in range(LRU_SLABS))
    final = lax.fori_loop(0, tl, scan_step, init, unroll=8)
    for d in range(2):
        for s in range(LRU_SLABS):
            h_sc[d, s] = final[d * LRU_SLABS + s]

    for d, o_ref in ((0, hf_ref), (1, hb_ref)):
        for bi in range(BATCH):
            for s in range(LRU_SLABS):
                o_ref[bi, :, s * LANES:(s + 1) * LANES] = (
                    y_sc[d, s, bi * pitch:bi * pitch + tl, :].astype(o_ref.dtype))


def _block_diag_chunks(w):
    per = MXU_DIM // LRU_BLOCK
    w = w.reshape(2, LRU_BLOCKS // per, per, LRU_BLOCK, LRU_BLOCK)
    eye = jnp.eye(per, dtype=w.dtype)
    bd = jnp.einsum('dcpij,pq->dcpiqj', w, eye)
    return bd.reshape(2, LRU_BLOCKS // per, MXU_DIM, MXU_DIM).astype(BF16)


def _lru_scan(u, conv_w, conv_b, gate_a_w, gate_a_b, gate_x_w, gate_x_b, lam):
    def fwd_blk(w, s):
        return (0, s, w)

    def bwd_blk(w, s):
        blk = jnp.where(s < LRU_CTX_BLKS, LRU_CTX_BLKS - 1 - s, LRU_BLKS - 1 - (s - LRU_CTX_BLKS))
        return (0, blk, w)

    blk = (BATCH, LRU_TL, LRU_WC)
    vec = pl.BlockSpec((2, 1, LRU_WC), lambda w, s: (0, 0, w))
    gate = pl.BlockSpec((2, LRU_WC // MXU_DIM, MXU_DIM, MXU_DIM), lambda w, s: (0, w, 0, 0))
    width = u.shape[-1]
    out = jax.ShapeDtypeStruct((BATCH, TOK, width), BF16)
    scan_buf = pltpu.VMEM((2, LRU_SLABS, BATCH * LRU_PITCH, LANES), F32)
    return pl.pallas_call(
        _lru_kernel,
        grid=(width // LRU_WC, LRU_BLKS),
        in_specs=[pl.BlockSpec(blk, fwd_blk), pl.BlockSpec(blk, bwd_blk),
                  pl.BlockSpec((2, LRU_CONV, LRU_WC), lambda w, s: (0, 0, w)), vec,
                  gate, vec, gate, vec, vec],
        out_specs=[pl.BlockSpec(blk, fwd_blk), pl.BlockSpec(blk, bwd_blk)],
        out_shape=[out, out],
        scratch_shapes=[pltpu.VMEM((2, BATCH * LRU_SLABS, LRU_TL + 2 * SUBLANES, LANES), F32),
                        scan_buf, scan_buf, scan_buf,
                        pltpu.VMEM((2, LRU_SLABS, BATCH, LANES), F32)],
        compiler_params=_params("parallel", "arbitrary"),
        name="lru_scan",
    )(u, u, conv_w, conv_b.reshape(2, 1, width),
      _block_diag_chunks(gate_a_w), 0.5 * gate_a_b.reshape(2, 1, width),
      _block_diag_chunks(gate_x_w), 0.5 * gate_x_b.reshape(2, 1, width),
      lam.reshape(2, 1, width))


def _residual_rows(x_refs, mod_ref, acts, wb_sc, o_ref, post=None):
    y = jnp.dot(jnp.concatenate(acts, axis=0).astype(BF16), wb_sc[...], preferred_element_type=F32)
    for bb in range(ROW_BATCH):
        out = _stream_tile(x_refs, bb) + mod_ref[bb, 0][2:3, :] * _tile_rows(y, bb)
        o_ref[bb] = out if post is None else post(out)


def _out_attn_kernel(ot_ref, g_ref, w_ref, *refs, n_x, final):
    x_refs = refs[:n_x]
    mod_ref, fg_ref, o_ref, wb_sc = refs[n_x:]
    _cast_once(w_ref, wb_sc)
    acts = [ot_ref[bb].astype(F32).T * _silu(g_ref[bb].astype(F32)) for bb in range(ROW_BATCH)]

    def final_norm(x):
        return x * lax.rsqrt(jnp.mean(x * x, axis=-1, keepdims=True) + EPS) * fg_ref[...]

    _residual_rows(x_refs, mod_ref, acts, wb_sc, o_ref, final_norm if final else None)


def _out_lru_kernel(hf_ref, hb_ref, g_ref, w_ref, x_ref, mod_ref, o_ref, wb_sc):
    _cast_once(w_ref, wb_sc)
    acts = [(hf_ref[bb].astype(F32) + hb_ref[bb].astype(F32)) * _silu(g_ref[bb].astype(F32))
            for bb in range(ROW_BATCH)]
    _residual_rows((x_ref,), mod_ref, acts, wb_sc, o_ref)


def _conv_layer_kernel(x_ref, xp_ref, xn_ref, mod_ref, g_ref, wu_ref, wb_ref, wc_ref, wg_ref,
                       cw_ref, cb_ref, wo_ref, o_ref, p_sc, wi_sc, wo_sc):
    j = pl.program_id(1)
    halo, cc = SUBLANES, CONV_CHUNK
    rows = ROW_TILE + 2 * halo
    for i, w_ref in enumerate((wu_ref, wb_ref, wc_ref, wg_ref)):
        _cast_once(w_ref, wi_sc.at[i])
    _cast_once(wo_ref, wo_sc)
    tiles = [_norm_mod(jnp.concatenate([xp_ref[bb], x_ref[bb], xn_ref[bb]], axis=0), g_ref[...],
                       mod_ref[bb, 0]) for bb in range(ROW_BATCH)]
    h = jnp.concatenate(tiles, axis=0).astype(BF16)
    keep_prev = jnp.where((j == 0) | (j == CTX_ROW_TILES), 0.0, 1.0)
    keep_next = jnp.where((j == CTX_ROW_TILES - 1) | (j == ROW_TILES - 1), 0.0, 1.0)
    acc = jnp.zeros((ROW_BATCH * ROW_TILE, D_MODEL), F32)
    for c in range(D_MODEL // cc):
        chans = slice(c * cc, (c + 1) * cc)
        u, bg, cg, g = (jnp.dot(h, wi_sc[i, :, chans], preferred_element_type=F32) for i in range(4))
        p = cg * u
        acts = []
        for bb in range(ROW_BATCH):
            top = bb * rows
            main = slice(top + halo, top + halo + ROW_TILE)
            for s in range(cc // LANES):
                lanes = slice(s * LANES, (s + 1) * LANES)
                p_sc[bb, c, s, 0:halo, :] = p[top:top + halo, lanes] * keep_prev
                p_sc[bb, c, s, halo:halo + ROW_TILE, :] = p[main, lanes]
                p_sc[bb, c, s, halo + ROW_TILE:rows, :] = p[top + halo + ROW_TILE:top + rows, lanes] * keep_next
            before = jnp.concatenate([p_sc[bb, c, s, pl.ds(halo - 1, ROW_TILE, stride=1), :]
                                      for s in range(cc // LANES)], axis=1)
            after = jnp.concatenate([p_sc[bb, c, s, pl.ds(halo + 1, ROW_TILE, stride=1), :]
                                     for s in range(cc // LANES)], axis=1)
            conv = (cw_ref[0:1, chans] * before + cw_ref[1:2, chans] * p[main] + cw_ref[2:3, chans] * after
                    + cb_ref[:, chans])
            acts.append(bg[main] * conv * _silu(g[main]))
        acc = acc + jnp.dot(jnp.concatenate(acts, axis=0).astype(BF16), wo_sc[chans, :],
                            preferred_element_type=F32)
    for bb in range(ROW_BATCH):
        o_ref[bb] = x_ref[bb] + mod_ref[bb, 0][2:3, :] * _tile_rows(acc, bb)


_W_OUT_SPEC = _const_spec((D_MODEL, D_MODEL))


def _out_attn(ot, gate, w_out, streams, mod, final_g):
    final = final_g is not None
    off = CTX_ROW_TILES if final else 0
    fg = (final_g if final else jnp.ones((D_MODEL,), F32)).reshape(1, D_MODEL)
    rows = SEQ if final else TOK
    x_specs = [_row_spec(D_MODEL, off)] if final else _stream_specs(len(streams) > 1)
    return pl.pallas_call(
        functools.partial(_out_attn_kernel, n_x=len(streams), final=final),
        grid=(BATCH // ROW_BATCH, ROW_TILES - off),
        in_specs=[pl.BlockSpec((ROW_BATCH, Q_DIM, ROW_TILE), lambda b, j: (b, 0, j)), _row_spec(Q_DIM, off),
                  _W_OUT_SPEC] + x_specs + [_mod_spec(off), _const_spec((1, D_MODEL))],
        out_specs=_row_spec(D_MODEL),
        out_shape=jax.ShapeDtypeStruct((BATCH, rows, D_MODEL), F32),
        scratch_shapes=[pltpu.VMEM((D_MODEL, D_MODEL), BF16)],
        compiler_params=_params("arbitrary", "arbitrary"),
        name="out_attn_final" if final else "out_attn",
    )(ot, gate, w_out, *streams, mod, fg)


def _out_lru(hf, hb, gate, w_out, xs, mod):
    return pl.pallas_call(
        _out_lru_kernel,
        grid=_ROW_GRID,
        in_specs=[_row_spec(D_MODEL)] * 3 + [_W_OUT_SPEC, _row_spec(D_MODEL), _mod_spec()],
        out_specs=_row_spec(D_MODEL),
        out_shape=jax.ShapeDtypeStruct((BATCH, TOK, D_MODEL), F32),
        scratch_shapes=[pltpu.VMEM((D_MODEL, D_MODEL), BF16)],
        compiler_params=_params("arbitrary", "arbitrary"),
        name="out_lru",
    )(hf, hb, gate, w_out, xs, mod)


def _conv_layer(xs, mod, norm_g, w_in, conv_w, conv_b, w_out):
    per = ROW_TILE // SUBLANES
    n8 = TOK // SUBLANES
    halo = lambda idx: pl.BlockSpec((ROW_BATCH, SUBLANES, D_MODEL), idx)
    prev = halo(lambda b, j: (b, jnp.maximum(j * per - 1, 0), 0))
    nxt = halo(lambda b, j: (b, jnp.minimum((j + 1) * per, n8 - 1), 0))
    w_blocks = [_const_spec((D_MODEL, D_MODEL), functools.partial(lambda i, b, j: (0, i), i)) for i in range(4)]
    return pl.pallas_call(
        _conv_layer_kernel,
        grid=_ROW_GRID,
        in_specs=[_row_spec(D_MODEL), prev, nxt, _mod_spec(), _const_spec((1, D_MODEL))] + w_blocks
                 + [_const_spec((CONV_K, D_MODEL)), _const_spec((1, D_MODEL)), _W_OUT_SPEC],
        out_specs=_row_spec(D_MODEL),
        out_shape=jax.ShapeDtypeStruct((BATCH, TOK, D_MODEL), F32),
        scratch_shapes=[pltpu.VMEM((ROW_BATCH, D_MODEL // CONV_CHUNK, CONV_CHUNK // LANES,
                                    ROW_TILE + 2 * SUBLANES, LANES), F32),
                        pltpu.VMEM((4, D_MODEL, D_MODEL), BF16), pltpu.VMEM((D_MODEL, D_MODEL), BF16)],
        compiler_params=_params("arbitrary", "arbitrary"),
        name="conv_layer",
    )(xs, xs, xs, mod, norm_g.reshape(1, D_MODEL), w_in, w_in, w_in, w_in, conv_w,
      conv_b.reshape(1, D_MODEL), w_out)


def _rope_tables():
    quarter = HEAD_DIM // 4
    pos = jnp.arange(SEQ)
    row = (pos // GRID_W).astype(F32)
    col = (pos % GRID_W).astype(F32)
    half = HEAD_DIM // 2
    inv = 1.0 / (ROPE_BASE ** (jnp.arange(0, half, 2, dtype=F32) / half))
    zero = jnp.zeros((SEQ, quarter), F32)
    parts_c, parts_a, parts_b = [], [], []
    for axis_pos in (row, col):
        ang = axis_pos[:, None] * inv
        c, s = jnp.cos(ang), jnp.sin(ang)
        parts_c += [c, c]
        parts_a += [-s, zero]
        parts_b += [zero, s]

    def table(parts, ctx_value):
        head = jnp.concatenate(parts, axis=1)
        head = jnp.concatenate([jnp.full((CTX_LEN, HEAD_DIM), ctx_value, F32), head], axis=0)
        return jnp.tile(head, (1, LANES // HEAD_DIM))

    return table(parts_c, 1.0), table(parts_a, 0.0), table(parts_b, 0.0)


def kernel(x, c, ctx, c_ctx, l0_norm_g, l0_mod_w, l0_mod_b, l0_w_in, l0_w_out, l0_sink, l1_norm_g, l1_mod_w, l1_mod_b, l1_w_in, l1_w_out, l1_conv_w, l1_conv_b, l1_gate_a_w, l1_gate_a_b, l1_gate_x_w, l1_gate_x_b, l1_lambda, l2_norm_g, l2_mod_w, l2_mod_b, l2_w_in, l2_w_out, l2_conv_w, l2_conv_b, l3_norm_g, l3_mod_w, l3_mod_b, l3_w_in, l3_w_out, l3_sink, final_norm_g):
    mods = _modulation(c, c_ctx, (l0_mod_w, l1_mod_w, l2_mod_w, l3_mod_w),
                       (l0_mod_b, l1_mod_b, l2_mod_b, l3_mod_b))
    rope_tabs = _rope_tables()

    qt, k, vt, gate = _in_attn((ctx, x), mods[0], l0_norm_g, l0_w_in, rope_tabs)
    ot = _attention(qt, k, vt, l0_sink, need_ctx=True)
    xs = _out_attn(ot, gate, l0_w_out, (ctx, x), mods[0], None)

    u, gate = _in_plain(xs, mods[1], l1_norm_g, l1_w_in, (F32, BF16), "in_lru")
    hf, hb = _lru_scan(u, l1_conv_w, l1_conv_b, l1_gate_a_w, l1_gate_a_b, l1_gate_x_w, l1_gate_x_b,
                       l1_lambda)
    xs = _out_lru(hf, hb, gate, l1_w_out, xs, mods[1])

    xs = _conv_layer(xs, mods[2], l2_norm_g, l2_w_in, l2_conv_w, l2_conv_b, l2_w_out)

    qt, k, vt, gate = _in_attn((xs,), mods[3], l3_norm_g, l3_w_in, rope_tabs)
    ot = _attention(qt, k, vt, l3_sink, need_ctx=False)
    return _out_attn(ot, gate, l3_w_out, (xs,), mods[3], final_norm_g)
```

```python
import functools
import math

import jax
import jax.numpy as jnp
import numpy as np
from jax import lax
from jax.experimental import pallas as pl
from jax.experimental.pallas import tpu as pltpu

D_MODEL = 1024
BATCH = 8
SEQ = 2048
CTX_LEN = 256
TOK = CTX_LEN + SEQ
GRID_W = 64
EPS = 1e-6
NEG_INF = -1e30
N_HEADS = 16
N_KV_HEADS = 4
N_GROUPS = N_HEADS // N_KV_HEADS
HEAD_DIM = 64
Q_DIM = N_HEADS * HEAD_DIM
KV_DIM = N_KV_HEADS * HEAD_DIM
WINDOW = 128
ROPE_BASE = 10000.0
LRU_BLOCKS = 16
LRU_BLOCK = 64
LRU_C = 8.0
LRU_CONV = 4
CONV_K = 3

LANES = 128
SUBLANES = 8
BF16_SUBLANES = 16
MXU_DIM = 256
VMEM_LIMIT_BYTES = 56 * 1024 * 1024

ROW_TILE = 256
ROW_BATCH = 2
OUT_ROW_BATCH = 4
CTX_ROW_TILES = CTX_LEN // ROW_TILE
ROW_TILES = TOK // ROW_TILE
Q_TILE = WINDOW
CTX_Q_TILES = CTX_LEN // Q_TILE
Q_TILES = TOK // Q_TILE
BAND = 3 * WINDOW
MAX_ROWS = 32
CONV_CHUNK = MXU_DIM

LRU_TL = 256
LRU_WC = MXU_DIM
LRU_PITCH = LRU_TL + SUBLANES
LRU_SLABS = LRU_WC // LANES
LRU_CTX_BLKS = CTX_LEN // LRU_TL
LRU_BLKS = TOK // LRU_TL

F32 = jnp.float32
BF16 = jnp.bfloat16
F32_TINY = float(jnp.finfo(jnp.float32).tiny)
LOG2E = math.log2(math.e)


def _params(*sem):
    return pltpu.CompilerParams(dimension_semantics=sem, vmem_limit_bytes=VMEM_LIMIT_BYTES)


def _sigmoid(x):
    return 0.5 * jnp.tanh(0.5 * x) + 0.5


def _silu(x):
    return x * _sigmoid(x)


def _mod_kernel(c_ref, b_ref, *refs):
    w_refs, o_ref = refs[:-1], refs[-1]
    layer = pl.program_id(0) // 3
    s = _silu(c_ref[...])
    s_hi = s.astype(BF16)
    s_lo = (s - s_hi.astype(F32)).astype(BF16)
    for l, w_ref in enumerate(w_refs):
        @pl.when(layer == l)
        def _layer(w_ref=w_ref):
            w = w_ref[...]
            w_hi = w.astype(BF16)
            w_lo = (w - w_hi.astype(F32)).astype(BF16)
            acc = jnp.dot(s_hi, w_hi, preferred_element_type=F32)
            acc += jnp.dot(s_hi, w_lo, preferred_element_type=F32)
            acc += jnp.dot(s_lo, w_hi, preferred_element_type=F32)
            m = acc + b_ref[0]
            o_ref[0, :, 0, :] = jnp.broadcast_to(m[BATCH:BATCH + 1], (BATCH, D_MODEL))
            o_ref[0, :, 1, :] = m[:BATCH]


def _modulation(c, c_ctx, mod_ws, mod_bs):
    n = len(mod_ws)
    cc = jnp.concatenate([c, c_ctx[None], jnp.zeros((2 * SUBLANES - BATCH - 1, D_MODEL), F32)], axis=0)
    rows = cc.shape[0]
    w_spec = lambda l: pl.BlockSpec((D_MODEL, D_MODEL), lambda s: (0, jnp.clip(s - 3 * l, 0, 2)))
    m = pl.pallas_call(
        _mod_kernel,
        grid=(3 * n,),
        in_specs=[pl.BlockSpec((rows, D_MODEL), lambda s: (0, 0)),
                  pl.BlockSpec((1, 1, D_MODEL), lambda s: (s // 3, 0, s % 3))] + [w_spec(l) for l in range(n)],
        out_specs=pl.BlockSpec((1, BATCH, 2, D_MODEL), lambda s: (s // 3, 0, 0, s % 3)),
        out_shape=jax.ShapeDtypeStruct((n, BATCH, 2, 3 * D_MODEL), F32),
        compiler_params=_params("arbitrary"),
        name="modulation",
    )(cc, jnp.stack(mod_bs).reshape(n, 1, 3 * D_MODEL), *mod_ws)
    return m.reshape(n, BATCH, 2, 3, D_MODEL)


def _mod_spec(tile_offset=0, rb=ROW_BATCH):
    def idx(b, j):
        return (b, jnp.where(j + tile_offset < CTX_ROW_TILES, 0, 1), 0, 0)
    return pl.BlockSpec((rb, 1, 3, D_MODEL), idx)


def _row_spec(width, tile_offset=0, rb=ROW_BATCH):
    return pl.BlockSpec((rb, ROW_TILE, width), lambda b, j: (b, j + tile_offset, 0))


def _const_spec(shape, index=None):
    index = index or (lambda b, j: (0,) * len(shape))
    return pl.BlockSpec(shape, index, pipeline_mode=pl.Buffered(1))


def _stream_specs(split, rb=ROW_BATCH):
    if not split:
        return [_row_spec(D_MODEL, rb=rb)]
    blk = (rb, ROW_TILE, D_MODEL)
    return [pl.BlockSpec(blk, lambda b, j: (b, 0, 0)),
            pl.BlockSpec(blk, lambda b, j: (b, jnp.maximum(j - CTX_ROW_TILES, 0), 0))]


def _stream_tile(x_refs, bb):
    if len(x_refs) == 1:
        return x_refs[0][bb]
    ctx_ref, lat_ref = x_refs
    return jnp.where(pl.program_id(1) < CTX_ROW_TILES, ctx_ref[bb], lat_ref[bb])


def _cast_once(w_ref, wb_sc):
    @pl.when((pl.program_id(0) == 0) & (pl.program_id(1) == 0))
    def _cast():
        wb_sc[...] = w_ref[...].astype(BF16)


_ROW_GRID = (BATCH // ROW_BATCH, ROW_TILES)


def _norm_mod(x, g, mod):
    y = x * lax.rsqrt(jnp.mean(x * x, axis=-1, keepdims=True) + EPS) * g
    return y * (1.0 + mod[1:2, :]) + mod[0:1, :]


def _norm_mod_rows(x_refs, g_ref, mod_ref):
    tiles = [_norm_mod(_stream_tile(x_refs, bb), g_ref[...], mod_ref[bb, 0]) for bb in range(ROW_BATCH)]
    return jnp.concatenate(tiles, axis=0).astype(BF16)


def _tile_rows(t, bb):
    return t[bb * ROW_TILE:(bb + 1) * ROW_TILE]


def _in_attn_kernel(*refs, n_x):
    x_refs = refs[:n_x]
    mod_ref, g_ref, w_ref, rc_ref, ra_ref, rb_ref, qt_ref, k_ref, vt_ref, gate_ref, wb_sc = refs[n_x:]
    _cast_once(w_ref, wb_sc)
    z_all = jnp.dot(_norm_mod_rows(x_refs, g_ref, mod_ref), wb_sc[...], preferred_element_type=F32)
    rc, ra, rb = rc_ref[...], ra_ref[...], rb_ref[...]

    def rope(t):
        reps = t.shape[-1] // LANES
        n = t.shape[-1]
        return (t * jnp.tile(rc, (1, reps))
                + pltpu.roll(t, n - HEAD_DIM // 4, 1) * jnp.tile(ra, (1, reps))
                + pltpu.roll(t, HEAD_DIM // 4, 1) * jnp.tile(rb, (1, reps)))

    for bb in range(ROW_BATCH):
        z = _tile_rows(z_all, bb)
        qt_ref[bb] = (rope(z[:, :Q_DIM]) * (LOG2E * HEAD_DIM ** -0.5)).T.astype(BF16)
        k = rope(z[:, Q_DIM:Q_DIM + KV_DIM]).astype(BF16)
        for kh in range(N_KV_HEADS):
            k_ref[bb, kh] = k[:, kh * HEAD_DIM:(kh + 1) * HEAD_DIM]
        vt_ref[bb] = z[:, Q_DIM + KV_DIM:Q_DIM + 2 * KV_DIM].T.astype(BF16)
        gate_ref[bb] = z[:, Q_DIM + 2 * KV_DIM:].astype(gate_ref.dtype)


def _in_attn(streams, mod, norm_g, w_in, rope_tabs):
    n = w_in.shape[1]
    col = lambda height: pl.BlockSpec((ROW_BATCH, height, ROW_TILE), lambda b, j: (b, 0, j))
    tab = pl.BlockSpec((ROW_TILE, LANES), lambda b, j: (j, 0))
    return pl.pallas_call(
        functools.partial(_in_attn_kernel, n_x=len(streams)),
        grid=_ROW_GRID,
        in_specs=_stream_specs(len(streams) > 1)
                 + [_mod_spec(), _const_spec((1, D_MODEL)), _const_spec((D_MODEL, n)), tab, tab, tab],
        out_specs=[col(Q_DIM),
                   pl.BlockSpec((ROW_BATCH, N_KV_HEADS, ROW_TILE, HEAD_DIM), lambda b, j: (b, 0, j, 0)),
                   col(KV_DIM), _row_spec(Q_DIM)],
        out_shape=[jax.ShapeDtypeStruct((BATCH, Q_DIM, TOK), BF16),
                   jax.ShapeDtypeStruct((BATCH, N_KV_HEADS, TOK, HEAD_DIM), BF16),
                   jax.ShapeDtypeStruct((BATCH, KV_DIM, TOK), BF16),
                   jax.ShapeDtypeStruct((BATCH, TOK, Q_DIM), BF16)],
        scratch_shapes=[pltpu.VMEM((D_MODEL, n), BF16)],
        compiler_params=_params("arbitrary", "arbitrary"),
        name="in_attn",
    )(*streams, mod, norm_g.reshape(1, D_MODEL), w_in, *rope_tabs)


def _in_plain_kernel(x_ref, mod_ref, g_ref, w_ref, *refs):
    o_refs, wb_sc = refs[:-1], refs[-1]
    _cast_once(w_ref, wb_sc)
    z = jnp.dot(_norm_mod_rows((x_ref,), g_ref, mod_ref), wb_sc[...], preferred_element_type=F32)
    width = z.shape[-1] // len(o_refs)
    for i, o_ref in enumerate(o_refs):
        for bb in range(ROW_BATCH):
            o_ref[bb] = _tile_rows(z, bb)[:, i * width:(i + 1) * width].astype(o_ref.dtype)


def _in_plain(xs, mod, norm_g, w_in, dtypes, name):
    n = w_in.shape[1]
    width = n // len(dtypes)
    return pl.pallas_call(
        _in_plain_kernel,
        grid=_ROW_GRID,
        in_specs=[_row_spec(D_MODEL), _mod_spec(), _const_spec((1, D_MODEL)), _const_spec((D_MODEL, n))],
        out_specs=[_row_spec(width)] * len(dtypes),
        out_shape=[jax.ShapeDtypeStruct((BATCH, TOK, width), dt) for dt in dtypes],
        scratch_shapes=[pltpu.VMEM((D_MODEL, n), BF16)],
        compiler_params=_params("arbitrary", "arbitrary"),
        name=name,
    )(xs, mod, norm_g.reshape(1, D_MODEL), w_in)


def _attn_kernel(sink_ref, qt_ref, k_ref, vt_ref, ot_ref, s_sc, bias_sc, *, first_tile):
    lanes = N_GROUPS * Q_TILE

    key = lax.broadcasted_iota(jnp.int32, (WINDOW, lanes), 0)
    query = lax.broadcasted_iota(jnp.int32, (WINDOW, lanes), 1) & (Q_TILE - 1)
    bias_sc[0] = jnp.where(key >= query, 0.0, NEG_INF)
    bias_sc[1] = jnp.where(key <= query, 0.0, NEG_INF)
    bias_sc[2] = jnp.full((WINDOW, lanes), NEG_INF, F32)

    def aligned(start):
        return start if isinstance(start, int) else pl.multiple_of(start, WINDOW)

    def heads_of(kh):
        return [kh * N_GROUPS + g for g in range(N_GROUPS)]

    def sink_row(kh):
        return jnp.concatenate([jnp.full((1, Q_TILE), sink_ref[h] * LOG2E, F32) for h in heads_of(kh)], axis=1)

    def score_steps(j, kh, chunks, m_box):
        cols = pl.ds(aligned(j * Q_TILE), Q_TILE)
        qt = jnp.concatenate([qt_ref[0, h * HEAD_DIM:(h + 1) * HEAD_DIM, cols] for h in heads_of(kh)], axis=1)
        mx = jnp.broadcast_to(sink_row(kh), (MAX_ROWS, lanes))
        keys = jnp.concatenate([k_ref[0, kh, pl.ds(ks, WINDOW), :] for ks, _ in chunks], axis=0)
        s_all = jnp.dot(keys, qt, preferred_element_type=F32)
        for c, (ks, bias) in enumerate(chunks):
            s = s_all[c * WINDOW:(c + 1) * WINDOW]
            if bias is not None:
                s = s + bias_sc[bias]
            s_sc[kh, c * WINDOW:(c + 1) * WINDOW, :] = s
            mx = jnp.maximum(mx, s.reshape(WINDOW // MAX_ROWS, MAX_ROWS, lanes).max(axis=0))
            yield
        m_box.append(mx.max(axis=0, keepdims=True))

    def value_steps(j, kh, chunks, m_box):
        m = m_box[0]
        acc = jnp.zeros((HEAD_DIM, lanes), F32)
        denom = jnp.zeros((SUBLANES, lanes), F32)
        group = []
        for c, (ks, _) in enumerate(chunks):
            p = jnp.exp2(s_sc[kh, c * WINDOW:(c + 1) * WINDOW, :] - m)
            denom = denom + p.reshape(WINDOW // SUBLANES, SUBLANES, lanes).sum(axis=0)
            group.append((vt_ref[0, kh * HEAD_DIM:(kh + 1) * HEAD_DIM, pl.ds(ks, WINDOW)], p.astype(BF16)))
            if len(group) == MXU_DIM // WINDOW or c == len(chunks) - 1:
                vt = jnp.concatenate([g[0] for g in group], axis=1)
                pp = jnp.concatenate([g[1] for g in group], axis=0)
                acc = acc + jnp.dot(vt, pp, preferred_element_type=F32)
                group = []
            yield
        denom = denom.sum(axis=0, keepdims=True) + jnp.exp2(sink_row(kh) - m)
        o = acc / denom
        cols = pl.ds(aligned((j - first_tile) * Q_TILE), Q_TILE)
        for g, h in enumerate(heads_of(kh)):
            ot_ref[0, h * HEAD_DIM:(h + 1) * HEAD_DIM, cols] = o[:, g * Q_TILE:(g + 1) * Q_TILE].astype(BF16)

    def interleave(*stages):
        stages = list(stages)
        while stages:
            stages = [st for st in stages if next(st, StopIteration) is not StopIteration]

    ctx_chunks = [(c * WINDOW, None) for c in range(CTX_LEN // WINDOW)]

    def latent_chunks(j):
        own = j * Q_TILE
        if isinstance(j, int):
            prev_bias = 2 if j == CTX_Q_TILES else 0
            next_bias = 2 if j == Q_TILES - 1 else 1
            next_start = min(own + WINDOW, TOK - WINDOW)
        else:
            prev_bias = jnp.where(j == CTX_Q_TILES, 2, 0)
            next_bias = jnp.where(j == Q_TILES - 1, 2, 1)
            next_start = jnp.minimum(own + WINDOW, TOK - WINDOW)
        return [(aligned(own - WINDOW), prev_bias), (aligned(own), None),
                (aligned(next_start), next_bias)] + ctx_chunks

    if first_tile == 0:
        units = [(j, kh, ctx_chunks) for j in range(CTX_Q_TILES) for kh in range(N_KV_HEADS)]
        boxes = [[] for _ in units]
        interleave(score_steps(*units[0], boxes[0]))
        for i, unit in enumerate(units):
            stages = [score_steps(*units[i + 1], boxes[i + 1])] if i + 1 < len(units) else []
            interleave(*stages, value_steps(*unit, boxes[i]))

    def latent_tile(j, m_head0, next_j):
        chunks = latent_chunks(j)
        box = [m_head0]
        for kh in range(N_KV_HEADS):
            nxt = []
            stages = []
            if kh + 1 < N_KV_HEADS:
                stages.append(score_steps(j, kh + 1, chunks, nxt))
            elif next_j is not None:
                stages.append(score_steps(next_j, 0, latent_chunks(next_j), nxt))
            interleave(*stages, value_steps(j, kh, chunks, box))
            box = nxt
        return box[0] if box else None

    first_box = []
    interleave(score_steps(CTX_Q_TILES, 0, latent_chunks(CTX_Q_TILES), first_box))
    m_last = lax.fori_loop(CTX_Q_TILES, Q_TILES - 1, lambda j, m: latent_tile(j, m, j + 1), first_box[0])
    latent_tile(Q_TILES - 1, m_last, None)


def _attention(qt, k, vt, sink, need_ctx):
    first_tile = 0 if need_ctx else CTX_Q_TILES
    n_keys = BAND + CTX_LEN
    out_cols = TOK - first_tile * Q_TILE
    return pl.pallas_call(
        functools.partial(_attn_kernel, first_tile=first_tile),
        grid=(BATCH,),
        in_specs=[pl.BlockSpec(memory_space=pltpu.SMEM),
                  pl.BlockSpec((1, Q_DIM, TOK), lambda b: (b, 0, 0)),
                  pl.BlockSpec((1, N_KV_HEADS, TOK, HEAD_DIM), lambda b: (b, 0, 0, 0)),
                  pl.BlockSpec((1, KV_DIM, TOK), lambda b: (b, 0, 0))],
        out_specs=pl.BlockSpec((1, Q_DIM, out_cols), lambda b: (b, 0, 0)),
        out_shape=jax.ShapeDtypeStruct((BATCH, Q_DIM, out_cols), BF16),
        scratch_shapes=[pltpu.VMEM((N_KV_HEADS, n_keys, N_GROUPS * Q_TILE), F32),
                        pltpu.VMEM((3, WINDOW, N_GROUPS * Q_TILE), F32)],
        compiler_params=_params("arbitrary"),
        name="attention",
    )(sink, qt, k, vt)


def _lru_kernel(uf_ref, ub_ref, cw_ref, cb_ref, wa_ref, ba_ref, wx_ref, bx_ref, lam_ref,
                hf_ref, hb_ref, u_sc, a_sc, b_sc, y_sc, h_sc):
    step = pl.program_id(1)
    tl, pitch, halo = LRU_TL, LRU_PITCH, SUBLANES
    segment_start = (step == 0) | (step == LRU_CTX_BLKS)

    @pl.when(segment_start)
    def _zero_halo():
        u_sc[0, :, 0:halo, :] = jnp.zeros((BATCH * LRU_SLABS, halo, LANES), F32)
        u_sc[1, :, halo + tl:, :] = jnp.zeros((BATCH * LRU_SLABS, halo, LANES), F32)

    @pl.when(jnp.logical_not(segment_start))
    def _carry_halo():
        u_sc[0, :, 0:halo, :] = u_sc[0, :, tl:tl + halo, :]
        u_sc[1, :, halo + tl:, :] = u_sc[1, :, halo:2 * halo, :]

    @pl.when(step == 0)
    def _sequence_start():
        h_sc[...] = jnp.zeros_like(h_sc)

    for d, u_ref in ((0, uf_ref), (1, ub_ref)):
        for bi in range(BATCH):
            for s in range(LRU_SLABS):
                u_sc[d, bi * LRU_SLABS + s, halo:halo + tl, :] = u_ref[bi, :, s * LANES:(s + 1) * LANES]

    for d in range(2):
        nl = -lam_ref[d]
        half_decay = (0.5 * LRU_C) * (jnp.maximum(nl, 0.0) + jnp.log1p(jnp.exp(-jnp.abs(nl))))
        for bi in range(BATCH):
            cols = []
            for s in range(LRU_SLABS):
                lanes = slice(s * LANES, (s + 1) * LANES)
                half_x = 0.5 * cb_ref[d][:, lanes]
                for kk in range(LRU_CONV):
                    off = halo + (kk - (LRU_CONV - 1) if d == 0 else (LRU_CONV - 1) - kk)
                    rows = pl.ds(off, tl, stride=1)
                    tap = 0.5 * cw_ref[d][kk:kk + 1, lanes]
                    half_x = half_x + tap * u_sc[d, bi * LRU_SLABS + s, rows, :]
                cols.append(half_x)
            half_x = jnp.concatenate(cols, axis=1)
            xb = half_x.astype(BF16)
            tr = jnp.tanh(jnp.dot(xb, wa_ref[d, 0], preferred_element_type=F32) + ba_ref[d])
            ti = jnp.tanh(jnp.dot(xb, wx_ref[d, 0], preferred_element_type=F32) + bx_ref[d])
            neg_log_a = tr * half_decay + half_decay
            ix = ti * half_x + half_x
            a = jnp.exp(-neg_log_a)
            var = jnp.tanh(neg_log_a) * (a * a + 1.0)
            b = (var * lax.rsqrt(jnp.maximum(var, F32_TINY))) * ix
            for s in range(LRU_SLABS):
                lanes = slice(s * LANES, (s + 1) * LANES)
                a_sc[d, s, bi * pitch:bi * pitch + tl, :] = a[:, lanes]
                b_sc[d, s, bi * pitch:bi * pitch + tl, :] = b[:, lanes]

    def scan_step(t, carry):
        new = []
        for d in range(2):
            tt = t if d == 0 else tl - 1 - t
            for s in range(LRU_SLABS):
                rows = pl.ds(tt, BATCH, stride=pitch)
                h = a_sc[d, s, rows, :] * carry[d * LRU_SLABS + s] + b_sc[d, s, rows, :]
                y_sc[d, s, rows, :] = h
                new.append(h)
        return tuple(new)

    init = tuple(h_sc[d, s] for d in range(2) for s in range(LRU_SLABS))
    final = lax.fori_loop(0, tl, scan_step, init, unroll=8)
    for d in range(2):
        for s in range(LRU_SLABS):
            h_sc[d, s] = final[d * LRU_SLABS + s]

    for d, o_ref in ((0, hf_ref), (1, hb_ref)):
        for bi in range(BATCH):
            for s in range(LRU_SLABS):
                o_ref[bi, :, s * LANES:(s + 1) * LANES] = (
                    y_sc[d, s, bi * pitch:bi * pitch + tl, :].astype(o_ref.dtype))


def _block_diag_chunks(w):
    per = MXU_DIM // LRU_BLOCK
    w = w.reshape(2, LRU_BLOCKS // per, per, LRU_BLOCK, LRU_BLOCK)
    eye = jnp.eye(per, dtype=w.dtype)
    bd = jnp.einsum('dcpij,pq->dcpiqj', w, eye)
    return bd.reshape(2, LRU_BLOCKS // per, MXU_DIM, MXU_DIM).astype(BF16)


def _lru_scan(u, conv_w, conv_b, gate_a_w, gate_a_b, gate_x_w, gate_x_b, lam):
    def fwd_blk(w, s):
        return (0, s, w)

    def bwd_blk(w, s):
        blk = jnp.where(s < LRU_CTX_BLKS, LRU_CTX_BLKS - 1 - s, LRU_BLKS - 1 - (s - LRU_CTX_BLKS))
        return (0, blk, w)

    blk = (BATCH, LRU_TL, LRU_WC)
    vec = pl.BlockSpec((2, 1, LRU_WC), lambda w, s: (0, 0, w))
    gate = pl.BlockSpec((2, LRU_WC // MXU_DIM, MXU_DIM, MXU_DIM), lambda w, s: (0, w, 0, 0))
    width = u.shape[-1]
    out = jax.ShapeDtypeStruct((BATCH, TOK, width), BF16)
    scan_buf = pltpu.VMEM((2, LRU_SLABS, BATCH * LRU_PITCH, LANES), F32)
    return pl.pallas_call(
        _lru_kernel,
        grid=(width // LRU_WC, LRU_BLKS),
        in_specs=[pl.BlockSpec(blk, fwd_blk), pl.BlockSpec(blk, bwd_blk),
                  pl.BlockSpec((2, LRU_CONV, LRU_WC), lambda w, s: (0, 0, w)), vec,
                  gate, vec, gate, vec, vec],
        out_specs=[pl.BlockSpec(blk, fwd_blk), pl.BlockSpec(blk, bwd_blk)],
        out_shape=[out, out],
        scratch_shapes=[pltpu.VMEM((2, BATCH * LRU_SLABS, LRU_TL + 2 * SUBLANES, LANES), F32),
                        scan_buf, scan_buf, scan_buf,
                        pltpu.VMEM((2, LRU_SLABS, BATCH, LANES), F32)],
        compiler_params=_params("parallel", "arbitrary"),
        name="lru_scan",
    )(u, u, conv_w, conv_b.reshape(2, 1, width),
      _block_diag_chunks(gate_a_w), 0.5 * gate_a_b.reshape(2, 1, width),
      _block_diag_chunks(gate_x_w), 0.5 * gate_x_b.reshape(2, 1, width),
      lam.reshape(2, 1, width))


def _residual_rows(x_refs, mod_ref, acts, wb_sc, o_ref, post=None):
    y = jnp.dot(jnp.concatenate(acts, axis=0).astype(BF16), wb_sc[...], preferred_element_type=F32)
    for bb in range(len(acts)):
        out = _stream_tile(x_refs, bb) + mod_ref[bb, 0][2:3, :] * _tile_rows(y, bb)
        o_ref[bb] = out if post is None else post(out)


def _out_attn_kernel(ot_ref, g_ref, w_ref, *refs, n_x, final):
    x_refs = refs[:n_x]
    mod_ref, fg_ref, o_ref, wb_sc = refs[n_x:]
    _cast_once(w_ref, wb_sc)
    acts = [ot_ref[bb].astype(F32).T * _silu(g_ref[bb].astype(F32)) for bb in range(OUT_ROW_BATCH)]

    def final_norm(x):
        return x * lax.rsqrt(jnp.mean(x * x, axis=-1, keepdims=True) + EPS) * fg_ref[...]

    _residual_rows(x_refs, mod_ref, acts, wb_sc, o_ref, final_norm if final else None)


def _out_lru_kernel(hf_ref, hb_ref, g_ref, w_ref, x_ref, mod_ref, o_ref, wb_sc):
    _cast_once(w_ref, wb_sc)
    acts = [(hf_ref[bb].astype(F32) + hb_ref[bb].astype(F32)) * _silu(g_ref[bb].astype(F32))
            for bb in range(OUT_ROW_BATCH)]
    _residual_rows((x_ref,), mod_ref, acts, wb_sc, o_ref)


def _conv_layer_kernel(x_ref, xp_ref, xn_ref, mod_ref, g_ref, wu_ref, wb_ref, wc_ref, wg_ref,
                       cw_ref, cb_ref, wo_ref, o_ref, p_sc, wi_sc, wo_sc):
    j = pl.program_id(1)
    halo, cc = SUBLANES, CONV_CHUNK
    rows = ROW_TILE + 2 * halo
    for i, w_ref in enumerate((wu_ref, wb_ref, wc_ref, wg_ref)):
        _cast_once(w_ref, wi_sc.at[i])
    _cast_once(wo_ref, wo_sc)
    tiles = [_norm_mod(jnp.concatenate([xp_ref[bb], x_ref[bb], xn_ref[bb]], axis=0), g_ref[...],
                       mod_ref[bb, 0]) for bb in range(ROW_BATCH)]
    h = jnp.concatenate(tiles, axis=0).astype(BF16)
    keep_prev = jnp.where((j == 0) | (j == CTX_ROW_TILES), 0.0, 1.0)
    keep_next = jnp.where((j == CTX_ROW_TILES - 1) | (j == ROW_TILES - 1), 0.0, 1.0)
    acc = jnp.zeros((ROW_BATCH * ROW_TILE, D_MODEL), F32)
    for c in range(D_MODEL // cc):
        chans = slice(c * cc, (c + 1) * cc)
        u, bg, cg, g = (jnp.dot(h, wi_sc[i, :, chans], preferred_element_type=F32) for i in range(4))
        p = cg * u
        acts = []
        for bb in range(ROW_BATCH):
            top = bb * rows
            main = slice(top + halo, top + halo + ROW_TILE)
            for s in range(cc // LANES):
                lanes = slice(s * LANES, (s + 1) * LANES)
                p_sc[bb, c, s, 0:halo, :] = p[top:top + halo, lanes] * keep_prev
                p_sc[bb, c, s, halo:halo + ROW_TILE, :] = p[main, lanes]
                p_sc[bb, c, s, halo + ROW_TILE:rows, :] = p[top + halo + ROW_TILE:top + rows, lanes] * keep_next
            before = jnp.concatenate([p_sc[bb, c, s, pl.ds(halo - 1, ROW_TILE, stride=1), :]
                                      for s in range(cc // LANES)], axis=1)
            after = jnp.concatenate([p_sc[bb, c, s, pl.ds(halo + 1, ROW_TILE, stride=1), :]
                                     for s in range(cc // LANES)], axis=1)
            conv = (cw_ref[0:1, chans] * before + cw_ref[1:2, chans] * p[main] + cw_ref[2:3, chans] * after
                    + cb_ref[:, chans])
            acts.append(bg[main] * conv * _silu(g[main]))
        acc = acc + jnp.dot(jnp.concatenate(acts, axis=0).astype(BF16), wo_sc[chans, :],
                            preferred_element_type=F32)
    for bb in range(ROW_BATCH):
        o_ref[bb] = x_ref[bb] + mod_ref[bb, 0][2:3, :] * _tile_rows(acc, bb)


_W_OUT_SPEC = _const_spec((D_MODEL, D_MODEL))


def _out_attn(ot, gate, w_out, streams, mod, final_g):
    final = final_g is not None
    off = CTX_ROW_TILES if final else 0
    fg = (final_g if final else jnp.ones((D_MODEL,), F32)).reshape(1, D_MODEL)
    rows = SEQ if final else TOK
    rb = OUT_ROW_BATCH
    x_specs = [_row_spec(D_MODEL, off, rb)] if final else _stream_specs(len(streams) > 1, rb)
    return pl.pallas_call(
        functools.partial(_out_attn_kernel, n_x=len(streams), final=final),
        grid=(BATCH // rb, ROW_TILES - off),
        in_specs=[pl.BlockSpec((rb, Q_DIM, ROW_TILE), lambda b, j: (b, 0, j)), _row_spec(Q_DIM, off, rb),
                  _W_OUT_SPEC] + x_specs + [_mod_spec(off, rb), _const_spec((1, D_MODEL))],
        out_specs=_row_spec(D_MODEL, rb=rb),
        out_shape=jax.ShapeDtypeStruct((BATCH, rows, D_MODEL), F32),
        scratch_shapes=[pltpu.VMEM((D_MODEL, D_MODEL), BF16)],
        compiler_params=_params("arbitrary", "arbitrary"),
        name="out_attn_final" if final else "out_attn",
    )(ot, gate, w_out, *streams, mod, fg)


def _out_lru(hf, hb, gate, w_out, xs, mod):
    return pl.pallas_call(
        _out_lru_kernel,
        grid=(BATCH // OUT_ROW_BATCH, ROW_TILES),
        in_specs=[_row_spec(D_MODEL, rb=OUT_ROW_BATCH)] * 3
                 + [_W_OUT_SPEC, _row_spec(D_MODEL, rb=OUT_ROW_BATCH), _mod_spec(rb=OUT_ROW_BATCH)],
        out_specs=_row_spec(D_MODEL, rb=OUT_ROW_BATCH),
        out_shape=jax.ShapeDtypeStruct((BATCH, TOK, D_MODEL), F32),
        scratch_shapes=[pltpu.VMEM((D_MODEL, D_MODEL), BF16)],
        compiler_params=_params("arbitrary", "arbitrary"),
        name="out_lru",
    )(hf, hb, gate, w_out, xs, mod)


def _conv_layer(xs, mod, norm_g, w_in, conv_w, conv_b, w_out):
    per = ROW_TILE // SUBLANES
    n8 = TOK // SUBLANES
    halo = lambda idx: pl.BlockSpec((ROW_BATCH, SUBLANES, D_MODEL), idx)
    prev = halo(lambda b, j: (b, jnp.maximum(j * per - 1, 0), 0))
    nxt = halo(lambda b, j: (b, jnp.minimum((j + 1) * per, n8 - 1), 0))
    w_blocks = [_const_spec((D_MODEL, D_MODEL), functools.partial(lambda i, b, j: (0, i), i)) for i in range(4)]
    return pl.pallas_call(
        _conv_layer_kernel,
        grid=_ROW_GRID,
        in_specs=[_row_spec(D_MODEL), prev, nxt, _mod_spec(), _const_spec((1, D_MODEL))] + w_blocks
                 + [_const_spec((CONV_K, D_MODEL)), _const_spec((1, D_MODEL)), _W_OUT_SPEC],
        out_specs=_row_spec(D_MODEL),
        out_shape=jax.ShapeDtypeStruct((BATCH, TOK, D_MODEL), F32),
        scratch_shapes=[pltpu.VMEM((ROW_BATCH, D_MODEL // CONV_CHUNK, CONV_CHUNK // LANES,
                                    ROW_TILE + 2 * SUBLANES, LANES), F32),
                        pltpu.VMEM((4, D_MODEL, D_MODEL), BF16), pltpu.VMEM((D_MODEL, D_MODEL), BF16)],
        compiler_params=_params("arbitrary", "arbitrary"),
        name="conv_layer",
    )(xs, xs, xs, mod, norm_g.reshape(1, D_MODEL), w_in, w_in, w_in, w_in, conv_w,
      conv_b.reshape(1, D_MODEL), w_out)


def _rope_tables():
    quarter = HEAD_DIM // 4
    pos = np.arange(SEQ)
    half = HEAD_DIM // 2
    inv = (1.0 / (ROPE_BASE ** (np.arange(0, half, 2, dtype=np.float32) / half))).astype(np.float32)
    zero = np.zeros((SEQ, quarter), np.float32)
    parts_c, parts_a, parts_b = [], [], []
    for axis_pos in ((pos // GRID_W).astype(np.float32), (pos % GRID_W).astype(np.float32)):
        ang = axis_pos[:, None] * inv
        c, s = np.cos(ang), np.sin(ang)
        parts_c += [c, c]
        parts_a += [-s, zero]
        parts_b += [zero, s]

    def table(parts, ctx_value):
        head = np.concatenate(parts, axis=1)
        head = np.concatenate([np.full((CTX_LEN, HEAD_DIM), ctx_value, np.float32), head], axis=0)
        return jnp.asarray(np.tile(head, (1, LANES // HEAD_DIM)).astype(np.float32))

    return table(parts_c, 1.0), table(parts_a, 0.0), table(parts_b, 0.0)


def kernel(x, c, ctx, c_ctx, l0_norm_g, l0_mod_w, l0_mod_b, l0_w_in, l0_w_out, l0_sink, l1_norm_g, l1_mod_w, l1_mod_b, l1_w_in, l1_w_out, l1_conv_w, l1_conv_b, l1_gate_a_w, l1_gate_a_b, l1_gate_x_w, l1_gate_x_b, l1_lambda, l2_norm_g, l2_mod_w, l2_mod_b, l2_w_in, l2_w_out, l2_conv_w, l2_conv_b, l3_norm_g, l3_mod_w, l3_mod_b, l3_w_in, l3_w_out, l3_sink, final_norm_g):
    mods = _modulation(c, c_ctx, (l0_mod_w, l1_mod_w, l2_mod_w, l3_mod_w),
                       (l0_mod_b, l1_mod_b, l2_mod_b, l3_mod_b))
    rope_tabs = _rope_tables()

    qt, k, vt, gate = _in_attn((ctx, x), mods[0], l0_norm_g, l0_w_in, rope_tabs)
    ot = _attention(qt, k, vt, l0_sink, need_ctx=True)
    xs = _out_attn(ot, gate, l0_w_out, (ctx, x), mods[0], None)

    u, gate = _in_plain(xs, mods[1], l1_norm_g, l1_w_in, (F32, BF16), "in_lru")
    hf, hb = _lru_scan(u, l1_conv_w, l1_conv_b, l1_gate_a_w, l1_gate_a_b, l1_gate_x_w, l1_gate_x_b,
                       l1_lambda)
    xs = _out_lru(hf, hb, gate, l1_w_out, xs, mods[1])

    xs = _conv_layer(xs, mods[2], l2_norm_g, l2_w_in, l2_conv_w, l2_conv_b, l2_w_out)

    qt, k, vt, gate = _in_attn((xs,), mods[3], l3_norm_g, l3_w_in, rope_tabs)
    ot = _attention(qt, k, vt, l3_sink, need_ctx=False)
    return _out_attn(ot, gate, l3_w_out, (xs,), mods[3], final_norm_g)
```

```python
import functools
import math

import jax
import jax.numpy as jnp
import numpy as np
from jax import lax
from jax.experimental import pallas as pl
from jax.experimental.pallas import tpu as pltpu

D_MODEL = 1024
BATCH = 8
SEQ = 2048
CTX_LEN = 256
TOK = CTX_LEN + SEQ
GRID_W = 64
EPS = 1e-6
NEG_INF = -1e30
N_HEADS = 16
N_KV_HEADS = 4
N_GROUPS = N_HEADS // N_KV_HEADS
HEAD_DIM = 64
Q_DIM = N_HEADS * HEAD_DIM
KV_DIM = N_KV_HEADS * HEAD_DIM
WINDOW = 128
ROPE_BASE = 10000.0
LRU_BLOCKS = 16
LRU_BLOCK = 64
LRU_C = 8.0
LRU_CONV = 4
CONV_K = 3

LANES = 128
SUBLANES = 8
BF16_SUBLANES = 16
MXU_DIM = 256
VMEM_LIMIT_BYTES = 56 * 1024 * 1024

ROW_TILE = 256
ROW_BATCH = 2
OUT_ROW_BATCH = 4
CTX_ROW_TILES = CTX_LEN // ROW_TILE
ROW_TILES = TOK // ROW_TILE
Q_TILE = WINDOW
CTX_Q_TILES = CTX_LEN // Q_TILE
Q_TILES = TOK // Q_TILE
BAND = 3 * WINDOW
MAX_ROWS = 32
CONV_CHUNK = MXU_DIM

LRU_TL = 256
LRU_WC = MXU_DIM
LRU_PITCH = LRU_TL + 4
LRU_SLABS = LRU_WC // LANES
LRU_CTX_BLKS = CTX_LEN // LRU_TL
LRU_BLKS = TOK // LRU_TL

F32 = jnp.float32
BF16 = jnp.bfloat16
F32_TINY = float(jnp.finfo(jnp.float32).tiny)
LOG2E = math.log2(math.e)


def _params(*sem):
    return pltpu.CompilerParams(dimension_semantics=sem, vmem_limit_bytes=VMEM_LIMIT_BYTES)


def _sigmoid(x):
    return 0.5 * jnp.tanh(0.5 * x) + 0.5


def _silu(x):
    return x * _sigmoid(x)


def _mod_kernel(c_ref, b_ref, *refs):
    w_refs, o_ref = refs[:-1], refs[-1]
    layer = pl.program_id(0) // 3
    s = _silu(c_ref[...])
    s_hi = s.astype(BF16)
    s_lo = (s - s_hi.astype(F32)).astype(BF16)
    for l, w_ref in enumerate(w_refs):
        @pl.when(layer == l)
        def _layer(w_ref=w_ref):
            w = w_ref[...]
            w_hi = w.astype(BF16)
            w_lo = (w - w_hi.astype(F32)).astype(BF16)
            acc = jnp.dot(s_hi, w_hi, preferred_element_type=F32)
            acc += jnp.dot(s_hi, w_lo, preferred_element_type=F32)
            acc += jnp.dot(s_lo, w_hi, preferred_element_type=F32)
            m = acc + b_ref[0]
            o_ref[0, :, 0, :] = jnp.broadcast_to(m[BATCH:BATCH + 1], (BATCH, D_MODEL))
            o_ref[0, :, 1, :] = m[:BATCH]


def _modulation(c, c_ctx, mod_ws, mod_bs):
    n = len(mod_ws)
    cc = jnp.concatenate([c, c_ctx[None], jnp.zeros((2 * SUBLANES - BATCH - 1, D_MODEL), F32)], axis=0)
    rows = cc.shape[0]
    w_spec = lambda l: pl.BlockSpec((D_MODEL, D_MODEL), lambda s: (0, jnp.clip(s - 3 * l, 0, 2)))
    m = pl.pallas_call(
        _mod_kernel,
        grid=(3 * n,),
        in_specs=[pl.BlockSpec((rows, D_MODEL), lambda s: (0, 0)),
                  pl.BlockSpec((1, 1, D_MODEL), lambda s: (s // 3, 0, s % 3))] + [w_spec(l) for l in range(n)],
        out_specs=pl.BlockSpec((1, BATCH, 2, D_MODEL), lambda s: (s // 3, 0, 0, s % 3)),
        out_shape=jax.ShapeDtypeStruct((n, BATCH, 2, 3 * D_MODEL), F32),
        compiler_params=_params("arbitrary"),
        name="modulation",
    )(cc, jnp.stack(mod_bs).reshape(n, 1, 3 * D_MODEL), *mod_ws)
    return m.reshape(n, BATCH, 2, 3, D_MODEL)


def _mod_spec(tile_offset=0, rb=ROW_BATCH):
    def idx(b, j):
        return (b, jnp.where(j + tile_offset < CTX_ROW_TILES, 0, 1), 0, 0)
    return pl.BlockSpec((rb, 1, 3, D_MODEL), idx)


def _row_spec(width, tile_offset=0, rb=ROW_BATCH):
    return pl.BlockSpec((rb, ROW_TILE, width), lambda b, j: (b, j + tile_offset, 0))


def _const_spec(shape, index=None):
    index = index or (lambda b, j: (0,) * len(shape))
    return pl.BlockSpec(shape, index, pipeline_mode=pl.Buffered(1))


def _stream_specs(split, rb=ROW_BATCH):
    if not split:
        return [_row_spec(D_MODEL, rb=rb)]
    blk = (rb, ROW_TILE, D_MODEL)
    return [pl.BlockSpec(blk, lambda b, j: (b, 0, 0)),
            pl.BlockSpec(blk, lambda b, j: (b, jnp.maximum(j - CTX_ROW_TILES, 0), 0))]


def _stream_tile(x_refs, bb):
    if len(x_refs) == 1:
        return x_refs[0][bb]
    ctx_ref, lat_ref = x_refs
    return jnp.where(pl.program_id(1) < CTX_ROW_TILES, ctx_ref[bb], lat_ref[bb])


def _cast_once(w_ref, wb_sc):
    @pl.when((pl.program_id(0) == 0) & (pl.program_id(1) == 0))
    def _cast():
        wb_sc[...] = w_ref[...].astype(BF16)


_ROW_GRID = (BATCH // ROW_BATCH, ROW_TILES)


def _norm_mod(x, g, mod):
    y = x * lax.rsqrt(jnp.mean(x * x, axis=-1, keepdims=True) + EPS) * g
    return y * (1.0 + mod[1:2, :]) + mod[0:1, :]


def _norm_mod_rows(x_refs, g_ref, mod_ref):
    tiles = [_norm_mod(_stream_tile(x_refs, bb), g_ref[...], mod_ref[bb, 0]) for bb in range(ROW_BATCH)]
    return jnp.concatenate(tiles, axis=0).astype(BF16)


def _tile_rows(t, bb):
    return t[bb * ROW_TILE:(bb + 1) * ROW_TILE]


def _in_attn_kernel(*refs, n_x):
    x_refs = refs[:n_x]
    mod_ref, g_ref, w_ref, rc_ref, ra_ref, rb_ref, qt_ref, k_ref, vt_ref, gate_ref, wb_sc = refs[n_x:]
    _cast_once(w_ref, wb_sc)
    z_all = jnp.dot(_norm_mod_rows(x_refs, g_ref, mod_ref), wb_sc[...], preferred_element_type=F32)
    rc, ra, rb = rc_ref[...], ra_ref[...], rb_ref[...]

    def rope(t):
        reps = t.shape[-1] // LANES
        n = t.shape[-1]
        return (t * jnp.tile(rc, (1, reps))
                + pltpu.roll(t, n - HEAD_DIM // 4, 1) * jnp.tile(ra, (1, reps))
                + pltpu.roll(t, HEAD_DIM // 4, 1) * jnp.tile(rb, (1, reps)))

    for bb in range(ROW_BATCH):
        z = _tile_rows(z_all, bb)
        qt_ref[bb] = (rope(z[:, :Q_DIM]) * (LOG2E * HEAD_DIM ** -0.5)).T.astype(BF16)
        k = rope(z[:, Q_DIM:Q_DIM + KV_DIM]).astype(BF16)
        for kh in range(N_KV_HEADS):
            k_ref[bb, kh] = k[:, kh * HEAD_DIM:(kh + 1) * HEAD_DIM]
        vt_ref[bb] = z[:, Q_DIM + KV_DIM:Q_DIM + 2 * KV_DIM].T.astype(BF16)
        gate_ref[bb] = z[:, Q_DIM + 2 * KV_DIM:].astype(gate_ref.dtype)


def _in_attn(streams, mod, norm_g, w_in, rope_tabs):
    n = w_in.shape[1]
    col = lambda height: pl.BlockSpec((ROW_BATCH, height, ROW_TILE), lambda b, j: (b, 0, j))
    tab = pl.BlockSpec((ROW_TILE, LANES), lambda b, j: (j, 0))
    return pl.pallas_call(
        functools.partial(_in_attn_kernel, n_x=len(streams)),
        grid=_ROW_GRID,
        in_specs=_stream_specs(len(streams) > 1)
                 + [_mod_spec(), _const_spec((1, D_MODEL)), _const_spec((D_MODEL, n)), tab, tab, tab],
        out_specs=[col(Q_DIM),
                   pl.BlockSpec((ROW_BATCH, N_KV_HEADS, ROW_TILE, HEAD_DIM), lambda b, j: (b, 0, j, 0)),
                   col(KV_DIM), _row_spec(Q_DIM)],
        out_shape=[jax.ShapeDtypeStruct((BATCH, Q_DIM, TOK), BF16),
                   jax.ShapeDtypeStruct((BATCH, N_KV_HEADS, TOK, HEAD_DIM), BF16),
                   jax.ShapeDtypeStruct((BATCH, KV_DIM, TOK), BF16),
                   jax.ShapeDtypeStruct((BATCH, TOK, Q_DIM), BF16)],
        scratch_shapes=[pltpu.VMEM((D_MODEL, n), BF16)],
        compiler_params=_params("arbitrary", "arbitrary"),
        name="in_attn",
    )(*streams, mod, norm_g.reshape(1, D_MODEL), w_in, *rope_tabs)


def _in_plain_kernel(x_ref, mod_ref, g_ref, w_ref, *refs):
    o_refs, wb_sc = refs[:-1], refs[-1]
    _cast_once(w_ref, wb_sc)
    z = jnp.dot(_norm_mod_rows((x_ref,), g_ref, mod_ref), wb_sc[...], preferred_element_type=F32)
    width = z.shape[-1] // len(o_refs)
    for i, o_ref in enumerate(o_refs):
        for bb in range(ROW_BATCH):
            o_ref[bb] = _tile_rows(z, bb)[:, i * width:(i + 1) * width].astype(o_ref.dtype)


def _in_plain(xs, mod, norm_g, w_in, dtypes, name):
    n = w_in.shape[1]
    width = n // len(dtypes)
    return pl.pallas_call(
        _in_plain_kernel,
        grid=_ROW_GRID,
        in_specs=[_row_spec(D_MODEL), _mod_spec(), _const_spec((1, D_MODEL)), _const_spec((D_MODEL, n))],
        out_specs=[_row_spec(width)] * len(dtypes),
        out_shape=[jax.ShapeDtypeStruct((BATCH, TOK, width), dt) for dt in dtypes],
        scratch_shapes=[pltpu.VMEM((D_MODEL, n), BF16)],
        compiler_params=_params("arbitrary", "arbitrary"),
        name=name,
    )(xs, mod, norm_g.reshape(1, D_MODEL), w_in)


def _attn_kernel(sink_ref, qt_ref, k_ref, vt_ref, ot_ref, s_sc, bias_sc, *, first_tile):
    lanes = N_GROUPS * Q_TILE

    key = lax.broadcasted_iota(jnp.int32, (WINDOW, lanes), 0)
    query = lax.broadcasted_iota(jnp.int32, (WINDOW, lanes), 1) & (Q_TILE - 1)
    bias_sc[0] = jnp.where(key >= query, 0.0, NEG_INF)
    bias_sc[1] = jnp.where(key <= query, 0.0, NEG_INF)
    bias_sc[2] = jnp.full((WINDOW, lanes), NEG_INF, F32)

    def aligned(start):
        return start if isinstance(start, int) else pl.multiple_of(start, WINDOW)

    def heads_of(kh):
        return [kh * N_GROUPS + g for g in range(N_GROUPS)]

    def sink_row(kh):
        return jnp.concatenate([jnp.full((1, Q_TILE), sink_ref[h] * LOG2E, F32) for h in heads_of(kh)], axis=1)

    def score_steps(j, kh, chunks, m_box):
        cols = pl.ds(aligned(j * Q_TILE), Q_TILE)
        qt = jnp.concatenate([qt_ref[0, h * HEAD_DIM:(h + 1) * HEAD_DIM, cols] for h in heads_of(kh)], axis=1)
        mx = jnp.broadcast_to(sink_row(kh), (MAX_ROWS, lanes))
        keys = jnp.concatenate([k_ref[0, kh, pl.ds(ks, WINDOW), :] for ks, _ in chunks], axis=0)
        s_all = jnp.dot(keys, qt, preferred_element_type=F32)
        for c, (ks, bias) in enumerate(chunks):
            s = s_all[c * WINDOW:(c + 1) * WINDOW]
            if bias is not None:
                s = s + bias_sc[bias]
            s_sc[kh, c * WINDOW:(c + 1) * WINDOW, :] = s
            mx = jnp.maximum(mx, s.reshape(WINDOW // MAX_ROWS, MAX_ROWS, lanes).max(axis=0))
            yield
        m_box.append(mx.max(axis=0, keepdims=True))

    def value_steps(j, kh, chunks, m_box):
        m = m_box[0]
        acc = jnp.zeros((HEAD_DIM, lanes), F32)
        denom = jnp.zeros((SUBLANES, lanes), F32)
        group = []
        for c, (ks, _) in enumerate(chunks):
            p = jnp.exp2(s_sc[kh, c * WINDOW:(c + 1) * WINDOW, :] - m)
            denom = denom + p.reshape(WINDOW // SUBLANES, SUBLANES, lanes).sum(axis=0)
            group.append((vt_ref[0, kh * HEAD_DIM:(kh + 1) * HEAD_DIM, pl.ds(ks, WINDOW)], p.astype(BF16)))
            if len(group) == MXU_DIM // WINDOW or c == len(chunks) - 1:
                vt = jnp.concatenate([g[0] for g in group], axis=1)
                pp = jnp.concatenate([g[1] for g in group], axis=0)
                acc = acc + jnp.dot(vt, pp, preferred_element_type=F32)
                group = []
            yield
        denom = denom.sum(axis=0, keepdims=True) + jnp.exp2(sink_row(kh) - m)
        o = acc / denom
        cols = pl.ds(aligned((j - first_tile) * Q_TILE), Q_TILE)
        for g, h in enumerate(heads_of(kh)):
            ot_ref[0, h * HEAD_DIM:(h + 1) * HEAD_DIM, cols] = o[:, g * Q_TILE:(g + 1) * Q_TILE].astype(BF16)

    def interleave(*stages):
        stages = list(stages)
        while stages:
            stages = [st for st in stages if next(st, StopIteration) is not StopIteration]

    ctx_chunks = [(c * WINDOW, None) for c in range(CTX_LEN // WINDOW)]

    def latent_chunks(j):
        own = j * Q_TILE
        if isinstance(j, int):
            prev_bias = 2 if j == CTX_Q_TILES else 0
            next_bias = 2 if j == Q_TILES - 1 else 1
            next_start = min(own + WINDOW, TOK - WINDOW)
        else:
            prev_bias = jnp.where(j == CTX_Q_TILES, 2, 0)
            next_bias = jnp.where(j == Q_TILES - 1, 2, 1)
            next_start = jnp.minimum(own + WINDOW, TOK - WINDOW)
        return [(aligned(own - WINDOW), prev_bias), (aligned(own), None),
                (aligned(next_start), next_bias)] + ctx_chunks

    if first_tile == 0:
        units = [(j, kh, ctx_chunks) for j in range(CTX_Q_TILES) for kh in range(N_KV_HEADS)]
        boxes = [[] for _ in units]
        interleave(score_steps(*units[0], boxes[0]))
        for i, unit in enumerate(units):
            stages = [score_steps(*units[i + 1], boxes[i + 1])] if i + 1 < len(units) else []
            interleave(*stages, value_steps(*unit, boxes[i]))

    def latent_tile(j, m_head0, next_j):
        chunks = latent_chunks(j)
        box = [m_head0]
        for kh in range(N_KV_HEADS):
            nxt = []
            stages = []
            if kh + 1 < N_KV_HEADS:
                stages.append(score_steps(j, kh + 1, chunks, nxt))
            elif next_j is not None:
                stages.append(score_steps(next_j, 0, latent_chunks(next_j), nxt))
            interleave(*stages, value_steps(j, kh, chunks, box))
            box = nxt
        return box[0] if box else None

    first_box = []
    interleave(score_steps(CTX_Q_TILES, 0, latent_chunks(CTX_Q_TILES), first_box))
    m_last = lax.fori_loop(CTX_Q_TILES, Q_TILES - 1, lambda j, m: latent_tile(j, m, j + 1), first_box[0])
    latent_tile(Q_TILES - 1, m_last, None)


def _attention(qt, k, vt, sink, need_ctx):
    first_tile = 0 if need_ctx else CTX_Q_TILES
    n_keys = BAND + CTX_LEN
    out_cols = TOK - first_tile * Q_TILE
    return pl.pallas_call(
        functools.partial(_attn_kernel, first_tile=first_tile),
        grid=(BATCH,),
        in_specs=[pl.BlockSpec(memory_space=pltpu.SMEM),
                  pl.BlockSpec((1, Q_DIM, TOK), lambda b: (b, 0, 0)),
                  pl.BlockSpec((1, N_KV_HEADS, TOK, HEAD_DIM), lambda b: (b, 0, 0, 0)),
                  pl.BlockSpec((1, KV_DIM, TOK), lambda b: (b, 0, 0))],
        out_specs=pl.BlockSpec((1, Q_DIM, out_cols), lambda b: (b, 0, 0)),
        out_shape=jax.ShapeDtypeStruct((BATCH, Q_DIM, out_cols), BF16),
        scratch_shapes=[pltpu.VMEM((N_KV_HEADS, n_keys, N_GROUPS * Q_TILE), F32),
                        pltpu.VMEM((3, WINDOW, N_GROUPS * Q_TILE), F32)],
        compiler_params=_params("arbitrary"),
        name="attention",
    )(sink, qt, k, vt)


def _lru_kernel(uf_ref, ub_ref, cw_ref, cb_ref, wa_ref, ba_ref, wx_ref, bx_ref, lam_ref,
                hf_ref, hb_ref, u_sc, a_sc, b_sc, y_sc, h_sc):
    step = pl.program_id(1)
    tl, pitch, halo = LRU_TL, LRU_PITCH, SUBLANES
    segment_start = (step == 0) | (step == LRU_CTX_BLKS)

    @pl.when(segment_start)
    def _zero_halo():
        u_sc[0, :, 0:halo, :] = jnp.zeros((BATCH * LRU_SLABS, halo, LANES), F32)
        u_sc[1, :, halo + tl:, :] = jnp.zeros((BATCH * LRU_SLABS, halo, LANES), F32)

    @pl.when(jnp.logical_not(segment_start))
    def _carry_halo():
        u_sc[0, :, 0:halo, :] = u_sc[0, :, tl:tl + halo, :]
        u_sc[1, :, halo + tl:, :] = u_sc[1, :, halo:2 * halo, :]

    @pl.when(step == 0)
    def _sequence_start():
        h_sc[...] = jnp.zeros_like(h_sc)

    for d, u_ref in ((0, uf_ref), (1, ub_ref)):
        for bi in range(BATCH):
            for s in range(LRU_SLABS):
                u_sc[d, bi * LRU_SLABS + s, halo:halo + tl, :] = u_ref[bi, :, s * LANES:(s + 1) * LANES]

    for d in range(2):
        nl = -lam_ref[d]
        half_decay = (0.5 * LRU_C) * (jnp.maximum(nl, 0.0) + jnp.log1p(jnp.exp(-jnp.abs(nl))))
        for bi in range(BATCH):
            cols = []
            for s in range(LRU_SLABS):
                lanes = slice(s * LANES, (s + 1) * LANES)
                half_x = 0.5 * cb_ref[d][:, lanes]
                for kk in range(LRU_CONV):
                    off = halo + (kk - (LRU_CONV - 1) if d == 0 else (LRU_CONV - 1) - kk)
                    rows = pl.ds(off, tl, stride=1)
                    tap = 0.5 * cw_ref[d][kk:kk + 1, lanes]
                    half_x = half_x + tap * u_sc[d, bi * LRU_SLABS + s, rows, :]
                cols.append(half_x)
            half_x = jnp.concatenate(cols, axis=1)
            xb = half_x.astype(BF16)
            tr = jnp.tanh(jnp.dot(xb, wa_ref[d, 0], preferred_element_type=F32) + ba_ref[d])
            ti = jnp.tanh(jnp.dot(xb, wx_ref[d, 0], preferred_element_type=F32) + bx_ref[d])
            neg_log_a = tr * half_decay + half_decay
            ix = ti * half_x + half_x
            a = jnp.exp(-neg_log_a)
            var = jnp.tanh(neg_log_a) * (a * a + 1.0)
            b = (var * lax.rsqrt(jnp.maximum(var, F32_TINY))) * ix
            for s in range(LRU_SLABS):
                lanes = slice(s * LANES, (s + 1) * LANES)
                rows = pl.ds(bi * pitch, tl, stride=1)
                a_sc[d, s, rows, :] = a[:, lanes]
                b_sc[d, s, rows, :] = b[:, lanes]

    def scan_step(t, carry):
        new = []
        for d in range(2):
            tt = t if d == 0 else tl - 1 - t
            for s in range(LRU_SLABS):
                rows = pl.ds(tt, BATCH, stride=pitch)
                h = a_sc[d, s, rows, :] * carry[d * LRU_SLABS + s] + b_sc[d, s, rows, :]
                y_sc[d, s, rows, :] = h
                new.append(h)
        return tuple(new)

    init = tuple(h_sc[d, s] for d in range(2) for s in range(LRU_SLABS))
    final = lax.fori_loop(0, tl, scan_step, init, unroll=8)
    for d in range(2):
        for s in range(LRU_SLABS):
            h_sc[d, s] = final[d * LRU_SLABS + s]

    for d, o_ref in ((0, hf_ref), (1, hb_ref)):
        for bi in range(BATCH):
            for s in range(LRU_SLABS):
                o_ref[bi, :, s * LANES:(s + 1) * LANES] = (
                    y_sc[d, s, pl.ds(bi * pitch, tl, stride=1), :].astype(o_ref.dtype))


def _block_diag_chunks(w):
    per = MXU_DIM // LRU_BLOCK
    w = w.reshape(2, LRU_BLOCKS // per, per, LRU_BLOCK, LRU_BLOCK)
    eye = jnp.eye(per, dtype=w.dtype)
    bd = jnp.einsum('dcpij,pq->dcpiqj', w, eye)
    return bd.reshape(2, LRU_BLOCKS // per, MXU_DIM, MXU_DIM).astype(BF16)


def _lru_scan(u, conv_w, conv_b, gate_a_w, gate_a_b, gate_x_w, gate_x_b, lam):
    def fwd_blk(w, s):
        return (0, s, w)

    def bwd_blk(w, s):
        blk = jnp.where(s < LRU_CTX_BLKS, LRU_CTX_BLKS - 1 - s, LRU_BLKS - 1 - (s - LRU_CTX_BLKS))
        return (0, blk, w)

    blk = (BATCH, LRU_TL, LRU_WC)
    vec = pl.BlockSpec((2, 1, LRU_WC), lambda w, s: (0, 0, w))
    gate = pl.BlockSpec((2, LRU_WC // MXU_DIM, MXU_DIM, MXU_DIM), lambda w, s: (0, w, 0, 0))
    width = u.shape[-1]
    out = jax.ShapeDtypeStruct((BATCH, TOK, width), BF16)
    scan_buf = pltpu.VMEM((2, LRU_SLABS, BATCH * LRU_PITCH, LANES), F32)
    return pl.pallas_call(
        _lru_kernel,
        grid=(width // LRU_WC, LRU_BLKS),
        in_specs=[pl.BlockSpec(blk, fwd_blk), pl.BlockSpec(blk, bwd_blk),
                  pl.BlockSpec((2, LRU_CONV, LRU_WC), lambda w, s: (0, 0, w)), vec,
                  gate, vec, gate, vec, vec],
        out_specs=[pl.BlockSpec(blk, fwd_blk), pl.BlockSpec(blk, bwd_blk)],
        out_shape=[out, out],
        scratch_shapes=[pltpu.VMEM((2, BATCH * LRU_SLABS, LRU_TL + 2 * SUBLANES, LANES), F32),
                        scan_buf, scan_buf, scan_buf,
                        pltpu.VMEM((2, LRU_SLABS, BATCH, LANES), F32)],
        compiler_params=_params("parallel", "arbitrary"),
        name="lru_scan",
    )(u, u, conv_w, conv_b.reshape(2, 1, width),
      _block_diag_chunks(gate_a_w), 0.5 * gate_a_b.reshape(2, 1, width),
      _block_diag_chunks(gate_x_w), 0.5 * gate_x_b.reshape(2, 1, width),
      lam.reshape(2, 1, width))


def _residual_rows(x_refs, mod_ref, acts, wb_sc, o_ref, post=None):
    y = jnp.dot(jnp.concatenate(acts, axis=0).astype(BF16), wb_sc[...], preferred_element_type=F32)
    for bb in range(len(acts)):
        out = _stream_tile(x_refs, bb) + mod_ref[bb, 0][2:3, :] * _tile_rows(y, bb)
        o_ref[bb] = out if post is None else post(out)


def _out_attn_kernel(ot_ref, g_ref, w_ref, *refs, n_x, final):
    x_refs = refs[:n_x]
    mod_ref, fg_ref, o_ref, wb_sc = refs[n_x:]
    _cast_once(w_ref, wb_sc)
    acts = [ot_ref[bb].astype(F32).T * _silu(g_ref[bb].astype(F32)) for bb in range(OUT_ROW_BATCH)]

    def final_norm(x):
        return x * lax.rsqrt(jnp.mean(x * x, axis=-1, keepdims=True) + EPS) * fg_ref[...]

    _residual_rows(x_refs, mod_ref, acts, wb_sc, o_ref, final_norm if final else None)


def _out_lru_kernel(hf_ref, hb_ref, g_ref, w_ref, x_ref, mod_ref, o_ref, wb_sc):
    _cast_once(w_ref, wb_sc)
    acts = [(hf_ref[bb].astype(F32) + hb_ref[bb].astype(F32)) * _silu(g_ref[bb].astype(F32))
            for bb in range(OUT_ROW_BATCH)]
    _residual_rows((x_ref,), mod_ref, acts, wb_sc, o_ref)


def _conv_layer_kernel(x_ref, xp_ref, xn_ref, mod_ref, g_ref, wu_ref, wb_ref, wc_ref, wg_ref,
                       cw_ref, cb_ref, wo_ref, o_ref, p_sc, wi_sc, wo_sc):
    j = pl.program_id(1)
    halo, cc = SUBLANES, CONV_CHUNK
    rows = ROW_TILE + 2 * halo
    for i, w_ref in enumerate((wu_ref, wb_ref, wc_ref, wg_ref)):
        _cast_once(w_ref, wi_sc.at[i])
    _cast_once(wo_ref, wo_sc)
    tiles = [_norm_mod(jnp.concatenate([xp_ref[bb], x_ref[bb], xn_ref[bb]], axis=0), g_ref[...],
                       mod_ref[bb, 0]) for bb in range(ROW_BATCH)]
    h = jnp.concatenate(tiles, axis=0).astype(BF16)
    keep_prev = jnp.where((j == 0) | (j == CTX_ROW_TILES), 0.0, 1.0)
    keep_next = jnp.where((j == CTX_ROW_TILES - 1) | (j == ROW_TILES - 1), 0.0, 1.0)
    acc = jnp.zeros((ROW_BATCH * ROW_TILE, D_MODEL), F32)
    for c in range(D_MODEL // cc):
        chans = slice(c * cc, (c + 1) * cc)
        u, bg, cg, g = (jnp.dot(h, wi_sc[i, :, chans], preferred_element_type=F32) for i in range(4))
        p = cg * u
        acts = []
        for bb in range(ROW_BATCH):
            top = bb * rows
            main = slice(top + halo, top + halo + ROW_TILE)
            for s in range(cc // LANES):
                lanes = slice(s * LANES, (s + 1) * LANES)
                p_sc[bb, c, s, 0:halo, :] = p[top:top + halo, lanes] * keep_prev
                p_sc[bb, c, s, halo:halo + ROW_TILE, :] = p[main, lanes]
                p_sc[bb, c, s, halo + ROW_TILE:rows, :] = p[top + halo + ROW_TILE:top + rows, lanes] * keep_next
            before = jnp.concatenate([p_sc[bb, c, s, pl.ds(halo - 1, ROW_TILE, stride=1), :]
                                      for s in range(cc // LANES)], axis=1)
            after = jnp.concatenate([p_sc[bb, c, s, pl.ds(halo + 1, ROW_TILE, stride=1), :]
                                     for s in range(cc // LANES)], axis=1)
            conv = (cw_ref[0:1, chans] * before + cw_ref[1:2, chans] * p[main] + cw_ref[2:3, chans] * after
                    + cb_ref[:, chans])
            acts.append(bg[main] * conv * _silu(g[main]))
        acc = acc + jnp.dot(jnp.concatenate(acts, axis=0).astype(BF16), wo_sc[chans, :],
                            preferred_element_type=F32)
    for bb in range(ROW_BATCH):
        o_ref[bb] = x_ref[bb] + mod_ref[bb, 0][2:3, :] * _tile_rows(acc, bb)


_W_OUT_SPEC = _const_spec((D_MODEL, D_MODEL))


def _out_attn(ot, gate, w_out, streams, mod, final_g):
    final = final_g is not None
    off = CTX_ROW_TILES if final else 0
    fg = (final_g if final else jnp.ones((D_MODEL,), F32)).reshape(1, D_MODEL)
    rows = SEQ if final else TOK
    rb = OUT_ROW_BATCH
    x_specs = [_row_spec(D_MODEL, off, rb)] if final else _stream_specs(len(streams) > 1, rb)
    return pl.pallas_call(
        functools.partial(_out_attn_kernel, n_x=len(streams), final=final),
        grid=(BATCH // rb, ROW_TILES - off),
        in_specs=[pl.BlockSpec((rb, Q_DIM, ROW_TILE), lambda b, j: (b, 0, j)), _row_spec(Q_DIM, off, rb),
                  _W_OUT_SPEC] + x_specs + [_mod_spec(off, rb), _const_spec((1, D_MODEL))],
        out_specs=_row_spec(D_MODEL, rb=rb),
        out_shape=jax.ShapeDtypeStruct((BATCH, rows, D_MODEL), F32),
        scratch_shapes=[pltpu.VMEM((D_MODEL, D_MODEL), BF16)],
        compiler_params=_params("arbitrary", "arbitrary"),
        name="out_attn_final" if final else "out_attn",
    )(ot, gate, w_out, *streams, mod, fg)


def _out_lru(hf, hb, gate, w_out, xs, mod):
    return pl.pallas_call(
        _out_lru_kernel,
        grid=(BATCH // OUT_ROW_BATCH, ROW_TILES),
        in_specs=[_row_spec(D_MODEL, rb=OUT_ROW_BATCH)] * 3
                 + [_W_OUT_SPEC, _row_spec(D_MODEL, rb=OUT_ROW_BATCH), _mod_spec(rb=OUT_ROW_BATCH)],
        out_specs=_row_spec(D_MODEL, rb=OUT_ROW_BATCH),
        out_shape=jax.ShapeDtypeStruct((BATCH, TOK, D_MODEL), F32),
        scratch_shapes=[pltpu.VMEM((D_MODEL, D_MODEL), BF16)],
        compiler_params=_params("arbitrary", "arbitrary"),
        name="out_lru",
    )(hf, hb, gate, w_out, xs, mod)


def _conv_layer(xs, mod, norm_g, w_in, conv_w, conv_b, w_out):
    per = ROW_TILE // SUBLANES
    n8 = TOK // SUBLANES
    halo = lambda idx: pl.BlockSpec((ROW_BATCH, SUBLANES, D_MODEL), idx)
    prev = halo(lambda b, j: (b, jnp.maximum(j * per - 1, 0), 0))
    nxt = halo(lambda b, j: (b, jnp.minimum((j + 1) * per, n8 - 1), 0))
    w_blocks = [_const_spec((D_MODEL, D_MODEL), functools.partial(lambda i, b, j: (0, i), i)) for i in range(4)]
    return pl.pallas_call(
        _conv_layer_kernel,
        grid=_ROW_GRID,
        in_specs=[_row_spec(D_MODEL), prev, nxt, _mod_spec(), _const_spec((1, D_MODEL))] + w_blocks
                 + [_const_spec((CONV_K, D_MODEL)), _const_spec((1, D_MODEL)), _W_OUT_SPEC],
        out_specs=_row_spec(D_MODEL),
        out_shape=jax.ShapeDtypeStruct((BATCH, TOK, D_MODEL), F32),
        scratch_shapes=[pltpu.VMEM((ROW_BATCH, D_MODEL // CONV_CHUNK, CONV_CHUNK // LANES,
                                    ROW_TILE + 2 * SUBLANES, LANES), F32),
                        pltpu.VMEM((4, D_MODEL, D_MODEL), BF16), pltpu.VMEM((D_MODEL, D_MODEL), BF16)],
        compiler_params=_params("arbitrary", "arbitrary"),
        name="conv_layer",
    )(xs, xs, xs, mod, norm_g.reshape(1, D_MODEL), w_in, w_in, w_in, w_in, conv_w,
      conv_b.reshape(1, D_MODEL), w_out)


def _rope_tables():
    quarter = HEAD_DIM // 4
    pos = np.arange(SEQ)
    half = HEAD_DIM // 2
    inv = (1.0 / (ROPE_BASE ** (np.arange(0, half, 2, dtype=np.float32) / half))).astype(np.float32)
    zero = np.zeros((SEQ, quarter), np.float32)
    parts_c, parts_a, parts_b = [], [], []
    for axis_pos in ((pos // GRID_W).astype(np.float32), (pos % GRID_W).astype(np.float32)):
        ang = axis_pos[:, None] * inv
        c, s = np.cos(ang), np.sin(ang)
        parts_c += [c, c]
        parts_a += [-s, zero]
        parts_b += [zero, s]

    def table(parts, ctx_value):
        head = np.concatenate(parts, axis=1)
        head = np.concatenate([np.full((CTX_LEN, HEAD_DIM), ctx_value, np.float32), head], axis=0)
        return jnp.asarray(np.tile(head, (1, LANES // HEAD_DIM)).astype(np.float32))

    return table(parts_c, 1.0), table(parts_a, 0.0), table(parts_b, 0.0)


def kernel(x, c, ctx, c_ctx, l0_norm_g, l0_mod_w, l0_mod_b, l0_w_in, l0_w_out, l0_sink, l1_norm_g, l1_mod_w, l1_mod_b, l1_w_in, l1_w_out, l1_conv_w, l1_conv_b, l1_gate_a_w, l1_gate_a_b, l1_gate_x_w, l1_gate_x_b, l1_lambda, l2_norm_g, l2_mod_w, l2_mod_b, l2_w_in, l2_w_out, l2_conv_w, l2_conv_b, l3_norm_g, l3_mod_w, l3_mod_b, l3_w_in, l3_w_out, l3_sink, final_norm_g):
    mods = _modulation(c, c_ctx, (l0_mod_w, l1_mod_w, l2_mod_w, l3_mod_w),
                       (l0_mod_b, l1_mod_b, l2_mod_b, l3_mod_b))
    rope_tabs = _rope_tables()

    qt, k, vt, gate = _in_attn((ctx, x), mods[0], l0_norm_g, l0_w_in, rope_tabs)
    ot = _attention(qt, k, vt, l0_sink, need_ctx=True)
    xs = _out_attn(ot, gate, l0_w_out, (ctx, x), mods[0], None)

    u, gate = _in_plain(xs, mods[1], l1_norm_g, l1_w_in, (F32, BF16), "in_lru")
    hf, hb = _lru_scan(u, l1_conv_w, l1_conv_b, l1_gate_a_w, l1_gate_a_b, l1_gate_x_w, l1_gate_x_b,
                       l1_lambda)
    xs = _out_lru(hf, hb, gate, l1_w_out, xs, mods[1])

    xs = _conv_layer(xs, mods[2], l2_norm_g, l2_w_in, l2_conv_w, l2_conv_b, l2_w_out)

    qt, k, vt, gate = _in_attn((xs,), mods[3], l3_norm_g, l3_w_in, rope_tabs)
    ot = _attention(qt, k, vt, l3_sink, need_ctx=False)
    return _out_attn(ot, gate, l3_w_out, (xs,), mods[3], final_norm_g)
```

```python
import functools
import math

import jax
import jax.numpy as jnp
import numpy as np
from jax import lax
from jax.experimental import pallas as pl
from jax.experimental.pallas import tpu as pltpu

D_MODEL = 1024
BATCH = 8
SEQ = 2048
CTX_LEN = 256
TOK = CTX_LEN + SEQ
GRID_W = 64
EPS = 1e-6
NEG_INF = -1e30
N_HEADS = 16
N_KV_HEADS = 4
N_GROUPS = N_HEADS // N_KV_HEADS
HEAD_DIM = 64
Q_DIM = N_HEADS * HEAD_DIM
KV_DIM = N_KV_HEADS * HEAD_DIM
WINDOW = 128
ROPE_BASE = 10000.0
LRU_BLOCKS = 16
LRU_BLOCK = 64
LRU_C = 8.0
LRU_CONV = 4
CONV_K = 3

LANES = 128
SUBLANES = 8
BF16_SUBLANES = 16
MXU_DIM = 256
VMEM_LIMIT_BYTES = 56 * 1024 * 1024

ROW_TILE = 256
ROW_BATCH = 2
OUT_ROW_BATCH = 4
CTX_ROW_TILES = CTX_LEN // ROW_TILE
ROW_TILES = TOK // ROW_TILE
Q_TILE = WINDOW
CTX_Q_TILES = CTX_LEN // Q_TILE
Q_TILES = TOK // Q_TILE
BAND = 3 * WINDOW
MAX_ROWS = 32
CONV_CHUNK = MXU_DIM

LRU_TL = 256
LRU_WC = MXU_DIM
LRU_PITCH = LRU_TL + 4
LRU_SLABS = LRU_WC // LANES
LRU_CTX_BLKS = CTX_LEN // LRU_TL
LRU_BLKS = TOK // LRU_TL

F32 = jnp.float32
BF16 = jnp.bfloat16
F32_TINY = float(jnp.finfo(jnp.float32).tiny)
LOG2E = math.log2(math.e)


def _params(*sem):
    return pltpu.CompilerParams(dimension_semantics=sem, vmem_limit_bytes=VMEM_LIMIT_BYTES)


def _sigmoid(x):
    return 0.5 * jnp.tanh(0.5 * x) + 0.5


def _silu(x):
    return x * _sigmoid(x)


def _mod_kernel(c_ref, b_ref, *refs):
    w_refs, o_ref = refs[:-1], refs[-1]
    layer = pl.program_id(0) // 3
    s = _silu(c_ref[...])
    s_hi = s.astype(BF16)
    s_lo = (s - s_hi.astype(F32)).astype(BF16)
    for l, w_ref in enumerate(w_refs):
        @pl.when(layer == l)
        def _layer(w_ref=w_ref):
            w = w_ref[...]
            w_hi = w.astype(BF16)
            w_lo = (w - w_hi.astype(F32)).astype(BF16)
            acc = jnp.dot(s_hi, w_hi, preferred_element_type=F32)
            acc += jnp.dot(s_hi, w_lo, preferred_element_type=F32)
            acc += jnp.dot(s_lo, w_hi, preferred_element_type=F32)
            m = acc + b_ref[0]
            o_ref[0, :, 0, :] = jnp.broadcast_to(m[BATCH:BATCH + 1], (BATCH, D_MODEL))
            o_ref[0, :, 1, :] = m[:BATCH]


def _modulation(c, c_ctx, mod_ws, mod_bs):
    n = len(mod_ws)
    cc = jnp.concatenate([c, c_ctx[None], jnp.zeros((2 * SUBLANES - BATCH - 1, D_MODEL), F32)], axis=0)
    rows = cc.shape[0]
    w_spec = lambda l: pl.BlockSpec((D_MODEL, D_MODEL), lambda s: (0, jnp.clip(s - 3 * l, 0, 2)))
    m = pl.pallas_call(
        _mod_kernel,
        grid=(3 * n,),
        in_specs=[pl.BlockSpec((rows, D_MODEL), lambda s: (0, 0)),
                  pl.BlockSpec((1, 1, D_MODEL), lambda s: (s // 3, 0, s % 3))] + [w_spec(l) for l in range(n)],
        out_specs=pl.BlockSpec((1, BATCH, 2, D_MODEL), lambda s: (s // 3, 0, 0, s % 3)),
        out_shape=jax.ShapeDtypeStruct((n, BATCH, 2, 3 * D_MODEL), F32),
        compiler_params=_params("arbitrary"),
        name="modulation",
    )(cc, jnp.stack(mod_bs).reshape(n, 1, 3 * D_MODEL), *mod_ws)
    return m.reshape(n, BATCH, 2, 3, D_MODEL)


def _mod_spec(tile_offset=0, rb=ROW_BATCH):
    def idx(b, j):
        return (b, jnp.where(j + tile_offset < CTX_ROW_TILES, 0, 1), 0, 0)
    return pl.BlockSpec((rb, 1, 3, D_MODEL), idx)


def _row_spec(width, tile_offset=0, rb=ROW_BATCH):
    return pl.BlockSpec((rb, ROW_TILE, width), lambda b, j: (b, j + tile_offset, 0))


def _const_spec(shape, index=None):
    index = index or (lambda b, j: (0,) * len(shape))
    return pl.BlockSpec(shape, index, pipeline_mode=pl.Buffered(1))


def _stream_specs(split, rb=ROW_BATCH):
    if not split:
        return [_row_spec(D_MODEL, rb=rb)]
    blk = (rb, ROW_TILE, D_MODEL)
    return [pl.BlockSpec(blk, lambda b, j: (b, 0, 0)),
            pl.BlockSpec(blk, lambda b, j: (b, jnp.maximum(j - CTX_ROW_TILES, 0), 0))]


def _stream_tile(x_refs, bb):
    if len(x_refs) == 1:
        return x_refs[0][bb]
    ctx_ref, lat_ref = x_refs
    return jnp.where(pl.program_id(1) < CTX_ROW_TILES, ctx_ref[bb], lat_ref[bb])


def _cast_once(w_ref, wb_sc):
    @pl.when((pl.program_id(0) == 0) & (pl.program_id(1) == 0))
    def _cast():
        wb_sc[...] = w_ref[...].astype(BF16)


_ROW_GRID = (BATCH // ROW_BATCH, ROW_TILES)


def _norm_mod(x, g, mod):
    y = x * lax.rsqrt(jnp.mean(x * x, axis=-1, keepdims=True) + EPS) * g
    return y * (1.0 + mod[1:2, :]) + mod[0:1, :]


def _norm_mod_rows(x_refs, g_ref, mod_ref):
    tiles = [_norm_mod(_stream_tile(x_refs, bb), g_ref[...], mod_ref[bb, 0]) for bb in range(ROW_BATCH)]
    return jnp.concatenate(tiles, axis=0).astype(BF16)


def _tile_rows(t, bb):
    return t[bb * ROW_TILE:(bb + 1) * ROW_TILE]


def _in_attn_kernel(*refs, n_x):
    x_refs = refs[:n_x]
    mod_ref, g_ref, w_ref, rc_ref, ra_ref, rb_ref, qt_ref, k_ref, vt_ref, gate_ref, wb_sc = refs[n_x:]
    _cast_once(w_ref, wb_sc)
    z_all = jnp.dot(_norm_mod_rows(x_refs, g_ref, mod_ref), wb_sc[...], preferred_element_type=F32)
    rc, ra, rb = rc_ref[...], ra_ref[...], rb_ref[...]

    def rope(t):
        reps = t.shape[-1] // LANES
        n = t.shape[-1]
        return (t * jnp.tile(rc, (1, reps))
                + pltpu.roll(t, n - HEAD_DIM // 4, 1) * jnp.tile(ra, (1, reps))
                + pltpu.roll(t, HEAD_DIM // 4, 1) * jnp.tile(rb, (1, reps)))

    for bb in range(ROW_BATCH):
        z = _tile_rows(z_all, bb)
        qt_ref[bb] = (rope(z[:, :Q_DIM]) * (LOG2E * HEAD_DIM ** -0.5)).T.astype(BF16)
        k = rope(z[:, Q_DIM:Q_DIM + KV_DIM]).astype(BF16)
        for kh in range(N_KV_HEADS):
            k_ref[bb, kh] = k[:, kh * HEAD_DIM:(kh + 1) * HEAD_DIM]
        vt_ref[bb] = z[:, Q_DIM + KV_DIM:Q_DIM + 2 * KV_DIM].T.astype(BF16)
        gate_ref[bb] = z[:, Q_DIM + 2 * KV_DIM:].astype(gate_ref.dtype)


def _in_attn(streams, mod, norm_g, w_in, rope_tabs):
    n = w_in.shape[1]
    col = lambda height: pl.BlockSpec((ROW_BATCH, height, ROW_TILE), lambda b, j: (b, 0, j))
    tab = pl.BlockSpec((ROW_TILE, LANES), lambda b, j: (j, 0))
    return pl.pallas_call(
        functools.partial(_in_attn_kernel, n_x=len(streams)),
        grid=_ROW_GRID,
        in_specs=_stream_specs(len(streams) > 1)
                 + [_mod_spec(), _const_spec((1, D_MODEL)), _const_spec((D_MODEL, n)), tab, tab, tab],
        out_specs=[col(Q_DIM),
                   pl.BlockSpec((ROW_BATCH, N_KV_HEADS, ROW_TILE, HEAD_DIM), lambda b, j: (b, 0, j, 0)),
                   col(KV_DIM), _row_spec(Q_DIM)],
        out_shape=[jax.ShapeDtypeStruct((BATCH, Q_DIM, TOK), BF16),
                   jax.ShapeDtypeStruct((BATCH, N_KV_HEADS, TOK, HEAD_DIM), BF16),
                   jax.ShapeDtypeStruct((BATCH, KV_DIM, TOK), BF16),
                   jax.ShapeDtypeStruct((BATCH, TOK, Q_DIM), BF16)],
        scratch_shapes=[pltpu.VMEM((D_MODEL, n), BF16)],
        compiler_params=_params("arbitrary", "arbitrary"),
        name="in_attn",
    )(*streams, mod, norm_g.reshape(1, D_MODEL), w_in, *rope_tabs)


def _attn_kernel(sink_ref, qt_ref, k_ref, vt_ref, ot_ref, s_sc, bias_sc, *, first_tile):
    lanes = N_GROUPS * Q_TILE

    key = lax.broadcasted_iota(jnp.int32, (WINDOW, lanes), 0)
    query = lax.broadcasted_iota(jnp.int32, (WINDOW, lanes), 1) & (Q_TILE - 1)
    bias_sc[0] = jnp.where(key >= query, 0.0, NEG_INF)
    bias_sc[1] = jnp.where(key <= query, 0.0, NEG_INF)
    bias_sc[2] = jnp.full((WINDOW, lanes), NEG_INF, F32)

    def aligned(start):
        return start if isinstance(start, int) else pl.multiple_of(start, WINDOW)

    def heads_of(kh):
        return [kh * N_GROUPS + g for g in range(N_GROUPS)]

    def sink_row(kh):
        return jnp.concatenate([jnp.full((1, Q_TILE), sink_ref[h] * LOG2E, F32) for h in heads_of(kh)], axis=1)

    def score_steps(j, kh, chunks, m_box):
        cols = pl.ds(aligned(j * Q_TILE), Q_TILE)
        qt = jnp.concatenate([qt_ref[0, h * HEAD_DIM:(h + 1) * HEAD_DIM, cols] for h in heads_of(kh)], axis=1)
        mx = jnp.broadcast_to(sink_row(kh), (MAX_ROWS, lanes))
        keys = jnp.concatenate([k_ref[0, kh, pl.ds(ks, WINDOW), :] for ks, _ in chunks], axis=0)
        s_all = jnp.dot(keys, qt, preferred_element_type=F32)
        for c, (ks, bias) in enumerate(chunks):
            s = s_all[c * WINDOW:(c + 1) * WINDOW]
            if bias is not None:
                s = s + bias_sc[bias]
            s_sc[kh, c * WINDOW:(c + 1) * WINDOW, :] = s
            mx = jnp.maximum(mx, s.reshape(WINDOW // MAX_ROWS, MAX_ROWS, lanes).max(axis=0))
            yield
        m_box.append(mx.max(axis=0, keepdims=True))

    def value_steps(j, kh, chunks, m_box):
        m = m_box[0]
        acc = jnp.zeros((HEAD_DIM, lanes), F32)
        denom = jnp.zeros((SUBLANES, lanes), F32)
        group = []
        for c, (ks, _) in enumerate(chunks):
            p = jnp.exp2(s_sc[kh, c * WINDOW:(c + 1) * WINDOW, :] - m)
            denom = denom + p.reshape(WINDOW // SUBLANES, SUBLANES, lanes).sum(axis=0)
            group.append((vt_ref[0, kh * HEAD_DIM:(kh + 1) * HEAD_DIM, pl.ds(ks, WINDOW)], p.astype(BF16)))
            if len(group) == MXU_DIM // WINDOW or c == len(chunks) - 1:
                vt = jnp.concatenate([g[0] for g in group], axis=1)
                pp = jnp.concatenate([g[1] for g in group], axis=0)
                acc = acc + jnp.dot(vt, pp, preferred_element_type=F32)
                group = []
            yield
        denom = denom.sum(axis=0, keepdims=True) + jnp.exp2(sink_row(kh) - m)
        o = acc / denom
        cols = pl.ds(aligned((j - first_tile) * Q_TILE), Q_TILE)
        for g, h in enumerate(heads_of(kh)):
            ot_ref[0, h * HEAD_DIM:(h + 1) * HEAD_DIM, cols] = o[:, g * Q_TILE:(g + 1) * Q_TILE].astype(BF16)

    def interleave(*stages):
        stages = list(stages)
        while stages:
            stages = [st for st in stages if next(st, StopIteration) is not StopIteration]

    ctx_chunks = [(c * WINDOW, None) for c in range(CTX_LEN // WINDOW)]

    def latent_chunks(j):
        own = j * Q_TILE
        if isinstance(j, int):
            prev_bias = 2 if j == CTX_Q_TILES else 0
            next_bias = 2 if j == Q_TILES - 1 else 1
            next_start = min(own + WINDOW, TOK - WINDOW)
        else:
            prev_bias = jnp.where(j == CTX_Q_TILES, 2, 0)
            next_bias = jnp.where(j == Q_TILES - 1, 2, 1)
            next_start = jnp.minimum(own + WINDOW, TOK - WINDOW)
        return [(aligned(own - WINDOW), prev_bias), (aligned(own), None),
                (aligned(next_start), next_bias)] + ctx_chunks

    if first_tile == 0:
        units = [(j, kh, ctx_chunks) for j in range(CTX_Q_TILES) for kh in range(N_KV_HEADS)]
        boxes = [[] for _ in units]
        interleave(score_steps(*units[0], boxes[0]))
        for i, unit in enumerate(units):
            stages = [score_steps(*units[i + 1], boxes[i + 1])] if i + 1 < len(units) else []
            interleave(*stages, value_steps(*unit, boxes[i]))

    def latent_tile(j, m_head0, next_j):
        chunks = latent_chunks(j)
        box = [m_head0]
        for kh in range(N_KV_HEADS):
            nxt = []
            stages = []
            if kh + 1 < N_KV_HEADS:
                stages.append(score_steps(j, kh + 1, chunks, nxt))
            elif next_j is not None:
                stages.append(score_steps(next_j, 0, latent_chunks(next_j), nxt))
            interleave(*stages, value_steps(j, kh, chunks, box))
            box = nxt
        return box[0] if box else None

    first_box = []
    interleave(score_steps(CTX_Q_TILES, 0, latent_chunks(CTX_Q_TILES), first_box))
    m_last = lax.fori_loop(CTX_Q_TILES, Q_TILES - 1, lambda j, m: latent_tile(j, m, j + 1), first_box[0])
    latent_tile(Q_TILES - 1, m_last, None)


def _attention(qt, k, vt, sink, need_ctx):
    first_tile = 0 if need_ctx else CTX_Q_TILES
    n_keys = BAND + CTX_LEN
    out_cols = TOK - first_tile * Q_TILE
    return pl.pallas_call(
        functools.partial(_attn_kernel, first_tile=first_tile),
        grid=(BATCH,),
        in_specs=[pl.BlockSpec(memory_space=pltpu.SMEM),
                  pl.BlockSpec((1, Q_DIM, TOK), lambda b: (b, 0, 0)),
                  pl.BlockSpec((1, N_KV_HEADS, TOK, HEAD_DIM), lambda b: (b, 0, 0, 0)),
                  pl.BlockSpec((1, KV_DIM, TOK), lambda b: (b, 0, 0))],
        out_specs=pl.BlockSpec((1, Q_DIM, out_cols), lambda b: (b, 0, 0)),
        out_shape=jax.ShapeDtypeStruct((BATCH, Q_DIM, out_cols), BF16),
        scratch_shapes=[pltpu.VMEM((N_KV_HEADS, n_keys, N_GROUPS * Q_TILE), F32),
                        pltpu.VMEM((3, WINDOW, N_GROUPS * Q_TILE), F32)],
        compiler_params=_params("arbitrary"),
        name="attention",
    )(sink, qt, k, vt)


def _lru_kernel(uf_ref, ub_ref, cw_ref, cb_ref, wa_ref, ba_ref, wx_ref, bx_ref, lam_ref,
                hf_ref, hb_ref, u_sc, a_sc, b_sc, y_sc, h_sc):
    step = pl.program_id(1)
    tl, pitch, halo = LRU_TL, LRU_PITCH, SUBLANES
    segment_start = (step == 0) | (step == LRU_CTX_BLKS)

    @pl.when(segment_start)
    def _zero_halo():
        u_sc[0, :, 0:halo, :] = jnp.zeros((BATCH * LRU_SLABS, halo, LANES), F32)
        u_sc[1, :, halo + tl:, :] = jnp.zeros((BATCH * LRU_SLABS, halo, LANES), F32)

    @pl.when(jnp.logical_not(segment_start))
    def _carry_halo():
        u_sc[0, :, 0:halo, :] = u_sc[0, :, tl:tl + halo, :]
        u_sc[1, :, halo + tl:, :] = u_sc[1, :, halo:2 * halo, :]

    @pl.when(step == 0)
    def _sequence_start():
        h_sc[...] = jnp.zeros_like(h_sc)

    for d, u_ref in ((0, uf_ref), (1, ub_ref)):
        for bi in range(BATCH):
            for s in range(LRU_SLABS):
                u_sc[d, bi * LRU_SLABS + s, halo:halo + tl, :] = u_ref[bi, :, s * LANES:(s + 1) * LANES]

    for d in range(2):
        nl = -lam_ref[d]
        half_decay = (0.5 * LRU_C) * (jnp.maximum(nl, 0.0) + jnp.log1p(jnp.exp(-jnp.abs(nl))))
        for bi in range(BATCH):
            cols = []
            for s in range(LRU_SLABS):
                lanes = slice(s * LANES, (s + 1) * LANES)
                half_x = 0.5 * cb_ref[d][:, lanes]
                for kk in range(LRU_CONV):
                    off = halo + (kk - (LRU_CONV - 1) if d == 0 else (LRU_CONV - 1) - kk)
                    rows = pl.ds(off, tl, stride=1)
                    tap = 0.5 * cw_ref[d][kk:kk + 1, lanes]
                    half_x = half_x + tap * u_sc[d, bi * LRU_SLABS + s, rows, :]
                cols.append(half_x)
            half_x = jnp.concatenate(cols, axis=1)
            xb = half_x.astype(BF16)
            tr = jnp.tanh(jnp.dot(xb, wa_ref[d, 0], preferred_element_type=F32) + ba_ref[d])
            ti = jnp.tanh(jnp.dot(xb, wx_ref[d, 0], preferred_element_type=F32) + bx_ref[d])
            neg_log_a = tr * half_decay + half_decay
            ix = ti * half_x + half_x
            a = jnp.exp(-neg_log_a)
            var = jnp.tanh(neg_log_a) * (a * a + 1.0)
            b = (var * lax.rsqrt(jnp.maximum(var, F32_TINY))) * ix
            for s in range(LRU_SLABS):
                lanes = slice(s * LANES, (s + 1) * LANES)
                rows = pl.ds(bi * pitch, tl, stride=1)
                a_sc[d, s, rows, :] = a[:, lanes]
                b_sc[d, s, rows, :] = b[:, lanes]

    def scan_step(t, carry):
        new = []
        for d in range(2):
            tt = t if d == 0 else tl - 1 - t
            for s in range(LRU_SLABS):
                rows = pl.ds(tt, BATCH, stride=pitch)
                h = a_sc[d, s, rows, :] * carry[d * LRU_SLABS + s] + b_sc[d, s, rows, :]
                y_sc[d, s, rows, :] = h
                new.append(h)
        return tuple(new)

    init = tuple(h_sc[d, s] for d in range(2) for s in range(LRU_SLABS))
    final = lax.fori_loop(0, tl, scan_step, init, unroll=8)
    for d in range(2):
        for s in range(LRU_SLABS):
            h_sc[d, s] = final[d * LRU_SLABS + s]

    for d, o_ref in ((0, hf_ref), (1, hb_ref)):
        for bi in range(BATCH):
            for s in range(LRU_SLABS):
                o_ref[bi, :, s * LANES:(s + 1) * LANES] = (
                    y_sc[d, s, pl.ds(bi * pitch, tl, stride=1), :].astype(o_ref.dtype))


def _block_diag_chunks(w):
    per = MXU_DIM // LRU_BLOCK
    w = w.reshape(2, LRU_BLOCKS // per, per, LRU_BLOCK, LRU_BLOCK)
    eye = jnp.eye(per, dtype=w.dtype)
    bd = jnp.einsum('dcpij,pq->dcpiqj', w, eye)
    return bd.reshape(2, LRU_BLOCKS // per, MXU_DIM, MXU_DIM).astype(BF16)


def _lru_scan(u, conv_w, conv_b, gate_a_w, gate_a_b, gate_x_w, gate_x_b, lam):
    def fwd_blk(w, s):
        return (0, s, w)

    def bwd_blk(w, s):
        blk = jnp.where(s < LRU_CTX_BLKS, LRU_CTX_BLKS - 1 - s, LRU_BLKS - 1 - (s - LRU_CTX_BLKS))
        return (0, blk, w)

    blk = (BATCH, LRU_TL, LRU_WC)
    vec = pl.BlockSpec((2, 1, LRU_WC), lambda w, s: (0, 0, w))
    gate = pl.BlockSpec((2, LRU_WC // MXU_DIM, MXU_DIM, MXU_DIM), lambda w, s: (0, w, 0, 0))
    width = u.shape[-1]
    out = jax.ShapeDtypeStruct((BATCH, TOK, width), BF16)
    scan_buf = pltpu.VMEM((2, LRU_SLABS, BATCH * LRU_PITCH, LANES), F32)
    return pl.pallas_call(
        _lru_kernel,
        grid=(width // LRU_WC, LRU_BLKS),
        in_specs=[pl.BlockSpec(blk, fwd_blk), pl.BlockSpec(blk, bwd_blk),
                  pl.BlockSpec((2, LRU_CONV, LRU_WC), lambda w, s: (0, 0, w)), vec,
                  gate, vec, gate, vec, vec],
        out_specs=[pl.BlockSpec(blk, fwd_blk), pl.BlockSpec(blk, bwd_blk)],
        out_shape=[out, out],
        scratch_shapes=[pltpu.VMEM((2, BATCH * LRU_SLABS, LRU_TL + 2 * SUBLANES, LANES), F32),
                        scan_buf, scan_buf, scan_buf,
                        pltpu.VMEM((2, LRU_SLABS, BATCH, LANES), F32)],
        compiler_params=_params("parallel", "arbitrary"),
        name="lru_scan",
    )(u, u, conv_w, conv_b.reshape(2, 1, width),
      _block_diag_chunks(gate_a_w), 0.5 * gate_a_b.reshape(2, 1, width),
      _block_diag_chunks(gate_x_w), 0.5 * gate_x_b.reshape(2, 1, width),
      lam.reshape(2, 1, width))


def _residual_rows(x_refs, mod_ref, acts, wb_sc, o_ref, post=None):
    y = jnp.dot(jnp.concatenate(acts, axis=0).astype(BF16), wb_sc[...], preferred_element_type=F32)
    for bb in range(len(acts)):
        out = _stream_tile(x_refs, bb) + mod_ref[bb, 0][2:3, :] * _tile_rows(y, bb)
        o_ref[bb] = out if post is None else post(out)


def _out_attn_final_kernel(ot_ref, g_ref, w_ref, x_ref, mod_ref, fg_ref, o_ref, wb_sc):
    _cast_once(w_ref, wb_sc)
    acts = [ot_ref[bb].astype(F32).T * _silu(g_ref[bb].astype(F32)) for bb in range(OUT_ROW_BATCH)]

    def final_norm(x):
        return x * lax.rsqrt(jnp.mean(x * x, axis=-1, keepdims=True) + EPS) * fg_ref[...]

    _residual_rows((x_ref,), mod_ref, acts, wb_sc, o_ref, final_norm)


def _out_attn_in_next_kernel(ot_ref, g_ref, wo_ref, ctx_ref, lat_ref, mod_ref, ng_ref, nmod_ref, wi_ref,
                             xs_ref, *refs):
    o_refs, (wo_sc, wi_sc) = refs[:-2], refs[-2:]
    _cast_once(wo_ref, wo_sc)
    _cast_once(wi_ref, wi_sc)
    acts = [ot_ref[bb].astype(F32).T * _silu(g_ref[bb].astype(F32)) for bb in range(ROW_BATCH)]
    y = jnp.dot(jnp.concatenate(acts, axis=0).astype(BF16), wo_sc[...], preferred_element_type=F32)
    tiles = []
    for bb in range(ROW_BATCH):
        out = _stream_tile((ctx_ref, lat_ref), bb) + mod_ref[bb, 0][2:3, :] * _tile_rows(y, bb)
        xs_ref[bb] = out
        tiles.append(_norm_mod(out, ng_ref[...], nmod_ref[bb, 0]))
    z = jnp.dot(jnp.concatenate(tiles, axis=0).astype(BF16), wi_sc[...], preferred_element_type=F32)
    width = z.shape[-1] // len(o_refs)
    for i, o_ref in enumerate(o_refs):
        for bb in range(ROW_BATCH):
            o_ref[bb] = _tile_rows(z, bb)[:, i * width:(i + 1) * width].astype(o_ref.dtype)


def _halo_rows(main_ref, prev_ref, next_ref, bb):
    return jnp.concatenate([prev_ref[bb], main_ref[bb], next_ref[bb]], axis=0).astype(F32)


def _out_lru_conv_kernel(hf_ref, hfp_ref, hfn_ref, hb_ref, hbp_ref, hbn_ref, gl_ref, glp_ref, gln_ref,
                         x_ref, xp_ref, xn_ref, lmod_ref, lwo_ref, mod_ref, g_ref, wu_ref, wb_ref, wc_ref,
                         wg_ref, cw_ref, cb_ref, wo_ref, o_ref, p_sc, lwo_sc, wo_sc):
    j = pl.program_id(1)
    halo, cc = SUBLANES, CONV_CHUNK
    rows = ROW_TILE + 2 * halo
    _cast_once(lwo_ref, lwo_sc)
    _cast_once(wo_ref, wo_sc)
    acts = [(_halo_rows(hf_ref, hfp_ref, hfn_ref, bb) + _halo_rows(hb_ref, hbp_ref, hbn_ref, bb))
            * _silu(_halo_rows(gl_ref, glp_ref, gln_ref, bb)) for bb in range(ROW_BATCH)]
    y = jnp.dot(jnp.concatenate(acts, axis=0).astype(BF16), lwo_sc[...], preferred_element_type=F32)
    x1 = [_halo_rows(x_ref, xp_ref, xn_ref, bb) + lmod_ref[bb, 0][2:3, :] * y[bb * rows:(bb + 1) * rows]
          for bb in range(ROW_BATCH)]
    tiles = [_norm_mod(x1[bb], g_ref[...], mod_ref[bb, 0]) for bb in range(ROW_BATCH)]
    h = jnp.concatenate(tiles, axis=0).astype(BF16)
    keep_prev = jnp.where((j == 0) | (j == CTX_ROW_TILES), 0.0, 1.0)
    keep_next = jnp.where((j == CTX_ROW_TILES - 1) | (j == ROW_TILES - 1), 0.0, 1.0)
    acc = jnp.zeros((ROW_BATCH * ROW_TILE, D_MODEL), F32)
    for c in range(D_MODEL // cc):
        chans = slice(c * cc, (c + 1) * cc)
        u, bg, cg, g = (jnp.dot(h, w_ref[:, chans], preferred_element_type=F32)
                        for w_ref in (wu_ref, wb_ref, wc_ref, wg_ref))
        p = cg * u
        acts = []
        for bb in range(ROW_BATCH):
            top = bb * rows
            main = slice(top + halo, top + halo + ROW_TILE)
            for s in range(cc // LANES):
                lanes = slice(s * LANES, (s + 1) * LANES)
                p_sc[bb, c, s, 0:halo, :] = p[top:top + halo, lanes] * keep_prev
                p_sc[bb, c, s, halo:halo + ROW_TILE, :] = p[main, lanes]
                p_sc[bb, c, s, halo + ROW_TILE:rows, :] = p[top + halo + ROW_TILE:top + rows, lanes] * keep_next
            before = jnp.concatenate([p_sc[bb, c, s, pl.ds(halo - 1, ROW_TILE, stride=1), :]
                                      for s in range(cc // LANES)], axis=1)
            after = jnp.concatenate([p_sc[bb, c, s, pl.ds(halo + 1, ROW_TILE, stride=1), :]
                                     for s in range(cc // LANES)], axis=1)
            conv = (cw_ref[0:1, chans] * before + cw_ref[1:2, chans] * p[main] + cw_ref[2:3, chans] * after
                    + cb_ref[:, chans])
            acts.append(bg[main] * conv * _silu(g[main]))
        acc = acc + jnp.dot(jnp.concatenate(acts, axis=0).astype(BF16), wo_sc[chans, :],
                            preferred_element_type=F32)
    for bb in range(ROW_BATCH):
        o_ref[bb] = x1[bb][halo:halo + ROW_TILE] + mod_ref[bb, 0][2:3, :] * _tile_rows(acc, bb)


_W_OUT_SPEC = _const_spec((D_MODEL, D_MODEL))


def _out_attn_final(ot, gate, w_out, xs, mod, final_g):
    off, rb = CTX_ROW_TILES, OUT_ROW_BATCH
    return pl.pallas_call(
        _out_attn_final_kernel,
        grid=(BATCH // rb, ROW_TILES - off),
        in_specs=[pl.BlockSpec((rb, Q_DIM, ROW_TILE), lambda b, j: (b, 0, j)), _row_spec(Q_DIM, off, rb),
                  _W_OUT_SPEC, _row_spec(D_MODEL, off, rb), _mod_spec(off, rb), _const_spec((1, D_MODEL))],
        out_specs=_row_spec(D_MODEL, rb=rb),
        out_shape=jax.ShapeDtypeStruct((BATCH, SEQ, D_MODEL), F32),
        scratch_shapes=[pltpu.VMEM((D_MODEL, D_MODEL), BF16)],
        compiler_params=_params("arbitrary", "arbitrary"),
        name="out_attn_final",
    )(ot, gate, w_out, xs, mod, final_g.reshape(1, D_MODEL))


def _out_attn_in_next(ot, gate, w_out, ctx, x, mod, next_norm_g, next_mod, next_w_in, dtypes):
    n = next_w_in.shape[1]
    width = n // len(dtypes)
    return pl.pallas_call(
        _out_attn_in_next_kernel,
        grid=_ROW_GRID,
        in_specs=[pl.BlockSpec((ROW_BATCH, Q_DIM, ROW_TILE), lambda b, j: (b, 0, j)), _row_spec(Q_DIM),
                  _W_OUT_SPEC] + _stream_specs(True)
                 + [_mod_spec(), _const_spec((1, D_MODEL)), _mod_spec(), _const_spec((D_MODEL, n))],
        out_specs=[_row_spec(D_MODEL)] + [_row_spec(width)] * len(dtypes),
        out_shape=[jax.ShapeDtypeStruct((BATCH, TOK, D_MODEL), F32)]
                  + [jax.ShapeDtypeStruct((BATCH, TOK, width), dt) for dt in dtypes],
        scratch_shapes=[pltpu.VMEM((D_MODEL, D_MODEL), BF16), pltpu.VMEM((D_MODEL, n), BF16)],
        compiler_params=_params("arbitrary", "arbitrary"),
        name="out_attn_in_lru",
    )(ot, gate, w_out, ctx, x, mod, next_norm_g.reshape(1, D_MODEL), next_mod, next_w_in)


def _out_lru_conv(hf, hb, gate, lru_w_out, xs, lru_mod, mod, norm_g, w_in, conv_w, conv_b, w_out):
    per = ROW_TILE // SUBLANES
    n8 = TOK // SUBLANES
    halo = lambda idx: pl.BlockSpec((ROW_BATCH, SUBLANES, D_MODEL), idx)
    prev = halo(lambda b, j: (b, jnp.maximum(j * per - 1, 0), 0))
    nxt = halo(lambda b, j: (b, jnp.minimum((j + 1) * per, n8 - 1), 0))
    with_halo = [_row_spec(D_MODEL), prev, nxt]
    wi = w_in.astype(BF16)
    w_blocks = [_const_spec((D_MODEL, D_MODEL), functools.partial(lambda i, b, j: (0, i), i)) for i in range(4)]
    return pl.pallas_call(
        _out_lru_conv_kernel,
        grid=_ROW_GRID,
        in_specs=with_halo * 4 + [_mod_spec(), _W_OUT_SPEC, _mod_spec(), _const_spec((1, D_MODEL))] + w_blocks
                 + [_const_spec((CONV_K, D_MODEL)), _const_spec((1, D_MODEL)), _W_OUT_SPEC],
        out_specs=_row_spec(D_MODEL),
        out_shape=jax.ShapeDtypeStruct((BATCH, TOK, D_MODEL), F32),
        scratch_shapes=[pltpu.VMEM((ROW_BATCH, D_MODEL // CONV_CHUNK, CONV_CHUNK // LANES,
                                    ROW_TILE + 2 * SUBLANES, LANES), F32),
                        pltpu.VMEM((D_MODEL, D_MODEL), BF16), pltpu.VMEM((D_MODEL, D_MODEL), BF16)],
        compiler_params=_params("arbitrary", "arbitrary"),
        name="out_lru_conv",
    )(hf, hf, hf, hb, hb, hb, gate, gate, gate, xs, xs, xs, lru_mod, lru_w_out, mod,
      norm_g.reshape(1, D_MODEL), wi, wi, wi, wi, conv_w, conv_b.reshape(1, D_MODEL), w_out)


def _rope_tables():
    quarter = HEAD_DIM // 4
    pos = np.arange(SEQ)
    half = HEAD_DIM // 2
    inv = (1.0 / (ROPE_BASE ** (np.arange(0, half, 2, dtype=np.float32) / half))).astype(np.float32)
    zero = np.zeros((SEQ, quarter), np.float32)
    parts_c, parts_a, parts_b = [], [], []
    for axis_pos in ((pos // GRID_W).astype(np.float32), (pos % GRID_W).astype(np.float32)):
        ang = axis_pos[:, None] * inv
        c, s = np.cos(ang), np.sin(ang)
        parts_c += [c, c]
        parts_a += [-s, zero]
        parts_b += [zero, s]

    def table(parts, ctx_value):
        head = np.concatenate(parts, axis=1)
        head = np.concatenate([np.full((CTX_LEN, HEAD_DIM), ctx_value, np.float32), head], axis=0)
        return jnp.asarray(np.tile(head, (1, LANES // HEAD_DIM)).astype(np.float32))

    return table(parts_c, 1.0), table(parts_a, 0.0), table(parts_b, 0.0)


def kernel(x, c, ctx, c_ctx, l0_norm_g, l0_mod_w, l0_mod_b, l0_w_in, l0_w_out, l0_sink, l1_norm_g, l1_mod_w, l1_mod_b, l1_w_in, l1_w_out, l1_conv_w, l1_conv_b, l1_gate_a_w, l1_gate_a_b, l1_gate_x_w, l1_gate_x_b, l1_lambda, l2_norm_g, l2_mod_w, l2_mod_b, l2_w_in, l2_w_out, l2_conv_w, l2_conv_b, l3_norm_g, l3_mod_w, l3_mod_b, l3_w_in, l3_w_out, l3_sink, final_norm_g):
    mods = _modulation(c, c_ctx, (l0_mod_w, l1_mod_w, l2_mod_w, l3_mod_w),
                       (l0_mod_b, l1_mod_b, l2_mod_b, l3_mod_b))
    rope_tabs = _rope_tables()

    qt, k, vt, gate = _in_attn((ctx, x), mods[0], l0_norm_g, l0_w_in, rope_tabs)
    ot = _attention(qt, k, vt, l0_sink, need_ctx=True)
    xs, u, gate = _out_attn_in_next(ot, gate, l0_w_out, ctx, x, mods[0], l1_norm_g, mods[1], l1_w_in,
                                    (F32, BF16))

    hf, hb = _lru_scan(u, l1_conv_w, l1_conv_b, l1_gate_a_w, l1_gate_a_b, l1_gate_x_w, l1_gate_x_b,
                       l1_lambda)
    xs = _out_lru_conv(hf, hb, gate, l1_w_out, xs, mods[1], mods[2], l2_norm_g, l2_w_in, l2_conv_w,
                       l2_conv_b, l2_w_out)

    qt, k, vt, gate = _in_attn((xs,), mods[3], l3_norm_g, l3_w_in, rope_tabs)
    ot = _attention(qt, k, vt, l3_sink, need_ctx=False)
    return _out_attn_final(ot, gate, l3_w_out, xs, mods[3], final_norm_g)
```

```python
import functools
import math

import jax
import jax.numpy as jnp
import numpy as np
from jax import lax
from jax.experimental import pallas as pl
from jax.experimental.pallas import tpu as pltpu

D_MODEL = 1024
BATCH = 8
SEQ = 2048
CTX_LEN = 256
TOK = CTX_LEN + SEQ
GRID_W = 64
EPS = 1e-6
NEG_INF = -1e30
N_HEADS = 16
N_KV_HEADS = 4
N_GROUPS = N_HEADS // N_KV_HEADS
HEAD_DIM = 64
Q_DIM = N_HEADS * HEAD_DIM
KV_DIM = N_KV_HEADS * HEAD_DIM
WINDOW = 128
ROPE_BASE = 10000.0
LRU_BLOCKS = 16
LRU_BLOCK = 64
LRU_C = 8.0
LRU_CONV = 4
CONV_K = 3

LANES = 128
SUBLANES = 8
MXU_DIM = 256
VMEM_LIMIT_BYTES = 56 * 1024 * 1024

ROW_TILE = 256
ROW_BATCH = 2
OUT_ROW_BATCH = 4
CTX_ROW_TILES = CTX_LEN // ROW_TILE
ROW_TILES = TOK // ROW_TILE
Q_TILE = WINDOW
CTX_Q_TILES = CTX_LEN // Q_TILE
Q_TILES = TOK // Q_TILE
BAND = 3 * WINDOW
MAX_ROWS = 32
CONV_CHUNK = MXU_DIM

LRU_TL = 256
LRU_WC = MXU_DIM
LRU_PITCH = LRU_TL + 4
LRU_SLABS = LRU_WC // LANES
LRU_CTX_BLKS = CTX_LEN // LRU_TL
LRU_BLKS = TOK // LRU_TL

F32 = jnp.float32
BF16 = jnp.bfloat16
F32_TINY = float(jnp.finfo(jnp.float32).tiny)
LOG2E = math.log2(math.e)


def _params(*sem):
    return pltpu.CompilerParams(dimension_semantics=sem, vmem_limit_bytes=VMEM_LIMIT_BYTES)


def _sigmoid(x):
    return 0.5 * jnp.tanh(0.5 * x) + 0.5


def _silu(x):
    return x * _sigmoid(x)


def _mod_kernel(c_ref, b_ref, *refs):
    w_refs, o_ref = refs[:-1], refs[-1]
    layer = pl.program_id(0) // 3
    s = _silu(c_ref[...])
    s_hi = s.astype(BF16)
    s_lo = (s - s_hi.astype(F32)).astype(BF16)
    for l, w_ref in enumerate(w_refs):
        @pl.when(layer == l)
        def _layer(w_ref=w_ref):
            w = w_ref[...]
            w_hi = w.astype(BF16)
            w_lo = (w - w_hi.astype(F32)).astype(BF16)
            acc = jnp.dot(s_hi, w_hi, preferred_element_type=F32)
            acc += jnp.dot(s_hi, w_lo, preferred_element_type=F32)
            acc += jnp.dot(s_lo, w_hi, preferred_element_type=F32)
            m = acc + b_ref[0]
            o_ref[0, :, 0, :] = jnp.broadcast_to(m[BATCH:BATCH + 1], (BATCH, D_MODEL))
            o_ref[0, :, 1, :] = m[:BATCH]


def _modulation(c, c_ctx, mod_ws, mod_bs):
    n = len(mod_ws)
    cc = jnp.concatenate([c, c_ctx[None], jnp.zeros((2 * SUBLANES - BATCH - 1, D_MODEL), F32)], axis=0)
    rows = cc.shape[0]
    w_spec = lambda l: pl.BlockSpec((D_MODEL, D_MODEL), lambda s: (0, jnp.clip(s - 3 * l, 0, 2)))
    m = pl.pallas_call(
        _mod_kernel,
        grid=(3 * n,),
        in_specs=[pl.BlockSpec((rows, D_MODEL), lambda s: (0, 0)),
                  pl.BlockSpec((1, 1, D_MODEL), lambda s: (s // 3, 0, s % 3))] + [w_spec(l) for l in range(n)],
        out_specs=pl.BlockSpec((1, BATCH, 2, D_MODEL), lambda s: (s // 3, 0, 0, s % 3)),
        out_shape=jax.ShapeDtypeStruct((n, BATCH, 2, 3 * D_MODEL), F32),
        compiler_params=_params("arbitrary"),
        name="modulation",
    )(cc, jnp.stack(mod_bs).reshape(n, 1, 3 * D_MODEL), *mod_ws)
    return m.reshape(n, BATCH, 2, 3, D_MODEL)


def _mod_spec(tile_offset=0, rb=ROW_BATCH):
    def idx(b, j):
        return (b, jnp.where(j + tile_offset < CTX_ROW_TILES, 0, 1), 0, 0)
    return pl.BlockSpec((rb, 1, 3, D_MODEL), idx)


def _row_spec(width, tile_offset=0, rb=ROW_BATCH):
    return pl.BlockSpec((rb, ROW_TILE, width), lambda b, j: (b, j + tile_offset, 0))


def _const_spec(shape, index=None):
    index = index or (lambda b, j: (0,) * len(shape))
    return pl.BlockSpec(shape, index, pipeline_mode=pl.Buffered(1))


def _stream_specs(split, rb=ROW_BATCH):
    if not split:
        return [_row_spec(D_MODEL, rb=rb)]
    blk = (rb, ROW_TILE, D_MODEL)
    return [pl.BlockSpec(blk, lambda b, j: (b, 0, 0)),
            pl.BlockSpec(blk, lambda b, j: (b, jnp.maximum(j - CTX_ROW_TILES, 0), 0))]


def _stream_tile(x_refs, bb):
    if len(x_refs) == 1:
        return x_refs[0][bb]
    ctx_ref, lat_ref = x_refs
    return jnp.where(pl.program_id(1) < CTX_ROW_TILES, ctx_ref[bb], lat_ref[bb])


def _cast_once(w_ref, wb_sc):
    @pl.when((pl.program_id(0) == 0) & (pl.program_id(1) == 0))
    def _cast():
        wb_sc[...] = w_ref[...].astype(BF16)


_ROW_GRID = (BATCH // ROW_BATCH, ROW_TILES)


def _norm_mod(x, g, mod):
    y = x * lax.rsqrt(jnp.mean(x * x, axis=-1, keepdims=True) + EPS) * g
    return y * (1.0 + mod[1:2, :]) + mod[0:1, :]


def _norm_mod_rows(x_refs, g_ref, mod_ref):
    tiles = [_norm_mod(_stream_tile(x_refs, bb), g_ref[...], mod_ref[bb, 0]) for bb in range(ROW_BATCH)]
    return jnp.concatenate(tiles, axis=0).astype(BF16)


def _tile_rows(t, bb):
    return t[bb * ROW_TILE:(bb + 1) * ROW_TILE]


def _in_attn_kernel(*refs, n_x, ctx_queries):
    x_refs = refs[:n_x]
    mod_ref, g_ref, w_ref, rc_ref, ra_ref, rb_ref, qt_ref, k_ref, vt_ref, gate_ref, wb_sc = refs[n_x:]
    _cast_once(w_ref, wb_sc)
    rc, ra, rb = rc_ref[...], ra_ref[...], rb_ref[...]

    def rope(t):
        reps = t.shape[-1] // LANES
        n = t.shape[-1]
        return (t * jnp.tile(rc, (1, reps))
                + pltpu.roll(t, n - HEAD_DIM // 4, 1) * jnp.tile(ra, (1, reps))
                + pltpu.roll(t, HEAD_DIM // 4, 1) * jnp.tile(rb, (1, reps)))

    def keys_values(z_kv, bb):
        k = rope(z_kv[:, :KV_DIM]).astype(BF16)
        for kh in range(N_KV_HEADS):
            k_ref[bb, kh] = k[:, kh * HEAD_DIM:(kh + 1) * HEAD_DIM]
        vt_ref[bb] = z_kv[:, KV_DIM:].T.astype(BF16)

    def all_columns():
        z_all = jnp.dot(_norm_mod_rows(x_refs, g_ref, mod_ref), wb_sc[...], preferred_element_type=F32)
        for bb in range(ROW_BATCH):
            z = _tile_rows(z_all, bb)
            qt_ref[bb] = (rope(z[:, :Q_DIM]) * (LOG2E * HEAD_DIM ** -0.5)).T.astype(BF16)
            keys_values(z[:, Q_DIM:Q_DIM + 2 * KV_DIM], bb)
            gate_ref[bb] = z[:, Q_DIM + 2 * KV_DIM:].astype(gate_ref.dtype)

    def keys_values_only():
        z_kv = jnp.dot(_norm_mod_rows(x_refs, g_ref, mod_ref), wb_sc[:, Q_DIM:Q_DIM + 2 * KV_DIM],
                       preferred_element_type=F32)
        for bb in range(ROW_BATCH):
            keys_values(_tile_rows(z_kv, bb), bb)
        qt_ref[...] = jnp.zeros_like(qt_ref)
        gate_ref[...] = jnp.zeros_like(gate_ref)

    if ctx_queries:
        all_columns()
    else:
        pl.when(pl.program_id(1) >= CTX_ROW_TILES)(all_columns)
        pl.when(pl.program_id(1) < CTX_ROW_TILES)(keys_values_only)


def _in_attn(streams, mod, norm_g, w_in, rope_tabs, ctx_queries):
    n = w_in.shape[1]
    col = lambda height: pl.BlockSpec((ROW_BATCH, height, ROW_TILE), lambda b, j: (b, 0, j))
    tab = pl.BlockSpec((ROW_TILE, LANES), lambda b, j: (j, 0))
    return pl.pallas_call(
        functools.partial(_in_attn_kernel, n_x=len(streams), ctx_queries=ctx_queries),
        grid=_ROW_GRID,
        in_specs=_stream_specs(len(streams) > 1)
                 + [_mod_spec(), _const_spec((1, D_MODEL)), _const_spec((D_MODEL, n)), tab, tab, tab],
        out_specs=[col(Q_DIM),
                   pl.BlockSpec((ROW_BATCH, N_KV_HEADS, ROW_TILE, HEAD_DIM), lambda b, j: (b, 0, j, 0)),
                   col(KV_DIM), _row_spec(Q_DIM)],
        out_shape=[jax.ShapeDtypeStruct((BATCH, Q_DIM, TOK), BF16),
                   jax.ShapeDtypeStruct((BATCH, N_KV_HEADS, TOK, HEAD_DIM), BF16),
                   jax.ShapeDtypeStruct((BATCH, KV_DIM, TOK), BF16),
                   jax.ShapeDtypeStruct((BATCH, TOK, Q_DIM), BF16)],
        scratch_shapes=[pltpu.VMEM((D_MODEL, n), BF16)],
        compiler_params=_params("arbitrary", "arbitrary"),
        name="in_attn",
    )(*streams, mod, norm_g.reshape(1, D_MODEL), w_in, *rope_tabs)


def _attn_kernel(sink_ref, qt_ref, k_ref, vt_ref, ot_ref, s_sc, bias_sc, *, first_tile):
    lanes = N_GROUPS * Q_TILE

    key = lax.broadcasted_iota(jnp.int32, (WINDOW, lanes), 0)
    query = lax.broadcasted_iota(jnp.int32, (WINDOW, lanes), 1) & (Q_TILE - 1)
    bias_sc[0] = jnp.where(key >= query, 0.0, NEG_INF)
    bias_sc[1] = jnp.where(key <= query, 0.0, NEG_INF)
    bias_sc[2] = jnp.full((WINDOW, lanes), NEG_INF, F32)

    def aligned(start):
        return start if isinstance(start, int) else pl.multiple_of(start, WINDOW)

    def heads_of(kh):
        return [kh * N_GROUPS + g for g in range(N_GROUPS)]

    def sink_row(kh):
        return jnp.concatenate([jnp.full((1, Q_TILE), sink_ref[h] * LOG2E, F32) for h in heads_of(kh)], axis=1)

    def score_steps(j, kh, chunks, m_box):
        cols = pl.ds(aligned(j * Q_TILE), Q_TILE)
        qt = jnp.concatenate([qt_ref[0, h * HEAD_DIM:(h + 1) * HEAD_DIM, cols] for h in heads_of(kh)], axis=1)
        mx = jnp.broadcast_to(sink_row(kh), (MAX_ROWS, lanes))
        keys = jnp.concatenate([k_ref[0, kh, pl.ds(ks, WINDOW), :] for ks, _ in chunks], axis=0)
        s_all = jnp.dot(keys, qt, preferred_element_type=F32)
        for c, (ks, bias) in enumerate(chunks):
            s = s_all[c * WINDOW:(c + 1) * WINDOW]
            if bias is not None:
                s = s + bias_sc[bias]
            s_sc[kh, c * WINDOW:(c + 1) * WINDOW, :] = s
            mx = jnp.maximum(mx, s.reshape(WINDOW // MAX_ROWS, MAX_ROWS, lanes).max(axis=0))
            yield
        m_box.append(mx.max(axis=0, keepdims=True))

    def value_steps(j, kh, chunks, m_box):
        m = m_box[0]
        acc = jnp.zeros((HEAD_DIM, lanes), F32)
        denom = jnp.zeros((SUBLANES, lanes), F32)
        group = []
        for c, (ks, _) in enumerate(chunks):
            p = jnp.exp2(s_sc[kh, c * WINDOW:(c + 1) * WINDOW, :] - m)
            denom = denom + p.reshape(WINDOW // SUBLANES, SUBLANES, lanes).sum(axis=0)
            group.append((vt_ref[0, kh * HEAD_DIM:(kh + 1) * HEAD_DIM, pl.ds(ks, WINDOW)], p.astype(BF16)))
            if len(group) == MXU_DIM // WINDOW or c == len(chunks) - 1:
                vt = jnp.concatenate([g[0] for g in group], axis=1)
                pp = jnp.concatenate([g[1] for g in group], axis=0)
                acc = acc + jnp.dot(vt, pp, preferred_element_type=F32)
                group = []
            yield
        denom = denom.sum(axis=0, keepdims=True) + jnp.exp2(sink_row(kh) - m)
        o = acc / denom
        cols = pl.ds(aligned((j - first_tile) * Q_TILE), Q_TILE)
        for g, h in enumerate(heads_of(kh)):
            ot_ref[0, h * HEAD_DIM:(h + 1) * HEAD_DIM, cols] = o[:, g * Q_TILE:(g + 1) * Q_TILE].astype(BF16)

    def interleave(*stages):
        stages = list(stages)
        while stages:
            stages = [st for st in stages if next(st, StopIteration) is not StopIteration]

    ctx_chunks = [(c * WINDOW, None) for c in range(CTX_LEN // WINDOW)]

    def latent_chunks(j):
        own = j * Q_TILE
        if isinstance(j, int):
            prev_bias = 2 if j == CTX_Q_TILES else 0
            next_bias = 2 if j == Q_TILES - 1 else 1
            next_start = min(own + WINDOW, TOK - WINDOW)
        else:
            prev_bias = jnp.where(j == CTX_Q_TILES, 2, 0)
            next_bias = jnp.where(j == Q_TILES - 1, 2, 1)
            next_start = jnp.minimum(own + WINDOW, TOK - WINDOW)
        return [(aligned(own - WINDOW), prev_bias), (aligned(own), None),
                (aligned(next_start), next_bias)] + ctx_chunks

    if first_tile == 0:
        units = [(j, kh, ctx_chunks) for j in range(CTX_Q_TILES) for kh in range(N_KV_HEADS)]
        boxes = [[] for _ in units]
        interleave(score_steps(*units[0], boxes[0]))
        for i, unit in enumerate(units):
            stages = [score_steps(*units[i + 1], boxes[i + 1])] if i + 1 < len(units) else []
            interleave(*stages, value_steps(*unit, boxes[i]))

    def latent_tile(j, m_head0, next_j):
        chunks = latent_chunks(j)
        box = [m_head0]
        for kh in range(N_KV_HEADS):
            nxt = []
            stages = []
            if kh + 1 < N_KV_HEADS:
                stages.append(score_steps(j, kh + 1, chunks, nxt))
            elif next_j is not None:
                stages.append(score_steps(next_j, 0, latent_chunks(next_j), nxt))
            interleave(*stages, value_steps(j, kh, chunks, box))
            box = nxt
        return box[0] if box else None

    first_box = []
    interleave(score_steps(CTX_Q_TILES, 0, latent_chunks(CTX_Q_TILES), first_box))
    m_last = lax.fori_loop(CTX_Q_TILES, Q_TILES - 1, lambda j, m: latent_tile(j, m, j + 1), first_box[0])
    latent_tile(Q_TILES - 1, m_last, None)


def _attention(qt, k, vt, sink, need_ctx):
    first_tile = 0 if need_ctx else CTX_Q_TILES
    n_keys = BAND + CTX_LEN
    out_cols = TOK - first_tile * Q_TILE
    return pl.pallas_call(
        functools.partial(_attn_kernel, first_tile=first_tile),
        grid=(BATCH,),
        in_specs=[pl.BlockSpec(memory_space=pltpu.SMEM),
                  pl.BlockSpec((1, Q_DIM, TOK), lambda b: (b, 0, 0)),
                  pl.BlockSpec((1, N_KV_HEADS, TOK, HEAD_DIM), lambda b: (b, 0, 0, 0)),
                  pl.BlockSpec((1, KV_DIM, TOK), lambda b: (b, 0, 0))],
        out_specs=pl.BlockSpec((1, Q_DIM, out_cols), lambda b: (b, 0, 0)),
        out_shape=jax.ShapeDtypeStruct((BATCH, Q_DIM, out_cols), BF16),
        scratch_shapes=[pltpu.VMEM((N_KV_HEADS, n_keys, N_GROUPS * Q_TILE), F32),
                        pltpu.VMEM((3, WINDOW, N_GROUPS * Q_TILE), F32)],
        compiler_params=_params("arbitrary"),
        name="attention",
    )(sink, qt, k, vt)


def _lru_kernel(uf_ref, ub_ref, cw_ref, cb_ref, wa_ref, ba_ref, wx_ref, bx_ref, lam_ref,
                hf_ref, hb_ref, u_sc, a_sc, b_sc, y_sc, h_sc):
    step = pl.program_id(1)
    tl, pitch, halo = LRU_TL, LRU_PITCH, SUBLANES
    segment_start = (step == 0) | (step == LRU_CTX_BLKS)

    @pl.when(segment_start)
    def _zero_halo():
        u_sc[0, :, 0:halo, :] = jnp.zeros((BATCH * LRU_SLABS, halo, LANES), F32)
        u_sc[1, :, halo + tl:, :] = jnp.zeros((BATCH * LRU_SLABS, halo, LANES), F32)

    @pl.when(jnp.logical_not(segment_start))
    def _carry_halo():
        u_sc[0, :, 0:halo, :] = u_sc[0, :, tl:tl + halo, :]
        u_sc[1, :, halo + tl:, :] = u_sc[1, :, halo:2 * halo, :]

    @pl.when(step == 0)
    def _sequence_start():
        h_sc[...] = jnp.zeros_like(h_sc)

    for d, u_ref in ((0, uf_ref), (1, ub_ref)):
        for bi in range(BATCH):
            for s in range(LRU_SLABS):
                u_sc[d, bi * LRU_SLABS + s, halo:halo + tl, :] = u_ref[bi, :, s * LANES:(s + 1) * LANES]

    for d in range(2):
        nl = -lam_ref[d]
        half_decay = (0.5 * LRU_C) * (jnp.maximum(nl, 0.0) + jnp.log1p(jnp.exp(-jnp.abs(nl))))
        for bi in range(BATCH):
            cols = []
            for s in range(LRU_SLABS):
                lanes = slice(s * LANES, (s + 1) * LANES)
                half_x = 0.5 * cb_ref[d][:, lanes]
                for kk in range(LRU_CONV):
                    off = halo + (kk - (LRU_CONV - 1) if d == 0 else (LRU_CONV - 1) - kk)
                    rows = pl.ds(off, tl, stride=1)
                    tap = 0.5 * cw_ref[d][kk:kk + 1, lanes]
                    half_x = half_x + tap * u_sc[d, bi * LRU_SLABS + s, rows, :]
                cols.append(half_x)
            half_x = jnp.concatenate(cols, axis=1)
            xb = half_x.astype(BF16)
            tr = jnp.tanh(jnp.dot(xb, wa_ref[d, 0], preferred_element_type=F32) + ba_ref[d])
            ti = jnp.tanh(jnp.dot(xb, wx_ref[d, 0], preferred_element_type=F32) + bx_ref[d])
            neg_log_a = tr * half_decay + half_decay
            ix = ti * half_x + half_x
            a = jnp.exp(-neg_log_a)
            var = jnp.tanh(neg_log_a) * (a * a + 1.0)
            b = (var * lax.rsqrt(jnp.maximum(var, F32_TINY))) * ix
            for s in range(LRU_SLABS):
                lanes = slice(s * LANES, (s + 1) * LANES)
                rows = pl.ds(bi * pitch, tl, stride=1)
                a_sc[d, s, rows, :] = a[:, lanes]
                b_sc[d, s, rows, :] = b[:, lanes]

    def scan_step(t, carry):
        new = []
        for d in range(2):
            tt = t if d == 0 else tl - 1 - t
            for s in range(LRU_SLABS):
                rows = pl.ds(tt, BATCH, stride=pitch)
                h = a_sc[d, s, rows, :] * carry[d * LRU_SLABS + s] + b_sc[d, s, rows, :]
                y_sc[d, s, rows, :] = h
                new.append(h)
        return tuple(new)

    init = tuple(h_sc[d, s] for d in range(2) for s in range(LRU_SLABS))
    final = lax.fori_loop(0, tl, scan_step, init, unroll=8)
    for d in range(2):
        for s in range(LRU_SLABS):
            h_sc[d, s] = final[d * LRU_SLABS + s]

    for d, o_ref in ((0, hf_ref), (1, hb_ref)):
        for bi in range(BATCH):
            for s in range(LRU_SLABS):
                o_ref[bi, :, s * LANES:(s + 1) * LANES] = (
                    y_sc[d, s, pl.ds(bi * pitch, tl, stride=1), :].astype(o_ref.dtype))


def _block_diag_chunks(w):
    per = MXU_DIM // LRU_BLOCK
    w = w.reshape(2, LRU_BLOCKS // per, per, LRU_BLOCK, LRU_BLOCK)
    eye = jnp.eye(per, dtype=w.dtype)
    bd = jnp.einsum('dcpij,pq->dcpiqj', w, eye)
    return bd.reshape(2, LRU_BLOCKS // per, MXU_DIM, MXU_DIM).astype(BF16)


def _lru_scan(u, conv_w, conv_b, gate_a_w, gate_a_b, gate_x_w, gate_x_b, lam):
    def fwd_blk(w, s):
        return (0, s, w)

    def bwd_blk(w, s):
        blk = jnp.where(s < LRU_CTX_BLKS, LRU_CTX_BLKS - 1 - s, LRU_BLKS - 1 - (s - LRU_CTX_BLKS))
        return (0, blk, w)

    blk = (BATCH, LRU_TL, LRU_WC)
    vec = pl.BlockSpec((2, 1, LRU_WC), lambda w, s: (0, 0, w))
    gate = pl.BlockSpec((2, LRU_WC // MXU_DIM, MXU_DIM, MXU_DIM), lambda w, s: (0, w, 0, 0))
    width = u.shape[-1]
    out = jax.ShapeDtypeStruct((BATCH, TOK, width), BF16)
    scan_buf = pltpu.VMEM((2, LRU_SLABS, BATCH * LRU_PITCH, LANES), F32)
    return pl.pallas_call(
        _lru_kernel,
        grid=(width // LRU_WC, LRU_BLKS),
        in_specs=[pl.BlockSpec(blk, fwd_blk), pl.BlockSpec(blk, bwd_blk),
                  pl.BlockSpec((2, LRU_CONV, LRU_WC), lambda w, s: (0, 0, w)), vec,
                  gate, vec, gate, vec, vec],
        out_specs=[pl.BlockSpec(blk, fwd_blk), pl.BlockSpec(blk, bwd_blk)],
        out_shape=[out, out],
        scratch_shapes=[pltpu.VMEM((2, BATCH * LRU_SLABS, LRU_TL + 2 * SUBLANES, LANES), F32),
                        scan_buf, scan_buf, scan_buf,
                        pltpu.VMEM((2, LRU_SLABS, BATCH, LANES), F32)],
        compiler_params=_params("parallel", "arbitrary"),
        name="lru_scan",
    )(u, u, conv_w, conv_b.reshape(2, 1, width),
      _block_diag_chunks(gate_a_w), 0.5 * gate_a_b.reshape(2, 1, width),
      _block_diag_chunks(gate_x_w), 0.5 * gate_x_b.reshape(2, 1, width),
      lam.reshape(2, 1, width))


def _residual_rows(x_refs, mod_ref, acts, wb_sc, o_ref, post=None):
    y = jnp.dot(jnp.concatenate(acts, axis=0).astype(BF16), wb_sc[...], preferred_element_type=F32)
    for bb in range(len(acts)):
        out = _stream_tile(x_refs, bb) + mod_ref[bb, 0][2:3, :] * _tile_rows(y, bb)
        o_ref[bb] = out if post is None else post(out)


def _out_attn_final_kernel(ot_ref, g_ref, w_ref, x_ref, mod_ref, fg_ref, o_ref, wb_sc):
    _cast_once(w_ref, wb_sc)
    acts = [ot_ref[bb].astype(F32).T * _silu(g_ref[bb].astype(F32)) for bb in range(OUT_ROW_BATCH)]

    def final_norm(x):
        return x * lax.rsqrt(jnp.mean(x * x, axis=-1, keepdims=True) + EPS) * fg_ref[...]

    _residual_rows((x_ref,), mod_ref, acts, wb_sc, o_ref, final_norm)


def _out_attn_in_next_kernel(ot_ref, g_ref, wo_ref, ctx_ref, lat_ref, mod_ref, ng_ref, nmod_ref, wi_ref,
                             xs_ref, *refs):
    o_refs, (wo_sc, wi_sc) = refs[:-2], refs[-2:]
    _cast_once(wo_ref, wo_sc)
    _cast_once(wi_ref, wi_sc)
    acts = [ot_ref[bb].astype(F32).T * _silu(g_ref[bb].astype(F32)) for bb in range(ROW_BATCH)]
    y = jnp.dot(jnp.concatenate(acts, axis=0).astype(BF16), wo_sc[...], preferred_element_type=F32)
    tiles = []
    for bb in range(ROW_BATCH):
        out = _stream_tile((ctx_ref, lat_ref), bb) + mod_ref[bb, 0][2:3, :] * _tile_rows(y, bb)
        xs_ref[bb] = out
        tiles.append(_norm_mod(out, ng_ref[...], nmod_ref[bb, 0]))
    z = jnp.dot(jnp.concatenate(tiles, axis=0).astype(BF16), wi_sc[...], preferred_element_type=F32)
    width = z.shape[-1] // len(o_refs)
    for i, o_ref in enumerate(o_refs):
        for bb in range(ROW_BATCH):
            o_ref[bb] = _tile_rows(z, bb)[:, i * width:(i + 1) * width].astype(o_ref.dtype)


def _halo_rows(main_ref, prev_ref, next_ref, bb):
    return jnp.concatenate([prev_ref[bb], main_ref[bb], next_ref[bb]], axis=0).astype(F32)


def _out_lru_conv_kernel(hf_ref, hfp_ref, hfn_ref, hb_ref, hbp_ref, hbn_ref, gl_ref, glp_ref, gln_ref,
                         x_ref, xp_ref, xn_ref, lmod_ref, lwo_ref, mod_ref, g_ref, wu_ref, wb_ref, wc_ref,
                         wg_ref, cw_ref, cb_ref, wo_ref, o_ref, p_sc, lwo_sc, wo_sc):
    j = pl.program_id(1)
    halo, cc = SUBLANES, CONV_CHUNK
    rows = ROW_TILE + 2 * halo
    _cast_once(lwo_ref, lwo_sc)
    _cast_once(wo_ref, wo_sc)
    acts = [(_halo_rows(hf_ref, hfp_ref, hfn_ref, bb) + _halo_rows(hb_ref, hbp_ref, hbn_ref, bb))
            * _silu(_halo_rows(gl_ref, glp_ref, gln_ref, bb)) for bb in range(ROW_BATCH)]
    y = jnp.dot(jnp.concatenate(acts, axis=0).astype(BF16), lwo_sc[...], preferred_element_type=F32)
    x1 = [_halo_rows(x_ref, xp_ref, xn_ref, bb) + lmod_ref[bb, 0][2:3, :] * y[bb * rows:(bb + 1) * rows]
          for bb in range(ROW_BATCH)]
    tiles = [_norm_mod(x1[bb], g_ref[...], mod_ref[bb, 0]) for bb in range(ROW_BATCH)]
    h = jnp.concatenate(tiles, axis=0).astype(BF16)
    h_main = jnp.concatenate([h[bb * rows + halo:bb * rows + halo + ROW_TILE] for bb in range(ROW_BATCH)], axis=0)
    keep_prev = jnp.where((j == 0) | (j == CTX_ROW_TILES), 0.0, 1.0)
    keep_next = jnp.where((j == CTX_ROW_TILES - 1) | (j == ROW_TILES - 1), 0.0, 1.0)
    acc = jnp.zeros((ROW_BATCH * ROW_TILE, D_MODEL), F32)
    for c in range(D_MODEL // cc):
        chans = slice(c * cc, (c + 1) * cc)
        u, cg = (jnp.dot(h, w_ref[:, chans], preferred_element_type=F32) for w_ref in (wu_ref, wc_ref))
        bg, g = (jnp.dot(h_main, w_ref[:, chans], preferred_element_type=F32) for w_ref in (wb_ref, wg_ref))
        p = cg * u
        acts = []
        for bb in range(ROW_BATCH):
            top = bb * rows
            main = slice(top + halo, top + halo + ROW_TILE)
            tile = slice(bb * ROW_TILE, (bb + 1) * ROW_TILE)
            for s in range(cc // LANES):
                lanes = slice(s * LANES, (s + 1) * LANES)
                p_sc[bb, c, s, 0:halo, :] = p[top:top + halo, lanes] * keep_prev
                p_sc[bb, c, s, halo:halo + ROW_TILE, :] = p[main, lanes]
                p_sc[bb, c, s, halo + ROW_TILE:rows, :] = p[top + halo + ROW_TILE:top + rows, lanes] * keep_next
            before = jnp.concatenate([p_sc[bb, c, s, pl.ds(halo - 1, ROW_TILE, stride=1), :]
                                      for s in range(cc // LANES)], axis=1)
            after = jnp.concatenate([p_sc[bb, c, s, pl.ds(halo + 1, ROW_TILE, stride=1), :]
                                     for s in range(cc // LANES)], axis=1)
            conv = (cw_ref[0:1, chans] * before + cw_ref[1:2, chans] * p[main] + cw_ref[2:3, chans] * after
                    + cb_ref[:, chans])
            acts.append(bg[tile] * conv * _silu(g[tile]))
        acc = acc + jnp.dot(jnp.concatenate(acts, axis=0).astype(BF16), wo_sc[chans, :],
                            preferred_element_type=F32)
    for bb in range(ROW_BATCH):
        o_ref[bb] = x1[bb][halo:halo + ROW_TILE] + mod_ref[bb, 0][2:3, :] * _tile_rows(acc, bb)


_W_OUT_SPEC = _const_spec((D_MODEL, D_MODEL))


def _out_attn_final(ot, gate, w_out, xs, mod, final_g):
    off, rb = CTX_ROW_TILES, OUT_ROW_BATCH
    return pl.pallas_call(
        _out_attn_final_kernel,
        grid=(BATCH // rb, ROW_TILES - off),
        in_specs=[pl.BlockSpec((rb, Q_DIM, ROW_TILE), lambda b, j: (b, 0, j)), _row_spec(Q_DIM, off, rb),
                  _W_OUT_SPEC, _row_spec(D_MODEL, off, rb), _mod_spec(off, rb), _const_spec((1, D_MODEL))],
        out_specs=_row_spec(D_MODEL, rb=rb),
        out_shape=jax.ShapeDtypeStruct((BATCH, SEQ, D_MODEL), F32),
        scratch_shapes=[pltpu.VMEM((D_MODEL, D_MODEL), BF16)],
        compiler_params=_params("arbitrary", "arbitrary"),
        name="out_attn_final",
    )(ot, gate, w_out, xs, mod, final_g.reshape(1, D_MODEL))


def _out_attn_in_next(ot, gate, w_out, ctx, x, mod, next_norm_g, next_mod, next_w_in, dtypes):
    n = next_w_in.shape[1]
    width = n // len(dtypes)
    return pl.pallas_call(
        _out_attn_in_next_kernel,
        grid=_ROW_GRID,
        in_specs=[pl.BlockSpec((ROW_BATCH, Q_DIM, ROW_TILE), lambda b, j: (b, 0, j)), _row_spec(Q_DIM),
                  _W_OUT_SPEC] + _stream_specs(True)
                 + [_mod_spec(), _const_spec((1, D_MODEL)), _mod_spec(), _const_spec((D_MODEL, n))],
        out_specs=[_row_spec(D_MODEL)] + [_row_spec(width)] * len(dtypes),
        out_shape=[jax.ShapeDtypeStruct((BATCH, TOK, D_MODEL), F32)]
                  + [jax.ShapeDtypeStruct((BATCH, TOK, width), dt) for dt in dtypes],
        scratch_shapes=[pltpu.VMEM((D_MODEL, D_MODEL), BF16), pltpu.VMEM((D_MODEL, n), BF16)],
        compiler_params=_params("arbitrary", "arbitrary"),
        name="out_attn_in_lru",
    )(ot, gate, w_out, ctx, x, mod, next_norm_g.reshape(1, D_MODEL), next_mod, next_w_in)


def _out_lru_conv(hf, hb, gate, lru_w_out, xs, lru_mod, mod, norm_g, w_in, conv_w, conv_b, w_out):
    per = ROW_TILE // SUBLANES
    n8 = TOK // SUBLANES
    halo = lambda idx: pl.BlockSpec((ROW_BATCH, SUBLANES, D_MODEL), idx)
    prev = halo(lambda b, j: (b, jnp.maximum(j * per - 1, 0), 0))
    nxt = halo(lambda b, j: (b, jnp.minimum((j + 1) * per, n8 - 1), 0))
    with_halo = [_row_spec(D_MODEL), prev, nxt]
    wi = w_in.astype(BF16)
    w_blocks = [_const_spec((D_MODEL, D_MODEL), functools.partial(lambda i, b, j: (0, i), i)) for i in range(4)]
    return pl.pallas_call(
        _out_lru_conv_kernel,
        grid=_ROW_GRID,
        in_specs=with_halo * 4 + [_mod_spec(), _W_OUT_SPEC, _mod_spec(), _const_spec((1, D_MODEL))] + w_blocks
                 + [_const_spec((CONV_K, D_MODEL)), _const_spec((1, D_MODEL)), _W_OUT_SPEC],
        out_specs=_row_spec(D_MODEL),
        out_shape=jax.ShapeDtypeStruct((BATCH, TOK, D_MODEL), F32),
        scratch_shapes=[pltpu.VMEM((ROW_BATCH, D_MODEL // CONV_CHUNK, CONV_CHUNK // LANES,
                                    ROW_TILE + 2 * SUBLANES, LANES), F32),
                        pltpu.VMEM((D_MODEL, D_MODEL), BF16), pltpu.VMEM((D_MODEL, D_MODEL), BF16)],
        compiler_params=_params("arbitrary", "arbitrary"),
        name="out_lru_conv",
    )(hf, hf, hf, hb, hb, hb, gate, gate, gate, xs, xs, xs, lru_mod, lru_w_out, mod,
      norm_g.reshape(1, D_MODEL), wi, wi, wi, wi, conv_w, conv_b.reshape(1, D_MODEL), w_out)


def _rope_tables():
    quarter = HEAD_DIM // 4
    pos = np.arange(SEQ)
    half = HEAD_DIM // 2
    inv = (1.0 / (ROPE_BASE ** (np.arange(0, half, 2, dtype=np.float32) / half))).astype(np.float32)
    zero = np.zeros((SEQ, quarter), np.float32)
    parts_c, parts_a, parts_b = [], [], []
    for axis_pos in ((pos // GRID_W).astype(np.float32), (pos % GRID_W).astype(np.float32)):
        ang = axis_pos[:, None] * inv
        c, s = np.cos(ang), np.sin(ang)
        parts_c += [c, c]
        parts_a += [-s, zero]
        parts_b += [zero, s]

    def table(parts, ctx_value):
        head = np.concatenate(parts, axis=1)
        head = np.concatenate([np.full((CTX_LEN, HEAD_DIM), ctx_value, np.float32), head], axis=0)
        return jnp.asarray(np.tile(head, (1, LANES // HEAD_DIM)).astype(np.float32))

    return table(parts_c, 1.0), table(parts_a, 0.0), table(parts_b, 0.0)


def kernel(x, c, ctx, c_ctx, l0_norm_g, l0_mod_w, l0_mod_b, l0_w_in, l0_w_out, l0_sink, l1_norm_g, l1_mod_w, l1_mod_b, l1_w_in, l1_w_out, l1_conv_w, l1_conv_b, l1_gate_a_w, l1_gate_a_b, l1_gate_x_w, l1_gate_x_b, l1_lambda, l2_norm_g, l2_mod_w, l2_mod_b, l2_w_in, l2_w_out, l2_conv_w, l2_conv_b, l3_norm_g, l3_mod_w, l3_mod_b, l3_w_in, l3_w_out, l3_sink, final_norm_g):
    mods = _modulation(c, c_ctx, (l0_mod_w, l1_mod_w, l2_mod_w, l3_mod_w),
                       (l0_mod_b, l1_mod_b, l2_mod_b, l3_mod_b))
    rope_tabs = _rope_tables()

    qt, k, vt, gate = _in_attn((ctx, x), mods[0], l0_norm_g, l0_w_in, rope_tabs, ctx_queries=True)
    ot = _attention(qt, k, vt, l0_sink, need_ctx=True)
    xs, u, gate = _out_attn_in_next(ot, gate, l0_w_out, ctx, x, mods[0], l1_norm_g, mods[1], l1_w_in,
                                    (F32, BF16))

    hf, hb = _lru_scan(u, l1_conv_w, l1_conv_b, l1_gate_a_w, l1_gate_a_b, l1_gate_x_w, l1_gate_x_b,
                       l1_lambda)
    xs = _out_lru_conv(hf, hb, gate, l1_w_out, xs, mods[1], mods[2], l2_norm_g, l2_w_in, l2_conv_w,
                       l2_conv_b, l2_w_out)

    qt, k, vt, gate = _in_attn((xs,), mods[3], l3_norm_g, l3_w_in, rope_tabs, ctx_queries=False)
    ot = _attention(qt, k, vt, l3_sink, need_ctx=False)
    return _out_attn_final(ot, gate, l3_w_out, xs, mods[3], final_norm_g)
```

```python
import functools
import math

import jax
import jax.numpy as jnp
import numpy as np
from jax import lax
from jax.experimental import pallas as pl
from jax.experimental.pallas import tpu as pltpu

D_MODEL = 1024
BATCH = 8
SEQ = 2048
CTX_LEN = 256
TOK = CTX_LEN + SEQ
GRID_W = 64
EPS = 1e-6
NEG_INF = -1e30
N_HEADS = 16
N_KV_HEADS = 4
N_GROUPS = N_HEADS // N_KV_HEADS
HEAD_DIM = 64
Q_DIM = N_HEADS * HEAD_DIM
KV_DIM = N_KV_HEADS * HEAD_DIM
WINDOW = 128
ROPE_BASE = 10000.0
LRU_BLOCKS = 16
LRU_BLOCK = 64
LRU_C = 8.0
LRU_CONV = 4
CONV_K = 3
MOD_PARTS = 3

LANES = 128
SUBLANES = 8
MXU_DIM = 256
VMEM_LIMIT_BYTES = 56 * 1024 * 1024

ROW_TILE = 256
ROW_BATCH = 2
OUT_ROW_BATCH = 4
CTX_ROW_TILES = CTX_LEN // ROW_TILE
ROW_TILES = TOK // ROW_TILE
Q_TILE = WINDOW
CTX_Q_TILES = CTX_LEN // Q_TILE
Q_TILES = TOK // Q_TILE
BAND = 3 * WINDOW
MAX_ROWS = 32
CONV_CHUNK = MXU_DIM

LRU_TL = 256
LRU_WC = MXU_DIM
LRU_PITCH = LRU_TL + 4
LRU_SLABS = LRU_WC // LANES
LRU_CTX_BLKS = CTX_LEN // LRU_TL
LRU_BLKS = TOK // LRU_TL

F32 = jnp.float32
BF16 = jnp.bfloat16
F32_TINY = float(jnp.finfo(jnp.float32).tiny)
LOG2E = math.log2(math.e)


def _params(*sem):
    return pltpu.CompilerParams(dimension_semantics=sem, vmem_limit_bytes=VMEM_LIMIT_BYTES)


def _sigmoid(x):
    return 0.5 * jnp.tanh(0.5 * x) + 0.5


def _silu(x):
    return x * _sigmoid(x)


def _mod_kernel(c_ref, b_ref, *refs):
    w_refs, o_ref = refs[:-1], refs[-1]
    layer = pl.program_id(0) // MOD_PARTS
    s = _silu(c_ref[...])
    s_hi = s.astype(BF16)
    s_lo = (s - s_hi.astype(F32)).astype(BF16)
    for l, w_ref in enumerate(w_refs):
        @pl.when(layer == l)
        def _layer(w_ref=w_ref):
            w = w_ref[...]
            w_hi = w.astype(BF16)
            w_lo = (w - w_hi.astype(F32)).astype(BF16)
            acc = jnp.dot(s_hi, w_hi, preferred_element_type=F32)
            acc += jnp.dot(s_hi, w_lo, preferred_element_type=F32)
            acc += jnp.dot(s_lo, w_hi, preferred_element_type=F32)
            m = acc + b_ref[0]
            o_ref[0, :, 0, :] = jnp.broadcast_to(m[BATCH:BATCH + 1], (BATCH, D_MODEL))
            o_ref[0, :, 1, :] = m[:BATCH]


def _modulation(c, c_ctx, mod_ws, mod_bs):
    n = len(mod_ws)
    cc = jnp.concatenate([c, c_ctx[None], jnp.zeros((2 * SUBLANES - BATCH - 1, D_MODEL), F32)], axis=0)
    rows = cc.shape[0]
    w_spec = lambda l: pl.BlockSpec((D_MODEL, D_MODEL),
                                    lambda s: (0, jnp.clip(s - MOD_PARTS * l, 0, MOD_PARTS - 1)))
    part = lambda s: (s // MOD_PARTS, s % MOD_PARTS)
    m = pl.pallas_call(
        _mod_kernel,
        grid=(MOD_PARTS * n,),
        in_specs=[pl.BlockSpec((rows, D_MODEL), lambda s: (0, 0)),
                  pl.BlockSpec((1, 1, D_MODEL), lambda s: (part(s)[0], 0, part(s)[1]))]
                 + [w_spec(l) for l in range(n)],
        out_specs=pl.BlockSpec((1, BATCH, 2, D_MODEL), lambda s: (part(s)[0], 0, 0, part(s)[1])),
        out_shape=jax.ShapeDtypeStruct((n, BATCH, 2, MOD_PARTS * D_MODEL), F32),
        compiler_params=_params("arbitrary"),
        name="modulation",
    )(cc, jnp.stack(mod_bs).reshape(n, 1, MOD_PARTS * D_MODEL), *mod_ws)
    return m.reshape(n, BATCH, 2, MOD_PARTS, D_MODEL)


def _mod_spec(tile_offset=0, rb=ROW_BATCH):
    def idx(b, j):
        return (b, jnp.where(j + tile_offset < CTX_ROW_TILES, 0, 1), 0, 0)
    return pl.BlockSpec((rb, 1, MOD_PARTS, D_MODEL), idx)


def _row_spec(width, tile_offset=0, rb=ROW_BATCH):
    return pl.BlockSpec((rb, ROW_TILE, width), lambda b, j: (b, j + tile_offset, 0))


def _const_spec(shape, index=None):
    index = index or (lambda b, j: (0,) * len(shape))
    return pl.BlockSpec(shape, index, pipeline_mode=pl.Buffered(1))


def _stream_specs(split, rb=ROW_BATCH):
    if not split:
        return [_row_spec(D_MODEL, rb=rb)]
    blk = (rb, ROW_TILE, D_MODEL)
    return [pl.BlockSpec(blk, lambda b, j: (b, 0, 0)),
            pl.BlockSpec(blk, lambda b, j: (b, jnp.maximum(j - CTX_ROW_TILES, 0), 0))]


def _stream_tile(x_refs, bb):
    if len(x_refs) == 1:
        return x_refs[0][bb]
    ctx_ref, lat_ref = x_refs
    return jnp.where(pl.program_id(1) < CTX_ROW_TILES, ctx_ref[bb], lat_ref[bb])


def _cast_once(w_ref, wb_sc):
    @pl.when((pl.program_id(0) == 0) & (pl.program_id(1) == 0))
    def _cast():
        wb_sc[...] = w_ref[...].astype(BF16)


_ROW_GRID = (BATCH // ROW_BATCH, ROW_TILES)


def _norm_mod(x, g, mod):
    y = x * lax.rsqrt(jnp.mean(x * x, axis=-1, keepdims=True) + EPS) * g
    return y * (1.0 + mod[1:2, :]) + mod[0:1, :]


def _norm_mod_rows(x_refs, g_ref, mod_ref):
    tiles = [_norm_mod(_stream_tile(x_refs, bb), g_ref[...], mod_ref[bb, 0]) for bb in range(ROW_BATCH)]
    return jnp.concatenate(tiles, axis=0).astype(BF16)


def _tile_rows(t, bb):
    return t[bb * ROW_TILE:(bb + 1) * ROW_TILE]


def _in_attn_kernel(*refs, n_x, ctx_queries):
    x_refs = refs[:n_x]
    mod_ref, g_ref, w_ref, rc_ref, ra_ref, rb_ref, qt_ref, k_ref, vt_ref, gate_ref, wb_sc = refs[n_x:]
    _cast_once(w_ref, wb_sc)
    rc, ra, rb = rc_ref[...], ra_ref[...], rb_ref[...]

    def rope(t):
        reps = t.shape[-1] // LANES
        n = t.shape[-1]
        return (t * jnp.tile(rc, (1, reps))
                + pltpu.roll(t, n - HEAD_DIM // 4, 1) * jnp.tile(ra, (1, reps))
                + pltpu.roll(t, HEAD_DIM // 4, 1) * jnp.tile(rb, (1, reps)))

    def keys_values(z_kv, bb):
        k = rope(z_kv[:, :KV_DIM]).astype(BF16)
        for kh in range(N_KV_HEADS):
            k_ref[bb, kh] = k[:, kh * HEAD_DIM:(kh + 1) * HEAD_DIM]
        vt_ref[bb] = z_kv[:, KV_DIM:].T.astype(BF16)

    def all_columns():
        z_all = jnp.dot(_norm_mod_rows(x_refs, g_ref, mod_ref), wb_sc[...], preferred_element_type=F32)
        for bb in range(ROW_BATCH):
            z = _tile_rows(z_all, bb)
            qt_ref[bb] = (rope(z[:, :Q_DIM]) * (LOG2E * HEAD_DIM ** -0.5)).T.astype(BF16)
            keys_values(z[:, Q_DIM:Q_DIM + 2 * KV_DIM], bb)
            gate_ref[bb] = z[:, Q_DIM + 2 * KV_DIM:].astype(gate_ref.dtype)

    def keys_values_only():
        z_kv = jnp.dot(_norm_mod_rows(x_refs, g_ref, mod_ref), wb_sc[:, Q_DIM:Q_DIM + 2 * KV_DIM],
                       preferred_element_type=F32)
        for bb in range(ROW_BATCH):
            keys_values(_tile_rows(z_kv, bb), bb)
        qt_ref[...] = jnp.zeros_like(qt_ref)
        gate_ref[...] = jnp.zeros_like(gate_ref)

    if ctx_queries:
        all_columns()
    else:
        pl.when(pl.program_id(1) >= CTX_ROW_TILES)(all_columns)
        pl.when(pl.program_id(1) < CTX_ROW_TILES)(keys_values_only)


def _in_attn(streams, mod, norm_g, w_in, rope_tabs, ctx_queries):
    n = w_in.shape[1]
    col = lambda height: pl.BlockSpec((ROW_BATCH, height, ROW_TILE), lambda b, j: (b, 0, j))
    tab = pl.BlockSpec((ROW_TILE, LANES), lambda b, j: (j, 0))
    return pl.pallas_call(
        functools.partial(_in_attn_kernel, n_x=len(streams), ctx_queries=ctx_queries),
        grid=_ROW_GRID,
        in_specs=_stream_specs(len(streams) > 1)
                 + [_mod_spec(), _const_spec((1, D_MODEL)), _const_spec((D_MODEL, n)), tab, tab, tab],
        out_specs=[col(Q_DIM),
                   pl.BlockSpec((ROW_BATCH, N_KV_HEADS, ROW_TILE, HEAD_DIM), lambda b, j: (b, 0, j, 0)),
                   col(KV_DIM), _row_spec(Q_DIM)],
        out_shape=[jax.ShapeDtypeStruct((BATCH, Q_DIM, TOK), BF16),
                   jax.ShapeDtypeStruct((BATCH, N_KV_HEADS, TOK, HEAD_DIM), BF16),
                   jax.ShapeDtypeStruct((BATCH, KV_DIM, TOK), BF16),
                   jax.ShapeDtypeStruct((BATCH, TOK, Q_DIM), BF16)],
        scratch_shapes=[pltpu.VMEM((D_MODEL, n), BF16)],
        compiler_params=_params("arbitrary", "arbitrary"),
        name="in_attn",
    )(*streams, mod, norm_g.reshape(1, D_MODEL), w_in, *rope_tabs)


def _attn_kernel(sink_ref, qt_ref, k_ref, vt_ref, ot_ref, s_sc, bias_sc, *, first_tile):
    lanes = N_GROUPS * Q_TILE

    key = lax.broadcasted_iota(jnp.int32, (WINDOW, lanes), 0)
    query = lax.broadcasted_iota(jnp.int32, (WINDOW, lanes), 1) & (Q_TILE - 1)
    bias_sc[0] = jnp.where(key >= query, 0.0, NEG_INF)
    bias_sc[1] = jnp.where(key <= query, 0.0, NEG_INF)
    bias_sc[2] = jnp.full((WINDOW, lanes), NEG_INF, F32)

    def aligned(start):
        return start if isinstance(start, int) else pl.multiple_of(start, WINDOW)

    def heads_of(kh):
        return [kh * N_GROUPS + g for g in range(N_GROUPS)]

    def sink_row(kh):
        return jnp.concatenate([jnp.full((1, Q_TILE), sink_ref[h] * LOG2E, F32) for h in heads_of(kh)], axis=1)

    def score_steps(j, kh, chunks, m_box):
        cols = pl.ds(aligned(j * Q_TILE), Q_TILE)
        qt = jnp.concatenate([qt_ref[0, h * HEAD_DIM:(h + 1) * HEAD_DIM, cols] for h in heads_of(kh)], axis=1)
        mx = jnp.broadcast_to(sink_row(kh), (MAX_ROWS, lanes))
        keys = jnp.concatenate([k_ref[0, kh, pl.ds(ks, WINDOW), :] for ks, _ in chunks], axis=0)
        s_all = jnp.dot(keys, qt, preferred_element_type=F32)
        for c, (ks, bias) in enumerate(chunks):
            s = s_all[c * WINDOW:(c + 1) * WINDOW]
            if bias is not None:
                s = s + bias_sc[bias]
            s_sc[kh, c * WINDOW:(c + 1) * WINDOW, :] = s
            mx = jnp.maximum(mx, s.reshape(WINDOW // MAX_ROWS, MAX_ROWS, lanes).max(axis=0))
            yield
        m_box.append(mx.max(axis=0, keepdims=True))

    def value_steps(j, kh, chunks, m_box):
        m = m_box[0]
        acc = jnp.zeros((HEAD_DIM, lanes), F32)
        denom = jnp.zeros((SUBLANES, lanes), F32)
        group = []
        for c, (ks, _) in enumerate(chunks):
            p = jnp.exp2(s_sc[kh, c * WINDOW:(c + 1) * WINDOW, :] - m)
            denom = denom + p.reshape(WINDOW // SUBLANES, SUBLANES, lanes).sum(axis=0)
            group.append((vt_ref[0, kh * HEAD_DIM:(kh + 1) * HEAD_DIM, pl.ds(ks, WINDOW)], p.astype(BF16)))
            if len(group) == MXU_DIM // WINDOW or c == len(chunks) - 1:
                vt = jnp.concatenate([g[0] for g in group], axis=1)
                pp = jnp.concatenate([g[1] for g in group], axis=0)
                acc = acc + jnp.dot(vt, pp, preferred_element_type=F32)
                group = []
            yield
        denom = denom.sum(axis=0, keepdims=True) + jnp.exp2(sink_row(kh) - m)
        o = acc / denom
        cols = pl.ds(aligned((j - first_tile) * Q_TILE), Q_TILE)
        for g, h in enumerate(heads_of(kh)):
            ot_ref[0, h * HEAD_DIM:(h + 1) * HEAD_DIM, cols] = o[:, g * Q_TILE:(g + 1) * Q_TILE].astype(BF16)

    def interleave(*stages):
        stages = list(stages)
        while stages:
            stages = [st for st in stages if next(st, StopIteration) is not StopIteration]

    ctx_chunks = [(c * WINDOW, None) for c in range(CTX_LEN // WINDOW)]

    def latent_chunks(j):
        own = j * Q_TILE
        if isinstance(j, int):
            prev_bias = 2 if j == CTX_Q_TILES else 0
            next_bias = 2 if j == Q_TILES - 1 else 1
            next_start = min(own + WINDOW, TOK - WINDOW)
        else:
            prev_bias = jnp.where(j == CTX_Q_TILES, 2, 0)
            next_bias = jnp.where(j == Q_TILES - 1, 2, 1)
            next_start = jnp.minimum(own + WINDOW, TOK - WINDOW)
        return [(aligned(own - WINDOW), prev_bias), (aligned(own), None),
                (aligned(next_start), next_bias)] + ctx_chunks

    if first_tile == 0:
        units = [(j, kh, ctx_chunks) for j in range(CTX_Q_TILES) for kh in range(N_KV_HEADS)]
        boxes = [[] for _ in units]
        interleave(score_steps(*units[0], boxes[0]))
        for i, unit in enumerate(units):
            stages = [score_steps(*units[i + 1], boxes[i + 1])] if i + 1 < len(units) else []
            interleave(*stages, value_steps(*unit, boxes[i]))

    def latent_tile(j, m_head0, next_j):
        chunks = latent_chunks(j)
        box = [m_head0]
        for kh in range(N_KV_HEADS):
            nxt = []
            stages = []
            if kh + 1 < N_KV_HEADS:
                stages.append(score_steps(j, kh + 1, chunks, nxt))
            elif next_j is not None:
                stages.append(score_steps(next_j, 0, latent_chunks(next_j), nxt))
            interleave(*stages, value_steps(j, kh, chunks, box))
            box = nxt
        return box[0] if box else None

    first_box = []
    interleave(score_steps(CTX_Q_TILES, 0, latent_chunks(CTX_Q_TILES), first_box))
    m_last = lax.fori_loop(CTX_Q_TILES, Q_TILES - 1, lambda j, m: latent_tile(j, m, j + 1), first_box[0])
    latent_tile(Q_TILES - 1, m_last, None)


def _attention(qt, k, vt, sink, need_ctx):
    first_tile = 0 if need_ctx else CTX_Q_TILES
    n_keys = BAND + CTX_LEN
    out_cols = TOK - first_tile * Q_TILE
    return pl.pallas_call(
        functools.partial(_attn_kernel, first_tile=first_tile),
        grid=(BATCH,),
        in_specs=[pl.BlockSpec(memory_space=pltpu.SMEM),
                  pl.BlockSpec((1, Q_DIM, TOK), lambda b: (b, 0, 0)),
                  pl.BlockSpec((1, N_KV_HEADS, TOK, HEAD_DIM), lambda b: (b, 0, 0, 0)),
                  pl.BlockSpec((1, KV_DIM, TOK), lambda b: (b, 0, 0))],
        out_specs=pl.BlockSpec((1, Q_DIM, out_cols), lambda b: (b, 0, 0)),
        out_shape=jax.ShapeDtypeStruct((BATCH, Q_DIM, out_cols), BF16),
        scratch_shapes=[pltpu.VMEM((N_KV_HEADS, n_keys, N_GROUPS * Q_TILE), F32),
                        pltpu.VMEM((3, WINDOW, N_GROUPS * Q_TILE), F32)],
        compiler_params=_params("arbitrary"),
        name="attention",
    )(sink, qt, k, vt)


def _lru_kernel(uf_ref, ub_ref, cw_ref, cb_ref, wa_ref, ba_ref, wx_ref, bx_ref, lam_ref,
                hf_ref, hb_ref, u_sc, a_sc, b_sc, y_sc, h_sc):
    step = pl.program_id(1)
    tl, pitch, halo = LRU_TL, LRU_PITCH, SUBLANES
    segment_start = (step == 0) | (step == LRU_CTX_BLKS)

    @pl.when(segment_start)
    def _zero_halo():
        u_sc[0, :, 0:halo, :] = jnp.zeros((BATCH * LRU_SLABS, halo, LANES), F32)
        u_sc[1, :, halo + tl:, :] = jnp.zeros((BATCH * LRU_SLABS, halo, LANES), F32)

    @pl.when(jnp.logical_not(segment_start))
    def _carry_halo():
        u_sc[0, :, 0:halo, :] = u_sc[0, :, tl:tl + halo, :]
        u_sc[1, :, halo + tl:, :] = u_sc[1, :, halo:2 * halo, :]

    @pl.when(step == 0)
    def _sequence_start():
        h_sc[...] = jnp.zeros_like(h_sc)

    for d, u_ref in ((0, uf_ref), (1, ub_ref)):
        for bi in range(BATCH):
            for s in range(LRU_SLABS):
                u_sc[d, bi * LRU_SLABS + s, halo:halo + tl, :] = u_ref[bi, :, s * LANES:(s + 1) * LANES]

    for d in range(2):
        nl = -lam_ref[d]
        half_decay = (0.5 * LRU_C) * (jnp.maximum(nl, 0.0) + jnp.log1p(jnp.exp(-jnp.abs(nl))))
        for bi in range(BATCH):
            cols = []
            for s in range(LRU_SLABS):
                lanes = slice(s * LANES, (s + 1) * LANES)
                half_x = 0.5 * cb_ref[d][:, lanes]
                for kk in range(LRU_CONV):
                    off = halo + (kk - (LRU_CONV - 1) if d == 0 else (LRU_CONV - 1) - kk)
                    rows = pl.ds(off, tl, stride=1)
                    tap = 0.5 * cw_ref[d][kk:kk + 1, lanes]
                    half_x = half_x + tap * u_sc[d, bi * LRU_SLABS + s, rows, :]
                cols.append(half_x)
            half_x = jnp.concatenate(cols, axis=1)
            xb = half_x.astype(BF16)
            tr = jnp.tanh(jnp.dot(xb, wa_ref[d, 0], preferred_element_type=F32) + ba_ref[d])
            ti = jnp.tanh(jnp.dot(xb, wx_ref[d, 0], preferred_element_type=F32) + bx_ref[d])
            neg_log_a = tr * half_decay + half_decay
            ix = ti * half_x + half_x
            a = jnp.exp(-neg_log_a)
            var = jnp.tanh(neg_log_a) * (a * a + 1.0)
            b = (var * lax.rsqrt(jnp.maximum(var, F32_TINY))) * ix
            for s in range(LRU_SLABS):
                lanes = slice(s * LANES, (s + 1) * LANES)
                rows = pl.ds(bi * pitch, tl, stride=1)
                a_sc[d, s, rows, :] = a[:, lanes]
                b_sc[d, s, rows, :] = b[:, lanes]

    def scan_step(t, carry):
        new = []
        for d in range(2):
            tt = t if d == 0 else tl - 1 - t
            for s in range(LRU_SLABS):
                rows = pl.ds(tt, BATCH, stride=pitch)
                h = a_sc[d, s, rows, :] * carry[d * LRU_SLABS + s] + b_sc[d, s, rows, :]
                y_sc[d, s, rows, :] = h
                new.append(h)
        return tuple(new)

    init = tuple(h_sc[d, s] for d in range(2) for s in range(LRU_SLABS))
    final = lax.fori_loop(0, tl, scan_step, init, unroll=8)
    for d in range(2):
        for s in range(LRU_SLABS):
            h_sc[d, s] = final[d * LRU_SLABS + s]

    for d, o_ref in ((0, hf_ref), (1, hb_ref)):
        for bi in range(BATCH):
            for s in range(LRU_SLABS):
                o_ref[bi, :, s * LANES:(s + 1) * LANES] = (
                    y_sc[d, s, pl.ds(bi * pitch, tl, stride=1), :].astype(o_ref.dtype))


def _block_diag_chunks(w):
    per = MXU_DIM // LRU_BLOCK
    w = w.reshape(2, LRU_BLOCKS // per, per, LRU_BLOCK, LRU_BLOCK)
    eye = jnp.eye(per, dtype=w.dtype)
    bd = jnp.einsum('dcpij,pq->dcpiqj', w, eye)
    return bd.reshape(2, LRU_BLOCKS // per, MXU_DIM, MXU_DIM).astype(BF16)


def _lru_scan(u, conv_w, conv_b, gate_a_w, gate_a_b, gate_x_w, gate_x_b, lam):
    def fwd_blk(w, s):
        return (0, s, w)

    def bwd_blk(w, s):
        blk = jnp.where(s < LRU_CTX_BLKS, LRU_CTX_BLKS - 1 - s, LRU_BLKS - 1 - (s - LRU_CTX_BLKS))
        return (0, blk, w)

    blk = (BATCH, LRU_TL, LRU_WC)
    vec = pl.BlockSpec((2, 1, LRU_WC), lambda w, s: (0, 0, w))
    gate = pl.BlockSpec((2, LRU_WC // MXU_DIM, MXU_DIM, MXU_DIM), lambda w, s: (0, w, 0, 0))
    width = u.shape[-1]
    out = jax.ShapeDtypeStruct((BATCH, TOK, width), BF16)
    scan_buf = pltpu.VMEM((2, LRU_SLABS, BATCH * LRU_PITCH, LANES), F32)
    return pl.pallas_call(
        _lru_kernel,
        grid=(width // LRU_WC, LRU_BLKS),
        in_specs=[pl.BlockSpec(blk, fwd_blk), pl.BlockSpec(blk, bwd_blk),
                  pl.BlockSpec((2, LRU_CONV, LRU_WC), lambda w, s: (0, 0, w)), vec,
                  gate, vec, gate, vec, vec],
        out_specs=[pl.BlockSpec(blk, fwd_blk), pl.BlockSpec(blk, bwd_blk)],
        out_shape=[out, out],
        scratch_shapes=[pltpu.VMEM((2, BATCH * LRU_SLABS, LRU_TL + 2 * SUBLANES, LANES), F32),
                        scan_buf, scan_buf, scan_buf,
                        pltpu.VMEM((2, LRU_SLABS, BATCH, LANES), F32)],
        compiler_params=_params("parallel", "arbitrary"),
        name="lru_scan",
    )(u, u, conv_w, conv_b.reshape(2, 1, width),
      _block_diag_chunks(gate_a_w), 0.5 * gate_a_b.reshape(2, 1, width),
      _block_diag_chunks(gate_x_w), 0.5 * gate_x_b.reshape(2, 1, width),
      lam.reshape(2, 1, width))


def _residual_rows(x_refs, mod_ref, acts, wb_sc, o_ref, post=None):
    y = jnp.dot(jnp.concatenate(acts, axis=0).astype(BF16), wb_sc[...], preferred_element_type=F32)
    for bb in range(len(acts)):
        out = _stream_tile(x_refs, bb) + mod_ref[bb, 0][2:3, :] * _tile_rows(y, bb)
        o_ref[bb] = out if post is None else post(out)


def _out_attn_final_kernel(ot_ref, g_ref, w_ref, x_ref, mod_ref, fg_ref, o_ref, wb_sc):
    _cast_once(w_ref, wb_sc)
    acts = [ot_ref[bb].astype(F32).T * _silu(g_ref[bb].astype(F32)) for bb in range(OUT_ROW_BATCH)]

    def final_norm(x):
        return x * lax.rsqrt(jnp.mean(x * x, axis=-1, keepdims=True) + EPS) * fg_ref[...]

    _residual_rows((x_ref,), mod_ref, acts, wb_sc, o_ref, final_norm)


def _out_attn_in_next_kernel(ot_ref, g_ref, wo_ref, ctx_ref, lat_ref, mod_ref, ng_ref, nmod_ref, wi_ref,
                             xs_ref, *refs):
    o_refs, (wo_sc, wi_sc) = refs[:-2], refs[-2:]
    _cast_once(wo_ref, wo_sc)
    _cast_once(wi_ref, wi_sc)
    acts = [ot_ref[bb].astype(F32).T * _silu(g_ref[bb].astype(F32)) for bb in range(ROW_BATCH)]
    y = jnp.dot(jnp.concatenate(acts, axis=0).astype(BF16), wo_sc[...], preferred_element_type=F32)
    tiles = []
    for bb in range(ROW_BATCH):
        out = _stream_tile((ctx_ref, lat_ref), bb) + mod_ref[bb, 0][2:3, :] * _tile_rows(y, bb)
        xs_ref[bb] = out
        tiles.append(_norm_mod(out, ng_ref[...], nmod_ref[bb, 0]))
    z = jnp.dot(jnp.concatenate(tiles, axis=0).astype(BF16), wi_sc[...], preferred_element_type=F32)
    width = z.shape[-1] // len(o_refs)
    for i, o_ref in enumerate(o_refs):
        for bb in range(ROW_BATCH):
            o_ref[bb] = _tile_rows(z, bb)[:, i * width:(i + 1) * width].astype(o_ref.dtype)


def _halo_rows(main_ref, prev_ref, next_ref, bb):
    return jnp.concatenate([prev_ref[bb], main_ref[bb], next_ref[bb]], axis=0).astype(F32)


def _out_lru_conv_kernel(hf_ref, hfp_ref, hfn_ref, hb_ref, hbp_ref, hbn_ref, gl_ref, glp_ref, gln_ref,
                         x_ref, xp_ref, xn_ref, lmod_ref, lwo_ref, mod_ref, g_ref, wu_ref, wb_ref, wc_ref,
                         wg_ref, cw_ref, cb_ref, wo_ref, o_ref, p_sc, lwo_sc, wo_sc):
    j = pl.program_id(1)
    halo, cc = SUBLANES, CONV_CHUNK
    rows = ROW_TILE + 2 * halo
    _cast_once(lwo_ref, lwo_sc)
    _cast_once(wo_ref, wo_sc)
    acts = [(_halo_rows(hf_ref, hfp_ref, hfn_ref, bb) + _halo_rows(hb_ref, hbp_ref, hbn_ref, bb))
            * _silu(_halo_rows(gl_ref, glp_ref, gln_ref, bb)) for bb in range(ROW_BATCH)]
    y = jnp.dot(jnp.concatenate(acts, axis=0).astype(BF16), lwo_sc[...], preferred_element_type=F32)
    x1 = [_halo_rows(x_ref, xp_ref, xn_ref, bb) + lmod_ref[bb, 0][2:3, :] * y[bb * rows:(bb + 1) * rows]
          for bb in range(ROW_BATCH)]
    tiles = [_norm_mod(x1[bb], g_ref[...], mod_ref[bb, 0]) for bb in range(ROW_BATCH)]
    h = jnp.concatenate(tiles, axis=0).astype(BF16)
    h_main = jnp.concatenate([h[bb * rows + halo:bb * rows + halo + ROW_TILE] for bb in range(ROW_BATCH)], axis=0)
    keep_prev = jnp.where((j == 0) | (j == CTX_ROW_TILES), 0.0, 1.0)
    keep_next = jnp.where((j == CTX_ROW_TILES - 1) | (j == ROW_TILES - 1), 0.0, 1.0)
    acc = jnp.zeros((ROW_BATCH * ROW_TILE, D_MODEL), F32)
    for c in range(D_MODEL // cc):
        chans = slice(c * cc, (c + 1) * cc)
        u, cg = (jnp.dot(h, w_ref[:, chans], preferred_element_type=F32) for w_ref in (wu_ref, wc_ref))
        bg, g = (jnp.dot(h_main, w_ref[:, chans], preferred_element_type=F32) for w_ref in (wb_ref, wg_ref))
        p = cg * u
        acts = []
        for bb in range(ROW_BATCH):
            top = bb * rows
            main = slice(top + halo, top + halo + ROW_TILE)
            tile = slice(bb * ROW_TILE, (bb + 1) * ROW_TILE)
            for s in range(cc // LANES):
                lanes = slice(s * LANES, (s + 1) * LANES)
                p_sc[bb, c, s, 0:halo, :] = p[top:top + halo, lanes] * keep_prev
                p_sc[bb, c, s, halo:halo + ROW_TILE, :] = p[main, lanes]
                p_sc[bb, c, s, halo + ROW_TILE:rows, :] = p[top + halo + ROW_TILE:top + rows, lanes] * keep_next
            before = jnp.concatenate([p_sc[bb, c, s, pl.ds(halo - 1, ROW_TILE, stride=1), :]
                                      for s in range(cc // LANES)], axis=1)
            after = jnp.concatenate([p_sc[bb, c, s, pl.ds(halo + 1, ROW_TILE, stride=1), :]
                                     for s in range(cc // LANES)], axis=1)
            conv = (cw_ref[0:1, chans] * before + cw_ref[1:2, chans] * p[main] + cw_ref[2:3, chans] * after
                    + cb_ref[:, chans])
            acts.append(bg[tile] * conv * _silu(g[tile]))
        acc = acc + jnp.dot(jnp.concatenate(acts, axis=0).astype(BF16), wo_sc[chans, :],
                            preferred_element_type=F32)
    for bb in range(ROW_BATCH):
        o_ref[bb] = x1[bb][halo:halo + ROW_TILE] + mod_ref[bb, 0][2:3, :] * _tile_rows(acc, bb)


_W_OUT_SPEC = _const_spec((D_MODEL, D_MODEL))


def _out_attn_final(ot, gate, w_out, xs, mod, final_g):
    off, rb = CTX_ROW_TILES, OUT_ROW_BATCH
    return pl.pallas_call(
        _out_attn_final_kernel,
        grid=(BATCH // rb, ROW_TILES - off),
        in_specs=[pl.BlockSpec((rb, Q_DIM, ROW_TILE), lambda b, j: (b, 0, j)), _row_spec(Q_DIM, off, rb),
                  _W_OUT_SPEC, _row_spec(D_MODEL, off, rb), _mod_spec(off, rb), _const_spec((1, D_MODEL))],
        out_specs=_row_spec(D_MODEL, rb=rb),
        out_shape=jax.ShapeDtypeStruct((BATCH, SEQ, D_MODEL), F32),
        scratch_shapes=[pltpu.VMEM((D_MODEL, D_MODEL), BF16)],
        compiler_params=_params("arbitrary", "arbitrary"),
        name="out_attn_final",
    )(ot, gate, w_out, xs, mod, final_g.reshape(1, D_MODEL))


def _out_attn_in_next(ot, gate, w_out, ctx, x, mod, next_norm_g, next_mod, next_w_in, dtypes):
    n = next_w_in.shape[1]
    width = n // len(dtypes)
    return pl.pallas_call(
        _out_attn_in_next_kernel,
        grid=_ROW_GRID,
        in_specs=[pl.BlockSpec((ROW_BATCH, Q_DIM, ROW_TILE), lambda b, j: (b, 0, j)), _row_spec(Q_DIM),
                  _W_OUT_SPEC] + _stream_specs(True)
                 + [_mod_spec(), _const_spec((1, D_MODEL)), _mod_spec(), _const_spec((D_MODEL, n))],
        out_specs=[_row_spec(D_MODEL)] + [_row_spec(width)] * len(dtypes),
        out_shape=[jax.ShapeDtypeStruct((BATCH, TOK, D_MODEL), F32)]
                  + [jax.ShapeDtypeStruct((BATCH, TOK, width), dt) for dt in dtypes],
        scratch_shapes=[pltpu.VMEM((D_MODEL, D_MODEL), BF16), pltpu.VMEM((D_MODEL, n), BF16)],
        compiler_params=_params("arbitrary", "arbitrary"),
        name="out_attn_in_lru",
    )(ot, gate, w_out, ctx, x, mod, next_norm_g.reshape(1, D_MODEL), next_mod, next_w_in)


def _out_lru_conv(hf, hb, gate, lru_w_out, xs, lru_mod, mod, norm_g, w_in, conv_w, conv_b, w_out):
    per = ROW_TILE // SUBLANES
    n8 = TOK // SUBLANES
    halo = lambda idx: pl.BlockSpec((ROW_BATCH, SUBLANES, D_MODEL), idx)
    prev = halo(lambda b, j: (b, jnp.maximum(j * per - 1, 0), 0))
    nxt = halo(lambda b, j: (b, jnp.minimum((j + 1) * per, n8 - 1), 0))
    with_halo = [_row_spec(D_MODEL), prev, nxt]
    wi = w_in.astype(BF16)
    w_blocks = [_const_spec((D_MODEL, D_MODEL), functools.partial(lambda i, b, j: (0, i), i)) for i in range(4)]
    return pl.pallas_call(
        _out_lru_conv_kernel,
        grid=_ROW_GRID,
        in_specs=with_halo * 4 + [_mod_spec(), _W_OUT_SPEC, _mod_spec(), _const_spec((1, D_MODEL))] + w_blocks
                 + [_const_spec((CONV_K, D_MODEL)), _const_spec((1, D_MODEL)), _W_OUT_SPEC],
        out_specs=_row_spec(D_MODEL),
        out_shape=jax.ShapeDtypeStruct((BATCH, TOK, D_MODEL), F32),
        scratch_shapes=[pltpu.VMEM((ROW_BATCH, D_MODEL // CONV_CHUNK, CONV_CHUNK // LANES,
                                    ROW_TILE + 2 * SUBLANES, LANES), F32),
                        pltpu.VMEM((D_MODEL, D_MODEL), BF16), pltpu.VMEM((D_MODEL, D_MODEL), BF16)],
        compiler_params=_params("arbitrary", "arbitrary"),
        name="out_lru_conv",
    )(hf, hf, hf, hb, hb, hb, gate, gate, gate, xs, xs, xs, lru_mod, lru_w_out, mod,
      norm_g.reshape(1, D_MODEL), wi, wi, wi, wi, conv_w, conv_b.reshape(1, D_MODEL), w_out)


def _rope_tables():
    quarter = HEAD_DIM // 4
    pos = np.arange(SEQ)
    half = HEAD_DIM // 2
    inv = (1.0 / (ROPE_BASE ** (np.arange(0, half, 2, dtype=np.float32) / half))).astype(np.float32)
    zero = np.zeros((SEQ, quarter), np.float32)
    parts_c, parts_a, parts_b = [], [], []
    for axis_pos in ((pos // GRID_W).astype(np.float32), (pos % GRID_W).astype(np.float32)):
        ang = axis_pos[:, None] * inv
        c, s = np.cos(ang), np.sin(ang)
        parts_c += [c, c]
        parts_a += [-s, zero]
        parts_b += [zero, s]

    def table(parts, ctx_value):
        head = np.concatenate(parts, axis=1)
        head = np.concatenate([np.full((CTX_LEN, HEAD_DIM), ctx_value, np.float32), head], axis=0)
        return jnp.asarray(np.tile(head, (1, LANES // HEAD_DIM)).astype(np.float32))

    return table(parts_c, 1.0), table(parts_a, 0.0), table(parts_b, 0.0)


def kernel(x, c, ctx, c_ctx, l0_norm_g, l0_mod_w, l0_mod_b, l0_w_in, l0_w_out, l0_sink, l1_norm_g, l1_mod_w, l1_mod_b, l1_w_in, l1_w_out, l1_conv_w, l1_conv_b, l1_gate_a_w, l1_gate_a_b, l1_gate_x_w, l1_gate_x_b, l1_lambda, l2_norm_g, l2_mod_w, l2_mod_b, l2_w_in, l2_w_out, l2_conv_w, l2_conv_b, l3_norm_g, l3_mod_w, l3_mod_b, l3_w_in, l3_w_out, l3_sink, final_norm_g):
    mods = _modulation(c, c_ctx, (l0_mod_w, l1_mod_w, l2_mod_w, l3_mod_w),
                       (l0_mod_b, l1_mod_b, l2_mod_b, l3_mod_b))
    rope_tabs = _rope_tables()

    qt, k, vt, gate = _in_attn((ctx, x), mods[0], l0_norm_g, l0_w_in, rope_tabs, ctx_queries=True)
    ot = _attention(qt, k, vt, l0_sink, need_ctx=True)
    xs, u, gate = _out_attn_in_next(ot, gate, l0_w_out, ctx, x, mods[0], l1_norm_g, mods[1], l1_w_in,
                                    (F32, BF16))

    hf, hb = _lru_scan(u, l1_conv_w, l1_conv_b, l1_gate_a_w, l1_gate_a_b, l1_gate_x_w, l1_gate_x_b,
                       l1_lambda)
    xs = _out_lru_conv(hf, hb, gate, l1_w_out, xs, mods[1], mods[2], l2_norm_g, l2_w_in, l2_conv_w,
                       l2_conv_b, l2_w_out)

    qt, k, vt, gate = _in_attn((xs,), mods[3], l3_norm_g, l3_w_in, rope_tabs, ctx_queries=False)
    ot = _attention(qt, k, vt, l3_sink, need_ctx=False)
    return _out_attn_final(ot, gate, l3_w_out, xs, mods[3], final_norm_g)
```

```python
import functools
import math

import jax
import jax.numpy as jnp
import numpy as np
from jax import lax
from jax.experimental import pallas as pl
from jax.experimental.pallas import tpu as pltpu

D_MODEL = 1024
BATCH = 8
SEQ = 2048
CTX_LEN = 256
TOK = CTX_LEN + SEQ
GRID_W = 64
EPS = 1e-6
NEG_INF = -1e30
N_HEADS = 16
N_KV_HEADS = 4
N_GROUPS = N_HEADS // N_KV_HEADS
HEAD_DIM = 64
Q_DIM = N_HEADS * HEAD_DIM
KV_DIM = N_KV_HEADS * HEAD_DIM
WINDOW = 128
ROPE_BASE = 10000.0
LRU_BLOCKS = 16
LRU_BLOCK = 64
LRU_C = 8.0
LRU_CONV = 4
CONV_K = 3
MOD_PARTS = 3

LANES = 128
SUBLANES = 8
MXU_DIM = 256
VMEM_LIMIT_BYTES = 56 * 1024 * 1024

ROW_TILE = 256
ROW_BATCH = 2
OUT_ROW_BATCH = 4
CTX_ROW_TILES = CTX_LEN // ROW_TILE
ROW_TILES = TOK // ROW_TILE
Q_TILE = WINDOW
CTX_Q_TILES = CTX_LEN // Q_TILE
Q_TILES = TOK // Q_TILE
BAND = 3 * WINDOW
MAX_ROWS = 32
TILE_UNROLL = 5
CONV_CHUNK = MXU_DIM

LRU_TL = 256
LRU_WC = MXU_DIM
LRU_PITCH = LRU_TL + 4
LRU_SLABS = LRU_WC // LANES
LRU_CTX_BLKS = CTX_LEN // LRU_TL
LRU_BLKS = TOK // LRU_TL

F32 = jnp.float32
BF16 = jnp.bfloat16
F32_TINY = float(jnp.finfo(jnp.float32).tiny)
LOG2E = math.log2(math.e)


def _params(*sem):
    return pltpu.CompilerParams(dimension_semantics=sem, vmem_limit_bytes=VMEM_LIMIT_BYTES)


def _sigmoid(x):
    return 0.5 * jnp.tanh(0.5 * x) + 0.5


def _silu(x):
    return x * _sigmoid(x)


def _mod_kernel(c_ref, b_ref, *refs):
    w_refs, o_ref = refs[:-1], refs[-1]
    layer = pl.program_id(0) // MOD_PARTS
    s = _silu(c_ref[...])
    s_hi = s.astype(BF16)
    s_lo = (s - s_hi.astype(F32)).astype(BF16)
    for l, w_ref in enumerate(w_refs):
        @pl.when(layer == l)
        def _layer(w_ref=w_ref):
            w = w_ref[...]
            w_hi = w.astype(BF16)
            w_lo = (w - w_hi.astype(F32)).astype(BF16)
            acc = jnp.dot(s_hi, w_hi, preferred_element_type=F32)
            acc += jnp.dot(s_hi, w_lo, preferred_element_type=F32)
            acc += jnp.dot(s_lo, w_hi, preferred_element_type=F32)
            m = acc + b_ref[0]
            o_ref[0, :, 0, :] = jnp.broadcast_to(m[BATCH:BATCH + 1], (BATCH, D_MODEL))
            o_ref[0, :, 1, :] = m[:BATCH]


def _modulation(c, c_ctx, mod_ws, mod_bs):
    n = len(mod_ws)
    cc = jnp.concatenate([c, c_ctx[None], jnp.zeros((2 * SUBLANES - BATCH - 1, D_MODEL), F32)], axis=0)
    rows = cc.shape[0]
    w_spec = lambda l: pl.BlockSpec((D_MODEL, D_MODEL),
                                    lambda s: (0, jnp.clip(s - MOD_PARTS * l, 0, MOD_PARTS - 1)))
    part = lambda s: (s // MOD_PARTS, s % MOD_PARTS)
    m = pl.pallas_call(
        _mod_kernel,
        grid=(MOD_PARTS * n,),
        in_specs=[pl.BlockSpec((rows, D_MODEL), lambda s: (0, 0)),
                  pl.BlockSpec((1, 1, D_MODEL), lambda s: (part(s)[0], 0, part(s)[1]))]
                 + [w_spec(l) for l in range(n)],
        out_specs=pl.BlockSpec((1, BATCH, 2, D_MODEL), lambda s: (part(s)[0], 0, 0, part(s)[1])),
        out_shape=jax.ShapeDtypeStruct((n, BATCH, 2, MOD_PARTS * D_MODEL), F32),
        compiler_params=_params("arbitrary"),
        name="modulation",
    )(cc, jnp.stack(mod_bs).reshape(n, 1, MOD_PARTS * D_MODEL), *mod_ws)
    return m.reshape(n, BATCH, 2, MOD_PARTS, D_MODEL)


def _mod_spec(tile_offset=0, rb=ROW_BATCH):
    def idx(b, j):
        return (b, jnp.where(j + tile_offset < CTX_ROW_TILES, 0, 1), 0, 0)
    return pl.BlockSpec((rb, 1, MOD_PARTS, D_MODEL), idx)


def _row_spec(width, tile_offset=0, rb=ROW_BATCH):
    return pl.BlockSpec((rb, ROW_TILE, width), lambda b, j: (b, j + tile_offset, 0))


def _const_spec(shape, index=None):
    index = index or (lambda b, j: (0,) * len(shape))
    return pl.BlockSpec(shape, index, pipeline_mode=pl.Buffered(1))


def _stream_specs(split, rb=ROW_BATCH):
    if not split:
        return [_row_spec(D_MODEL, rb=rb)]
    blk = (rb, ROW_TILE, D_MODEL)
    return [pl.BlockSpec(blk, lambda b, j: (b, 0, 0)),
            pl.BlockSpec(blk, lambda b, j: (b, jnp.maximum(j - CTX_ROW_TILES, 0), 0))]


def _stream_tile(x_refs, bb):
    if len(x_refs) == 1:
        return x_refs[0][bb]
    ctx_ref, lat_ref = x_refs
    return jnp.where(pl.program_id(1) < CTX_ROW_TILES, ctx_ref[bb], lat_ref[bb])


def _cast_once(w_ref, wb_sc):
    @pl.when((pl.program_id(0) == 0) & (pl.program_id(1) == 0))
    def _cast():
        wb_sc[...] = w_ref[...].astype(BF16)


_ROW_GRID = (BATCH // ROW_BATCH, ROW_TILES)


def _norm_mod(x, g, mod):
    y = x * lax.rsqrt(jnp.mean(x * x, axis=-1, keepdims=True) + EPS) * g
    return y * (1.0 + mod[1:2, :]) + mod[0:1, :]


def _norm_mod_rows(x_refs, g_ref, mod_ref):
    tiles = [_norm_mod(_stream_tile(x_refs, bb), g_ref[...], mod_ref[bb, 0]) for bb in range(ROW_BATCH)]
    return jnp.concatenate(tiles, axis=0).astype(BF16)


def _tile_rows(t, bb):
    return t[bb * ROW_TILE:(bb + 1) * ROW_TILE]


def _in_attn_kernel(*refs, n_x, ctx_queries):
    x_refs = refs[:n_x]
    mod_ref, g_ref, w_ref, rc_ref, ra_ref, rb_ref, qt_ref, k_ref, vt_ref, gate_ref, wb_sc = refs[n_x:]
    _cast_once(w_ref, wb_sc)
    rc, ra, rb = rc_ref[...], ra_ref[...], rb_ref[...]

    def rope(t):
        reps = t.shape[-1] // LANES
        n = t.shape[-1]
        return (t * jnp.tile(rc, (1, reps))
                + pltpu.roll(t, n - HEAD_DIM // 4, 1) * jnp.tile(ra, (1, reps))
                + pltpu.roll(t, HEAD_DIM // 4, 1) * jnp.tile(rb, (1, reps)))

    def keys_values(z_kv, bb):
        k = rope(z_kv[:, :KV_DIM]).astype(BF16)
        for kh in range(N_KV_HEADS):
            k_ref[bb, kh] = k[:, kh * HEAD_DIM:(kh + 1) * HEAD_DIM]
        vt_ref[bb] = z_kv[:, KV_DIM:].T.astype(BF16)

    def all_columns():
        z_all = jnp.dot(_norm_mod_rows(x_refs, g_ref, mod_ref), wb_sc[...], preferred_element_type=F32)
        for bb in range(ROW_BATCH):
            z = _tile_rows(z_all, bb)
            qt_ref[bb] = (rope(z[:, :Q_DIM]) * (LOG2E * HEAD_DIM ** -0.5)).T.astype(BF16)
            keys_values(z[:, Q_DIM:Q_DIM + 2 * KV_DIM], bb)
            gate_ref[bb] = z[:, Q_DIM + 2 * KV_DIM:].astype(gate_ref.dtype)

    def keys_values_only():
        z_kv = jnp.dot(_norm_mod_rows(x_refs, g_ref, mod_ref), wb_sc[:, Q_DIM:Q_DIM + 2 * KV_DIM],
                       preferred_element_type=F32)
        for bb in range(ROW_BATCH):
            keys_values(_tile_rows(z_kv, bb), bb)
        qt_ref[...] = jnp.zeros_like(qt_ref)
        gate_ref[...] = jnp.zeros_like(gate_ref)

    if ctx_queries:
        all_columns()
    else:
        pl.when(pl.program_id(1) >= CTX_ROW_TILES)(all_columns)
        pl.when(pl.program_id(1) < CTX_ROW_TILES)(keys_values_only)


def _in_attn(streams, mod, norm_g, w_in, rope_tabs, ctx_queries):
    n = w_in.shape[1]
    col = lambda height: pl.BlockSpec((ROW_BATCH, height, ROW_TILE), lambda b, j: (b, 0, j))
    tab = pl.BlockSpec((ROW_TILE, LANES), lambda b, j: (j, 0))
    return pl.pallas_call(
        functools.partial(_in_attn_kernel, n_x=len(streams), ctx_queries=ctx_queries),
        grid=_ROW_GRID,
        in_specs=_stream_specs(len(streams) > 1)
                 + [_mod_spec(), _const_spec((1, D_MODEL)), _const_spec((D_MODEL, n)), tab, tab, tab],
        out_specs=[col(Q_DIM),
                   pl.BlockSpec((ROW_BATCH, N_KV_HEADS, ROW_TILE, HEAD_DIM), lambda b, j: (b, 0, j, 0)),
                   col(KV_DIM), _row_spec(Q_DIM)],
        out_shape=[jax.ShapeDtypeStruct((BATCH, Q_DIM, TOK), BF16),
                   jax.ShapeDtypeStruct((BATCH, N_KV_HEADS, TOK, HEAD_DIM), BF16),
                   jax.ShapeDtypeStruct((BATCH, KV_DIM, TOK), BF16),
                   jax.ShapeDtypeStruct((BATCH, TOK, Q_DIM), BF16)],
        scratch_shapes=[pltpu.VMEM((D_MODEL, n), BF16)],
        compiler_params=_params("arbitrary", "arbitrary"),
        name="in_attn",
    )(*streams, mod, norm_g.reshape(1, D_MODEL), w_in, *rope_tabs)


def _attn_kernel(sink_ref, qt_ref, k_ref, vt_ref, ot_ref, s_sc, bias_sc, *, first_tile):
    lanes = N_GROUPS * Q_TILE

    key = lax.broadcasted_iota(jnp.int32, (WINDOW, lanes), 0)
    query = lax.broadcasted_iota(jnp.int32, (WINDOW, lanes), 1) & (Q_TILE - 1)
    bias_sc[0] = jnp.where(key >= query, 0.0, NEG_INF)
    bias_sc[1] = jnp.where(key <= query, 0.0, NEG_INF)
    bias_sc[2] = jnp.full((WINDOW, lanes), NEG_INF, F32)

    def aligned(start):
        return start if isinstance(start, int) else pl.multiple_of(start, WINDOW)

    def heads_of(kh):
        return [kh * N_GROUPS + g for g in range(N_GROUPS)]

    def sink_row(kh):
        return jnp.concatenate([jnp.full((1, Q_TILE), sink_ref[h] * LOG2E, F32) for h in heads_of(kh)], axis=1)

    def score_steps(j, kh, chunks, m_box):
        cols = pl.ds(aligned(j * Q_TILE), Q_TILE)
        qt = jnp.concatenate([qt_ref[0, h * HEAD_DIM:(h + 1) * HEAD_DIM, cols] for h in heads_of(kh)], axis=1)
        mx = jnp.broadcast_to(sink_row(kh), (MAX_ROWS, lanes))
        keys = jnp.concatenate([k_ref[0, kh, pl.ds(ks, WINDOW), :] for ks, _ in chunks], axis=0)
        s_all = jnp.dot(keys, qt, preferred_element_type=F32)
        for c, (ks, bias) in enumerate(chunks):
            s = s_all[c * WINDOW:(c + 1) * WINDOW]
            if bias is not None:
                s = s + bias_sc[bias]
            s_sc[kh, c * WINDOW:(c + 1) * WINDOW, :] = s
            mx = jnp.maximum(mx, s.reshape(WINDOW // MAX_ROWS, MAX_ROWS, lanes).max(axis=0))
            yield
        m_box.append(mx.max(axis=0, keepdims=True))

    def value_steps(j, kh, chunks, m_box):
        m = m_box[0]
        acc = jnp.zeros((HEAD_DIM, lanes), F32)
        denom = jnp.zeros((SUBLANES, lanes), F32)
        group = []
        for c, (ks, _) in enumerate(chunks):
            p = jnp.exp2(s_sc[kh, c * WINDOW:(c + 1) * WINDOW, :] - m)
            denom = denom + p.reshape(WINDOW // SUBLANES, SUBLANES, lanes).sum(axis=0)
            group.append((vt_ref[0, kh * HEAD_DIM:(kh + 1) * HEAD_DIM, pl.ds(ks, WINDOW)], p.astype(BF16)))
            if len(group) == MXU_DIM // WINDOW or c == len(chunks) - 1:
                vt = jnp.concatenate([g[0] for g in group], axis=1)
                pp = jnp.concatenate([g[1] for g in group], axis=0)
                acc = acc + jnp.dot(vt, pp, preferred_element_type=F32)
                group = []
            yield
        denom = denom.sum(axis=0, keepdims=True) + jnp.exp2(sink_row(kh) - m)
        o = acc / denom
        cols = pl.ds(aligned((j - first_tile) * Q_TILE), Q_TILE)
        for g, h in enumerate(heads_of(kh)):
            ot_ref[0, h * HEAD_DIM:(h + 1) * HEAD_DIM, cols] = o[:, g * Q_TILE:(g + 1) * Q_TILE].astype(BF16)

    def interleave(*stages):
        stages = list(stages)
        while stages:
            stages = [st for st in stages if next(st, StopIteration) is not StopIteration]

    ctx_chunks = [(c * WINDOW, None) for c in range(CTX_LEN // WINDOW)]

    def latent_chunks(j):
        own = j * Q_TILE
        if isinstance(j, int):
            prev_bias = 2 if j == CTX_Q_TILES else 0
            next_bias = 2 if j == Q_TILES - 1 else 1
            next_start = min(own + WINDOW, TOK - WINDOW)
        else:
            prev_bias = jnp.where(j == CTX_Q_TILES, 2, 0)
            next_bias = jnp.where(j == Q_TILES - 1, 2, 1)
            next_start = jnp.minimum(own + WINDOW, TOK - WINDOW)
        return [(aligned(own - WINDOW), prev_bias), (aligned(own), None),
                (aligned(next_start), next_bias)] + ctx_chunks

    if first_tile == 0:
        units = [(j, kh, ctx_chunks) for j in range(CTX_Q_TILES) for kh in range(N_KV_HEADS)]
        boxes = [[] for _ in units]
        interleave(score_steps(*units[0], boxes[0]))
        for i, unit in enumerate(units):
            stages = [score_steps(*units[i + 1], boxes[i + 1])] if i + 1 < len(units) else []
            interleave(*stages, value_steps(*unit, boxes[i]))

    def latent_tile(j, m_head0, next_j):
        chunks = latent_chunks(j)
        box = [m_head0]
        for kh in range(N_KV_HEADS):
            nxt = []
            stages = []
            if kh + 1 < N_KV_HEADS:
                stages.append(score_steps(j, kh + 1, chunks, nxt))
            elif next_j is not None:
                stages.append(score_steps(next_j, 0, latent_chunks(next_j), nxt))
            interleave(*stages, value_steps(j, kh, chunks, box))
            box = nxt
        return box[0] if box else None

    first_box = []
    interleave(score_steps(CTX_Q_TILES, 0, latent_chunks(CTX_Q_TILES), first_box))
    m_last = lax.fori_loop(CTX_Q_TILES, Q_TILES - 1, lambda j, m: latent_tile(j, m, j + 1), first_box[0],
                           unroll=TILE_UNROLL)
    latent_tile(Q_TILES - 1, m_last, None)


def _attention(qt, k, vt, sink, need_ctx):
    first_tile = 0 if need_ctx else CTX_Q_TILES
    n_keys = BAND + CTX_LEN
    out_cols = TOK - first_tile * Q_TILE
    return pl.pallas_call(
        functools.partial(_attn_kernel, first_tile=first_tile),
        grid=(BATCH,),
        in_specs=[pl.BlockSpec(memory_space=pltpu.SMEM),
                  pl.BlockSpec((1, Q_DIM, TOK), lambda b: (b, 0, 0)),
                  pl.BlockSpec((1, N_KV_HEADS, TOK, HEAD_DIM), lambda b: (b, 0, 0, 0)),
                  pl.BlockSpec((1, KV_DIM, TOK), lambda b: (b, 0, 0))],
        out_specs=pl.BlockSpec((1, Q_DIM, out_cols), lambda b: (b, 0, 0)),
        out_shape=jax.ShapeDtypeStruct((BATCH, Q_DIM, out_cols), BF16),
        scratch_shapes=[pltpu.VMEM((N_KV_HEADS, n_keys, N_GROUPS * Q_TILE), F32),
                        pltpu.VMEM((3, WINDOW, N_GROUPS * Q_TILE), F32)],
        compiler_params=_params("arbitrary"),
        name="attention",
    )(sink, qt, k, vt)


def _lru_kernel(uf_ref, ub_ref, cw_ref, cb_ref, wa_ref, ba_ref, wx_ref, bx_ref, lam_ref,
                hf_ref, hb_ref, u_sc, a_sc, b_sc, y_sc, h_sc):
    step = pl.program_id(1)
    tl, pitch, halo = LRU_TL, LRU_PITCH, SUBLANES
    segment_start = (step == 0) | (step == LRU_CTX_BLKS)

    @pl.when(segment_start)
    def _zero_halo():
        u_sc[0, :, 0:halo, :] = jnp.zeros((BATCH * LRU_SLABS, halo, LANES), F32)
        u_sc[1, :, halo + tl:, :] = jnp.zeros((BATCH * LRU_SLABS, halo, LANES), F32)

    @pl.when(jnp.logical_not(segment_start))
    def _carry_halo():
        u_sc[0, :, 0:halo, :] = u_sc[0, :, tl:tl + halo, :]
        u_sc[1, :, halo + tl:, :] = u_sc[1, :, halo:2 * halo, :]

    @pl.when(step == 0)
    def _sequence_start():
        h_sc[...] = jnp.zeros_like(h_sc)

    for d, u_ref in ((0, uf_ref), (1, ub_ref)):
        for bi in range(BATCH):
            for s in range(LRU_SLABS):
                u_sc[d, bi * LRU_SLABS + s, halo:halo + tl, :] = u_ref[bi, :, s * LANES:(s + 1) * LANES]

    for d in range(2):
        nl = -lam_ref[d]
        half_decay = (0.5 * LRU_C) * (jnp.maximum(nl, 0.0) + jnp.log1p(jnp.exp(-jnp.abs(nl))))
        for bi in range(BATCH):
            cols = []
            for s in range(LRU_SLABS):
                lanes = slice(s * LANES, (s + 1) * LANES)
                half_x = 0.5 * cb_ref[d][:, lanes]
                for kk in range(LRU_CONV):
                    off = halo + (kk - (LRU_CONV - 1) if d == 0 else (LRU_CONV - 1) - kk)
                    rows = pl.ds(off, tl, stride=1)
                    tap = 0.5 * cw_ref[d][kk:kk + 1, lanes]
                    half_x = half_x + tap * u_sc[d, bi * LRU_SLABS + s, rows, :]
                cols.append(half_x)
            half_x = jnp.concatenate(cols, axis=1)
            xb = half_x.astype(BF16)
            tr = jnp.tanh(jnp.dot(xb, wa_ref[d, 0], preferred_element_type=F32) + ba_ref[d])
            ti = jnp.tanh(jnp.dot(xb, wx_ref[d, 0], preferred_element_type=F32) + bx_ref[d])
            neg_log_a = tr * half_decay + half_decay
            ix = ti * half_x + half_x
            a = jnp.exp(-neg_log_a)
            var = jnp.tanh(neg_log_a) * (a * a + 1.0)
            b = (var * lax.rsqrt(jnp.maximum(var, F32_TINY))) * ix
            for s in range(LRU_SLABS):
                lanes = slice(s * LANES, (s + 1) * LANES)
                rows = pl.ds(bi * pitch, tl, stride=1)
                a_sc[d, s, rows, :] = a[:, lanes]
                b_sc[d, s, rows, :] = b[:, lanes]

    def scan_step(t, carry):
        new = []
        for d in range(2):
            tt = t if d == 0 else tl - 1 - t
            for s in range(LRU_SLABS):
                rows = pl.ds(tt, BATCH, stride=pitch)
                h = a_sc[d, s, rows, :] * carry[d * LRU_SLABS + s] + b_sc[d, s, rows, :]
                y_sc[d, s, rows, :] = h
                new.append(h)
        return tuple(new)

    init = tuple(h_sc[d, s] for d in range(2) for s in range(LRU_SLABS))
    final = lax.fori_loop(0, tl, scan_step, init, unroll=8)
    for d in range(2):
        for s in range(LRU_SLABS):
            h_sc[d, s] = final[d * LRU_SLABS + s]

    for d, o_ref in ((0, hf_ref), (1, hb_ref)):
        for bi in range(BATCH):
            for s in range(LRU_SLABS):
                o_ref[bi, :, s * LANES:(s + 1) * LANES] = (
                    y_sc[d, s, pl.ds(bi * pitch, tl, stride=1), :].astype(o_ref.dtype))


def _block_diag_chunks(w):
    per = MXU_DIM // LRU_BLOCK
    w = w.reshape(2, LRU_BLOCKS // per, per, LRU_BLOCK, LRU_BLOCK)
    eye = jnp.eye(per, dtype=w.dtype)
    bd = jnp.einsum('dcpij,pq->dcpiqj', w, eye)
    return bd.reshape(2, LRU_BLOCKS // per, MXU_DIM, MXU_DIM).astype(BF16)


def _lru_scan(u, conv_w, conv_b, gate_a_w, gate_a_b, gate_x_w, gate_x_b, lam):
    def fwd_blk(w, s):
        return (0, s, w)

    def bwd_blk(w, s):
        blk = jnp.where(s < LRU_CTX_BLKS, LRU_CTX_BLKS - 1 - s, LRU_BLKS - 1 - (s - LRU_CTX_BLKS))
        return (0, blk, w)

    blk = (BATCH, LRU_TL, LRU_WC)
    vec = pl.BlockSpec((2, 1, LRU_WC), lambda w, s: (0, 0, w))
    gate = pl.BlockSpec((2, LRU_WC // MXU_DIM, MXU_DIM, MXU_DIM), lambda w, s: (0, w, 0, 0))
    width = u.shape[-1]
    out = jax.ShapeDtypeStruct((BATCH, TOK, width), BF16)
    scan_buf = pltpu.VMEM((2, LRU_SLABS, BATCH * LRU_PITCH, LANES), F32)
    return pl.pallas_call(
        _lru_kernel,
        grid=(width // LRU_WC, LRU_BLKS),
        in_specs=[pl.BlockSpec(blk, fwd_blk), pl.BlockSpec(blk, bwd_blk),
                  pl.BlockSpec((2, LRU_CONV, LRU_WC), lambda w, s: (0, 0, w)), vec,
                  gate, vec, gate, vec, vec],
        out_specs=[pl.BlockSpec(blk, fwd_blk), pl.BlockSpec(blk, bwd_blk)],
        out_shape=[out, out],
        scratch_shapes=[pltpu.VMEM((2, BATCH * LRU_SLABS, LRU_TL + 2 * SUBLANES, LANES), F32),
                        scan_buf, scan_buf, scan_buf,
                        pltpu.VMEM((2, LRU_SLABS, BATCH, LANES), F32)],
        compiler_params=_params("parallel", "arbitrary"),
        name="lru_scan",
    )(u, u, conv_w, conv_b.reshape(2, 1, width),
      _block_diag_chunks(gate_a_w), 0.5 * gate_a_b.reshape(2, 1, width),
      _block_diag_chunks(gate_x_w), 0.5 * gate_x_b.reshape(2, 1, width),
      lam.reshape(2, 1, width))


def _residual_rows(x_refs, mod_ref, acts, wb_sc, o_ref, post=None):
    y = jnp.dot(jnp.concatenate(acts, axis=0).astype(BF16), wb_sc[...], preferred_element_type=F32)
    for bb in range(len(acts)):
        out = _stream_tile(x_refs, bb) + mod_ref[bb, 0][2:3, :] * _tile_rows(y, bb)
        o_ref[bb] = out if post is None else post(out)


def _out_attn_final_kernel(ot_ref, g_ref, w_ref, x_ref, mod_ref, fg_ref, o_ref, wb_sc):
    _cast_once(w_ref, wb_sc)
    acts = [ot_ref[bb].astype(F32).T * _silu(g_ref[bb].astype(F32)) for bb in range(OUT_ROW_BATCH)]

    def final_norm(x):
        return x * lax.rsqrt(jnp.mean(x * x, axis=-1, keepdims=True) + EPS) * fg_ref[...]

    _residual_rows((x_ref,), mod_ref, acts, wb_sc, o_ref, final_norm)


def _out_attn_in_next_kernel(ot_ref, g_ref, wo_ref, ctx_ref, lat_ref, mod_ref, ng_ref, nmod_ref, wi_ref,
                             xs_ref, *refs):
    o_refs, (wo_sc, wi_sc) = refs[:-2], refs[-2:]
    _cast_once(wo_ref, wo_sc)
    _cast_once(wi_ref, wi_sc)
    acts = [ot_ref[bb].astype(F32).T * _silu(g_ref[bb].astype(F32)) for bb in range(ROW_BATCH)]
    y = jnp.dot(jnp.concatenate(acts, axis=0).astype(BF16), wo_sc[...], preferred_element_type=F32)
    tiles = []
    for bb in range(ROW_BATCH):
        out = _stream_tile((ctx_ref, lat_ref), bb) + mod_ref[bb, 0][2:3, :] * _tile_rows(y, bb)
        xs_ref[bb] = out
        tiles.append(_norm_mod(out, ng_ref[...], nmod_ref[bb, 0]))
    z = jnp.dot(jnp.concatenate(tiles, axis=0).astype(BF16), wi_sc[...], preferred_element_type=F32)
    width = z.shape[-1] // len(o_refs)
    for i, o_ref in enumerate(o_refs):
        for bb in range(ROW_BATCH):
            o_ref[bb] = _tile_rows(z, bb)[:, i * width:(i + 1) * width].astype(o_ref.dtype)


def _halo_rows(main_ref, prev_ref, next_ref, bb):
    return jnp.concatenate([prev_ref[bb], main_ref[bb], next_ref[bb]], axis=0).astype(F32)


def _out_lru_conv_kernel(hf_ref, hfp_ref, hfn_ref, hb_ref, hbp_ref, hbn_ref, gl_ref, glp_ref, gln_ref,
                         x_ref, xp_ref, xn_ref, lmod_ref, lwo_ref, mod_ref, g_ref, wu_ref, wb_ref, wc_ref,
                         wg_ref, cw_ref, cb_ref, wo_ref, o_ref, p_sc, lwo_sc, wo_sc):
    j = pl.program_id(1)
    halo, cc = SUBLANES, CONV_CHUNK
    rows = ROW_TILE + 2 * halo
    _cast_once(lwo_ref, lwo_sc)
    _cast_once(wo_ref, wo_sc)
    acts = [(_halo_rows(hf_ref, hfp_ref, hfn_ref, bb) + _halo_rows(hb_ref, hbp_ref, hbn_ref, bb))
            * _silu(_halo_rows(gl_ref, glp_ref, gln_ref, bb)) for bb in range(ROW_BATCH)]
    y = jnp.dot(jnp.concatenate(acts, axis=0).astype(BF16), lwo_sc[...], preferred_element_type=F32)
    x1 = [_halo_rows(x_ref, xp_ref, xn_ref, bb) + lmod_ref[bb, 0][2:3, :] * y[bb * rows:(bb + 1) * rows]
          for bb in range(ROW_BATCH)]
    tiles = [_norm_mod(x1[bb], g_ref[...], mod_ref[bb, 0]) for bb in range(ROW_BATCH)]
    h = jnp.concatenate(tiles, axis=0).astype(BF16)
    h_main = jnp.concatenate([h[bb * rows + halo:bb * rows + halo + ROW_TILE] for bb in range(ROW_BATCH)], axis=0)
    keep_prev = jnp.where((j == 0) | (j == CTX_ROW_TILES), 0.0, 1.0)
    keep_next = jnp.where((j == CTX_ROW_TILES - 1) | (j == ROW_TILES - 1), 0.0, 1.0)
    acc = jnp.zeros((ROW_BATCH * ROW_TILE, D_MODEL), F32)
    for c in range(D_MODEL // cc):
        chans = slice(c * cc, (c + 1) * cc)
        u, cg = (jnp.dot(h, w_ref[:, chans], preferred_element_type=F32) for w_ref in (wu_ref, wc_ref))
        bg, g = (jnp.dot(h_main, w_ref[:, chans], preferred_element_type=F32) for w_ref in (wb_ref, wg_ref))
        p = cg * u
        acts = []
        for bb in range(ROW_BATCH):
            top = bb * rows
            main = slice(top + halo, top + halo + ROW_TILE)
            tile = slice(bb * ROW_TILE, (bb + 1) * ROW_TILE)
            for s in range(cc // LANES):
                lanes = slice(s * LANES, (s + 1) * LANES)
                p_sc[bb, c, s, 0:halo, :] = p[top:top + halo, lanes] * keep_prev
                p_sc[bb, c, s, halo:halo + ROW_TILE, :] = p[main, lanes]
                p_sc[bb, c, s, halo + ROW_TILE:rows, :] = p[top + halo + ROW_TILE:top + rows, lanes] * keep_next
            before = jnp.concatenate([p_sc[bb, c, s, pl.ds(halo - 1, ROW_TILE, stride=1), :]
                                      for s in range(cc // LANES)], axis=1)
            after = jnp.concatenate([p_sc[bb, c, s, pl.ds(halo + 1, ROW_TILE, stride=1), :]
                                     for s in range(cc // LANES)], axis=1)
            conv = (cw_ref[0:1, chans] * before + cw_ref[1:2, chans] * p[main] + cw_ref[2:3, chans] * after
                    + cb_ref[:, chans])
            acts.append(bg[tile] * conv * _silu(g[tile]))
        acc = acc + jnp.dot(jnp.concatenate(acts, axis=0).astype(BF16), wo_sc[chans, :],
                            preferred_element_type=F32)
    for bb in range(ROW_BATCH):
        o_ref[bb] = x1[bb][halo:halo + ROW_TILE] + mod_ref[bb, 0][2:3, :] * _tile_rows(acc, bb)


_W_OUT_SPEC = _const_spec((D_MODEL, D_MODEL))


def _out_attn_final(ot, gate, w_out, xs, mod, final_g):
    off, rb = CTX_ROW_TILES, OUT_ROW_BATCH
    return pl.pallas_call(
        _out_attn_final_kernel,
        grid=(BATCH // rb, ROW_TILES - off),
        in_specs=[pl.BlockSpec((rb, Q_DIM, ROW_TILE), lambda b, j: (b, 0, j)), _row_spec(Q_DIM, off, rb),
                  _W_OUT_SPEC, _row_spec(D_MODEL, off, rb), _mod_spec(off, rb), _const_spec((1, D_MODEL))],
        out_specs=_row_spec(D_MODEL, rb=rb),
        out_shape=jax.ShapeDtypeStruct((BATCH, SEQ, D_MODEL), F32),
        scratch_shapes=[pltpu.VMEM((D_MODEL, D_MODEL), BF16)],
        compiler_params=_params("arbitrary", "arbitrary"),
        name="out_attn_final",
    )(ot, gate, w_out, xs, mod, final_g.reshape(1, D_MODEL))


def _out_attn_in_next(ot, gate, w_out, ctx, x, mod, next_norm_g, next_mod, next_w_in, dtypes):
    n = next_w_in.shape[1]
    width = n // len(dtypes)
    return pl.pallas_call(
        _out_attn_in_next_kernel,
        grid=_ROW_GRID,
        in_specs=[pl.BlockSpec((ROW_BATCH, Q_DIM, ROW_TILE), lambda b, j: (b, 0, j)), _row_spec(Q_DIM),
                  _W_OUT_SPEC] + _stream_specs(True)
                 + [_mod_spec(), _const_spec((1, D_MODEL)), _mod_spec(), _const_spec((D_MODEL, n))],
        out_specs=[_row_spec(D_MODEL)] + [_row_spec(width)] * len(dtypes),
        out_shape=[jax.ShapeDtypeStruct((BATCH, TOK, D_MODEL), F32)]
                  + [jax.ShapeDtypeStruct((BATCH, TOK, width), dt) for dt in dtypes],
        scratch_shapes=[pltpu.VMEM((D_MODEL, D_MODEL), BF16), pltpu.VMEM((D_MODEL, n), BF16)],
        compiler_params=_params("arbitrary", "arbitrary"),
        name="out_attn_in_lru",
    )(ot, gate, w_out, ctx, x, mod, next_norm_g.reshape(1, D_MODEL), next_mod, next_w_in)


def _out_lru_conv(hf, hb, gate, lru_w_out, xs, lru_mod, mod, norm_g, w_in, conv_w, conv_b, w_out):
    per = ROW_TILE // SUBLANES
    n8 = TOK // SUBLANES
    halo = lambda idx: pl.BlockSpec((ROW_BATCH, SUBLANES, D_MODEL), idx)
    prev = halo(lambda b, j: (b, jnp.maximum(j * per - 1, 0), 0))
    nxt = halo(lambda b, j: (b, jnp.minimum((j + 1) * per, n8 - 1), 0))
    with_halo = [_row_spec(D_MODEL), prev, nxt]
    wi = w_in.astype(BF16)
    w_blocks = [_const_spec((D_MODEL, D_MODEL), functools.partial(lambda i, b, j: (0, i), i)) for i in range(4)]
    return pl.pallas_call(
        _out_lru_conv_kernel,
        grid=_ROW_GRID,
        in_specs=with_halo * 4 + [_mod_spec(), _W_OUT_SPEC, _mod_spec(), _const_spec((1, D_MODEL))] + w_blocks
                 + [_const_spec((CONV_K, D_MODEL)), _const_spec((1, D_MODEL)), _W_OUT_SPEC],
        out_specs=_row_spec(D_MODEL),
        out_shape=jax.ShapeDtypeStruct((BATCH, TOK, D_MODEL), F32),
        scratch_shapes=[pltpu.VMEM((ROW_BATCH, D_MODEL // CONV_CHUNK, CONV_CHUNK // LANES,
                                    ROW_TILE + 2 * SUBLANES, LANES), F32),
                        pltpu.VMEM((D_MODEL, D_MODEL), BF16), pltpu.VMEM((D_MODEL, D_MODEL), BF16)],
        compiler_params=_params("arbitrary", "arbitrary"),
        name="out_lru_conv",
    )(hf, hf, hf, hb, hb, hb, gate, gate, gate, xs, xs, xs, lru_mod, lru_w_out, mod,
      norm_g.reshape(1, D_MODEL), wi, wi, wi, wi, conv_w, conv_b.reshape(1, D_MODEL), w_out)


def _rope_tables():
    quarter = HEAD_DIM // 4
    pos = np.arange(SEQ)
    half = HEAD_DIM // 2
    inv = (1.0 / (ROPE_BASE ** (np.arange(0, half, 2, dtype=np.float32) / half))).astype(np.float32)
    zero = np.zeros((SEQ, quarter), np.float32)
    parts_c, parts_a, parts_b = [], [], []
    for axis_pos in ((pos // GRID_W).astype(np.float32), (pos % GRID_W).astype(np.float32)):
        ang = axis_pos[:, None] * inv
        c, s = np.cos(ang), np.sin(ang)
        parts_c += [c, c]
        parts_a += [-s, zero]
        parts_b += [zero, s]

    def table(parts, ctx_value):
        head = np.concatenate(parts, axis=1)
        head = np.concatenate([np.full((CTX_LEN, HEAD_DIM), ctx_value, np.float32), head], axis=0)
        return jnp.asarray(np.tile(head, (1, LANES // HEAD_DIM)).astype(np.float32))

    return table(parts_c, 1.0), table(parts_a, 0.0), table(parts_b, 0.0)


def kernel(x, c, ctx, c_ctx, l0_norm_g, l0_mod_w, l0_mod_b, l0_w_in, l0_w_out, l0_sink, l1_norm_g, l1_mod_w, l1_mod_b, l1_w_in, l1_w_out, l1_conv_w, l1_conv_b, l1_gate_a_w, l1_gate_a_b, l1_gate_x_w, l1_gate_x_b, l1_lambda, l2_norm_g, l2_mod_w, l2_mod_b, l2_w_in, l2_w_out, l2_conv_w, l2_conv_b, l3_norm_g, l3_mod_w, l3_mod_b, l3_w_in, l3_w_out, l3_sink, final_norm_g):
    mods = _modulation(c, c_ctx, (l0_mod_w, l1_mod_w, l2_mod_w, l3_mod_w),
                       (l0_mod_b, l1_mod_b, l2_mod_b, l3_mod_b))
    rope_tabs = _rope_tables()

    qt, k, vt, gate = _in_attn((ctx, x), mods[0], l0_norm_g, l0_w_in, rope_tabs, ctx_queries=True)
    ot = _attention(qt, k, vt, l0_sink, need_ctx=True)
    xs, u, gate = _out_attn_in_next(ot, gate, l0_w_out, ctx, x, mods[0], l1_norm_g, mods[1], l1_w_in,
                                    (F32, BF16))

    hf, hb = _lru_scan(u, l1_conv_w, l1_conv_b, l1_gate_a_w, l1_gate_a_b, l1_gate_x_w, l1_gate_x_b,
                       l1_lambda)
    xs = _out_lru_conv(hf, hb, gate, l1_w_out, xs, mods[1], mods[2], l2_norm_g, l2_w_in, l2_conv_w,
                       l2_conv_b, l2_w_out)

    qt, k, vt, gate = _in_attn((xs,), mods[3], l3_norm_g, l3_w_in, rope_tabs, ctx_queries=False)
    ot = _attention(qt, k, vt, l3_sink, need_ctx=False)
    return _out_attn_final(ot, gate, l3_w_out, xs, mods[3], final_norm_g)
```

```python
import functools
import math

import jax
import jax.numpy as jnp
import numpy as np
from jax import lax
from jax.experimental import pallas as pl
from jax.experimental.pallas import tpu as pltpu

D_MODEL = 1024
BATCH = 8
SEQ = 2048
CTX_LEN = 256
TOK = CTX_LEN + SEQ
GRID_W = 64
EPS = 1e-6
NEG_INF = -1e30
N_HEADS = 16
N_KV_HEADS = 4
N_GROUPS = N_HEADS // N_KV_HEADS
HEAD_DIM = 64
Q_DIM = N_HEADS * HEAD_DIM
KV_DIM = N_KV_HEADS * HEAD_DIM
WINDOW = 128
ROPE_BASE = 10000.0
LRU_BLOCKS = 16
LRU_BLOCK = 64
LRU_C = 8.0
LRU_CONV = 4
CONV_K = 3
MOD_PARTS = 3

LANES = 128
SUBLANES = 8
MXU_DIM = 256
VMEM_LIMIT_BYTES = 56 * 1024 * 1024

ROW_TILE = 256
ROW_BATCH = 2
OUT_ROW_BATCH = 4
CTX_ROW_TILES = CTX_LEN // ROW_TILE
ROW_TILES = TOK // ROW_TILE
Q_TILE = WINDOW
CTX_Q_TILES = CTX_LEN // Q_TILE
Q_TILES = TOK // Q_TILE
BAND = 3 * WINDOW
MAX_ROWS = 32
TILE_UNROLL = 5
CONV_CHUNK = MXU_DIM

LRU_TL = 256
LRU_WC = MXU_DIM
LRU_PITCH = LRU_TL + 4
LRU_SLABS = LRU_WC // LANES
LRU_CTX_BLKS = CTX_LEN // LRU_TL
LRU_BLKS = TOK // LRU_TL

F32 = jnp.float32
BF16 = jnp.bfloat16
F32_TINY = float(jnp.finfo(jnp.float32).tiny)
LOG2E = math.log2(math.e)


def _params(*sem):
    return pltpu.CompilerParams(dimension_semantics=sem, vmem_limit_bytes=VMEM_LIMIT_BYTES)


def _sigmoid(x):
    return 0.5 * jnp.tanh(0.5 * x) + 0.5


def _silu(x):
    return x * _sigmoid(x)


def _mod_kernel(c_ref, b_ref, *refs):
    w_refs, o_ref = refs[:-1], refs[-1]
    layer = pl.program_id(0) // MOD_PARTS
    s = _silu(c_ref[...])
    s_hi = s.astype(BF16)
    s_lo = (s - s_hi.astype(F32)).astype(BF16)
    for l, w_ref in enumerate(w_refs):
        @pl.when(layer == l)
        def _layer(w_ref=w_ref):
            w = w_ref[...]
            w_hi = w.astype(BF16)
            w_lo = (w - w_hi.astype(F32)).astype(BF16)
            acc = jnp.dot(s_hi, w_hi, preferred_element_type=F32)
            acc += jnp.dot(s_hi, w_lo, preferred_element_type=F32)
            acc += jnp.dot(s_lo, w_hi, preferred_element_type=F32)
            m = acc + b_ref[0]
            o_ref[0, :, 0, :] = jnp.broadcast_to(m[BATCH:BATCH + 1], (BATCH, D_MODEL))
            o_ref[0, :, 1, :] = m[:BATCH]


def _modulation(c, c_ctx, mod_ws, mod_bs):
    n = len(mod_ws)
    cc = jnp.concatenate([c, c_ctx[None], jnp.zeros((2 * SUBLANES - BATCH - 1, D_MODEL), F32)], axis=0)
    rows = cc.shape[0]
    w_spec = lambda l: pl.BlockSpec((D_MODEL, D_MODEL),
                                    lambda s: (0, jnp.clip(s - MOD_PARTS * l, 0, MOD_PARTS - 1)))
    part = lambda s: (s // MOD_PARTS, s % MOD_PARTS)
    m = pl.pallas_call(
        _mod_kernel,
        grid=(MOD_PARTS * n,),
        in_specs=[pl.BlockSpec((rows, D_MODEL), lambda s: (0, 0)),
                  pl.BlockSpec((1, 1, D_MODEL), lambda s: (part(s)[0], 0, part(s)[1]))]
                 + [w_spec(l) for l in range(n)],
        out_specs=pl.BlockSpec((1, BATCH, 2, D_MODEL), lambda s: (part(s)[0], 0, 0, part(s)[1])),
        out_shape=jax.ShapeDtypeStruct((n, BATCH, 2, MOD_PARTS * D_MODEL), F32),
        compiler_params=_params("arbitrary"),
        name="modulation",
    )(cc, jnp.stack(mod_bs).reshape(n, 1, MOD_PARTS * D_MODEL), *mod_ws)
    return m.reshape(n, BATCH, 2, MOD_PARTS, D_MODEL)


def _mod_spec(tile_offset=0, rb=ROW_BATCH):
    def idx(b, j):
        return (b, jnp.where(j + tile_offset < CTX_ROW_TILES, 0, 1), 0, 0)
    return pl.BlockSpec((rb, 1, MOD_PARTS, D_MODEL), idx)


def _row_spec(width, tile_offset=0, rb=ROW_BATCH):
    return pl.BlockSpec((rb, ROW_TILE, width), lambda b, j: (b, j + tile_offset, 0))


def _const_spec(shape, index=None):
    index = index or (lambda b, j: (0,) * len(shape))
    return pl.BlockSpec(shape, index, pipeline_mode=pl.Buffered(1))


def _stream_specs(split, rb=ROW_BATCH):
    if not split:
        return [_row_spec(D_MODEL, rb=rb)]
    blk = (rb, ROW_TILE, D_MODEL)
    return [pl.BlockSpec(blk, lambda b, j: (b, 0, 0)),
            pl.BlockSpec(blk, lambda b, j: (b, jnp.maximum(j - CTX_ROW_TILES, 0), 0))]


def _stream_tile(x_refs, bb):
    if len(x_refs) == 1:
        return x_refs[0][bb]
    ctx_ref, lat_ref = x_refs
    return jnp.where(pl.program_id(1) < CTX_ROW_TILES, ctx_ref[bb], lat_ref[bb])


def _cast_once(w_ref, wb_sc):
    @pl.when((pl.program_id(0) == 0) & (pl.program_id(1) == 0))
    def _cast():
        wb_sc[...] = w_ref[...].astype(BF16)


_ROW_GRID = (BATCH // ROW_BATCH, ROW_TILES)


def _norm_mod(x, g, mod):
    y = x * lax.rsqrt(jnp.mean(x * x, axis=-1, keepdims=True) + EPS) * g
    return y * (1.0 + mod[1:2, :]) + mod[0:1, :]


def _norm_mod_rows(x_refs, g_ref, mod_ref):
    tiles = [_norm_mod(_stream_tile(x_refs, bb), g_ref[...], mod_ref[bb, 0]) for bb in range(ROW_BATCH)]
    return jnp.concatenate(tiles, axis=0).astype(BF16)


def _tile_rows(t, bb):
    return t[bb * ROW_TILE:(bb + 1) * ROW_TILE]


def _in_attn_kernel(*refs, n_x, ctx_queries):
    x_refs = refs[:n_x]
    mod_ref, g_ref, w_ref, rc_ref, ra_ref, rb_ref, qt_ref, k_ref, vt_ref, gate_ref, wb_sc = refs[n_x:]
    _cast_once(w_ref, wb_sc)
    rc, ra, rb = rc_ref[...], ra_ref[...], rb_ref[...]

    def rope(t):
        reps = t.shape[-1] // LANES
        n = t.shape[-1]
        return (t * jnp.tile(rc, (1, reps))
                + pltpu.roll(t, n - HEAD_DIM // 4, 1) * jnp.tile(ra, (1, reps))
                + pltpu.roll(t, HEAD_DIM // 4, 1) * jnp.tile(rb, (1, reps)))

    def keys_values(z_kv, bb):
        k = rope(z_kv[:, :KV_DIM]).astype(BF16)
        for kh in range(N_KV_HEADS):
            k_ref[bb, kh] = k[:, kh * HEAD_DIM:(kh + 1) * HEAD_DIM]
        vt_ref[bb] = z_kv[:, KV_DIM:].T.astype(BF16)

    def all_columns():
        z_all = jnp.dot(_norm_mod_rows(x_refs, g_ref, mod_ref), wb_sc[...], preferred_element_type=F32)
        for bb in range(ROW_BATCH):
            z = _tile_rows(z_all, bb)
            qt_ref[bb] = (rope(z[:, :Q_DIM]) * (LOG2E * HEAD_DIM ** -0.5)).T.astype(BF16)
            keys_values(z[:, Q_DIM:Q_DIM + 2 * KV_DIM], bb)
            gate_ref[bb] = z[:, Q_DIM + 2 * KV_DIM:].astype(gate_ref.dtype)

    def keys_values_only():
        z_kv = jnp.dot(_norm_mod_rows(x_refs, g_ref, mod_ref), wb_sc[:, Q_DIM:Q_DIM + 2 * KV_DIM],
                       preferred_element_type=F32)
        for bb in range(ROW_BATCH):
            keys_values(_tile_rows(z_kv, bb), bb)
        qt_ref[...] = jnp.zeros_like(qt_ref)
        gate_ref[...] = jnp.zeros_like(gate_ref)

    if ctx_queries:
        all_columns()
    else:
        pl.when(pl.program_id(1) >= CTX_ROW_TILES)(all_columns)
        pl.when(pl.program_id(1) < CTX_ROW_TILES)(keys_values_only)


def _in_attn(streams, mod, norm_g, w_in, rope_tabs, ctx_queries):
    n = w_in.shape[1]
    col = lambda height: pl.BlockSpec((ROW_BATCH, height, ROW_TILE), lambda b, j: (b, 0, j))
    tab = pl.BlockSpec((ROW_TILE, LANES), lambda b, j: (j, 0))
    return pl.pallas_call(
        functools.partial(_in_attn_kernel, n_x=len(streams), ctx_queries=ctx_queries),
        grid=_ROW_GRID,
        in_specs=_stream_specs(len(streams) > 1)
                 + [_mod_spec(), _const_spec((1, D_MODEL)), _const_spec((D_MODEL, n)), tab, tab, tab],
        out_specs=[col(Q_DIM),
                   pl.BlockSpec((ROW_BATCH, N_KV_HEADS, ROW_TILE, HEAD_DIM), lambda b, j: (b, 0, j, 0)),
                   col(KV_DIM), _row_spec(Q_DIM)],
        out_shape=[jax.ShapeDtypeStruct((BATCH, Q_DIM, TOK), BF16),
                   jax.ShapeDtypeStruct((BATCH, N_KV_HEADS, TOK, HEAD_DIM), BF16),
                   jax.ShapeDtypeStruct((BATCH, KV_DIM, TOK), BF16),
                   jax.ShapeDtypeStruct((BATCH, TOK, Q_DIM), BF16)],
        scratch_shapes=[pltpu.VMEM((D_MODEL, n), BF16)],
        compiler_params=_params("arbitrary", "arbitrary"),
        name="in_attn",
    )(*streams, mod, norm_g.reshape(1, D_MODEL), w_in, *rope_tabs)


def _attn_kernel(sink_ref, qt_ref, k_ref, vt_ref, ot_ref, s_sc, bias_sc, *, first_tile):
    lanes = N_GROUPS * Q_TILE

    key = lax.broadcasted_iota(jnp.int32, (WINDOW, lanes), 0)
    query = lax.broadcasted_iota(jnp.int32, (WINDOW, lanes), 1) & (Q_TILE - 1)
    bias_sc[0] = jnp.where(key >= query, 0.0, NEG_INF)
    bias_sc[1] = jnp.where(key <= query, 0.0, NEG_INF)
    bias_sc[2] = jnp.full((WINDOW, lanes), NEG_INF, F32)

    def aligned(start):
        return start if isinstance(start, int) else pl.multiple_of(start, WINDOW)

    def heads_of(kh):
        return [kh * N_GROUPS + g for g in range(N_GROUPS)]

    def sink_row(kh):
        return jnp.concatenate([jnp.full((1, Q_TILE), sink_ref[h] * LOG2E, F32) for h in heads_of(kh)], axis=1)

    def score_steps(j, kh, chunks, m_box):
        cols = pl.ds(aligned(j * Q_TILE), Q_TILE)
        qt = jnp.concatenate([qt_ref[0, h * HEAD_DIM:(h + 1) * HEAD_DIM, cols] for h in heads_of(kh)], axis=1)
        mx = jnp.broadcast_to(sink_row(kh), (MAX_ROWS, lanes))
        keys = jnp.concatenate([k_ref[0, kh, pl.ds(ks, WINDOW), :] for ks, _ in chunks], axis=0)
        s_all = jnp.dot(keys, qt, preferred_element_type=F32)
        for c, (ks, bias) in enumerate(chunks):
            s = s_all[c * WINDOW:(c + 1) * WINDOW]
            if bias is not None:
                s = s + bias_sc[bias]
            s_sc[kh, c * WINDOW:(c + 1) * WINDOW, :] = s
            mx = jnp.maximum(mx, s.reshape(WINDOW // MAX_ROWS, MAX_ROWS, lanes).max(axis=0))
            yield
        m_box.append(mx.max(axis=0, keepdims=True))

    def value_steps(j, kh, chunks, m_box):
        m = m_box[0]
        acc = jnp.zeros((HEAD_DIM, lanes), F32)
        denom = jnp.zeros((SUBLANES, lanes), F32)
        group = []
        for c, (ks, _) in enumerate(chunks):
            p = jnp.exp2(s_sc[kh, c * WINDOW:(c + 1) * WINDOW, :] - m)
            denom = denom + p.reshape(WINDOW // SUBLANES, SUBLANES, lanes).sum(axis=0)
            group.append((vt_ref[0, kh * HEAD_DIM:(kh + 1) * HEAD_DIM, pl.ds(ks, WINDOW)], p.astype(BF16)))
            if len(group) == MXU_DIM // WINDOW or c == len(chunks) - 1:
                vt = jnp.concatenate([g[0] for g in group], axis=1)
                pp = jnp.concatenate([g[1] for g in group], axis=0)
                acc = acc + jnp.dot(vt, pp, preferred_element_type=F32)
                group = []
            yield
        denom = denom.sum(axis=0, keepdims=True) + jnp.exp2(sink_row(kh) - m)
        o = acc / denom
        cols = pl.ds(aligned((j - first_tile) * Q_TILE), Q_TILE)
        for g, h in enumerate(heads_of(kh)):
            ot_ref[0, h * HEAD_DIM:(h + 1) * HEAD_DIM, cols] = o[:, g * Q_TILE:(g + 1) * Q_TILE].astype(BF16)

    def interleave(*stages):
        stages = list(stages)
        while stages:
            stages = [st for st in stages if next(st, StopIteration) is not StopIteration]

    ctx_chunks = [(c * WINDOW, None) for c in range(CTX_LEN // WINDOW)]

    def latent_chunks(j):
        own = j * Q_TILE
        if isinstance(j, int):
            prev_bias = 2 if j == CTX_Q_TILES else 0
            next_bias = 2 if j == Q_TILES - 1 else 1
            next_start = min(own + WINDOW, TOK - WINDOW)
        else:
            prev_bias = jnp.where(j == CTX_Q_TILES, 2, 0)
            next_bias = jnp.where(j == Q_TILES - 1, 2, 1)
            next_start = jnp.minimum(own + WINDOW, TOK - WINDOW)
        return [(aligned(own - WINDOW), prev_bias), (aligned(own), None),
                (aligned(next_start), next_bias)] + ctx_chunks

    if first_tile == 0:
        units = [(j, kh, ctx_chunks) for j in range(CTX_Q_TILES) for kh in range(N_KV_HEADS)]
        boxes = [[] for _ in units]
        interleave(score_steps(*units[0], boxes[0]))
        for i, unit in enumerate(units):
            stages = [score_steps(*units[i + 1], boxes[i + 1])] if i + 1 < len(units) else []
            interleave(*stages, value_steps(*unit, boxes[i]))

    def latent_tile(j, m_head0, next_j):
        chunks = latent_chunks(j)
        box = [m_head0]
        for kh in range(N_KV_HEADS):
            nxt = []
            stages = []
            if kh + 1 < N_KV_HEADS:
                stages.append(score_steps(j, kh + 1, chunks, nxt))
            elif next_j is not None:
                stages.append(score_steps(next_j, 0, latent_chunks(next_j), nxt))
            interleave(*stages, value_steps(j, kh, chunks, box))
            box = nxt
        return box[0] if box else None

    first_box = []
    interleave(score_steps(CTX_Q_TILES, 0, latent_chunks(CTX_Q_TILES), first_box))
    m_last = lax.fori_loop(CTX_Q_TILES, Q_TILES - 1, lambda j, m: latent_tile(j, m, j + 1), first_box[0],
                           unroll=TILE_UNROLL)
    latent_tile(Q_TILES - 1, m_last, None)


def _attention(qt, k, vt, sink, need_ctx):
    first_tile = 0 if need_ctx else CTX_Q_TILES
    n_keys = BAND + CTX_LEN
    out_cols = TOK - first_tile * Q_TILE
    return pl.pallas_call(
        functools.partial(_attn_kernel, first_tile=first_tile),
        grid=(BATCH,),
        in_specs=[pl.BlockSpec(memory_space=pltpu.SMEM),
                  pl.BlockSpec((1, Q_DIM, TOK), lambda b: (b, 0, 0)),
                  pl.BlockSpec((1, N_KV_HEADS, TOK, HEAD_DIM), lambda b: (b, 0, 0, 0)),
                  pl.BlockSpec((1, KV_DIM, TOK), lambda b: (b, 0, 0))],
        out_specs=pl.BlockSpec((1, Q_DIM, out_cols), lambda b: (b, 0, 0)),
        out_shape=jax.ShapeDtypeStruct((BATCH, Q_DIM, out_cols), BF16),
        scratch_shapes=[pltpu.VMEM((N_KV_HEADS, n_keys, N_GROUPS * Q_TILE), F32),
                        pltpu.VMEM((3, WINDOW, N_GROUPS * Q_TILE), F32)],
        compiler_params=_params("arbitrary"),
        name="attention",
    )(sink, qt, k, vt)


def _lru_kernel(uf_ref, ub_ref, cw_ref, cb_ref, wa_ref, ba_ref, wx_ref, bx_ref, lam_ref,
                hf_ref, hb_ref, u_sc, a_sc, b_sc, y_sc, h_sc):
    step = pl.program_id(1)
    tl, pitch, halo = LRU_TL, LRU_PITCH, SUBLANES
    segment_start = (step == 0) | (step == LRU_CTX_BLKS)

    @pl.when(segment_start)
    def _zero_halo():
        u_sc[0, :, 0:halo, :] = jnp.zeros((BATCH * LRU_SLABS, halo, LANES), F32)
        u_sc[1, :, halo + tl:, :] = jnp.zeros((BATCH * LRU_SLABS, halo, LANES), F32)

    @pl.when(jnp.logical_not(segment_start))
    def _carry_halo():
        u_sc[0, :, 0:halo, :] = u_sc[0, :, tl:tl + halo, :]
        u_sc[1, :, halo + tl:, :] = u_sc[1, :, halo:2 * halo, :]

    @pl.when(step == 0)
    def _sequence_start():
        h_sc[...] = jnp.zeros_like(h_sc)

    for d, u_ref in ((0, uf_ref), (1, ub_ref)):
        for bi in range(BATCH):
            for s in range(LRU_SLABS):
                u_sc[d, bi * LRU_SLABS + s, halo:halo + tl, :] = u_ref[bi, :, s * LANES:(s + 1) * LANES]

    for d in range(2):
        nl = -lam_ref[d]
        half_decay = (0.5 * LRU_C) * (jnp.maximum(nl, 0.0) + jnp.log1p(jnp.exp(-jnp.abs(nl))))
        for bi in range(BATCH):
            cols = []
            for s in range(LRU_SLABS):
                lanes = slice(s * LANES, (s + 1) * LANES)
                half_x = 0.5 * cb_ref[d][:, lanes]
                for kk in range(LRU_CONV):
                    off = halo + (kk - (LRU_CONV - 1) if d == 0 else (LRU_CONV - 1) - kk)
                    rows = pl.ds(off, tl, stride=1)
                    tap = 0.5 * cw_ref[d][kk:kk + 1, lanes]
                    half_x = half_x + tap * u_sc[d, bi * LRU_SLABS + s, rows, :]
                cols.append(half_x)
            half_x = jnp.concatenate(cols, axis=1)
            xb = half_x.astype(BF16)
            tr = jnp.tanh(jnp.dot(xb, wa_ref[d, 0], preferred_element_type=F32) + ba_ref[d])
            ti = jnp.tanh(jnp.dot(xb, wx_ref[d, 0], preferred_element_type=F32) + bx_ref[d])
            neg_log_a = tr * half_decay + half_decay
            ix = ti * half_x + half_x
            a = jnp.exp(-neg_log_a)
            var = jnp.tanh(neg_log_a) * (a * a + 1.0)
            b = (var * lax.rsqrt(jnp.maximum(var, F32_TINY))) * ix
            for s in range(LRU_SLABS):
                lanes = slice(s * LANES, (s + 1) * LANES)
                rows = pl.ds(bi * pitch, tl, stride=1)
                a_sc[d, s, rows, :] = a[:, lanes]
                b_sc[d, s, rows, :] = b[:, lanes]

    def scan_step(t, carry):
        new = []
        for d in range(2):
            tt = t if d == 0 else tl - 1 - t
            for s in range(LRU_SLABS):
                rows = pl.ds(tt, BATCH, stride=pitch)
                h = a_sc[d, s, rows, :] * carry[d * LRU_SLABS + s] + b_sc[d, s, rows, :]
                y_sc[d, s, rows, :] = h
                new.append(h)
        return tuple(new)

    init = tuple(h_sc[d, s] for d in range(2) for s in range(LRU_SLABS))
    final = lax.fori_loop(0, tl, scan_step, init, unroll=8)
    for d in range(2):
        for s in range(LRU_SLABS):
            h_sc[d, s] = final[d * LRU_SLABS + s]

    for d, o_ref in ((0, hf_ref), (1, hb_ref)):
        for bi in range(BATCH):
            for s in range(LRU_SLABS):
                o_ref[bi, :, s * LANES:(s + 1) * LANES] = (
                    y_sc[d, s, pl.ds(bi * pitch, tl, stride=1), :].astype(o_ref.dtype))


def _block_diag_chunks(w):
    per = MXU_DIM // LRU_BLOCK
    w = w.reshape(2, LRU_BLOCKS // per, per, LRU_BLOCK, LRU_BLOCK)
    eye = jnp.eye(per, dtype=w.dtype)
    bd = jnp.einsum('dcpij,pq->dcpiqj', w, eye)
    return bd.reshape(2, LRU_BLOCKS // per, MXU_DIM, MXU_DIM).astype(BF16)


def _lru_scan(u, conv_w, conv_b, gate_a_w, gate_a_b, gate_x_w, gate_x_b, lam):
    def fwd_blk(w, s):
        return (0, s, w)

    def bwd_blk(w, s):
        blk = jnp.where(s < LRU_CTX_BLKS, LRU_CTX_BLKS - 1 - s, LRU_BLKS - 1 - (s - LRU_CTX_BLKS))
        return (0, blk, w)

    blk = (BATCH, LRU_TL, LRU_WC)
    vec = pl.BlockSpec((2, 1, LRU_WC), lambda w, s: (0, 0, w))
    gate = pl.BlockSpec((2, LRU_WC // MXU_DIM, MXU_DIM, MXU_DIM), lambda w, s: (0, w, 0, 0))
    width = u.shape[-1]
    out = jax.ShapeDtypeStruct((BATCH, TOK, width), F32)
    scan_buf = pltpu.VMEM((2, LRU_SLABS, BATCH * LRU_PITCH, LANES), F32)
    return pl.pallas_call(
        _lru_kernel,
        grid=(width // LRU_WC, LRU_BLKS),
        in_specs=[pl.BlockSpec(blk, fwd_blk), pl.BlockSpec(blk, bwd_blk),
                  pl.BlockSpec((2, LRU_CONV, LRU_WC), lambda w, s: (0, 0, w)), vec,
                  gate, vec, gate, vec, vec],
        out_specs=[pl.BlockSpec(blk, fwd_blk), pl.BlockSpec(blk, bwd_blk)],
        out_shape=[out, out],
        scratch_shapes=[pltpu.VMEM((2, BATCH * LRU_SLABS, LRU_TL + 2 * SUBLANES, LANES), F32),
                        scan_buf, scan_buf, scan_buf,
                        pltpu.VMEM((2, LRU_SLABS, BATCH, LANES), F32)],
        compiler_params=_params("parallel", "arbitrary"),
        name="lru_scan",
    )(u, u, conv_w, conv_b.reshape(2, 1, width),
      _block_diag_chunks(gate_a_w), 0.5 * gate_a_b.reshape(2, 1, width),
      _block_diag_chunks(gate_x_w), 0.5 * gate_x_b.reshape(2, 1, width),
      lam.reshape(2, 1, width))


def _residual_rows(x_refs, mod_ref, acts, wb_sc, o_ref, post=None):
    y = jnp.dot(jnp.concatenate(acts, axis=0).astype(BF16), wb_sc[...], preferred_element_type=F32)
    for bb in range(len(acts)):
        out = _stream_tile(x_refs, bb) + mod_ref[bb, 0][2:3, :] * _tile_rows(y, bb)
        o_ref[bb] = out if post is None else post(out)


def _out_attn_final_kernel(ot_ref, g_ref, w_ref, x_ref, mod_ref, fg_ref, o_ref, wb_sc):
    _cast_once(w_ref, wb_sc)
    acts = [ot_ref[bb].astype(F32).T * _silu(g_ref[bb].astype(F32)) for bb in range(OUT_ROW_BATCH)]

    def final_norm(x):
        return x * lax.rsqrt(jnp.mean(x * x, axis=-1, keepdims=True) + EPS) * fg_ref[...]

    _residual_rows((x_ref,), mod_ref, acts, wb_sc, o_ref, final_norm)


def _out_attn_in_next_kernel(ot_ref, g_ref, wo_ref, ctx_ref, lat_ref, mod_ref, ng_ref, nmod_ref, wi_ref,
                             xs_ref, *refs):
    o_refs, (wo_sc, wi_sc) = refs[:-2], refs[-2:]
    _cast_once(wo_ref, wo_sc)
    _cast_once(wi_ref, wi_sc)
    acts = [ot_ref[bb].astype(F32).T * _silu(g_ref[bb].astype(F32)) for bb in range(ROW_BATCH)]
    y = jnp.dot(jnp.concatenate(acts, axis=0).astype(BF16), wo_sc[...], preferred_element_type=F32)
    tiles = []
    for bb in range(ROW_BATCH):
        out = _stream_tile((ctx_ref, lat_ref), bb) + mod_ref[bb, 0][2:3, :] * _tile_rows(y, bb)
        xs_ref[bb] = out
        tiles.append(_norm_mod(out, ng_ref[...], nmod_ref[bb, 0]))
    z = jnp.dot(jnp.concatenate(tiles, axis=0).astype(BF16), wi_sc[...], preferred_element_type=F32)
    width = z.shape[-1] // len(o_refs)
    for i, o_ref in enumerate(o_refs):
        for bb in range(ROW_BATCH):
            o_ref[bb] = _tile_rows(z, bb)[:, i * width:(i + 1) * width].astype(o_ref.dtype)


def _halo_rows(main_ref, prev_ref, next_ref, bb):
    return jnp.concatenate([prev_ref[bb], main_ref[bb], next_ref[bb]], axis=0).astype(F32)


def _out_lru_conv_kernel(hf_ref, hfp_ref, hfn_ref, hb_ref, hbp_ref, hbn_ref, gl_ref, glp_ref, gln_ref,
                         x_ref, xp_ref, xn_ref, lmod_ref, lwo_ref, mod_ref, g_ref, wu_ref, wb_ref, wc_ref,
                         wg_ref, cw_ref, cb_ref, wo_ref, o_ref, p_sc, lwo_sc, wo_sc):
    j = pl.program_id(1)
    halo, cc = SUBLANES, CONV_CHUNK
    rows = ROW_TILE + 2 * halo
    _cast_once(lwo_ref, lwo_sc)
    _cast_once(wo_ref, wo_sc)
    acts = [(_halo_rows(hf_ref, hfp_ref, hfn_ref, bb) + _halo_rows(hb_ref, hbp_ref, hbn_ref, bb))
            * _silu(_halo_rows(gl_ref, glp_ref, gln_ref, bb)) for bb in range(ROW_BATCH)]
    y = jnp.dot(jnp.concatenate(acts, axis=0).astype(BF16), lwo_sc[...], preferred_element_type=F32)
    x1 = [_halo_rows(x_ref, xp_ref, xn_ref, bb) + lmod_ref[bb, 0][2:3, :] * y[bb * rows:(bb + 1) * rows]
          for bb in range(ROW_BATCH)]
    tiles = [_norm_mod(x1[bb], g_ref[...], mod_ref[bb, 0]) for bb in range(ROW_BATCH)]
    h = jnp.concatenate(tiles, axis=0).astype(BF16)
    h_main = jnp.concatenate([h[bb * rows + halo:bb * rows + halo + ROW_TILE] for bb in range(ROW_BATCH)], axis=0)
    keep_prev = jnp.where((j == 0) | (j == CTX_ROW_TILES), 0.0, 1.0)
    keep_next = jnp.where((j == CTX_ROW_TILES - 1) | (j == ROW_TILES - 1), 0.0, 1.0)
    acc = jnp.zeros((ROW_BATCH * ROW_TILE, D_MODEL), F32)
    for c in range(D_MODEL // cc):
        chans = slice(c * cc, (c + 1) * cc)
        u, cg = (jnp.dot(h, w_ref[:, chans], preferred_element_type=F32) for w_ref in (wu_ref, wc_ref))
        bg, g = (jnp.dot(h_main, w_ref[:, chans], preferred_element_type=F32) for w_ref in (wb_ref, wg_ref))
        p = cg * u
        acts = []
        for bb in range(ROW_BATCH):
            top = bb * rows
            main = slice(top + halo, top + halo + ROW_TILE)
            tile = slice(bb * ROW_TILE, (bb + 1) * ROW_TILE)
            for s in range(cc // LANES):
                lanes = slice(s * LANES, (s + 1) * LANES)
                p_sc[bb, c, s, 0:halo, :] = p[top:top + halo, lanes] * keep_prev
                p_sc[bb, c, s, halo:halo + ROW_TILE, :] = p[main, lanes]
                p_sc[bb, c, s, halo + ROW_TILE:rows, :] = p[top + halo + ROW_TILE:top + rows, lanes] * keep_next
            before = jnp.concatenate([p_sc[bb, c, s, pl.ds(halo - 1, ROW_TILE, stride=1), :]
                                      for s in range(cc // LANES)], axis=1)
            after = jnp.concatenate([p_sc[bb, c, s, pl.ds(halo + 1, ROW_TILE, stride=1), :]
                                     for s in range(cc // LANES)], axis=1)
            conv = (cw_ref[0:1, chans] * before + cw_ref[1:2, chans] * p[main] + cw_ref[2:3, chans] * after
                    + cb_ref[:, chans])
            acts.append(bg[tile] * conv * _silu(g[tile]))
        acc = acc + jnp.dot(jnp.concatenate(acts, axis=0).astype(BF16), wo_sc[chans, :],
                            preferred_element_type=F32)
    for bb in range(ROW_BATCH):
        o_ref[bb] = x1[bb][halo:halo + ROW_TILE] + mod_ref[bb, 0][2:3, :] * _tile_rows(acc, bb)


_W_OUT_SPEC = _const_spec((D_MODEL, D_MODEL))


def _out_attn_final(ot, gate, w_out, xs, mod, final_g):
    off, rb = CTX_ROW_TILES, OUT_ROW_BATCH
    return pl.pallas_call(
        _out_attn_final_kernel,
        grid=(BATCH // rb, ROW_TILES - off),
        in_specs=[pl.BlockSpec((rb, Q_DIM, ROW_TILE), lambda b, j: (b, 0, j)), _row_spec(Q_DIM, off, rb),
                  _W_OUT_SPEC, _row_spec(D_MODEL, off, rb), _mod_spec(off, rb), _const_spec((1, D_MODEL))],
        out_specs=_row_spec(D_MODEL, rb=rb),
        out_shape=jax.ShapeDtypeStruct((BATCH, SEQ, D_MODEL), F32),
        scratch_shapes=[pltpu.VMEM((D_MODEL, D_MODEL), BF16)],
        compiler_params=_params("arbitrary", "arbitrary"),
        name="out_attn_final",
    )(ot, gate, w_out, xs, mod, final_g.reshape(1, D_MODEL))


def _out_attn_in_next(ot, gate, w_out, ctx, x, mod, next_norm_g, next_mod, next_w_in, dtypes):
    n = next_w_in.shape[1]
    width = n // len(dtypes)
    return pl.pallas_call(
        _out_attn_in_next_kernel,
        grid=_ROW_GRID,
        in_specs=[pl.BlockSpec((ROW_BATCH, Q_DIM, ROW_TILE), lambda b, j: (b, 0, j)), _row_spec(Q_DIM),
                  _W_OUT_SPEC] + _stream_specs(True)
                 + [_mod_spec(), _const_spec((1, D_MODEL)), _mod_spec(), _const_spec((D_MODEL, n))],
        out_specs=[_row_spec(D_MODEL)] + [_row_spec(width)] * len(dtypes),
        out_shape=[jax.ShapeDtypeStruct((BATCH, TOK, D_MODEL), F32)]
                  + [jax.ShapeDtypeStruct((BATCH, TOK, width), dt) for dt in dtypes],
        scratch_shapes=[pltpu.VMEM((D_MODEL, D_MODEL), BF16), pltpu.VMEM((D_MODEL, n), BF16)],
        compiler_params=_params("arbitrary", "arbitrary"),
        name="out_attn_in_lru",
    )(ot, gate, w_out, ctx, x, mod, next_norm_g.reshape(1, D_MODEL), next_mod, next_w_in)


def _out_lru_conv(hf, hb, gate, lru_w_out, xs, lru_mod, mod, norm_g, w_in, conv_w, conv_b, w_out):
    per = ROW_TILE // SUBLANES
    n8 = TOK // SUBLANES
    halo = lambda idx: pl.BlockSpec((ROW_BATCH, SUBLANES, D_MODEL), idx)
    prev = halo(lambda b, j: (b, jnp.maximum(j * per - 1, 0), 0))
    nxt = halo(lambda b, j: (b, jnp.minimum((j + 1) * per, n8 - 1), 0))
    with_halo = [_row_spec(D_MODEL), prev, nxt]
    wi = w_in.astype(BF16)
    w_blocks = [_const_spec((D_MODEL, D_MODEL), functools.partial(lambda i, b, j: (0, i), i)) for i in range(4)]
    return pl.pallas_call(
        _out_lru_conv_kernel,
        grid=_ROW_GRID,
        in_specs=with_halo * 4 + [_mod_spec(), _W_OUT_SPEC, _mod_spec(), _const_spec((1, D_MODEL))] + w_blocks
                 + [_const_spec((CONV_K, D_MODEL)), _const_spec((1, D_MODEL)), _W_OUT_SPEC],
        out_specs=_row_spec(D_MODEL),
        out_shape=jax.ShapeDtypeStruct((BATCH, TOK, D_MODEL), F32),
        scratch_shapes=[pltpu.VMEM((ROW_BATCH, D_MODEL // CONV_CHUNK, CONV_CHUNK // LANES,
                                    ROW_TILE + 2 * SUBLANES, LANES), F32),
                        pltpu.VMEM((D_MODEL, D_MODEL), BF16), pltpu.VMEM((D_MODEL, D_MODEL), BF16)],
        compiler_params=_params("arbitrary", "arbitrary"),
        name="out_lru_conv",
    )(hf, hf, hf, hb, hb, hb, gate, gate, gate, xs, xs, xs, lru_mod, lru_w_out, mod,
      norm_g.reshape(1, D_MODEL), wi, wi, wi, wi, conv_w, conv_b.reshape(1, D_MODEL), w_out)


def _rope_tables():
    quarter = HEAD_DIM // 4
    pos = np.arange(SEQ)
    half = HEAD_DIM // 2
    inv = (1.0 / (ROPE_BASE ** (np.arange(0, half, 2, dtype=np.float32) / half))).astype(np.float32)
    zero = np.zeros((SEQ, quarter), np.float32)
    parts_c, parts_a, parts_b = [], [], []
    for axis_pos in ((pos // GRID_W).astype(np.float32), (pos % GRID_W).astype(np.float32)):
        ang = axis_pos[:, None] * inv
        c, s = np.cos(ang), np.sin(ang)
        parts_c += [c, c]
        parts_a += [-s, zero]
        parts_b += [zero, s]

    def table(parts, ctx_value):
        head = np.concatenate(parts, axis=1)
        head = np.concatenate([np.full((CTX_LEN, HEAD_DIM), ctx_value, np.float32), head], axis=0)
        return jnp.asarray(np.tile(head, (1, LANES // HEAD_DIM)).astype(np.float32))

    return table(parts_c, 1.0), table(parts_a, 0.0), table(parts_b, 0.0)


def kernel(x, c, ctx, c_ctx, l0_norm_g, l0_mod_w, l0_mod_b, l0_w_in, l0_w_out, l0_sink, l1_norm_g, l1_mod_w, l1_mod_b, l1_w_in, l1_w_out, l1_conv_w, l1_conv_b, l1_gate_a_w, l1_gate_a_b, l1_gate_x_w, l1_gate_x_b, l1_lambda, l2_norm_g, l2_mod_w, l2_mod_b, l2_w_in, l2_w_out, l2_conv_w, l2_conv_b, l3_norm_g, l3_mod_w, l3_mod_b, l3_w_in, l3_w_out, l3_sink, final_norm_g):
    mods = _modulation(c, c_ctx, (l0_mod_w, l1_mod_w, l2_mod_w, l3_mod_w),
                       (l0_mod_b, l1_mod_b, l2_mod_b, l3_mod_b))
    rope_tabs = _rope_tables()

    qt, k, vt, gate = _in_attn((ctx, x), mods[0], l0_norm_g, l0_w_in, rope_tabs, ctx_queries=True)
    ot = _attention(qt, k, vt, l0_sink, need_ctx=True)
    xs, u, gate = _out_attn_in_next(ot, gate, l0_w_out, ctx, x, mods[0], l1_norm_g, mods[1], l1_w_in,
                                    (F32, BF16))

    hf, hb = _lru_scan(u, l1_conv_w, l1_conv_b, l1_gate_a_w, l1_gate_a_b, l1_gate_x_w, l1_gate_x_b,
                       l1_lambda)
    xs = _out_lru_conv(hf, hb, gate, l1_w_out, xs, mods[1], mods[2], l2_norm_g, l2_w_in, l2_conv_w,
                       l2_conv_b, l2_w_out)

    qt, k, vt, gate = _in_attn((xs,), mods[3], l3_norm_g, l3_w_in, rope_tabs, ctx_queries=False)
    ot = _attention(qt, k, vt, l3_sink, need_ctx=False)
    return _out_attn_final(ot, gate, l3_w_out, xs, mods[3], final_norm_g)
```

```python
import functools
import math

import jax
import jax.numpy as jnp
import numpy as np
from jax import lax
from jax.experimental import pallas as pl
from jax.experimental.pallas import tpu as pltpu

D_MODEL = 1024
BATCH = 8
SEQ = 2048
CTX_LEN = 256
TOK = CTX_LEN + SEQ
GRID_W = 64
EPS = 1e-6
NEG_INF = -1e30
N_HEADS = 16
N_KV_HEADS = 4
N_GROUPS = N_HEADS // N_KV_HEADS
HEAD_DIM = 64
Q_DIM = N_HEADS * HEAD_DIM
KV_DIM = N_KV_HEADS * HEAD_DIM
WINDOW = 128
ROPE_BASE = 10000.0
LRU_BLOCKS = 16
LRU_BLOCK = 64
LRU_C = 8.0
LRU_CONV = 4
CONV_K = 3
MOD_PARTS = 3

LANES = 128
SUBLANES = 8
MXU_DIM = 256
VMEM_LIMIT_BYTES = 56 * 1024 * 1024

ROW_TILE = 256
ROW_BATCH = 2
OUT_ROW_BATCH = 4
CTX_ROW_TILES = CTX_LEN // ROW_TILE
ROW_TILES = TOK // ROW_TILE
Q_TILE = WINDOW
CTX_Q_TILES = CTX_LEN // Q_TILE
Q_TILES = TOK // Q_TILE
BAND = 3 * WINDOW
MAX_ROWS = SUBLANES
TILE_UNROLL = 5
CONV_CHUNK = MXU_DIM

LRU_TL = 256
LRU_WC = MXU_DIM
LRU_PITCH = LRU_TL + 4
LRU_SLABS = LRU_WC // LANES
LRU_CTX_BLKS = CTX_LEN // LRU_TL
LRU_BLKS = TOK // LRU_TL

F32 = jnp.float32
BF16 = jnp.bfloat16
F32_TINY = float(jnp.finfo(jnp.float32).tiny)
LOG2E = math.log2(math.e)


def _params(*sem):
    return pltpu.CompilerParams(dimension_semantics=sem, vmem_limit_bytes=VMEM_LIMIT_BYTES)


def _sigmoid(x):
    return 0.5 * jnp.tanh(0.5 * x) + 0.5


def _silu(x):
    return x * _sigmoid(x)


def _mod_kernel(c_ref, b_ref, *refs):
    w_refs, o_ref = refs[:-1], refs[-1]
    layer = pl.program_id(0) // MOD_PARTS
    s = _silu(c_ref[...])
    s_hi = s.astype(BF16)
    s_lo = (s - s_hi.astype(F32)).astype(BF16)
    for l, w_ref in enumerate(w_refs):
        @pl.when(layer == l)
        def _layer(w_ref=w_ref):
            w = w_ref[...]
            w_hi = w.astype(BF16)
            w_lo = (w - w_hi.astype(F32)).astype(BF16)
            acc = jnp.dot(s_hi, w_hi, preferred_element_type=F32)
            acc += jnp.dot(s_hi, w_lo, preferred_element_type=F32)
            acc += jnp.dot(s_lo, w_hi, preferred_element_type=F32)
            m = acc + b_ref[0]
            o_ref[0, :, 0, :] = jnp.broadcast_to(m[BATCH:BATCH + 1], (BATCH, D_MODEL))
            o_ref[0, :, 1, :] = m[:BATCH]


def _modulation(c, c_ctx, mod_ws, mod_bs):
    n = len(mod_ws)
    cc = jnp.concatenate([c, c_ctx[None], jnp.zeros((2 * SUBLANES - BATCH - 1, D_MODEL), F32)], axis=0)
    rows = cc.shape[0]
    w_spec = lambda l: pl.BlockSpec((D_MODEL, D_MODEL),
                                    lambda s: (0, jnp.clip(s - MOD_PARTS * l, 0, MOD_PARTS - 1)))
    part = lambda s: (s // MOD_PARTS, s % MOD_PARTS)
    m = pl.pallas_call(
        _mod_kernel,
        grid=(MOD_PARTS * n,),
        in_specs=[pl.BlockSpec((rows, D_MODEL), lambda s: (0, 0)),
                  pl.BlockSpec((1, 1, D_MODEL), lambda s: (part(s)[0], 0, part(s)[1]))]
                 + [w_spec(l) for l in range(n)],
        out_specs=pl.BlockSpec((1, BATCH, 2, D_MODEL), lambda s: (part(s)[0], 0, 0, part(s)[1])),
        out_shape=jax.ShapeDtypeStruct((n, BATCH, 2, MOD_PARTS * D_MODEL), F32),
        compiler_params=_params("arbitrary"),
        name="modulation",
    )(cc, jnp.stack(mod_bs).reshape(n, 1, MOD_PARTS * D_MODEL), *mod_ws)
    return m.reshape(n, BATCH, 2, MOD_PARTS, D_MODEL)


def _mod_spec(tile_offset=0, rb=ROW_BATCH):
    def idx(b, j):
        return (b, jnp.where(j + tile_offset < CTX_ROW_TILES, 0, 1), 0, 0)
    return pl.BlockSpec((rb, 1, MOD_PARTS, D_MODEL), idx)


def _row_spec(width, tile_offset=0, rb=ROW_BATCH):
    return pl.BlockSpec((rb, ROW_TILE, width), lambda b, j: (b, j + tile_offset, 0))


def _const_spec(shape, index=None):
    index = index or (lambda b, j: (0,) * len(shape))
    return pl.BlockSpec(shape, index, pipeline_mode=pl.Buffered(1))


def _stream_specs(split, rb=ROW_BATCH):
    if not split:
        return [_row_spec(D_MODEL, rb=rb)]
    blk = (rb, ROW_TILE, D_MODEL)
    return [pl.BlockSpec(blk, lambda b, j: (b, 0, 0)),
            pl.BlockSpec(blk, lambda b, j: (b, jnp.maximum(j - CTX_ROW_TILES, 0), 0))]


def _stream_tile(x_refs, bb):
    if len(x_refs) == 1:
        return x_refs[0][bb]
    ctx_ref, lat_ref = x_refs
    return jnp.where(pl.program_id(1) < CTX_ROW_TILES, ctx_ref[bb], lat_ref[bb])


def _cast_once(w_ref, wb_sc):
    @pl.when((pl.program_id(0) == 0) & (pl.program_id(1) == 0))
    def _cast():
        wb_sc[...] = w_ref[...].astype(BF16)


_ROW_GRID = (BATCH // ROW_BATCH, ROW_TILES)


def _norm_mod(x, g, mod):
    y = x * lax.rsqrt(jnp.mean(x * x, axis=-1, keepdims=True) + EPS) * g
    return y * (1.0 + mod[1:2, :]) + mod[0:1, :]


def _norm_mod_rows(x_refs, g_ref, mod_ref):
    tiles = [_norm_mod(_stream_tile(x_refs, bb), g_ref[...], mod_ref[bb, 0]) for bb in range(ROW_BATCH)]
    return jnp.concatenate(tiles, axis=0).astype(BF16)


def _tile_rows(t, bb):
    return t[bb * ROW_TILE:(bb + 1) * ROW_TILE]


def _in_attn_kernel(*refs, n_x, ctx_queries):
    x_refs = refs[:n_x]
    mod_ref, g_ref, w_ref, rc_ref, ra_ref, rb_ref, qt_ref, k_ref, vt_ref, gate_ref, wb_sc = refs[n_x:]
    _cast_once(w_ref, wb_sc)
    rc, ra, rb = rc_ref[...], ra_ref[...], rb_ref[...]

    def rope(t):
        reps = t.shape[-1] // LANES
        n = t.shape[-1]
        return (t * jnp.tile(rc, (1, reps))
                + pltpu.roll(t, n - HEAD_DIM // 4, 1) * jnp.tile(ra, (1, reps))
                + pltpu.roll(t, HEAD_DIM // 4, 1) * jnp.tile(rb, (1, reps)))

    def keys_values(z_kv, bb):
        k = rope(z_kv[:, :KV_DIM]).astype(BF16)
        for kh in range(N_KV_HEADS):
            k_ref[bb, kh] = k[:, kh * HEAD_DIM:(kh + 1) * HEAD_DIM]
        vt_ref[bb] = z_kv[:, KV_DIM:].T.astype(BF16)

    def all_columns():
        z_all = jnp.dot(_norm_mod_rows(x_refs, g_ref, mod_ref), wb_sc[...], preferred_element_type=F32)
        for bb in range(ROW_BATCH):
            z = _tile_rows(z_all, bb)
            qt_ref[bb] = (rope(z[:, :Q_DIM]) * (LOG2E * HEAD_DIM ** -0.5)).T.astype(BF16)
            keys_values(z[:, Q_DIM:Q_DIM + 2 * KV_DIM], bb)
            gate_ref[bb] = z[:, Q_DIM + 2 * KV_DIM:].astype(gate_ref.dtype)

    def keys_values_only():
        z_kv = jnp.dot(_norm_mod_rows(x_refs, g_ref, mod_ref), wb_sc[:, Q_DIM:Q_DIM + 2 * KV_DIM],
                       preferred_element_type=F32)
        for bb in range(ROW_BATCH):
            keys_values(_tile_rows(z_kv, bb), bb)
        qt_ref[...] = jnp.zeros_like(qt_ref)
        gate_ref[...] = jnp.zeros_like(gate_ref)

    if ctx_queries:
        all_columns()
    else:
        pl.when(pl.program_id(1) >= CTX_ROW_TILES)(all_columns)
        pl.when(pl.program_id(1) < CTX_ROW_TILES)(keys_values_only)


def _in_attn(streams, mod, norm_g, w_in, rope_tabs, ctx_queries):
    n = w_in.shape[1]
    col = lambda height: pl.BlockSpec((ROW_BATCH, height, ROW_TILE), lambda b, j: (b, 0, j))
    tab = pl.BlockSpec((ROW_TILE, LANES), lambda b, j: (j, 0))
    return pl.pallas_call(
        functools.partial(_in_attn_kernel, n_x=len(streams), ctx_queries=ctx_queries),
        grid=_ROW_GRID,
        in_specs=_stream_specs(len(streams) > 1)
                 + [_mod_spec(), _const_spec((1, D_MODEL)), _const_spec((D_MODEL, n)), tab, tab, tab],
        out_specs=[col(Q_DIM),
                   pl.BlockSpec((ROW_BATCH, N_KV_HEADS, ROW_TILE, HEAD_DIM), lambda b, j: (b, 0, j, 0)),
                   col(KV_DIM), _row_spec(Q_DIM)],
        out_shape=[jax.ShapeDtypeStruct((BATCH, Q_DIM, TOK), BF16),
                   jax.ShapeDtypeStruct((BATCH, N_KV_HEADS, TOK, HEAD_DIM), BF16),
                   jax.ShapeDtypeStruct((BATCH, KV_DIM, TOK), BF16),
                   jax.ShapeDtypeStruct((BATCH, TOK, Q_DIM), BF16)],
        scratch_shapes=[pltpu.VMEM((D_MODEL, n), BF16)],
        compiler_params=_params("arbitrary", "arbitrary"),
        name="in_attn",
    )(*streams, mod, norm_g.reshape(1, D_MODEL), w_in, *rope_tabs)


def _attn_kernel(sink_ref, qt_ref, k_ref, vt_ref, ot_ref, s_sc, bias_sc, *, first_tile):
    lanes = N_GROUPS * Q_TILE

    key = lax.broadcasted_iota(jnp.int32, (WINDOW, lanes), 0)
    query = lax.broadcasted_iota(jnp.int32, (WINDOW, lanes), 1) & (Q_TILE - 1)
    bias_sc[0] = jnp.where(key >= query, 0.0, NEG_INF)
    bias_sc[1] = jnp.where(key <= query, 0.0, NEG_INF)
    bias_sc[2] = jnp.full((WINDOW, lanes), NEG_INF, F32)

    def aligned(start):
        return start if isinstance(start, int) else pl.multiple_of(start, WINDOW)

    def heads_of(kh):
        return [kh * N_GROUPS + g for g in range(N_GROUPS)]

    def sink_row(kh):
        return jnp.concatenate([jnp.full((1, Q_TILE), sink_ref[h] * LOG2E, F32) for h in heads_of(kh)], axis=1)

    def score_steps(j, kh, chunks, m_box):
        cols = pl.ds(aligned(j * Q_TILE), Q_TILE)
        qt = jnp.concatenate([qt_ref[0, h * HEAD_DIM:(h + 1) * HEAD_DIM, cols] for h in heads_of(kh)], axis=1)
        mx = jnp.broadcast_to(sink_row(kh), (MAX_ROWS, lanes))
        keys = jnp.concatenate([k_ref[0, kh, pl.ds(ks, WINDOW), :] for ks, _ in chunks], axis=0)
        s_all = jnp.dot(keys, qt, preferred_element_type=F32)
        for c, (ks, bias) in enumerate(chunks):
            s = s_all[c * WINDOW:(c + 1) * WINDOW]
            if bias is not None:
                s = s + bias_sc[bias]
            s_sc[kh, c * WINDOW:(c + 1) * WINDOW, :] = s
            mx = jnp.maximum(mx, s.reshape(WINDOW // MAX_ROWS, MAX_ROWS, lanes).max(axis=0))
            yield
        m_box.append(mx.max(axis=0, keepdims=True))

    def value_steps(j, kh, chunks, m_box):
        m = m_box[0]
        acc = jnp.zeros((HEAD_DIM, lanes), F32)
        denom = jnp.zeros((SUBLANES, lanes), F32)
        group = []
        for c, (ks, _) in enumerate(chunks):
            p = jnp.exp2(s_sc[kh, c * WINDOW:(c + 1) * WINDOW, :] - m)
            denom = denom + p.reshape(WINDOW // SUBLANES, SUBLANES, lanes).sum(axis=0)
            group.append((vt_ref[0, kh * HEAD_DIM:(kh + 1) * HEAD_DIM, pl.ds(ks, WINDOW)], p.astype(BF16)))
            if len(group) == MXU_DIM // WINDOW or c == len(chunks) - 1:
                vt = jnp.concatenate([g[0] for g in group], axis=1)
                pp = jnp.concatenate([g[1] for g in group], axis=0)
                acc = acc + jnp.dot(vt, pp, preferred_element_type=F32)
                group = []
            yield
        denom = denom.sum(axis=0, keepdims=True) + jnp.exp2(sink_row(kh) - m)
        o = acc / denom
        cols = pl.ds(aligned((j - first_tile) * Q_TILE), Q_TILE)
        for g, h in enumerate(heads_of(kh)):
            ot_ref[0, h * HEAD_DIM:(h + 1) * HEAD_DIM, cols] = o[:, g * Q_TILE:(g + 1) * Q_TILE].astype(BF16)

    def interleave(*stages):
        stages = list(stages)
        while stages:
            stages = [st for st in stages if next(st, StopIteration) is not StopIteration]

    ctx_chunks = [(c * WINDOW, None) for c in range(CTX_LEN // WINDOW)]

    def latent_chunks(j):
        own = j * Q_TILE
        if isinstance(j, int):
            prev_bias = 2 if j == CTX_Q_TILES else 0
            next_bias = 2 if j == Q_TILES - 1 else 1
            next_start = min(own + WINDOW, TOK - WINDOW)
        else:
            prev_bias = jnp.where(j == CTX_Q_TILES, 2, 0)
            next_bias = jnp.where(j == Q_TILES - 1, 2, 1)
            next_start = jnp.minimum(own + WINDOW, TOK - WINDOW)
        return [(aligned(own - WINDOW), prev_bias), (aligned(own), None),
                (aligned(next_start), next_bias)] + ctx_chunks

    if first_tile == 0:
        units = [(j, kh, ctx_chunks) for j in range(CTX_Q_TILES) for kh in range(N_KV_HEADS)]
        boxes = [[] for _ in units]
        interleave(score_steps(*units[0], boxes[0]))
        for i, unit in enumerate(units):
            stages = [score_steps(*units[i + 1], boxes[i + 1])] if i + 1 < len(units) else []
            interleave(*stages, value_steps(*unit, boxes[i]))

    def latent_tile(j, m_head0, next_j):
        chunks = latent_chunks(j)
        box = [m_head0]
        for kh in range(N_KV_HEADS):
            nxt = []
            stages = []
            if kh + 1 < N_KV_HEADS:
                stages.append(score_steps(j, kh + 1, chunks, nxt))
            elif next_j is not None:
                stages.append(score_steps(next_j, 0, latent_chunks(next_j), nxt))
            interleave(*stages, value_steps(j, kh, chunks, box))
            box = nxt
        return box[0] if box else None

    first_box = []
    interleave(score_steps(CTX_Q_TILES, 0, latent_chunks(CTX_Q_TILES), first_box))
    m_last = lax.fori_loop(CTX_Q_TILES, Q_TILES - 1, lambda j, m: latent_tile(j, m, j + 1), first_box[0],
                           unroll=TILE_UNROLL)
    latent_tile(Q_TILES - 1, m_last, None)


def _attention(qt, k, vt, sink, need_ctx):
    first_tile = 0 if need_ctx else CTX_Q_TILES
    n_keys = BAND + CTX_LEN
    out_cols = TOK - first_tile * Q_TILE
    return pl.pallas_call(
        functools.partial(_attn_kernel, first_tile=first_tile),
        grid=(BATCH,),
        in_specs=[pl.BlockSpec(memory_space=pltpu.SMEM),
                  pl.BlockSpec((1, Q_DIM, TOK), lambda b: (b, 0, 0)),
                  pl.BlockSpec((1, N_KV_HEADS, TOK, HEAD_DIM), lambda b: (b, 0, 0, 0)),
                  pl.BlockSpec((1, KV_DIM, TOK), lambda b: (b, 0, 0))],
        out_specs=pl.BlockSpec((1, Q_DIM, out_cols), lambda b: (b, 0, 0)),
        out_shape=jax.ShapeDtypeStruct((BATCH, Q_DIM, out_cols), BF16),
        scratch_shapes=[pltpu.VMEM((N_KV_HEADS, n_keys, N_GROUPS * Q_TILE), F32),
                        pltpu.VMEM((3, WINDOW, N_GROUPS * Q_TILE), F32)],
        compiler_params=_params("arbitrary"),
        name="attention",
    )(sink, qt, k, vt)


def _lru_kernel(uf_ref, ub_ref, cw_ref, cb_ref, wa_ref, ba_ref, wx_ref, bx_ref, lam_ref,
                hf_ref, hb_ref, u_sc, a_sc, b_sc, y_sc, h_sc):
    step = pl.program_id(1)
    tl, pitch, halo = LRU_TL, LRU_PITCH, SUBLANES
    segment_start = (step == 0) | (step == LRU_CTX_BLKS)

    @pl.when(segment_start)
    def _zero_halo():
        u_sc[0, :, 0:halo, :] = jnp.zeros((BATCH * LRU_SLABS, halo, LANES), F32)
        u_sc[1, :, halo + tl:, :] = jnp.zeros((BATCH * LRU_SLABS, halo, LANES), F32)

    @pl.when(jnp.logical_not(segment_start))
    def _carry_halo():
        u_sc[0, :, 0:halo, :] = u_sc[0, :, tl:tl + halo, :]
        u_sc[1, :, halo + tl:, :] = u_sc[1, :, halo:2 * halo, :]

    @pl.when(step == 0)
    def _sequence_start():
        h_sc[...] = jnp.zeros_like(h_sc)

    for d, u_ref in ((0, uf_ref), (1, ub_ref)):
        for bi in range(BATCH):
            for s in range(LRU_SLABS):
                u_sc[d, bi * LRU_SLABS + s, halo:halo + tl, :] = u_ref[bi, :, s * LANES:(s + 1) * LANES]

    for d in range(2):
        nl = -lam_ref[d]
        half_decay = (0.5 * LRU_C) * (jnp.maximum(nl, 0.0) + jnp.log1p(jnp.exp(-jnp.abs(nl))))
        for bi in range(BATCH):
            cols = []
            for s in range(LRU_SLABS):
                lanes = slice(s * LANES, (s + 1) * LANES)
                half_x = 0.5 * cb_ref[d][:, lanes]
                for kk in range(LRU_CONV):
                    off = halo + (kk - (LRU_CONV - 1) if d == 0 else (LRU_CONV - 1) - kk)
                    rows = pl.ds(off, tl, stride=1)
                    tap = 0.5 * cw_ref[d][kk:kk + 1, lanes]
                    half_x = half_x + tap * u_sc[d, bi * LRU_SLABS + s, rows, :]
                cols.append(half_x)
            half_x = jnp.concatenate(cols, axis=1)
            xb = half_x.astype(BF16)
            tr = jnp.tanh(jnp.dot(xb, wa_ref[d, 0], preferred_element_type=F32) + ba_ref[d])
            ti = jnp.tanh(jnp.dot(xb, wx_ref[d, 0], preferred_element_type=F32) + bx_ref[d])
            neg_log_a = tr * half_decay + half_decay
            ix = ti * half_x + half_x
            a = jnp.exp(-neg_log_a)
            var = jnp.tanh(neg_log_a) * (a * a + 1.0)
            b = (var * lax.rsqrt(jnp.maximum(var, F32_TINY))) * ix
            for s in range(LRU_SLABS):
                lanes = slice(s * LANES, (s + 1) * LANES)
                rows = pl.ds(bi * pitch, tl, stride=1)
                a_sc[d, s, rows, :] = a[:, lanes]
                b_sc[d, s, rows, :] = b[:, lanes]

    def scan_step(t, carry):
        new = []
        for d in range(2):
            tt = t if d == 0 else tl - 1 - t
            for s in range(LRU_SLABS):
                rows = pl.ds(tt, BATCH, stride=pitch)
                h = a_sc[d, s, rows, :] * carry[d * LRU_SLABS + s] + b_sc[d, s, rows, :]
                y_sc[d, s, rows, :] = h
                new.append(h)
        return tuple(new)

    init = tuple(h_sc[d, s] for d in range(2) for s in range(LRU_SLABS))
    final = lax.fori_loop(0, tl, scan_step, init, unroll=8)
    for d in range(2):
        for s in range(LRU_SLABS):
            h_sc[d, s] = final[d * LRU_SLABS + s]

    for d, o_ref in ((0, hf_ref), (1, hb_ref)):
        for bi in range(BATCH):
            for s in range(LRU_SLABS):
                o_ref[bi, :, s * LANES:(s + 1) * LANES] = (
                    y_sc[d, s, pl.ds(bi * pitch, tl, stride=1), :].astype(o_ref.dtype))


def _block_diag_chunks(w):
    per = MXU_DIM // LRU_BLOCK
    w = w.reshape(2, LRU_BLOCKS // per, per, LRU_BLOCK, LRU_BLOCK)
    eye = jnp.eye(per, dtype=w.dtype)
    bd = jnp.einsum('dcpij,pq->dcpiqj', w, eye)
    return bd.reshape(2, LRU_BLOCKS // per, MXU_DIM, MXU_DIM).astype(BF16)


def _lru_scan(u, conv_w, conv_b, gate_a_w, gate_a_b, gate_x_w, gate_x_b, lam):
    def fwd_blk(w, s):
        return (0, s, w)

    def bwd_blk(w, s):
        blk = jnp.where(s < LRU_CTX_BLKS, LRU_CTX_BLKS - 1 - s, LRU_BLKS - 1 - (s - LRU_CTX_BLKS))
        return (0, blk, w)

    blk = (BATCH, LRU_TL, LRU_WC)
    vec = pl.BlockSpec((2, 1, LRU_WC), lambda w, s: (0, 0, w))
    gate = pl.BlockSpec((2, LRU_WC // MXU_DIM, MXU_DIM, MXU_DIM), lambda w, s: (0, w, 0, 0))
    width = u.shape[-1]
    out = jax.ShapeDtypeStruct((BATCH, TOK, width), BF16)
    scan_buf = pltpu.VMEM((2, LRU_SLABS, BATCH * LRU_PITCH, LANES), F32)
    return pl.pallas_call(
        _lru_kernel,
        grid=(width // LRU_WC, LRU_BLKS),
        in_specs=[pl.BlockSpec(blk, fwd_blk), pl.BlockSpec(blk, bwd_blk),
                  pl.BlockSpec((2, LRU_CONV, LRU_WC), lambda w, s: (0, 0, w)), vec,
                  gate, vec, gate, vec, vec],
        out_specs=[pl.BlockSpec(blk, fwd_blk), pl.BlockSpec(blk, bwd_blk)],
        out_shape=[out, out],
        scratch_shapes=[pltpu.VMEM((2, BATCH * LRU_SLABS, LRU_TL + 2 * SUBLANES, LANES), F32),
                        scan_buf, scan_buf, scan_buf,
                        pltpu.VMEM((2, LRU_SLABS, BATCH, LANES), F32)],
        compiler_params=_params("parallel", "arbitrary"),
        name="lru_scan",
    )(u, u, conv_w, conv_b.reshape(2, 1, width),
      _block_diag_chunks(gate_a_w), 0.5 * gate_a_b.reshape(2, 1, width),
      _block_diag_chunks(gate_x_w), 0.5 * gate_x_b.reshape(2, 1, width),
      lam.reshape(2, 1, width))


def _residual_rows(x_refs, mod_ref, acts, wb_sc, o_ref, post=None):
    y = jnp.dot(jnp.concatenate(acts, axis=0).astype(BF16), wb_sc[...], preferred_element_type=F32)
    for bb in range(len(acts)):
        out = _stream_tile(x_refs, bb) + mod_ref[bb, 0][2:3, :] * _tile_rows(y, bb)
        o_ref[bb] = out if post is None else post(out)


def _out_attn_final_kernel(ot_ref, g_ref, w_ref, x_ref, mod_ref, fg_ref, o_ref, wb_sc):
    _cast_once(w_ref, wb_sc)
    acts = [ot_ref[bb].astype(F32).T * _silu(g_ref[bb].astype(F32)) for bb in range(OUT_ROW_BATCH)]

    def final_norm(x):
        return x * lax.rsqrt(jnp.mean(x * x, axis=-1, keepdims=True) + EPS) * fg_ref[...]

    _residual_rows((x_ref,), mod_ref, acts, wb_sc, o_ref, final_norm)


def _out_attn_in_next_kernel(ot_ref, g_ref, wo_ref, ctx_ref, lat_ref, mod_ref, ng_ref, nmod_ref, wi_ref,
                             xs_ref, *refs):
    o_refs, (wo_sc, wi_sc) = refs[:-2], refs[-2:]
    _cast_once(wo_ref, wo_sc)
    _cast_once(wi_ref, wi_sc)
    acts = [ot_ref[bb].astype(F32).T * _silu(g_ref[bb].astype(F32)) for bb in range(ROW_BATCH)]
    y = jnp.dot(jnp.concatenate(acts, axis=0).astype(BF16), wo_sc[...], preferred_element_type=F32)
    tiles = []
    for bb in range(ROW_BATCH):
        out = _stream_tile((ctx_ref, lat_ref), bb) + mod_ref[bb, 0][2:3, :] * _tile_rows(y, bb)
        xs_ref[bb] = out
        tiles.append(_norm_mod(out, ng_ref[...], nmod_ref[bb, 0]))
    z = jnp.dot(jnp.concatenate(tiles, axis=0).astype(BF16), wi_sc[...], preferred_element_type=F32)
    width = z.shape[-1] // len(o_refs)
    for i, o_ref in enumerate(o_refs):
        for bb in range(ROW_BATCH):
            o_ref[bb] = _tile_rows(z, bb)[:, i * width:(i + 1) * width].astype(o_ref.dtype)


def _halo_rows(main_ref, prev_ref, next_ref, bb):
    return jnp.concatenate([prev_ref[bb], main_ref[bb], next_ref[bb]], axis=0).astype(F32)


def _out_lru_conv_kernel(hf_ref, hfp_ref, hfn_ref, hb_ref, hbp_ref, hbn_ref, gl_ref, glp_ref, gln_ref,
                         x_ref, xp_ref, xn_ref, lmod_ref, lwo_ref, mod_ref, g_ref, wu_ref, wb_ref, wc_ref,
                         wg_ref, cw_ref, cb_ref, wo_ref, o_ref, p_sc, lwo_sc, wo_sc):
    j = pl.program_id(1)
    halo, cc = SUBLANES, CONV_CHUNK
    rows = ROW_TILE + 2 * halo
    _cast_once(lwo_ref, lwo_sc)
    _cast_once(wo_ref, wo_sc)
    acts = [(_halo_rows(hf_ref, hfp_ref, hfn_ref, bb) + _halo_rows(hb_ref, hbp_ref, hbn_ref, bb))
            * _silu(_halo_rows(gl_ref, glp_ref, gln_ref, bb)) for bb in range(ROW_BATCH)]
    y = jnp.dot(jnp.concatenate(acts, axis=0).astype(BF16), lwo_sc[...], preferred_element_type=F32)
    x1 = [_halo_rows(x_ref, xp_ref, xn_ref, bb) + lmod_ref[bb, 0][2:3, :] * y[bb * rows:(bb + 1) * rows]
          for bb in range(ROW_BATCH)]
    tiles = [_norm_mod(x1[bb], g_ref[...], mod_ref[bb, 0]) for bb in range(ROW_BATCH)]
    h = jnp.concatenate(tiles, axis=0).astype(BF16)
    h_main = jnp.concatenate([h[bb * rows + halo:bb * rows + halo + ROW_TILE] for bb in range(ROW_BATCH)], axis=0)
    keep_prev = jnp.where((j == 0) | (j == CTX_ROW_TILES), 0.0, 1.0)
    keep_next = jnp.where((j == CTX_ROW_TILES - 1) | (j == ROW_TILES - 1), 0.0, 1.0)
    acc = jnp.zeros((ROW_BATCH * ROW_TILE, D_MODEL), F32)
    for c in range(D_MODEL // cc):
        chans = slice(c * cc, (c + 1) * cc)
        u, cg = (jnp.dot(h, w_ref[:, chans], preferred_element_type=F32) for w_ref in (wu_ref, wc_ref))
        bg, g = (jnp.dot(h_main, w_ref[:, chans], preferred_element_type=F32) for w_ref in (wb_ref, wg_ref))
        p = cg * u
        acts = []
        for bb in range(ROW_BATCH):
            top = bb * rows
            main = slice(top + halo, top + halo + ROW_TILE)
            tile = slice(bb * ROW_TILE, (bb + 1) * ROW_TILE)
            for s in range(cc // LANES):
                lanes = slice(s * LANES, (s + 1) * LANES)
                p_sc[bb, c, s, 0:halo, :] = p[top:top + halo, lanes] * keep_prev
                p_sc[bb, c, s, halo:halo + ROW_TILE, :] = p[main, lanes]
                p_sc[bb, c, s, halo + ROW_TILE:rows, :] = p[top + halo + ROW_TILE:top + rows, lanes] * keep_next
            before = jnp.concatenate([p_sc[bb, c, s, pl.ds(halo - 1, ROW_TILE, stride=1), :]
                                      for s in range(cc // LANES)], axis=1)
            after = jnp.concatenate([p_sc[bb, c, s, pl.ds(halo + 1, ROW_TILE, stride=1), :]
                                     for s in range(cc // LANES)], axis=1)
            conv = (cw_ref[0:1, chans] * before + cw_ref[1:2, chans] * p[main] + cw_ref[2:3, chans] * after
                    + cb_ref[:, chans])
            acts.append(bg[tile] * conv * _silu(g[tile]))
        acc = acc + jnp.dot(jnp.concatenate(acts, axis=0).astype(BF16), wo_sc[chans, :],
                            preferred_element_type=F32)
    for bb in range(ROW_BATCH):
        o_ref[bb] = x1[bb][halo:halo + ROW_TILE] + mod_ref[bb, 0][2:3, :] * _tile_rows(acc, bb)


_W_OUT_SPEC = _const_spec((D_MODEL, D_MODEL))


def _out_attn_final(ot, gate, w_out, xs, mod, final_g):
    off, rb = CTX_ROW_TILES, OUT_ROW_BATCH
    return pl.pallas_call(
        _out_attn_final_kernel,
        grid=(BATCH // rb, ROW_TILES - off),
        in_specs=[pl.BlockSpec((rb, Q_DIM, ROW_TILE), lambda b, j: (b, 0, j)), _row_spec(Q_DIM, off, rb),
                  _W_OUT_SPEC, _row_spec(D_MODEL, off, rb), _mod_spec(off, rb), _const_spec((1, D_MODEL))],
        out_specs=_row_spec(D_MODEL, rb=rb),
        out_shape=jax.ShapeDtypeStruct((BATCH, SEQ, D_MODEL), F32),
        scratch_shapes=[pltpu.VMEM((D_MODEL, D_MODEL), BF16)],
        compiler_params=_params("arbitrary", "arbitrary"),
        name="out_attn_final",
    )(ot, gate, w_out, xs, mod, final_g.reshape(1, D_MODEL))


def _out_attn_in_next(ot, gate, w_out, ctx, x, mod, next_norm_g, next_mod, next_w_in, dtypes):
    n = next_w_in.shape[1]
    width = n // len(dtypes)
    return pl.pallas_call(
        _out_attn_in_next_kernel,
        grid=_ROW_GRID,
        in_specs=[pl.BlockSpec((ROW_BATCH, Q_DIM, ROW_TILE), lambda b, j: (b, 0, j)), _row_spec(Q_DIM),
                  _W_OUT_SPEC] + _stream_specs(True)
                 + [_mod_spec(), _const_spec((1, D_MODEL)), _mod_spec(), _const_spec((D_MODEL, n))],
        out_specs=[_row_spec(D_MODEL)] + [_row_spec(width)] * len(dtypes),
        out_shape=[jax.ShapeDtypeStruct((BATCH, TOK, D_MODEL), F32)]
                  + [jax.ShapeDtypeStruct((BATCH, TOK, width), dt) for dt in dtypes],
        scratch_shapes=[pltpu.VMEM((D_MODEL, D_MODEL), BF16), pltpu.VMEM((D_MODEL, n), BF16)],
        compiler_params=_params("arbitrary", "arbitrary"),
        name="out_attn_in_lru",
    )(ot, gate, w_out, ctx, x, mod, next_norm_g.reshape(1, D_MODEL), next_mod, next_w_in)


def _out_lru_conv(hf, hb, gate, lru_w_out, xs, lru_mod, mod, norm_g, w_in, conv_w, conv_b, w_out):
    per = ROW_TILE // SUBLANES
    n8 = TOK // SUBLANES
    halo = lambda idx: pl.BlockSpec((ROW_BATCH, SUBLANES, D_MODEL), idx)
    prev = halo(lambda b, j: (b, jnp.maximum(j * per - 1, 0), 0))
    nxt = halo(lambda b, j: (b, jnp.minimum((j + 1) * per, n8 - 1), 0))
    with_halo = [_row_spec(D_MODEL), prev, nxt]
    wi = w_in.astype(BF16)
    w_blocks = [_const_spec((D_MODEL, D_MODEL), functools.partial(lambda i, b, j: (0, i), i)) for i in range(4)]
    return pl.pallas_call(
        _out_lru_conv_kernel,
        grid=_ROW_GRID,
        in_specs=with_halo * 4 + [_mod_spec(), _W_OUT_SPEC, _mod_spec(), _const_spec((1, D_MODEL))] + w_blocks
                 + [_const_spec((CONV_K, D_MODEL)), _const_spec((1, D_MODEL)), _W_OUT_SPEC],
        out_specs=_row_spec(D_MODEL),
        out_shape=jax.ShapeDtypeStruct((BATCH, TOK, D_MODEL), F32),
        scratch_shapes=[pltpu.VMEM((ROW_BATCH, D_MODEL // CONV_CHUNK, CONV_CHUNK // LANES,
                                    ROW_TILE + 2 * SUBLANES, LANES), F32),
                        pltpu.VMEM((D_MODEL, D_MODEL), BF16), pltpu.VMEM((D_MODEL, D_MODEL), BF16)],
        compiler_params=_params("arbitrary", "arbitrary"),
        name="out_lru_conv",
    )(hf, hf, hf, hb, hb, hb, gate, gate, gate, xs, xs, xs, lru_mod, lru_w_out, mod,
      norm_g.reshape(1, D_MODEL), wi, wi, wi, wi, conv_w, conv_b.reshape(1, D_MODEL), w_out)


def _rope_tables():
    quarter = HEAD_DIM // 4
    pos = np.arange(SEQ)
    half = HEAD_DIM // 2
    inv = (1.0 / (ROPE_BASE ** (np.arange(0, half, 2, dtype=np.float32) / half))).astype(np.float32)
    zero = np.zeros((SEQ, quarter), np.float32)
    parts_c, parts_a, parts_b = [], [], []
    for axis_pos in ((pos // GRID_W).astype(np.float32), (pos % GRID_W).astype(np.float32)):
        ang = axis_pos[:, None] * inv
        c, s = np.cos(ang), np.sin(ang)
        parts_c += [c, c]
        parts_a += [-s, zero]
        parts_b += [zero, s]

    def table(parts, ctx_value):
        head = np.concatenate(parts, axis=1)
        head = np.concatenate([np.full((CTX_LEN, HEAD_DIM), ctx_value, np.float32), head], axis=0)
        return jnp.asarray(np.tile(head, (1, LANES // HEAD_DIM)).astype(np.float32))

    return table(parts_c, 1.0), table(parts_a, 0.0), table(parts_b, 0.0)


def kernel(x, c, ctx, c_ctx, l0_norm_g, l0_mod_w, l0_mod_b, l0_w_in, l0_w_out, l0_sink, l1_norm_g, l1_mod_w, l1_mod_b, l1_w_in, l1_w_out, l1_conv_w, l1_conv_b, l1_gate_a_w, l1_gate_a_b, l1_gate_x_w, l1_gate_x_b, l1_lambda, l2_norm_g, l2_mod_w, l2_mod_b, l2_w_in, l2_w_out, l2_conv_w, l2_conv_b, l3_norm_g, l3_mod_w, l3_mod_b, l3_w_in, l3_w_out, l3_sink, final_norm_g):
    mods = _modulation(c, c_ctx, (l0_mod_w, l1_mod_w, l2_mod_w, l3_mod_w),
                       (l0_mod_b, l1_mod_b, l2_mod_b, l3_mod_b))
    rope_tabs = _rope_tables()

    qt, k, vt, gate = _in_attn((ctx, x), mods[0], l0_norm_g, l0_w_in, rope_tabs, ctx_queries=True)
    ot = _attention(qt, k, vt, l0_sink, need_ctx=True)
    xs, u, gate = _out_attn_in_next(ot, gate, l0_w_out, ctx, x, mods[0], l1_norm_g, mods[1], l1_w_in,
                                    (F32, BF16))

    hf, hb = _lru_scan(u, l1_conv_w, l1_conv_b, l1_gate_a_w, l1_gate_a_b, l1_gate_x_w, l1_gate_x_b,
                       l1_lambda)
    xs = _out_lru_conv(hf, hb, gate, l1_w_out, xs, mods[1], mods[2], l2_norm_g, l2_w_in, l2_conv_w,
                       l2_conv_b, l2_w_out)

    qt, k, vt, gate = _in_attn((xs,), mods[3], l3_norm_g, l3_w_in, rope_tabs, ctx_queries=False)
    ot = _attention(qt, k, vt, l3_sink, need_ctx=False)
    return _out_attn_final(ot, gate, l3_w_out, xs, mods[3], final_norm_g)
```

```python
import functools
import math

import jax
import jax.numpy as jnp
import numpy as np
from jax import lax
from jax.experimental import pallas as pl
from jax.experimental.pallas import tpu as pltpu

D_MODEL = 1024
BATCH = 8
SEQ = 2048
CTX_LEN = 256
TOK = CTX_LEN + SEQ
GRID_W = 64
EPS = 1e-6
NEG_INF = -1e30
N_HEADS = 16
N_KV_HEADS = 4
N_GROUPS = N_HEADS // N_KV_HEADS
HEAD_DIM = 64
Q_DIM = N_HEADS * HEAD_DIM
KV_DIM = N_KV_HEADS * HEAD_DIM
WINDOW = 128
ROPE_BASE = 10000.0
LRU_BLOCKS = 16
LRU_BLOCK = 64
LRU_C = 8.0
LRU_CONV = 4
CONV_K = 3
MOD_PARTS = 3

LANES = 128
SUBLANES = 8
MXU_DIM = 256
VMEM_LIMIT_BYTES = 56 * 1024 * 1024

ROW_TILE = 256
ROW_BATCH = 2
OUT_ROW_BATCH = 4
CTX_ROW_TILES = CTX_LEN // ROW_TILE
ROW_TILES = TOK // ROW_TILE
Q_TILE = WINDOW
CTX_Q_TILES = CTX_LEN // Q_TILE
Q_TILES = TOK // Q_TILE
BAND = 3 * WINDOW
MAX_ROWS = SUBLANES
TILE_UNROLL = 5
CONV_CHUNK = MXU_DIM

LRU_TL = 256
LRU_WC = MXU_DIM
LRU_PITCH = LRU_TL + 4
LRU_SLABS = LRU_WC // LANES
LRU_CTX_BLKS = CTX_LEN // LRU_TL
LRU_BLKS = TOK // LRU_TL

F32 = jnp.float32
BF16 = jnp.bfloat16
F32_TINY = float(jnp.finfo(jnp.float32).tiny)
LOG2E = math.log2(math.e)


def _params(*sem):
    return pltpu.CompilerParams(dimension_semantics=sem, vmem_limit_bytes=VMEM_LIMIT_BYTES)


def _sigmoid(x):
    return 0.5 * jnp.tanh(0.5 * x) + 0.5


def _silu(x):
    return x * _sigmoid(x)


def _mod_kernel(c_ref, b_ref, *refs):
    w_refs, o_ref = refs[:-1], refs[-1]
    layer = pl.program_id(0) // MOD_PARTS
    s = _silu(c_ref[...])
    s_hi = s.astype(BF16)
    s_lo = (s - s_hi.astype(F32)).astype(BF16)
    for l, w_ref in enumerate(w_refs):
        @pl.when(layer == l)
        def _layer(w_ref=w_ref):
            w = w_ref[...]
            w_hi = w.astype(BF16)
            w_lo = (w - w_hi.astype(F32)).astype(BF16)
            acc = jnp.dot(s_hi, w_hi, preferred_element_type=F32)
            acc += jnp.dot(s_hi, w_lo, preferred_element_type=F32)
            acc += jnp.dot(s_lo, w_hi, preferred_element_type=F32)
            m = acc + b_ref[0]
            o_ref[0, :, 0, :] = jnp.broadcast_to(m[BATCH:BATCH + 1], (BATCH, D_MODEL))
            o_ref[0, :, 1, :] = m[:BATCH]


def _modulation(c, c_ctx, mod_ws, mod_bs):
    n = len(mod_ws)
    cc = jnp.concatenate([c, c_ctx[None], jnp.zeros((2 * SUBLANES - BATCH - 1, D_MODEL), F32)], axis=0)
    rows = cc.shape[0]
    w_spec = lambda l: pl.BlockSpec((D_MODEL, D_MODEL),
                                    lambda s: (0, jnp.clip(s - MOD_PARTS * l, 0, MOD_PARTS - 1)))
    part = lambda s: (s // MOD_PARTS, s % MOD_PARTS)
    m = pl.pallas_call(
        _mod_kernel,
        grid=(MOD_PARTS * n,),
        in_specs=[pl.BlockSpec((rows, D_MODEL), lambda s: (0, 0)),
                  pl.BlockSpec((1, 1, D_MODEL), lambda s: (part(s)[0], 0, part(s)[1]))]
                 + [w_spec(l) for l in range(n)],
        out_specs=pl.BlockSpec((1, BATCH, 2, D_MODEL), lambda s: (part(s)[0], 0, 0, part(s)[1])),
        out_shape=jax.ShapeDtypeStruct((n, BATCH, 2, MOD_PARTS * D_MODEL), F32),
        compiler_params=_params("arbitrary"),
        name="modulation",
    )(cc, jnp.stack(mod_bs).reshape(n, 1, MOD_PARTS * D_MODEL), *mod_ws)
    return m.reshape(n, BATCH, 2, MOD_PARTS, D_MODEL)


def _mod_spec(tile_offset=0, rb=ROW_BATCH):
    def idx(b, j):
        return (b, jnp.where(j + tile_offset < CTX_ROW_TILES, 0, 1), 0, 0)
    return pl.BlockSpec((rb, 1, MOD_PARTS, D_MODEL), idx)


def _row_spec(width, tile_offset=0, rb=ROW_BATCH):
    return pl.BlockSpec((rb, ROW_TILE, width), lambda b, j: (b, j + tile_offset, 0))


def _const_spec(shape, index=None):
    index = index or (lambda b, j: (0,) * len(shape))
    return pl.BlockSpec(shape, index, pipeline_mode=pl.Buffered(1))


def _stream_specs(split, rb=ROW_BATCH):
    if not split:
        return [_row_spec(D_MODEL, rb=rb)]
    blk = (rb, ROW_TILE, D_MODEL)
    return [pl.BlockSpec(blk, lambda b, j: (b, 0, 0)),
            pl.BlockSpec(blk, lambda b, j: (b, jnp.maximum(j - CTX_ROW_TILES, 0), 0))]


def _stream_tile(x_refs, bb):
    if len(x_refs) == 1:
        return x_refs[0][bb]
    ctx_ref, lat_ref = x_refs
    return jnp.where(pl.program_id(1) < CTX_ROW_TILES, ctx_ref[bb], lat_ref[bb])


def _cast_once(w_ref, wb_sc):
    @pl.when((pl.program_id(0) == 0) & (pl.program_id(1) == 0))
    def _cast():
        wb_sc[...] = w_ref[...].astype(BF16)


_ROW_GRID = (BATCH // ROW_BATCH, ROW_TILES)


def _norm_mod(x, g, mod):
    y = x * lax.rsqrt(jnp.mean(x * x, axis=-1, keepdims=True) + EPS) * g
    return y * (1.0 + mod[1:2, :]) + mod[0:1, :]


def _norm_mod_rows(x_refs, g_ref, mod_ref):
    tiles = [_norm_mod(_stream_tile(x_refs, bb), g_ref[...], mod_ref[bb, 0]) for bb in range(ROW_BATCH)]
    return jnp.concatenate(tiles, axis=0).astype(BF16)


def _tile_rows(t, bb):
    return t[bb * ROW_TILE:(bb + 1) * ROW_TILE]


def _in_attn_kernel(*refs, n_x, ctx_queries):
    x_refs = refs[:n_x]
    mod_ref, g_ref, w_ref, rc_ref, ra_ref, rb_ref, qt_ref, k_ref, vt_ref, gate_ref, wb_sc = refs[n_x:]
    _cast_once(w_ref, wb_sc)
    rc, ra, rb = rc_ref[...], ra_ref[...], rb_ref[...]

    def rope(t):
        reps = t.shape[-1] // LANES
        n = t.shape[-1]
        return (t * jnp.tile(rc, (1, reps))
                + pltpu.roll(t, n - HEAD_DIM // 4, 1) * jnp.tile(ra, (1, reps))
                + pltpu.roll(t, HEAD_DIM // 4, 1) * jnp.tile(rb, (1, reps)))

    def keys_values(z_kv, bb):
        k = rope(z_kv[:, :KV_DIM]).astype(BF16)
        for kh in range(N_KV_HEADS):
            k_ref[bb, kh] = k[:, kh * HEAD_DIM:(kh + 1) * HEAD_DIM]
        vt_ref[bb] = z_kv[:, KV_DIM:].T.astype(BF16)

    def all_columns():
        z_all = jnp.dot(_norm_mod_rows(x_refs, g_ref, mod_ref), wb_sc[...], preferred_element_type=F32)
        for bb in range(ROW_BATCH):
            z = _tile_rows(z_all, bb)
            qt_ref[bb] = (rope(z[:, :Q_DIM]) * (LOG2E * HEAD_DIM ** -0.5)).T.astype(BF16)
            keys_values(z[:, Q_DIM:Q_DIM + 2 * KV_DIM], bb)
            gate_ref[bb] = z[:, Q_DIM + 2 * KV_DIM:].astype(gate_ref.dtype)

    def keys_values_only():
        z_kv = jnp.dot(_norm_mod_rows(x_refs, g_ref, mod_ref), wb_sc[:, Q_DIM:Q_DIM + 2 * KV_DIM],
                       preferred_element_type=F32)
        for bb in range(ROW_BATCH):
            keys_values(_tile_rows(z_kv, bb), bb)
        qt_ref[...] = jnp.zeros_like(qt_ref)
        gate_ref[...] = jnp.zeros_like(gate_ref)

    if ctx_queries:
        all_columns()
    else:
        pl.when(pl.program_id(1) >= CTX_ROW_TILES)(all_columns)
        pl.when(pl.program_id(1) < CTX_ROW_TILES)(keys_values_only)


def _in_attn(streams, mod, norm_g, w_in, rope_tabs, ctx_queries):
    n = w_in.shape[1]
    col = lambda height: pl.BlockSpec((ROW_BATCH, height, ROW_TILE), lambda b, j: (b, 0, j))
    tab = pl.BlockSpec((ROW_TILE, LANES), lambda b, j: (j, 0))
    return pl.pallas_call(
        functools.partial(_in_attn_kernel, n_x=len(streams), ctx_queries=ctx_queries),
        grid=_ROW_GRID,
        in_specs=_stream_specs(len(streams) > 1)
                 + [_mod_spec(), _const_spec((1, D_MODEL)), _const_spec((D_MODEL, n)), tab, tab, tab],
        out_specs=[col(Q_DIM),
                   pl.BlockSpec((ROW_BATCH, N_KV_HEADS, ROW_TILE, HEAD_DIM), lambda b, j: (b, 0, j, 0)),
                   col(KV_DIM), _row_spec(Q_DIM)],
        out_shape=[jax.ShapeDtypeStruct((BATCH, Q_DIM, TOK), BF16),
                   jax.ShapeDtypeStruct((BATCH, N_KV_HEADS, TOK, HEAD_DIM), BF16),
                   jax.ShapeDtypeStruct((BATCH, KV_DIM, TOK), BF16),
                   jax.ShapeDtypeStruct((BATCH, TOK, Q_DIM), BF16)],
        scratch_shapes=[pltpu.VMEM((D_MODEL, n), BF16)],
        compiler_params=_params("arbitrary", "arbitrary"),
        name="in_attn",
    )(*streams, mod, norm_g.reshape(1, D_MODEL), w_in, *rope_tabs)


def _attn_kernel(sink_ref, qt_ref, k_ref, vt_ref, ot_ref, s_sc, bias_sc, *, first_tile):
    lanes = N_GROUPS * Q_TILE

    key = lax.broadcasted_iota(jnp.int32, (WINDOW, lanes), 0)
    query = lax.broadcasted_iota(jnp.int32, (WINDOW, lanes), 1) & (Q_TILE - 1)
    bias_sc[0] = jnp.where(key >= query, 0.0, NEG_INF)
    bias_sc[1] = jnp.where(key <= query, 0.0, NEG_INF)
    bias_sc[2] = jnp.full((WINDOW, lanes), NEG_INF, F32)

    def aligned(start):
        return start if isinstance(start, int) else pl.multiple_of(start, WINDOW)

    def heads_of(kh):
        return [kh * N_GROUPS + g for g in range(N_GROUPS)]

    def sink_row(kh):
        return jnp.concatenate([jnp.full((1, Q_TILE), sink_ref[h] * LOG2E, F32) for h in heads_of(kh)], axis=1)

    def score_steps(j, kh, chunks, m_box):
        cols = pl.ds(aligned(j * Q_TILE), Q_TILE)
        qt = jnp.concatenate([qt_ref[0, h * HEAD_DIM:(h + 1) * HEAD_DIM, cols] for h in heads_of(kh)], axis=1)
        mx = jnp.broadcast_to(sink_row(kh), (MAX_ROWS, lanes))
        keys = jnp.concatenate([k_ref[0, kh, pl.ds(ks, WINDOW), :] for ks, _ in chunks], axis=0)
        s_all = jnp.dot(keys, qt, preferred_element_type=F32)
        for c, (ks, bias) in enumerate(chunks):
            s = s_all[c * WINDOW:(c + 1) * WINDOW]
            if bias is not None:
                s = s + bias_sc[bias]
            s_sc[kh, c * WINDOW:(c + 1) * WINDOW, :] = s
            mx = jnp.maximum(mx, s.reshape(WINDOW // MAX_ROWS, MAX_ROWS, lanes).max(axis=0))
            yield
        m_box.append(mx.max(axis=0, keepdims=True))

    def value_steps(j, kh, chunks, m_box):
        m = m_box[0]
        acc = jnp.zeros((HEAD_DIM, lanes), F32)
        denom = jnp.zeros((SUBLANES, lanes), F32)
        group = []
        for c, (ks, _) in enumerate(chunks):
            p = jnp.exp2(s_sc[kh, c * WINDOW:(c + 1) * WINDOW, :] - m)
            denom = denom + p.reshape(WINDOW // SUBLANES, SUBLANES, lanes).sum(axis=0)
            group.append((vt_ref[0, kh * HEAD_DIM:(kh + 1) * HEAD_DIM, pl.ds(ks, WINDOW)], p.astype(BF16)))
            if len(group) == MXU_DIM // WINDOW or c == len(chunks) - 1:
                vt = jnp.concatenate([g[0] for g in group], axis=1)
                pp = jnp.concatenate([g[1] for g in group], axis=0)
                acc = acc + jnp.dot(vt, pp, preferred_element_type=F32)
                group = []
            yield
        denom = denom.sum(axis=0, keepdims=True) + jnp.exp2(sink_row(kh) - m)
        o = acc / denom
        cols = pl.ds(aligned((j - first_tile) * Q_TILE), Q_TILE)
        for g, h in enumerate(heads_of(kh)):
            ot_ref[0, h * HEAD_DIM:(h + 1) * HEAD_DIM, cols] = o[:, g * Q_TILE:(g + 1) * Q_TILE].astype(BF16)

    def interleave(*stages):
        stages = list(stages)
        while stages:
            stages = [st for st in stages if next(st, StopIteration) is not StopIteration]

    ctx_chunks = [(c * WINDOW, None) for c in range(CTX_LEN // WINDOW)]

    def latent_chunks(j):
        own = j * Q_TILE
        if isinstance(j, int):
            prev_bias = 2 if j == CTX_Q_TILES else 0
            next_bias = 2 if j == Q_TILES - 1 else 1
            next_start = min(own + WINDOW, TOK - WINDOW)
        else:
            prev_bias = jnp.where(j == CTX_Q_TILES, 2, 0)
            next_bias = jnp.where(j == Q_TILES - 1, 2, 1)
            next_start = jnp.minimum(own + WINDOW, TOK - WINDOW)
        return [(aligned(own - WINDOW), prev_bias), (aligned(own), None),
                (aligned(next_start), next_bias)] + ctx_chunks

    if first_tile == 0:
        units = [(j, kh, ctx_chunks) for j in range(CTX_Q_TILES) for kh in range(N_KV_HEADS)]
        boxes = [[] for _ in units]
        interleave(score_steps(*units[0], boxes[0]))
        for i, unit in enumerate(units):
            stages = [score_steps(*units[i + 1], boxes[i + 1])] if i + 1 < len(units) else []
            interleave(*stages, value_steps(*unit, boxes[i]))

    def latent_tile(j, m_head0, next_j):
        chunks = latent_chunks(j)
        box = [m_head0]
        for kh in range(N_KV_HEADS):
            nxt = []
            stages = []
            if kh + 1 < N_KV_HEADS:
                stages.append(score_steps(j, kh + 1, chunks, nxt))
            elif next_j is not None:
                stages.append(score_steps(next_j, 0, latent_chunks(next_j), nxt))
            interleave(*stages, value_steps(j, kh, chunks, box))
            box = nxt
        return box[0] if box else None

    first_box = []
    interleave(score_steps(CTX_Q_TILES, 0, latent_chunks(CTX_Q_TILES), first_box))
    m_last = lax.fori_loop(CTX_Q_TILES, Q_TILES - 1, lambda j, m: latent_tile(j, m, j + 1), first_box[0],
                           unroll=TILE_UNROLL)
    latent_tile(Q_TILES - 1, m_last, None)


def _attention(qt, k, vt, sink, need_ctx):
    first_tile = 0 if need_ctx else CTX_Q_TILES
    n_keys = BAND + CTX_LEN
    out_cols = TOK - first_tile * Q_TILE
    return pl.pallas_call(
        functools.partial(_attn_kernel, first_tile=first_tile),
        grid=(BATCH,),
        in_specs=[pl.BlockSpec(memory_space=pltpu.SMEM),
                  pl.BlockSpec((1, Q_DIM, TOK), lambda b: (b, 0, 0)),
                  pl.BlockSpec((1, N_KV_HEADS, TOK, HEAD_DIM), lambda b: (b, 0, 0, 0)),
                  pl.BlockSpec((1, KV_DIM, TOK), lambda b: (b, 0, 0))],
        out_specs=pl.BlockSpec((1, Q_DIM, out_cols), lambda b: (b, 0, 0)),
        out_shape=jax.ShapeDtypeStruct((BATCH, Q_DIM, out_cols), BF16),
        scratch_shapes=[pltpu.VMEM((N_KV_HEADS, n_keys, N_GROUPS * Q_TILE), F32),
                        pltpu.VMEM((3, WINDOW, N_GROUPS * Q_TILE), F32)],
        compiler_params=_params("arbitrary"),
        name="attention",
    )(sink, qt, k, vt)


def _lru_kernel(uf_ref, ub_ref, cw_ref, cb_ref, wa_ref, ba_ref, wx_ref, bx_ref, lam_ref,
                hf_ref, hb_ref, u_sc, a_sc, b_sc, y_sc, h_sc):
    step = pl.program_id(1)
    tl, pitch, halo = LRU_TL, LRU_PITCH, SUBLANES
    segment_start = (step == 0) | (step == LRU_CTX_BLKS)

    @pl.when(segment_start)
    def _zero_halo():
        u_sc[0, :, 0:halo, :] = jnp.zeros((BATCH * LRU_SLABS, halo, LANES), F32)
        u_sc[1, :, halo + tl:, :] = jnp.zeros((BATCH * LRU_SLABS, halo, LANES), F32)

    @pl.when(jnp.logical_not(segment_start))
    def _carry_halo():
        u_sc[0, :, 0:halo, :] = u_sc[0, :, tl:tl + halo, :]
        u_sc[1, :, halo + tl:, :] = u_sc[1, :, halo:2 * halo, :]

    @pl.when(step == 0)
    def _sequence_start():
        h_sc[...] = jnp.zeros_like(h_sc)

    for d, u_ref in ((0, uf_ref), (1, ub_ref)):
        for bi in range(BATCH):
            for s in range(LRU_SLABS):
                u_sc[d, bi * LRU_SLABS + s, halo:halo + tl, :] = u_ref[bi, :, s * LANES:(s + 1) * LANES]

    for d in range(2):
        nl = -lam_ref[d]
        half_decay = (0.5 * LRU_C) * (jnp.maximum(nl, 0.0) + jnp.log1p(jnp.exp(-jnp.abs(nl))))
        for bi in range(BATCH):
            cols = []
            for s in range(LRU_SLABS):
                lanes = slice(s * LANES, (s + 1) * LANES)
                half_x = 0.5 * cb_ref[d][:, lanes]
                for kk in range(LRU_CONV):
                    off = halo + (kk - (LRU_CONV - 1) if d == 0 else (LRU_CONV - 1) - kk)
                    rows = pl.ds(off, tl, stride=1)
                    tap = 0.5 * cw_ref[d][kk:kk + 1, lanes]
                    half_x = half_x + tap * u_sc[d, bi * LRU_SLABS + s, rows, :]
                cols.append(half_x)
            half_x = jnp.concatenate(cols, axis=1)
            xb = half_x.astype(BF16)
            tr = jnp.tanh(jnp.dot(xb, wa_ref[d, 0], preferred_element_type=F32) + ba_ref[d])
            ti = jnp.tanh(jnp.dot(xb, wx_ref[d, 0], preferred_element_type=F32) + bx_ref[d])
            neg_log_a = tr * half_decay + half_decay
            ix = ti * half_x + half_x
            a = jnp.exp(-neg_log_a)
            var = jnp.tanh(neg_log_a) * (a * a + 1.0)
            b = (var * lax.rsqrt(jnp.maximum(var, F32_TINY))) * ix
            for s in range(LRU_SLABS):
                lanes = slice(s * LANES, (s + 1) * LANES)
                rows = pl.ds(bi * pitch, tl, stride=1)
                a_sc[d, s, rows, :] = a[:, lanes]
                b_sc[d, s, rows, :] = b[:, lanes]

    def scan_step(t, carry):
        new = []
        for d in range(2):
            tt = t if d == 0 else tl - 1 - t
            for s in range(LRU_SLABS):
                rows = pl.ds(tt, BATCH, stride=pitch)
                h = a_sc[d, s, rows, :] * carry[d * LRU_SLABS + s] + b_sc[d, s, rows, :]
                y_sc[d, s, rows, :] = h
                new.append(h)
        return tuple(new)

    init = tuple(h_sc[d, s] for d in range(2) for s in range(LRU_SLABS))
    final = lax.fori_loop(0, tl, scan_step, init, unroll=8)
    for d in range(2):
        for s in range(LRU_SLABS):
            h_sc[d, s] = final[d * LRU_SLABS + s]

    for d, o_ref in ((0, hf_ref), (1, hb_ref)):
        for bi in range(BATCH):
            for s in range(LRU_SLABS):
                o_ref[bi, :, s * LANES:(s + 1) * LANES] = (
                    y_sc[d, s, pl.ds(bi * pitch, tl, stride=1), :].astype(o_ref.dtype))


def _block_diag_chunks(w):
    per = MXU_DIM // LRU_BLOCK
    rows = w.reshape(2, LRU_BLOCKS // per, MXU_DIM, LRU_BLOCK)
    tiled = jnp.concatenate([rows] * per, axis=-1)
    blocks = np.arange(MXU_DIM) // LRU_BLOCK
    on_diagonal = jnp.asarray(blocks[:, None] == blocks[None, :])
    return jnp.where(on_diagonal, tiled, 0.0).astype(BF16)


def _lru_scan(u, conv_w, conv_b, gate_a_w, gate_a_b, gate_x_w, gate_x_b, lam):
    def fwd_blk(w, s):
        return (0, s, w)

    def bwd_blk(w, s):
        blk = jnp.where(s < LRU_CTX_BLKS, LRU_CTX_BLKS - 1 - s, LRU_BLKS - 1 - (s - LRU_CTX_BLKS))
        return (0, blk, w)

    blk = (BATCH, LRU_TL, LRU_WC)
    vec = pl.BlockSpec((2, 1, LRU_WC), lambda w, s: (0, 0, w))
    gate = pl.BlockSpec((2, LRU_WC // MXU_DIM, MXU_DIM, MXU_DIM), lambda w, s: (0, w, 0, 0))
    width = u.shape[-1]
    out = jax.ShapeDtypeStruct((BATCH, TOK, width), BF16)
    scan_buf = pltpu.VMEM((2, LRU_SLABS, BATCH * LRU_PITCH, LANES), F32)
    return pl.pallas_call(
        _lru_kernel,
        grid=(width // LRU_WC, LRU_BLKS),
        in_specs=[pl.BlockSpec(blk, fwd_blk), pl.BlockSpec(blk, bwd_blk),
                  pl.BlockSpec((2, LRU_CONV, LRU_WC), lambda w, s: (0, 0, w)), vec,
                  gate, vec, gate, vec, vec],
        out_specs=[pl.BlockSpec(blk, fwd_blk), pl.BlockSpec(blk, bwd_blk)],
        out_shape=[out, out],
        scratch_shapes=[pltpu.VMEM((2, BATCH * LRU_SLABS, LRU_TL + 2 * SUBLANES, LANES), F32),
                        scan_buf, scan_buf, scan_buf,
                        pltpu.VMEM((2, LRU_SLABS, BATCH, LANES), F32)],
        compiler_params=_params("parallel", "arbitrary"),
        name="lru_scan",
    )(u, u, conv_w, conv_b.reshape(2, 1, width),
      _block_diag_chunks(gate_a_w), 0.5 * gate_a_b.reshape(2, 1, width),
      _block_diag_chunks(gate_x_w), 0.5 * gate_x_b.reshape(2, 1, width),
      lam.reshape(2, 1, width))


def _residual_rows(x_refs, mod_ref, acts, wb_sc, o_ref, post=None):
    y = jnp.dot(jnp.concatenate(acts, axis=0).astype(BF16), wb_sc[...], preferred_element_type=F32)
    for bb in range(len(acts)):
        out = _stream_tile(x_refs, bb) + mod_ref[bb, 0][2:3, :] * _tile_rows(y, bb)
        o_ref[bb] = out if post is None else post(out)


def _out_attn_final_kernel(ot_ref, g_ref, w_ref, x_ref, mod_ref, fg_ref, o_ref, wb_sc):
    _cast_once(w_ref, wb_sc)
    acts = [ot_ref[bb].astype(F32).T * _silu(g_ref[bb].astype(F32)) for bb in range(OUT_ROW_BATCH)]

    def final_norm(x):
        return x * lax.rsqrt(jnp.mean(x * x, axis=-1, keepdims=True) + EPS) * fg_ref[...]

    _residual_rows((x_ref,), mod_ref, acts, wb_sc, o_ref, final_norm)


def _out_attn_in_next_kernel(ot_ref, g_ref, wo_ref, ctx_ref, lat_ref, mod_ref, ng_ref, nmod_ref, wi_ref,
                             xs_ref, *refs):
    o_refs, (wo_sc, wi_sc) = refs[:-2], refs[-2:]
    _cast_once(wo_ref, wo_sc)
    _cast_once(wi_ref, wi_sc)
    acts = [ot_ref[bb].astype(F32).T * _silu(g_ref[bb].astype(F32)) for bb in range(ROW_BATCH)]
    y = jnp.dot(jnp.concatenate(acts, axis=0).astype(BF16), wo_sc[...], preferred_element_type=F32)
    tiles = []
    for bb in range(ROW_BATCH):
        out = _stream_tile((ctx_ref, lat_ref), bb) + mod_ref[bb, 0][2:3, :] * _tile_rows(y, bb)
        xs_ref[bb] = out
        tiles.append(_norm_mod(out, ng_ref[...], nmod_ref[bb, 0]))
    z = jnp.dot(jnp.concatenate(tiles, axis=0).astype(BF16), wi_sc[...], preferred_element_type=F32)
    width = z.shape[-1] // len(o_refs)
    for i, o_ref in enumerate(o_refs):
        for bb in range(ROW_BATCH):
            o_ref[bb] = _tile_rows(z, bb)[:, i * width:(i + 1) * width].astype(o_ref.dtype)


def _halo_rows(main_ref, prev_ref, next_ref, bb):
    return jnp.concatenate([prev_ref[bb], main_ref[bb], next_ref[bb]], axis=0).astype(F32)


def _out_lru_conv_kernel(hf_ref, hfp_ref, hfn_ref, hb_ref, hbp_ref, hbn_ref, gl_ref, glp_ref, gln_ref,
                         x_ref, xp_ref, xn_ref, lmod_ref, lwo_ref, mod_ref, g_ref, wu_ref, wb_ref, wc_ref,
                         wg_ref, cw_ref, cb_ref, wo_ref, o_ref, p_sc, lwo_sc, wo_sc):
    j = pl.program_id(1)
    halo, cc = SUBLANES, CONV_CHUNK
    rows = ROW_TILE + 2 * halo
    _cast_once(lwo_ref, lwo_sc)
    _cast_once(wo_ref, wo_sc)
    acts = [(_halo_rows(hf_ref, hfp_ref, hfn_ref, bb) + _halo_rows(hb_ref, hbp_ref, hbn_ref, bb))
            * _silu(_halo_rows(gl_ref, glp_ref, gln_ref, bb)) for bb in range(ROW_BATCH)]
    y = jnp.dot(jnp.concatenate(acts, axis=0).astype(BF16), lwo_sc[...], preferred_element_type=F32)
    x1 = [_halo_rows(x_ref, xp_ref, xn_ref, bb) + lmod_ref[bb, 0][2:3, :] * y[bb * rows:(bb + 1) * rows]
          for bb in range(ROW_BATCH)]
    tiles = [_norm_mod(x1[bb], g_ref[...], mod_ref[bb, 0]) for bb in range(ROW_BATCH)]
    h = jnp.concatenate(tiles, axis=0).astype(BF16)
    h_main = jnp.concatenate([h[bb * rows + halo:bb * rows + halo + ROW_TILE] for bb in range(ROW_BATCH)], axis=0)
    keep_prev = jnp.where((j == 0) | (j == CTX_ROW_TILES), 0.0, 1.0)
    keep_next = jnp.where((j == CTX_ROW_TILES - 1) | (j == ROW_TILES - 1), 0.0, 1.0)
    acc = jnp.zeros((ROW_BATCH * ROW_TILE, D_MODEL), F32)
    for c in range(D_MODEL // cc):
        chans = slice(c * cc, (c + 1) * cc)
        u, cg = (jnp.dot(h, w_ref[:, chans], preferred_element_type=F32) for w_ref in (wu_ref, wc_ref))
        bg, g = (jnp.dot(h_main, w_ref[:, chans], preferred_element_type=F32) for w_ref in (wb_ref, wg_ref))
        p = cg * u
        acts = []
        for bb in range(ROW_BATCH):
            top = bb * rows
            main = slice(top + halo, top + halo + ROW_TILE)
            tile = slice(bb * ROW_TILE, (bb + 1) * ROW_TILE)
            for s in range(cc // LANES):
                lanes = slice(s * LANES, (s + 1) * LANES)
                p_sc[bb, c, s, 0:halo, :] = p[top:top + halo, lanes] * keep_prev
                p_sc[bb, c, s, halo:halo + ROW_TILE, :] = p[main, lanes]
                p_sc[bb, c, s, halo + ROW_TILE:rows, :] = p[top + halo + ROW_TILE:top + rows, lanes] * keep_next
            before = jnp.concatenate([p_sc[bb, c, s, pl.ds(halo - 1, ROW_TILE, stride=1), :]
                                      for s in range(cc // LANES)], axis=1)
            after = jnp.concatenate([p_sc[bb, c, s, pl.ds(halo + 1, ROW_TILE, stride=1), :]
                                     for s in range(cc // LANES)], axis=1)
            conv = (cw_ref[0:1, chans] * before + cw_ref[1:2, chans] * p[main] + cw_ref[2:3, chans] * after
                    + cb_ref[:, chans])
            acts.append(bg[tile] * conv * _silu(g[tile]))
        acc = acc + jnp.dot(jnp.concatenate(acts, axis=0).astype(BF16), wo_sc[chans, :],
                            preferred_element_type=F32)
    for bb in range(ROW_BATCH):
        o_ref[bb] = x1[bb][halo:halo + ROW_TILE] + mod_ref[bb, 0][2:3, :] * _tile_rows(acc, bb)


_W_OUT_SPEC = _const_spec((D_MODEL, D_MODEL))


def _out_attn_final(ot, gate, w_out, xs, mod, final_g):
    off, rb = CTX_ROW_TILES, OUT_ROW_BATCH
    return pl.pallas_call(
        _out_attn_final_kernel,
        grid=(BATCH // rb, ROW_TILES - off),
        in_specs=[pl.BlockSpec((rb, Q_DIM, ROW_TILE), lambda b, j: (b, 0, j)), _row_spec(Q_DIM, off, rb),
                  _W_OUT_SPEC, _row_spec(D_MODEL, off, rb), _mod_spec(off, rb), _const_spec((1, D_MODEL))],
        out_specs=_row_spec(D_MODEL, rb=rb),
        out_shape=jax.ShapeDtypeStruct((BATCH, SEQ, D_MODEL), F32),
        scratch_shapes=[pltpu.VMEM((D_MODEL, D_MODEL), BF16)],
        compiler_params=_params("arbitrary", "arbitrary"),
        name="out_attn_final",
    )(ot, gate, w_out, xs, mod, final_g.reshape(1, D_MODEL))


def _out_attn_in_next(ot, gate, w_out, ctx, x, mod, next_norm_g, next_mod, next_w_in, dtypes):
    n = next_w_in.shape[1]
    width = n // len(dtypes)
    return pl.pallas_call(
        _out_attn_in_next_kernel,
        grid=_ROW_GRID,
        in_specs=[pl.BlockSpec((ROW_BATCH, Q_DIM, ROW_TILE), lambda b, j: (b, 0, j)), _row_spec(Q_DIM),
                  _W_OUT_SPEC] + _stream_specs(True)
                 + [_mod_spec(), _const_spec((1, D_MODEL)), _mod_spec(), _const_spec((D_MODEL, n))],
        out_specs=[_row_spec(D_MODEL)] + [_row_spec(width)] * len(dtypes),
        out_shape=[jax.ShapeDtypeStruct((BATCH, TOK, D_MODEL), F32)]
                  + [jax.ShapeDtypeStruct((BATCH, TOK, width), dt) for dt in dtypes],
        scratch_shapes=[pltpu.VMEM((D_MODEL, D_MODEL), BF16), pltpu.VMEM((D_MODEL, n), BF16)],
        compiler_params=_params("arbitrary", "arbitrary"),
        name="out_attn_in_lru",
    )(ot, gate, w_out, ctx, x, mod, next_norm_g.reshape(1, D_MODEL), next_mod, next_w_in)


def _out_lru_conv(hf, hb, gate, lru_w_out, xs, lru_mod, mod, norm_g, w_in, conv_w, conv_b, w_out):
    per = ROW_TILE // SUBLANES
    n8 = TOK // SUBLANES
    halo = lambda idx: pl.BlockSpec((ROW_BATCH, SUBLANES, D_MODEL), idx)
    prev = halo(lambda b, j: (b, jnp.maximum(j * per - 1, 0), 0))
    nxt = halo(lambda b, j: (b, jnp.minimum((j + 1) * per, n8 - 1), 0))
    with_halo = [_row_spec(D_MODEL), prev, nxt]
    wi = w_in.astype(BF16)
    w_blocks = [_const_spec((D_MODEL, D_MODEL), functools.partial(lambda i, b, j: (0, i), i)) for i in range(4)]
    return pl.pallas_call(
        _out_lru_conv_kernel,
        grid=_ROW_GRID,
        in_specs=with_halo * 4 + [_mod_spec(), _W_OUT_SPEC, _mod_spec(), _const_spec((1, D_MODEL))] + w_blocks
                 + [_const_spec((CONV_K, D_MODEL)), _const_spec((1, D_MODEL)), _W_OUT_SPEC],
        out_specs=_row_spec(D_MODEL),
        out_shape=jax.ShapeDtypeStruct((BATCH, TOK, D_MODEL), F32),
        scratch_shapes=[pltpu.VMEM((ROW_BATCH, D_MODEL // CONV_CHUNK, CONV_CHUNK // LANES,
                                    ROW_TILE + 2 * SUBLANES, LANES), F32),
                        pltpu.VMEM((D_MODEL, D_MODEL), BF16), pltpu.VMEM((D_MODEL, D_MODEL), BF16)],
        compiler_params=_params("arbitrary", "arbitrary"),
        name="out_lru_conv",
    )(hf, hf, hf, hb, hb, hb, gate, gate, gate, xs, xs, xs, lru_mod, lru_w_out, mod,
      norm_g.reshape(1, D_MODEL), wi, wi, wi, wi, conv_w, conv_b.reshape(1, D_MODEL), w_out)


def _rope_tables():
    quarter = HEAD_DIM // 4
    pos = np.arange(SEQ)
    half = HEAD_DIM // 2
    inv = (1.0 / (ROPE_BASE ** (np.arange(0, half, 2, dtype=np.float32) / half))).astype(np.float32)
    zero = np.zeros((SEQ, quarter), np.float32)
    parts_c, parts_a, parts_b = [], [], []
    for axis_pos in ((pos // GRID_W).astype(np.float32), (pos % GRID_W).astype(np.float32)):
        ang = axis_pos[:, None] * inv
        c, s = np.cos(ang), np.sin(ang)
        parts_c += [c, c]
        parts_a += [-s, zero]
        parts_b += [zero, s]

    def table(parts, ctx_value):
        head = np.concatenate(parts, axis=1)
        head = np.concatenate([np.full((CTX_LEN, HEAD_DIM), ctx_value, np.float32), head], axis=0)
        return jnp.asarray(np.tile(head, (1, LANES // HEAD_DIM)).astype(np.float32))

    return table(parts_c, 1.0), table(parts_a, 0.0), table(parts_b, 0.0)


def kernel(x, c, ctx, c_ctx, l0_norm_g, l0_mod_w, l0_mod_b, l0_w_in, l0_w_out, l0_sink, l1_norm_g, l1_mod_w, l1_mod_b, l1_w_in, l1_w_out, l1_conv_w, l1_conv_b, l1_gate_a_w, l1_gate_a_b, l1_gate_x_w, l1_gate_x_b, l1_lambda, l2_norm_g, l2_mod_w, l2_mod_b, l2_w_in, l2_w_out, l2_conv_w, l2_conv_b, l3_norm_g, l3_mod_w, l3_mod_b, l3_w_in, l3_w_out, l3_sink, final_norm_g):
    mods = _modulation(c, c_ctx, (l0_mod_w, l1_mod_w, l2_mod_w, l3_mod_w),
                       (l0_mod_b, l1_mod_b, l2_mod_b, l3_mod_b))
    rope_tabs = _rope_tables()

    qt, k, vt, gate = _in_attn((ctx, x), mods[0], l0_norm_g, l0_w_in, rope_tabs, ctx_queries=True)
    ot = _attention(qt, k, vt, l0_sink, need_ctx=True)
    xs, u, gate = _out_attn_in_next(ot, gate, l0_w_out, ctx, x, mods[0], l1_norm_g, mods[1], l1_w_in,
                                    (F32, BF16))

    hf, hb = _lru_scan(u, l1_conv_w, l1_conv_b, l1_gate_a_w, l1_gate_a_b, l1_gate_x_w, l1_gate_x_b,
                       l1_lambda)
    xs = _out_lru_conv(hf, hb, gate, l1_w_out, xs, mods[1], mods[2], l2_norm_g, l2_w_in, l2_conv_w,
                       l2_conv_b, l2_w_out)

    qt, k, vt, gate = _in_attn((xs,), mods[3], l3_norm_g, l3_w_in, rope_tabs, ctx_queries=False)
    ot = _attention(qt, k, vt, l3_sink, need_ctx=False)
    return _out_attn_final(ot, gate, l3_w_out, xs, mods[3], final_norm_g)
```

```python
import functools
import math

import jax
import jax.numpy as jnp
import numpy as np
from jax import lax
from jax.experimental import pallas as pl
from jax.experimental.pallas import tpu as pltpu

D_MODEL = 1024
BATCH = 8
SEQ = 2048
CTX_LEN = 256
TOK = CTX_LEN + SEQ
GRID_W = 64
EPS = 1e-6
NEG_INF = -1e30
N_HEADS = 16
N_KV_HEADS = 4
N_GROUPS = N_HEADS // N_KV_HEADS
HEAD_DIM = 64
Q_DIM = N_HEADS * HEAD_DIM
KV_DIM = N_KV_HEADS * HEAD_DIM
WINDOW = 128
ROPE_BASE = 10000.0
LRU_BLOCKS = 16
LRU_BLOCK = 64
LRU_C = 8.0
LRU_CONV = 4
CONV_K = 3
MOD_PARTS = 3

LANES = 128
SUBLANES = 8
MXU_DIM = 256
VMEM_LIMIT_BYTES = 56 * 1024 * 1024

ROW_TILE = 256
ROW_BATCH = 2
OUT_ROW_BATCH = 4
CTX_ROW_TILES = CTX_LEN // ROW_TILE
ROW_TILES = TOK // ROW_TILE
Q_TILE = WINDOW
CTX_Q_TILES = CTX_LEN // Q_TILE
Q_TILES = TOK // Q_TILE
BAND = 3 * WINDOW
MAX_ROWS = SUBLANES
TILE_UNROLL = 5
CONV_CHUNK = MXU_DIM

LRU_TL = 256
LRU_WC = MXU_DIM
LRU_PITCH = LRU_TL + 4
LRU_SLABS = LRU_WC // LANES
LRU_CTX_BLKS = CTX_LEN // LRU_TL
LRU_BLKS = TOK // LRU_TL

F32 = jnp.float32
BF16 = jnp.bfloat16
F32_TINY = float(jnp.finfo(jnp.float32).tiny)
LOG2E = math.log2(math.e)


def _params(*sem):
    return pltpu.CompilerParams(dimension_semantics=sem, vmem_limit_bytes=VMEM_LIMIT_BYTES)


def _sigmoid(x):
    return 0.5 * jnp.tanh(0.5 * x) + 0.5


def _silu(x):
    return x * _sigmoid(x)


def _mod_kernel(c_ref, b_ref, *refs):
    w_refs, o_ref = refs[:-1], refs[-1]
    layer = pl.program_id(0) // MOD_PARTS
    s = _silu(c_ref[...])
    s_hi = s.astype(BF16)
    s_lo = (s - s_hi.astype(F32)).astype(BF16)
    for l, w_ref in enumerate(w_refs):
        @pl.when(layer == l)
        def _layer(w_ref=w_ref):
            w = w_ref[...]
            w_hi = w.astype(BF16)
            w_lo = (w - w_hi.astype(F32)).astype(BF16)
            acc = jnp.dot(s_hi, w_hi, preferred_element_type=F32)
            acc += jnp.dot(s_hi, w_lo, preferred_element_type=F32)
            acc += jnp.dot(s_lo, w_hi, preferred_element_type=F32)
            m = acc + b_ref[0]
            o_ref[0, :, 0, :] = jnp.broadcast_to(m[BATCH:BATCH + 1], (BATCH, D_MODEL))
            o_ref[0, :, 1, :] = m[:BATCH]


def _modulation(c, c_ctx, mod_ws, mod_bs):
    n = len(mod_ws)
    cc = jnp.concatenate([c, c_ctx[None], jnp.zeros((2 * SUBLANES - BATCH - 1, D_MODEL), F32)], axis=0)
    rows = cc.shape[0]
    w_spec = lambda l: pl.BlockSpec((D_MODEL, D_MODEL),
                                    lambda s: (0, jnp.clip(s - MOD_PARTS * l, 0, MOD_PARTS - 1)))
    part = lambda s: (s // MOD_PARTS, s % MOD_PARTS)
    m = pl.pallas_call(
        _mod_kernel,
        grid=(MOD_PARTS * n,),
        in_specs=[pl.BlockSpec((rows, D_MODEL), lambda s: (0, 0)),
                  pl.BlockSpec((1, 1, D_MODEL), lambda s: (part(s)[0], 0, part(s)[1]))]
                 + [w_spec(l) for l in range(n)],
        out_specs=pl.BlockSpec((1, BATCH, 2, D_MODEL), lambda s: (part(s)[0], 0, 0, part(s)[1])),
        out_shape=jax.ShapeDtypeStruct((n, BATCH, 2, MOD_PARTS * D_MODEL), F32),
        compiler_params=_params("arbitrary"),
        name="modulation",
    )(cc, jnp.stack(mod_bs).reshape(n, 1, MOD_PARTS * D_MODEL), *mod_ws)
    return m.reshape(n, BATCH, 2, MOD_PARTS, D_MODEL)


def _mod_spec(tile_offset=0, rb=ROW_BATCH):
    def idx(b, j):
        return (b, jnp.where(j + tile_offset < CTX_ROW_TILES, 0, 1), 0, 0)
    return pl.BlockSpec((rb, 1, MOD_PARTS, D_MODEL), idx)


def _row_spec(width, tile_offset=0, rb=ROW_BATCH):
    return pl.BlockSpec((rb, ROW_TILE, width), lambda b, j: (b, j + tile_offset, 0))


def _const_spec(shape, index=None):
    index = index or (lambda b, j: (0,) * len(shape))
    return pl.BlockSpec(shape, index, pipeline_mode=pl.Buffered(1))


def _stream_specs(split, rb=ROW_BATCH):
    if not split:
        return [_row_spec(D_MODEL, rb=rb)]
    blk = (rb, ROW_TILE, D_MODEL)
    return [pl.BlockSpec(blk, lambda b, j: (b, 0, 0)),
            pl.BlockSpec(blk, lambda b, j: (b, jnp.maximum(j - CTX_ROW_TILES, 0), 0))]


def _stream_tile(x_refs, bb):
    if len(x_refs) == 1:
        return x_refs[0][bb]
    ctx_ref, lat_ref = x_refs
    return jnp.where(pl.program_id(1) < CTX_ROW_TILES, ctx_ref[bb], lat_ref[bb])


def _cast_once(w_ref, wb_sc):
    @pl.when((pl.program_id(0) == 0) & (pl.program_id(1) == 0))
    def _cast():
        wb_sc[...] = w_ref[...].astype(BF16)


_ROW_GRID = (BATCH // ROW_BATCH, ROW_TILES)


def _norm_mod(x, g, mod):
    y = x * lax.rsqrt(jnp.mean(x * x, axis=-1, keepdims=True) + EPS) * g
    return y * (1.0 + mod[1:2, :]) + mod[0:1, :]


def _norm_mod_rows(x_refs, g_ref, mod_ref):
    tiles = [_norm_mod(_stream_tile(x_refs, bb), g_ref[...], mod_ref[bb, 0]) for bb in range(ROW_BATCH)]
    return jnp.concatenate(tiles, axis=0).astype(BF16)


def _tile_rows(t, bb):
    return t[bb * ROW_TILE:(bb + 1) * ROW_TILE]


def _in_attn_kernel(*refs, n_x, ctx_queries):
    x_refs = refs[:n_x]
    mod_ref, g_ref, w_ref, rc_ref, ra_ref, rb_ref, qt_ref, k_ref, vt_ref, gate_ref, wb_sc = refs[n_x:]
    _cast_once(w_ref, wb_sc)
    rc, ra, rb = rc_ref[...], ra_ref[...], rb_ref[...]

    def rope(t):
        reps = t.shape[-1] // LANES
        n = t.shape[-1]
        return (t * jnp.tile(rc, (1, reps))
                + pltpu.roll(t, n - HEAD_DIM // 4, 1) * jnp.tile(ra, (1, reps))
                + pltpu.roll(t, HEAD_DIM // 4, 1) * jnp.tile(rb, (1, reps)))

    def keys_values(z_kv, bb):
        k = rope(z_kv[:, :KV_DIM]).astype(BF16)
        for kh in range(N_KV_HEADS):
            k_ref[bb, kh] = k[:, kh * HEAD_DIM:(kh + 1) * HEAD_DIM]
        vt_ref[bb] = z_kv[:, KV_DIM:].T.astype(BF16)

    def all_columns():
        z_all = jnp.dot(_norm_mod_rows(x_refs, g_ref, mod_ref), wb_sc[...], preferred_element_type=F32)
        for bb in range(ROW_BATCH):
            z = _tile_rows(z_all, bb)
            qt_ref[bb] = (rope(z[:, :Q_DIM]) * (LOG2E * HEAD_DIM ** -0.5)).T.astype(BF16)
            keys_values(z[:, Q_DIM:Q_DIM + 2 * KV_DIM], bb)
            gate_ref[bb] = z[:, Q_DIM + 2 * KV_DIM:].astype(gate_ref.dtype)

    def keys_values_only():
        z_kv = jnp.dot(_norm_mod_rows(x_refs, g_ref, mod_ref), wb_sc[:, Q_DIM:Q_DIM + 2 * KV_DIM],
                       preferred_element_type=F32)
        for bb in range(ROW_BATCH):
            keys_values(_tile_rows(z_kv, bb), bb)
        qt_ref[...] = jnp.zeros_like(qt_ref)
        gate_ref[...] = jnp.zeros_like(gate_ref)

    if ctx_queries:
        all_columns()
    else:
        pl.when(pl.program_id(1) >= CTX_ROW_TILES)(all_columns)
        pl.when(pl.program_id(1) < CTX_ROW_TILES)(keys_values_only)


def _in_attn(streams, mod, norm_g, w_in, rope_tabs, ctx_queries):
    n = w_in.shape[1]
    col = lambda height: pl.BlockSpec((ROW_BATCH, height, ROW_TILE), lambda b, j: (b, 0, j))
    tab = pl.BlockSpec((ROW_TILE, LANES), lambda b, j: (j, 0))
    return pl.pallas_call(
        functools.partial(_in_attn_kernel, n_x=len(streams), ctx_queries=ctx_queries),
        grid=_ROW_GRID,
        in_specs=_stream_specs(len(streams) > 1)
                 + [_mod_spec(), _const_spec((1, D_MODEL)), _const_spec((D_MODEL, n)), tab, tab, tab],
        out_specs=[col(Q_DIM),
                   pl.BlockSpec((ROW_BATCH, N_KV_HEADS, ROW_TILE, HEAD_DIM), lambda b, j: (b, 0, j, 0)),
                   col(KV_DIM), _row_spec(Q_DIM)],
        out_shape=[jax.ShapeDtypeStruct((BATCH, Q_DIM, TOK), BF16),
                   jax.ShapeDtypeStruct((BATCH, N_KV_HEADS, TOK, HEAD_DIM), BF16),
                   jax.ShapeDtypeStruct((BATCH, KV_DIM, TOK), BF16),
                   jax.ShapeDtypeStruct((BATCH, TOK, Q_DIM), BF16)],
        scratch_shapes=[pltpu.VMEM((D_MODEL, n), BF16)],
        compiler_params=_params("arbitrary", "arbitrary"),
        name="in_attn",
    )(*streams, mod, norm_g.reshape(1, D_MODEL), w_in, *rope_tabs)


def _attn_kernel(sink_ref, qt_ref, k_ref, vt_ref, *refs, first_tile, side_cast):
    lanes = N_GROUPS * Q_TILE
    if side_cast:
        w_ref, ot_ref, wb_ref, s_sc, bias_sc = refs
        wb_ref[...] = w_ref[...].astype(BF16)
    else:
        ot_ref, s_sc, bias_sc = refs

    key = lax.broadcasted_iota(jnp.int32, (WINDOW, lanes), 0)
    query = lax.broadcasted_iota(jnp.int32, (WINDOW, lanes), 1) & (Q_TILE - 1)
    bias_sc[0] = jnp.where(key >= query, 0.0, NEG_INF)
    bias_sc[1] = jnp.where(key <= query, 0.0, NEG_INF)
    bias_sc[2] = jnp.full((WINDOW, lanes), NEG_INF, F32)

    def aligned(start):
        return start if isinstance(start, int) else pl.multiple_of(start, WINDOW)

    def heads_of(kh):
        return [kh * N_GROUPS + g for g in range(N_GROUPS)]

    def sink_row(kh):
        return jnp.concatenate([jnp.full((1, Q_TILE), sink_ref[h] * LOG2E, F32) for h in heads_of(kh)], axis=1)

    def score_steps(j, kh, chunks, m_box):
        cols = pl.ds(aligned(j * Q_TILE), Q_TILE)
        qt = jnp.concatenate([qt_ref[0, h * HEAD_DIM:(h + 1) * HEAD_DIM, cols] for h in heads_of(kh)], axis=1)
        mx = jnp.broadcast_to(sink_row(kh), (MAX_ROWS, lanes))
        keys = jnp.concatenate([k_ref[0, kh, pl.ds(ks, WINDOW), :] for ks, _ in chunks], axis=0)
        s_all = jnp.dot(keys, qt, preferred_element_type=F32)
        for c, (ks, bias) in enumerate(chunks):
            s = s_all[c * WINDOW:(c + 1) * WINDOW]
            if bias is not None:
                s = s + bias_sc[bias]
            s_sc[kh, c * WINDOW:(c + 1) * WINDOW, :] = s
            mx = jnp.maximum(mx, s.reshape(WINDOW // MAX_ROWS, MAX_ROWS, lanes).max(axis=0))
            yield
        m_box.append(mx.max(axis=0, keepdims=True))

    def value_steps(j, kh, chunks, m_box):
        m = m_box[0]
        acc = jnp.zeros((HEAD_DIM, lanes), F32)
        denom = jnp.zeros((SUBLANES, lanes), F32)
        group = []
        for c, (ks, _) in enumerate(chunks):
            p = jnp.exp2(s_sc[kh, c * WINDOW:(c + 1) * WINDOW, :] - m)
            denom = denom + p.reshape(WINDOW // SUBLANES, SUBLANES, lanes).sum(axis=0)
            group.append((vt_ref[0, kh * HEAD_DIM:(kh + 1) * HEAD_DIM, pl.ds(ks, WINDOW)], p.astype(BF16)))
            if len(group) == MXU_DIM // WINDOW or c == len(chunks) - 1:
                vt = jnp.concatenate([g[0] for g in group], axis=1)
                pp = jnp.concatenate([g[1] for g in group], axis=0)
                acc = acc + jnp.dot(vt, pp, preferred_element_type=F32)
                group = []
            yield
        denom = denom.sum(axis=0, keepdims=True) + jnp.exp2(sink_row(kh) - m)
        o = acc / denom
        cols = pl.ds(aligned((j - first_tile) * Q_TILE), Q_TILE)
        for g, h in enumerate(heads_of(kh)):
            ot_ref[0, h * HEAD_DIM:(h + 1) * HEAD_DIM, cols] = o[:, g * Q_TILE:(g + 1) * Q_TILE].astype(BF16)

    def interleave(*stages):
        stages = list(stages)
        while stages:
            stages = [st for st in stages if next(st, StopIteration) is not StopIteration]

    ctx_chunks = [(c * WINDOW, None) for c in range(CTX_LEN // WINDOW)]

    def latent_chunks(j):
        own = j * Q_TILE
        if isinstance(j, int):
            prev_bias = 2 if j == CTX_Q_TILES else 0
            next_bias = 2 if j == Q_TILES - 1 else 1
            next_start = min(own + WINDOW, TOK - WINDOW)
        else:
            prev_bias = jnp.where(j == CTX_Q_TILES, 2, 0)
            next_bias = jnp.where(j == Q_TILES - 1, 2, 1)
            next_start = jnp.minimum(own + WINDOW, TOK - WINDOW)
        return [(aligned(own - WINDOW), prev_bias), (aligned(own), None),
                (aligned(next_start), next_bias)] + ctx_chunks

    if first_tile == 0:
        units = [(j, kh, ctx_chunks) for j in range(CTX_Q_TILES) for kh in range(N_KV_HEADS)]
        boxes = [[] for _ in units]
        interleave(score_steps(*units[0], boxes[0]))
        for i, unit in enumerate(units):
            stages = [score_steps(*units[i + 1], boxes[i + 1])] if i + 1 < len(units) else []
            interleave(*stages, value_steps(*unit, boxes[i]))

    def latent_tile(j, m_head0, next_j):
        chunks = latent_chunks(j)
        box = [m_head0]
        for kh in range(N_KV_HEADS):
            nxt = []
            stages = []
            if kh + 1 < N_KV_HEADS:
                stages.append(score_steps(j, kh + 1, chunks, nxt))
            elif next_j is not None:
                stages.append(score_steps(next_j, 0, latent_chunks(next_j), nxt))
            interleave(*stages, value_steps(j, kh, chunks, box))
            box = nxt
        return box[0] if box else None

    first_box = []
    interleave(score_steps(CTX_Q_TILES, 0, latent_chunks(CTX_Q_TILES), first_box))
    m_last = lax.fori_loop(CTX_Q_TILES, Q_TILES - 1, lambda j, m: latent_tile(j, m, j + 1), first_box[0],
                           unroll=TILE_UNROLL)
    latent_tile(Q_TILES - 1, m_last, None)


def _attention(qt, k, vt, sink, need_ctx, later_weight=None):
    first_tile = 0 if need_ctx else CTX_Q_TILES
    n_keys = BAND + CTX_LEN
    out_cols = TOK - first_tile * Q_TILE
    in_specs = [pl.BlockSpec(memory_space=pltpu.SMEM),
                pl.BlockSpec((1, Q_DIM, TOK), lambda b: (b, 0, 0)),
                pl.BlockSpec((1, N_KV_HEADS, TOK, HEAD_DIM), lambda b: (b, 0, 0, 0)),
                pl.BlockSpec((1, KV_DIM, TOK), lambda b: (b, 0, 0))]
    out_specs = pl.BlockSpec((1, Q_DIM, out_cols), lambda b: (b, 0, 0))
    out_shape = jax.ShapeDtypeStruct((BATCH, Q_DIM, out_cols), BF16)
    args = (sink, qt, k, vt)
    if later_weight is not None:
        w_rows, w_cols = later_weight.shape
        w_spec = pl.BlockSpec((w_rows // BATCH, w_cols), lambda b: (b, 0))
        in_specs, args = in_specs + [w_spec], args + (later_weight,)
        out_specs = (out_specs, w_spec)
        out_shape = (out_shape, jax.ShapeDtypeStruct(later_weight.shape, BF16))
    return pl.pallas_call(
        functools.partial(_attn_kernel, first_tile=first_tile, side_cast=later_weight is not None),
        grid=(BATCH,),
        in_specs=in_specs,
        out_specs=out_specs,
        out_shape=out_shape,
        scratch_shapes=[pltpu.VMEM((N_KV_HEADS, n_keys, N_GROUPS * Q_TILE), F32),
                        pltpu.VMEM((3, WINDOW, N_GROUPS * Q_TILE), F32)],
        compiler_params=_params("arbitrary"),
        name="attention",
    )(*args)


def _lru_kernel(uf_ref, ub_ref, cw_ref, cb_ref, wa_ref, ba_ref, wx_ref, bx_ref, lam_ref,
                hf_ref, hb_ref, u_sc, a_sc, b_sc, y_sc, h_sc):
    step = pl.program_id(1)
    tl, pitch, halo = LRU_TL, LRU_PITCH, SUBLANES
    segment_start = (step == 0) | (step == LRU_CTX_BLKS)

    @pl.when(segment_start)
    def _zero_halo():
        u_sc[0, :, 0:halo, :] = jnp.zeros((BATCH * LRU_SLABS, halo, LANES), F32)
        u_sc[1, :, halo + tl:, :] = jnp.zeros((BATCH * LRU_SLABS, halo, LANES), F32)

    @pl.when(jnp.logical_not(segment_start))
    def _carry_halo():
        u_sc[0, :, 0:halo, :] = u_sc[0, :, tl:tl + halo, :]
        u_sc[1, :, halo + tl:, :] = u_sc[1, :, halo:2 * halo, :]

    @pl.when(step == 0)
    def _sequence_start():
        h_sc[...] = jnp.zeros_like(h_sc)

    for d, u_ref in ((0, uf_ref), (1, ub_ref)):
        for bi in range(BATCH):
            for s in range(LRU_SLABS):
                u_sc[d, bi * LRU_SLABS + s, halo:halo + tl, :] = u_ref[bi, :, s * LANES:(s + 1) * LANES]

    for d in range(2):
        nl = -lam_ref[d]
        half_decay = (0.5 * LRU_C) * (jnp.maximum(nl, 0.0) + jnp.log1p(jnp.exp(-jnp.abs(nl))))
        for bi in range(BATCH):
            cols = []
            for s in range(LRU_SLABS):
                lanes = slice(s * LANES, (s + 1) * LANES)
                half_x = 0.5 * cb_ref[d][:, lanes]
                for kk in range(LRU_CONV):
                    off = halo + (kk - (LRU_CONV - 1) if d == 0 else (LRU_CONV - 1) - kk)
                    rows = pl.ds(off, tl, stride=1)
                    tap = 0.5 * cw_ref[d][kk:kk + 1, lanes]
                    half_x = half_x + tap * u_sc[d, bi * LRU_SLABS + s, rows, :]
                cols.append(half_x)
            half_x = jnp.concatenate(cols, axis=1)
            xb = half_x.astype(BF16)
            tr = jnp.tanh(jnp.dot(xb, wa_ref[d, 0], preferred_element_type=F32) + ba_ref[d])
            ti = jnp.tanh(jnp.dot(xb, wx_ref[d, 0], preferred_element_type=F32) + bx_ref[d])
            neg_log_a = tr * half_decay + half_decay
            ix = ti * half_x + half_x
            a = jnp.exp(-neg_log_a)
            var = jnp.tanh(neg_log_a) * (a * a + 1.0)
            b = (var * lax.rsqrt(jnp.maximum(var, F32_TINY))) * ix
            for s in range(LRU_SLABS):
                lanes = slice(s * LANES, (s + 1) * LANES)
                rows = pl.ds(bi * pitch, tl, stride=1)
                a_sc[d, s, rows, :] = a[:, lanes]
                b_sc[d, s, rows, :] = b[:, lanes]

    def scan_step(t, carry):
        new = []
        for d in range(2):
            tt = t if d == 0 else tl - 1 - t
            for s in range(LRU_SLABS):
                rows = pl.ds(tt, BATCH, stride=pitch)
                h = a_sc[d, s, rows, :] * carry[d * LRU_SLABS + s] + b_sc[d, s, rows, :]
                y_sc[d, s, rows, :] = h
                new.append(h)
        return tuple(new)

    init = tuple(h_sc[d, s] for d in range(2) for s in range(LRU_SLABS))
    final = lax.fori_loop(0, tl, scan_step, init, unroll=8)
    for d in range(2):
        for s in range(LRU_SLABS):
            h_sc[d, s] = final[d * LRU_SLABS + s]

    for d, o_ref in ((0, hf_ref), (1, hb_ref)):
        for bi in range(BATCH):
            for s in range(LRU_SLABS):
                o_ref[bi, :, s * LANES:(s + 1) * LANES] = (
                    y_sc[d, s, pl.ds(bi * pitch, tl, stride=1), :].astype(o_ref.dtype))


def _block_diag_chunks(w):
    per = MXU_DIM // LRU_BLOCK
    rows = w.reshape(2, LRU_BLOCKS // per, MXU_DIM, LRU_BLOCK)
    tiled = jnp.concatenate([rows] * per, axis=-1)
    blocks = np.arange(MXU_DIM) // LRU_BLOCK
    on_diagonal = jnp.asarray(blocks[:, None] == blocks[None, :])
    return jnp.where(on_diagonal, tiled, 0.0).astype(BF16)


def _lru_scan(u, conv_w, conv_b, gate_a_w, gate_a_b, gate_x_w, gate_x_b, lam):
    def fwd_blk(w, s):
        return (0, s, w)

    def bwd_blk(w, s):
        blk = jnp.where(s < LRU_CTX_BLKS, LRU_CTX_BLKS - 1 - s, LRU_BLKS - 1 - (s - LRU_CTX_BLKS))
        return (0, blk, w)

    blk = (BATCH, LRU_TL, LRU_WC)
    vec = pl.BlockSpec((2, 1, LRU_WC), lambda w, s: (0, 0, w))
    gate = pl.BlockSpec((2, LRU_WC // MXU_DIM, MXU_DIM, MXU_DIM), lambda w, s: (0, w, 0, 0))
    width = u.shape[-1]
    out = jax.ShapeDtypeStruct((BATCH, TOK, width), BF16)
    scan_buf = pltpu.VMEM((2, LRU_SLABS, BATCH * LRU_PITCH, LANES), F32)
    return pl.pallas_call(
        _lru_kernel,
        grid=(width // LRU_WC, LRU_BLKS),
        in_specs=[pl.BlockSpec(blk, fwd_blk), pl.BlockSpec(blk, bwd_blk),
                  pl.BlockSpec((2, LRU_CONV, LRU_WC), lambda w, s: (0, 0, w)), vec,
                  gate, vec, gate, vec, vec],
        out_specs=[pl.BlockSpec(blk, fwd_blk), pl.BlockSpec(blk, bwd_blk)],
        out_shape=[out, out],
        scratch_shapes=[pltpu.VMEM((2, BATCH * LRU_SLABS, LRU_TL + 2 * SUBLANES, LANES), F32),
                        scan_buf, scan_buf, scan_buf,
                        pltpu.VMEM((2, LRU_SLABS, BATCH, LANES), F32)],
        compiler_params=_params("parallel", "arbitrary"),
        name="lru_scan",
    )(u, u, conv_w, conv_b.reshape(2, 1, width),
      _block_diag_chunks(gate_a_w), 0.5 * gate_a_b.reshape(2, 1, width),
      _block_diag_chunks(gate_x_w), 0.5 * gate_x_b.reshape(2, 1, width),
      lam.reshape(2, 1, width))


def _residual_rows(x_refs, mod_ref, acts, wb_sc, o_ref, post=None):
    y = jnp.dot(jnp.concatenate(acts, axis=0).astype(BF16), wb_sc[...], preferred_element_type=F32)
    for bb in range(len(acts)):
        out = _stream_tile(x_refs, bb) + mod_ref[bb, 0][2:3, :] * _tile_rows(y, bb)
        o_ref[bb] = out if post is None else post(out)


def _out_attn_final_kernel(ot_ref, g_ref, w_ref, x_ref, mod_ref, fg_ref, o_ref, wb_sc):
    _cast_once(w_ref, wb_sc)
    acts = [ot_ref[bb].astype(F32).T * _silu(g_ref[bb].astype(F32)) for bb in range(OUT_ROW_BATCH)]

    def final_norm(x):
        return x * lax.rsqrt(jnp.mean(x * x, axis=-1, keepdims=True) + EPS) * fg_ref[...]

    _residual_rows((x_ref,), mod_ref, acts, wb_sc, o_ref, final_norm)


def _out_attn_in_next_kernel(ot_ref, g_ref, wo_ref, ctx_ref, lat_ref, mod_ref, ng_ref, nmod_ref, wi_ref,
                             xs_ref, *refs):
    o_refs, (wo_sc, wi_sc) = refs[:-2], refs[-2:]
    _cast_once(wo_ref, wo_sc)
    _cast_once(wi_ref, wi_sc)
    acts = [ot_ref[bb].astype(F32).T * _silu(g_ref[bb].astype(F32)) for bb in range(ROW_BATCH)]
    y = jnp.dot(jnp.concatenate(acts, axis=0).astype(BF16), wo_sc[...], preferred_element_type=F32)
    tiles = []
    for bb in range(ROW_BATCH):
        out = _stream_tile((ctx_ref, lat_ref), bb) + mod_ref[bb, 0][2:3, :] * _tile_rows(y, bb)
        xs_ref[bb] = out
        tiles.append(_norm_mod(out, ng_ref[...], nmod_ref[bb, 0]))
    z = jnp.dot(jnp.concatenate(tiles, axis=0).astype(BF16), wi_sc[...], preferred_element_type=F32)
    width = z.shape[-1] // len(o_refs)
    for i, o_ref in enumerate(o_refs):
        for bb in range(ROW_BATCH):
            o_ref[bb] = _tile_rows(z, bb)[:, i * width:(i + 1) * width].astype(o_ref.dtype)


def _halo_rows(main_ref, prev_ref, next_ref, bb):
    return jnp.concatenate([prev_ref[bb], main_ref[bb], next_ref[bb]], axis=0).astype(F32)


def _out_lru_conv_kernel(hf_ref, hfp_ref, hfn_ref, hb_ref, hbp_ref, hbn_ref, gl_ref, glp_ref, gln_ref,
                         x_ref, xp_ref, xn_ref, lmod_ref, lwo_ref, mod_ref, g_ref, wu_ref, wb_ref, wc_ref,
                         wg_ref, cw_ref, cb_ref, wo_ref, o_ref, p_sc, lwo_sc, wo_sc):
    j = pl.program_id(1)
    halo, cc = SUBLANES, CONV_CHUNK
    rows = ROW_TILE + 2 * halo
    _cast_once(lwo_ref, lwo_sc)
    _cast_once(wo_ref, wo_sc)
    acts = [(_halo_rows(hf_ref, hfp_ref, hfn_ref, bb) + _halo_rows(hb_ref, hbp_ref, hbn_ref, bb))
            * _silu(_halo_rows(gl_ref, glp_ref, gln_ref, bb)) for bb in range(ROW_BATCH)]
    y = jnp.dot(jnp.concatenate(acts, axis=0).astype(BF16), lwo_sc[...], preferred_element_type=F32)
    x1 = [_halo_rows(x_ref, xp_ref, xn_ref, bb) + lmod_ref[bb, 0][2:3, :] * y[bb * rows:(bb + 1) * rows]
          for bb in range(ROW_BATCH)]
    tiles = [_norm_mod(x1[bb], g_ref[...], mod_ref[bb, 0]) for bb in range(ROW_BATCH)]
    h = jnp.concatenate(tiles, axis=0).astype(BF16)
    h_main = jnp.concatenate([h[bb * rows + halo:bb * rows + halo + ROW_TILE] for bb in range(ROW_BATCH)], axis=0)
    keep_prev = jnp.where((j == 0) | (j == CTX_ROW_TILES), 0.0, 1.0)
    keep_next = jnp.where((j == CTX_ROW_TILES - 1) | (j == ROW_TILES - 1), 0.0, 1.0)
    acc = jnp.zeros((ROW_BATCH * ROW_TILE, D_MODEL), F32)
    for c in range(D_MODEL // cc):
        chans = slice(c * cc, (c + 1) * cc)
        u, cg = (jnp.dot(h, w_ref[:, chans], preferred_element_type=F32) for w_ref in (wu_ref, wc_ref))
        bg, g = (jnp.dot(h_main, w_ref[:, chans], preferred_element_type=F32) for w_ref in (wb_ref, wg_ref))
        p = cg * u
        acts = []
        for bb in range(ROW_BATCH):
            top = bb * rows
            main = slice(top + halo, top + halo + ROW_TILE)
            tile = slice(bb * ROW_TILE, (bb + 1) * ROW_TILE)
            for s in range(cc // LANES):
                lanes = slice(s * LANES, (s + 1) * LANES)
                p_sc[bb, c, s, 0:halo, :] = p[top:top + halo, lanes] * keep_prev
                p_sc[bb, c, s, halo:halo + ROW_TILE, :] = p[main, lanes]
                p_sc[bb, c, s, halo + ROW_TILE:rows, :] = p[top + halo + ROW_TILE:top + rows, lanes] * keep_next
            before = jnp.concatenate([p_sc[bb, c, s, pl.ds(halo - 1, ROW_TILE, stride=1), :]
                                      for s in range(cc // LANES)], axis=1)
            after = jnp.concatenate([p_sc[bb, c, s, pl.ds(halo + 1, ROW_TILE, stride=1), :]
                                     for s in range(cc // LANES)], axis=1)
            conv = (cw_ref[0:1, chans] * before + cw_ref[1:2, chans] * p[main] + cw_ref[2:3, chans] * after
                    + cb_ref[:, chans])
            acts.append(bg[tile] * conv * _silu(g[tile]))
        acc = acc + jnp.dot(jnp.concatenate(acts, axis=0).astype(BF16), wo_sc[chans, :],
                            preferred_element_type=F32)
    for bb in range(ROW_BATCH):
        o_ref[bb] = x1[bb][halo:halo + ROW_TILE] + mod_ref[bb, 0][2:3, :] * _tile_rows(acc, bb)


_W_OUT_SPEC = _const_spec((D_MODEL, D_MODEL))


def _out_attn_final(ot, gate, w_out, xs, mod, final_g):
    off, rb = CTX_ROW_TILES, OUT_ROW_BATCH
    return pl.pallas_call(
        _out_attn_final_kernel,
        grid=(BATCH // rb, ROW_TILES - off),
        in_specs=[pl.BlockSpec((rb, Q_DIM, ROW_TILE), lambda b, j: (b, 0, j)), _row_spec(Q_DIM, off, rb),
                  _W_OUT_SPEC, _row_spec(D_MODEL, off, rb), _mod_spec(off, rb), _const_spec((1, D_MODEL))],
        out_specs=_row_spec(D_MODEL, rb=rb),
        out_shape=jax.ShapeDtypeStruct((BATCH, SEQ, D_MODEL), F32),
        scratch_shapes=[pltpu.VMEM((D_MODEL, D_MODEL), BF16)],
        compiler_params=_params("arbitrary", "arbitrary"),
        name="out_attn_final",
    )(ot, gate, w_out, xs, mod, final_g.reshape(1, D_MODEL))


def _out_attn_in_next(ot, gate, w_out, ctx, x, mod, next_norm_g, next_mod, next_w_in, dtypes):
    n = next_w_in.shape[1]
    width = n // len(dtypes)
    return pl.pallas_call(
        _out_attn_in_next_kernel,
        grid=_ROW_GRID,
        in_specs=[pl.BlockSpec((ROW_BATCH, Q_DIM, ROW_TILE), lambda b, j: (b, 0, j)), _row_spec(Q_DIM),
                  _W_OUT_SPEC] + _stream_specs(True)
                 + [_mod_spec(), _const_spec((1, D_MODEL)), _mod_spec(), _const_spec((D_MODEL, n))],
        out_specs=[_row_spec(D_MODEL)] + [_row_spec(width)] * len(dtypes),
        out_shape=[jax.ShapeDtypeStruct((BATCH, TOK, D_MODEL), F32)]
                  + [jax.ShapeDtypeStruct((BATCH, TOK, width), dt) for dt in dtypes],
        scratch_shapes=[pltpu.VMEM((D_MODEL, D_MODEL), BF16), pltpu.VMEM((D_MODEL, n), BF16)],
        compiler_params=_params("arbitrary", "arbitrary"),
        name="out_attn_in_lru",
    )(ot, gate, w_out, ctx, x, mod, next_norm_g.reshape(1, D_MODEL), next_mod, next_w_in)


def _out_lru_conv(hf, hb, gate, lru_w_out, xs, lru_mod, mod, norm_g, w_in, conv_w, conv_b, w_out):
    per = ROW_TILE // SUBLANES
    n8 = TOK // SUBLANES
    halo = lambda idx: pl.BlockSpec((ROW_BATCH, SUBLANES, D_MODEL), idx)
    prev = halo(lambda b, j: (b, jnp.maximum(j * per - 1, 0), 0))
    nxt = halo(lambda b, j: (b, jnp.minimum((j + 1) * per, n8 - 1), 0))
    with_halo = [_row_spec(D_MODEL), prev, nxt]
    w_blocks = [_const_spec((D_MODEL, D_MODEL), functools.partial(lambda i, b, j: (0, i), i)) for i in range(4)]
    return pl.pallas_call(
        _out_lru_conv_kernel,
        grid=_ROW_GRID,
        in_specs=with_halo * 4 + [_mod_spec(), _W_OUT_SPEC, _mod_spec(), _const_spec((1, D_MODEL))] + w_blocks
                 + [_const_spec((CONV_K, D_MODEL)), _const_spec((1, D_MODEL)), _W_OUT_SPEC],
        out_specs=_row_spec(D_MODEL),
        out_shape=jax.ShapeDtypeStruct((BATCH, TOK, D_MODEL), F32),
        scratch_shapes=[pltpu.VMEM((ROW_BATCH, D_MODEL // CONV_CHUNK, CONV_CHUNK // LANES,
                                    ROW_TILE + 2 * SUBLANES, LANES), F32),
                        pltpu.VMEM((D_MODEL, D_MODEL), BF16), pltpu.VMEM((D_MODEL, D_MODEL), BF16)],
        compiler_params=_params("arbitrary", "arbitrary"),
        name="out_lru_conv",
    )(hf, hf, hf, hb, hb, hb, gate, gate, gate, xs, xs, xs, lru_mod, lru_w_out, mod,
      norm_g.reshape(1, D_MODEL), w_in, w_in, w_in, w_in, conv_w, conv_b.reshape(1, D_MODEL), w_out)


def _rope_tables():
    quarter = HEAD_DIM // 4
    pos = np.arange(SEQ)
    half = HEAD_DIM // 2
    inv = (1.0 / (ROPE_BASE ** (np.arange(0, half, 2, dtype=np.float32) / half))).astype(np.float32)
    zero = np.zeros((SEQ, quarter), np.float32)
    parts_c, parts_a, parts_b = [], [], []
    for axis_pos in ((pos // GRID_W).astype(np.float32), (pos % GRID_W).astype(np.float32)):
        ang = axis_pos[:, None] * inv
        c, s = np.cos(ang), np.sin(ang)
        parts_c += [c, c]
        parts_a += [-s, zero]
        parts_b += [zero, s]

    def table(parts, ctx_value):
        head = np.concatenate(parts, axis=1)
        head = np.concatenate([np.full((CTX_LEN, HEAD_DIM), ctx_value, np.float32), head], axis=0)
        return jnp.asarray(np.tile(head, (1, LANES // HEAD_DIM)).astype(np.float32))

    return table(parts_c, 1.0), table(parts_a, 0.0), table(parts_b, 0.0)


def kernel(x, c, ctx, c_ctx, l0_norm_g, l0_mod_w, l0_mod_b, l0_w_in, l0_w_out, l0_sink, l1_norm_g, l1_mod_w, l1_mod_b, l1_w_in, l1_w_out, l1_conv_w, l1_conv_b, l1_gate_a_w, l1_gate_a_b, l1_gate_x_w, l1_gate_x_b, l1_lambda, l2_norm_g, l2_mod_w, l2_mod_b, l2_w_in, l2_w_out, l2_conv_w, l2_conv_b, l3_norm_g, l3_mod_w, l3_mod_b, l3_w_in, l3_w_out, l3_sink, final_norm_g):
    mods = _modulation(c, c_ctx, (l0_mod_w, l1_mod_w, l2_mod_w, l3_mod_w),
                       (l0_mod_b, l1_mod_b, l2_mod_b, l3_mod_b))
    rope_tabs = _rope_tables()

    qt, k, vt, gate = _in_attn((ctx, x), mods[0], l0_norm_g, l0_w_in, rope_tabs, ctx_queries=True)
    ot, l2_w_in_bf16 = _attention(qt, k, vt, l0_sink, need_ctx=True, later_weight=l2_w_in)
    xs, u, gate = _out_attn_in_next(ot, gate, l0_w_out, ctx, x, mods[0], l1_norm_g, mods[1], l1_w_in,
                                    (F32, BF16))

    hf, hb = _lru_scan(u, l1_conv_w, l1_conv_b, l1_gate_a_w, l1_gate_a_b, l1_gate_x_w, l1_gate_x_b,
                       l1_lambda)
    xs = _out_lru_conv(hf, hb, gate, l1_w_out, xs, mods[1], mods[2], l2_norm_g, l2_w_in_bf16, l2_conv_w,
                       l2_conv_b, l2_w_out)

    qt, k, vt, gate = _in_attn((xs,), mods[3], l3_norm_g, l3_w_in, rope_tabs, ctx_queries=False)
    ot = _attention(qt, k, vt, l3_sink, need_ctx=False)
    return _out_attn_final(ot, gate, l3_w_out, xs, mods[3], final_norm_g)
```

```python
import functools
import math

import jax
import jax.numpy as jnp
import numpy as np
from jax import lax
from jax.experimental import pallas as pl
from jax.experimental.pallas import tpu as pltpu

D_MODEL = 1024
BATCH = 8
SEQ = 2048
CTX_LEN = 256
TOK = CTX_LEN + SEQ
GRID_W = 64
EPS = 1e-6
NEG_INF = -1e30
N_HEADS = 16
N_KV_HEADS = 4
N_GROUPS = N_HEADS // N_KV_HEADS
HEAD_DIM = 64
Q_DIM = N_HEADS * HEAD_DIM
KV_DIM = N_KV_HEADS * HEAD_DIM
WINDOW = 128
ROPE_BASE = 10000.0
LRU_BLOCKS = 16
LRU_BLOCK = 64
LRU_C = 8.0
LRU_CONV = 4
CONV_K = 3
MOD_PARTS = 3
MOD_K_ROWS = 256
MOD_BUFFERS = 3

LANES = 128
SUBLANES = 8
MXU_DIM = 256
VMEM_LIMIT_BYTES = 56 * 1024 * 1024

ROW_TILE = 256
ROW_BATCH = 2
OUT_ROW_BATCH = 4
CTX_ROW_TILES = CTX_LEN // ROW_TILE
ROW_TILES = TOK // ROW_TILE
Q_TILE = WINDOW
CTX_Q_TILES = CTX_LEN // Q_TILE
Q_TILES = TOK // Q_TILE
BAND = 3 * WINDOW
MAX_ROWS = SUBLANES
TILE_UNROLL = 5
CONV_CHUNK = MXU_DIM

LRU_TL = 256
LRU_WC = MXU_DIM
LRU_PITCH = LRU_TL + 4
LRU_SLABS = LRU_WC // LANES
LRU_CTX_BLKS = CTX_LEN // LRU_TL
LRU_BLKS = TOK // LRU_TL

F32 = jnp.float32
BF16 = jnp.bfloat16
F32_TINY = float(jnp.finfo(jnp.float32).tiny)
LOG2E = math.log2(math.e)


def _params(*sem):
    return pltpu.CompilerParams(dimension_semantics=sem, vmem_limit_bytes=VMEM_LIMIT_BYTES)


def _sigmoid(x):
    return 0.5 * jnp.tanh(0.5 * x) + 0.5


def _silu(x):
    return x * _sigmoid(x)


def _mod_kernel(c_ref, b_ref, *refs):
    n = len(refs) - 3
    w_hbm, (o_ref, buf, sem) = refs[:n], refs[n:]
    k_blocks = D_MODEL // MOD_K_ROWS
    chunks = [(l, kb) for l in range(n) for kb in range(k_blocks)]

    def copy(i):
        l, kb = chunks[i]
        slot = i % MOD_BUFFERS
        return pltpu.make_async_copy(w_hbm[l].at[pl.ds(kb * MOD_K_ROWS, MOD_K_ROWS), :], buf.at[slot], sem.at[slot])

    for i in range(MOD_BUFFERS):
        copy(i).start()
    s = _silu(c_ref[...])
    s_hi = s.astype(BF16)
    s_lo = (s - s_hi.astype(F32)).astype(BF16)
    acc = None
    for i, (l, kb) in enumerate(chunks):
        copy(i).wait()
        w = buf[i % MOD_BUFFERS]
        rows = slice(kb * MOD_K_ROWS, (kb + 1) * MOD_K_ROWS)
        w_hi = w.astype(BF16)
        w_lo = (w - w_hi.astype(F32)).astype(BF16)
        part = jnp.dot(s_hi[:, rows], w_hi, preferred_element_type=F32)
        part += jnp.dot(s_hi[:, rows], w_lo, preferred_element_type=F32)
        part += jnp.dot(s_lo[:, rows], w_hi, preferred_element_type=F32)
        acc = part if kb == 0 else acc + part
        if i + MOD_BUFFERS < len(chunks):
            copy(i + MOD_BUFFERS).start()
        if kb == k_blocks - 1:
            m = acc + b_ref[l]
            o_ref[l, :, 0, :] = jnp.broadcast_to(m[BATCH:BATCH + 1], (BATCH, MOD_PARTS * D_MODEL))
            o_ref[l, :, 1, :] = m[:BATCH]


def _modulation(c, c_ctx, mod_ws, mod_bs):
    n = len(mod_ws)
    cc = jnp.concatenate([c, c_ctx[None], jnp.zeros((2 * SUBLANES - BATCH - 1, D_MODEL), F32)], axis=0)
    width = MOD_PARTS * D_MODEL
    in_vmem = pl.BlockSpec(memory_space=pltpu.VMEM)
    m = pl.pallas_call(
        _mod_kernel,
        in_specs=[in_vmem, in_vmem] + [pl.BlockSpec(memory_space=pl.ANY)] * n,
        out_specs=pl.BlockSpec(memory_space=pltpu.VMEM),
        out_shape=jax.ShapeDtypeStruct((n, BATCH, 2, width), F32),
        scratch_shapes=[pltpu.VMEM((MOD_BUFFERS, MOD_K_ROWS, width), F32),
                        pltpu.SemaphoreType.DMA((MOD_BUFFERS,))],
        compiler_params=pltpu.CompilerParams(vmem_limit_bytes=VMEM_LIMIT_BYTES),
        name="modulation",
    )(cc, jnp.stack(mod_bs).reshape(n, 1, width), *mod_ws)
    return m.reshape(n, BATCH, 2, MOD_PARTS, D_MODEL)


def _mod_spec(tile_offset=0, rb=ROW_BATCH):
    def idx(b, j):
        return (b, jnp.where(j + tile_offset < CTX_ROW_TILES, 0, 1), 0, 0)
    return pl.BlockSpec((rb, 1, MOD_PARTS, D_MODEL), idx)


def _row_spec(width, tile_offset=0, rb=ROW_BATCH):
    return pl.BlockSpec((rb, ROW_TILE, width), lambda b, j: (b, j + tile_offset, 0))


def _const_spec(shape, index=None):
    index = index or (lambda b, j: (0,) * len(shape))
    return pl.BlockSpec(shape, index, pipeline_mode=pl.Buffered(1))


def _stream_specs(split, rb=ROW_BATCH):
    if not split:
        return [_row_spec(D_MODEL, rb=rb)]
    blk = (rb, ROW_TILE, D_MODEL)
    return [pl.BlockSpec(blk, lambda b, j: (b, 0, 0)),
            pl.BlockSpec(blk, lambda b, j: (b, jnp.maximum(j - CTX_ROW_TILES, 0), 0))]


def _stream_tile(x_refs, bb):
    if len(x_refs) == 1:
        return x_refs[0][bb]
    ctx_ref, lat_ref = x_refs
    return jnp.where(pl.program_id(1) < CTX_ROW_TILES, ctx_ref[bb], lat_ref[bb])


def _cast_once(w_ref, wb_sc):
    @pl.when((pl.program_id(0) == 0) & (pl.program_id(1) == 0))
    def _cast():
        wb_sc[...] = w_ref[...].astype(BF16)


_ROW_GRID = (BATCH // ROW_BATCH, ROW_TILES)


def _norm_mod(x, g, mod):
    y = x * lax.rsqrt(jnp.mean(x * x, axis=-1, keepdims=True) + EPS) * g
    return y * (1.0 + mod[1:2, :]) + mod[0:1, :]


def _norm_mod_rows(x_refs, g_ref, mod_ref):
    tiles = [_norm_mod(_stream_tile(x_refs, bb), g_ref[...], mod_ref[bb, 0]) for bb in range(ROW_BATCH)]
    return jnp.concatenate(tiles, axis=0).astype(BF16)


def _tile_rows(t, bb):
    return t[bb * ROW_TILE:(bb + 1) * ROW_TILE]


def _in_attn_kernel(*refs, n_x, ctx_queries):
    x_refs = refs[:n_x]
    mod_ref, g_ref, w_ref, rc_ref, ra_ref, rb_ref, qt_ref, k_ref, vt_ref, gate_ref, wb_sc = refs[n_x:]
    _cast_once(w_ref, wb_sc)
    rc, ra, rb = rc_ref[...], ra_ref[...], rb_ref[...]

    def rope(t):
        reps = t.shape[-1] // LANES
        n = t.shape[-1]
        return (t * jnp.tile(rc, (1, reps))
                + pltpu.roll(t, n - HEAD_DIM // 4, 1) * jnp.tile(ra, (1, reps))
                + pltpu.roll(t, HEAD_DIM // 4, 1) * jnp.tile(rb, (1, reps)))

    def keys_values(z_kv, bb):
        k = rope(z_kv[:, :KV_DIM]).astype(BF16)
        for kh in range(N_KV_HEADS):
            k_ref[bb, kh] = k[:, kh * HEAD_DIM:(kh + 1) * HEAD_DIM]
        vt_ref[bb] = z_kv[:, KV_DIM:].T.astype(BF16)

    def all_columns():
        z_all = jnp.dot(_norm_mod_rows(x_refs, g_ref, mod_ref), wb_sc[...], preferred_element_type=F32)
        for bb in range(ROW_BATCH):
            z = _tile_rows(z_all, bb)
            qt_ref[bb] = (rope(z[:, :Q_DIM]) * (LOG2E * HEAD_DIM ** -0.5)).T.astype(BF16)
            keys_values(z[:, Q_DIM:Q_DIM + 2 * KV_DIM], bb)
            gate_ref[bb] = z[:, Q_DIM + 2 * KV_DIM:].astype(gate_ref.dtype)

    def keys_values_only():
        z_kv = jnp.dot(_norm_mod_rows(x_refs, g_ref, mod_ref), wb_sc[:, Q_DIM:Q_DIM + 2 * KV_DIM],
                       preferred_element_type=F32)
        for bb in range(ROW_BATCH):
            keys_values(_tile_rows(z_kv, bb), bb)
        qt_ref[...] = jnp.zeros_like(qt_ref)
        gate_ref[...] = jnp.zeros_like(gate_ref)

    if ctx_queries:
        all_columns()
    else:
        pl.when(pl.program_id(1) >= CTX_ROW_TILES)(all_columns)
        pl.when(pl.program_id(1) < CTX_ROW_TILES)(keys_values_only)


def _in_attn(streams, mod, norm_g, w_in, rope_tabs, ctx_queries):
    n = w_in.shape[1]
    col = lambda height: pl.BlockSpec((ROW_BATCH, height, ROW_TILE), lambda b, j: (b, 0, j))
    tab = pl.BlockSpec((ROW_TILE, LANES), lambda b, j: (j, 0))
    return pl.pallas_call(
        functools.partial(_in_attn_kernel, n_x=len(streams), ctx_queries=ctx_queries),
        grid=_ROW_GRID,
        in_specs=_stream_specs(len(streams) > 1)
                 + [_mod_spec(), _const_spec((1, D_MODEL)), _const_spec((D_MODEL, n)), tab, tab, tab],
        out_specs=[col(Q_DIM),
                   pl.BlockSpec((ROW_BATCH, N_KV_HEADS, ROW_TILE, HEAD_DIM), lambda b, j: (b, 0, j, 0)),
                   col(KV_DIM), _row_spec(Q_DIM)],
        out_shape=[jax.ShapeDtypeStruct((BATCH, Q_DIM, TOK), BF16),
                   jax.ShapeDtypeStruct((BATCH, N_KV_HEADS, TOK, HEAD_DIM), BF16),
                   jax.ShapeDtypeStruct((BATCH, KV_DIM, TOK), BF16),
                   jax.ShapeDtypeStruct((BATCH, TOK, Q_DIM), BF16)],
        scratch_shapes=[pltpu.VMEM((D_MODEL, n), BF16)],
        compiler_params=_params("arbitrary", "arbitrary"),
        name="in_attn",
    )(*streams, mod, norm_g.reshape(1, D_MODEL), w_in, *rope_tabs)


def _attn_kernel(sink_ref, qt_ref, k_ref, vt_ref, *refs, first_tile, side_cast):
    lanes = N_GROUPS * Q_TILE
    if side_cast:
        w_ref, ot_ref, wb_ref, s_sc, bias_sc = refs
        wb_ref[...] = w_ref[...].astype(BF16)
    else:
        ot_ref, s_sc, bias_sc = refs

    key = lax.broadcasted_iota(jnp.int32, (WINDOW, lanes), 0)
    query = lax.broadcasted_iota(jnp.int32, (WINDOW, lanes), 1) & (Q_TILE - 1)
    bias_sc[0] = jnp.where(key >= query, 0.0, NEG_INF)
    bias_sc[1] = jnp.where(key <= query, 0.0, NEG_INF)
    bias_sc[2] = jnp.full((WINDOW, lanes), NEG_INF, F32)

    def aligned(start):
        return start if isinstance(start, int) else pl.multiple_of(start, WINDOW)

    def heads_of(kh):
        return [kh * N_GROUPS + g for g in range(N_GROUPS)]

    def sink_row(kh):
        return jnp.concatenate([jnp.full((1, Q_TILE), sink_ref[h] * LOG2E, F32) for h in heads_of(kh)], axis=1)

    def score_steps(j, kh, chunks, m_box):
        cols = pl.ds(aligned(j * Q_TILE), Q_TILE)
        qt = jnp.concatenate([qt_ref[0, h * HEAD_DIM:(h + 1) * HEAD_DIM, cols] for h in heads_of(kh)], axis=1)
        mx = jnp.broadcast_to(sink_row(kh), (MAX_ROWS, lanes))
        keys = jnp.concatenate([k_ref[0, kh, pl.ds(ks, WINDOW), :] for ks, _ in chunks], axis=0)
        s_all = jnp.dot(keys, qt, preferred_element_type=F32)
        for c, (ks, bias) in enumerate(chunks):
            s = s_all[c * WINDOW:(c + 1) * WINDOW]
            if bias is not None:
                s = s + bias_sc[bias]
            s_sc[kh, c * WINDOW:(c + 1) * WINDOW, :] = s
            mx = jnp.maximum(mx, s.reshape(WINDOW // MAX_ROWS, MAX_ROWS, lanes).max(axis=0))
            yield
        m_box.append(mx.max(axis=0, keepdims=True))

    def value_steps(j, kh, chunks, m_box):
        m = m_box[0]
        acc = jnp.zeros((HEAD_DIM, lanes), F32)
        denom = jnp.zeros((SUBLANES, lanes), F32)
        group = []
        for c, (ks, _) in enumerate(chunks):
            p = jnp.exp2(s_sc[kh, c * WINDOW:(c + 1) * WINDOW, :] - m)
            denom = denom + p.reshape(WINDOW // SUBLANES, SUBLANES, lanes).sum(axis=0)
            group.append((vt_ref[0, kh * HEAD_DIM:(kh + 1) * HEAD_DIM, pl.ds(ks, WINDOW)], p.astype(BF16)))
            if len(group) == MXU_DIM // WINDOW or c == len(chunks) - 1:
                vt = jnp.concatenate([g[0] for g in group], axis=1)
                pp = jnp.concatenate([g[1] for g in group], axis=0)
                acc = acc + jnp.dot(vt, pp, preferred_element_type=F32)
                group = []
            yield
        denom = denom.sum(axis=0, keepdims=True) + jnp.exp2(sink_row(kh) - m)
        o = acc / denom
        cols = pl.ds(aligned((j - first_tile) * Q_TILE), Q_TILE)
        for g, h in enumerate(heads_of(kh)):
            ot_ref[0, h * HEAD_DIM:(h + 1) * HEAD_DIM, cols] = o[:, g * Q_TILE:(g + 1) * Q_TILE].astype(BF16)

    def interleave(*stages):
        stages = list(stages)
        while stages:
            stages = [st for st in stages if next(st, StopIteration) is not StopIteration]

    ctx_chunks = [(c * WINDOW, None) for c in range(CTX_LEN // WINDOW)]

    def latent_chunks(j):
        own = j * Q_TILE
        if isinstance(j, int):
            prev_bias = 2 if j == CTX_Q_TILES else 0
            next_bias = 2 if j == Q_TILES - 1 else 1
            next_start = min(own + WINDOW, TOK - WINDOW)
        else:
            prev_bias = jnp.where(j == CTX_Q_TILES, 2, 0)
            next_bias = jnp.where(j == Q_TILES - 1, 2, 1)
            next_start = jnp.minimum(own + WINDOW, TOK - WINDOW)
        return [(aligned(own - WINDOW), prev_bias), (aligned(own), None),
                (aligned(next_start), next_bias)] + ctx_chunks

    if first_tile == 0:
        units = [(j, kh, ctx_chunks) for j in range(CTX_Q_TILES) for kh in range(N_KV_HEADS)]
        boxes = [[] for _ in units]
        interleave(score_steps(*units[0], boxes[0]))
        for i, unit in enumerate(units):
            stages = [score_steps(*units[i + 1], boxes[i + 1])] if i + 1 < len(units) else []
            interleave(*stages, value_steps(*unit, boxes[i]))

    def latent_tile(j, m_head0, next_j):
        chunks = latent_chunks(j)
        box = [m_head0]
        for kh in range(N_KV_HEADS):
            nxt = []
            stages = []
            if kh + 1 < N_KV_HEADS:
                stages.append(score_steps(j, kh + 1, chunks, nxt))
            elif next_j is not None:
                stages.append(score_steps(next_j, 0, latent_chunks(next_j), nxt))
            interleave(*stages, value_steps(j, kh, chunks, box))
            box = nxt
        return box[0] if box else None

    first_box = []
    interleave(score_steps(CTX_Q_TILES, 0, latent_chunks(CTX_Q_TILES), first_box))
    m_last = lax.fori_loop(CTX_Q_TILES, Q_TILES - 1, lambda j, m: latent_tile(j, m, j + 1), first_box[0],
                           unroll=TILE_UNROLL)
    latent_tile(Q_TILES - 1, m_last, None)


def _attention(qt, k, vt, sink, need_ctx, later_weight=None):
    first_tile = 0 if need_ctx else CTX_Q_TILES
    n_keys = BAND + CTX_LEN
    out_cols = TOK - first_tile * Q_TILE
    in_specs = [pl.BlockSpec(memory_space=pltpu.SMEM),
                pl.BlockSpec((1, Q_DIM, TOK), lambda b: (b, 0, 0)),
                pl.BlockSpec((1, N_KV_HEADS, TOK, HEAD_DIM), lambda b: (b, 0, 0, 0)),
                pl.BlockSpec((1, KV_DIM, TOK), lambda b: (b, 0, 0))]
    out_specs = pl.BlockSpec((1, Q_DIM, out_cols), lambda b: (b, 0, 0))
    out_shape = jax.ShapeDtypeStruct((BATCH, Q_DIM, out_cols), BF16)
    args = (sink, qt, k, vt)
    if later_weight is not None:
        w_rows, w_cols = later_weight.shape
        w_spec = pl.BlockSpec((w_rows // BATCH, w_cols), lambda b: (b, 0))
        in_specs, args = in_specs + [w_spec], args + (later_weight,)
        out_specs = (out_specs, w_spec)
        out_shape = (out_shape, jax.ShapeDtypeStruct(later_weight.shape, BF16))
    return pl.pallas_call(
        functools.partial(_attn_kernel, first_tile=first_tile, side_cast=later_weight is not None),
        grid=(BATCH,),
        in_specs=in_specs,
        out_specs=out_specs,
        out_shape=out_shape,
        scratch_shapes=[pltpu.VMEM((N_KV_HEADS, n_keys, N_GROUPS * Q_TILE), F32),
                        pltpu.VMEM((3, WINDOW, N_GROUPS * Q_TILE), F32)],
        compiler_params=_params("arbitrary"),
        name="attention",
    )(*args)


def _lru_kernel(uf_ref, ub_ref, cw_ref, cb_ref, wa_ref, ba_ref, wx_ref, bx_ref, lam_ref,
                hf_ref, hb_ref, u_sc, a_sc, b_sc, y_sc, h_sc):
    step = pl.program_id(1)
    tl, pitch, halo = LRU_TL, LRU_PITCH, SUBLANES
    segment_start = (step == 0) | (step == LRU_CTX_BLKS)

    @pl.when(segment_start)
    def _zero_halo():
        u_sc[0, :, 0:halo, :] = jnp.zeros((BATCH * LRU_SLABS, halo, LANES), F32)
        u_sc[1, :, halo + tl:, :] = jnp.zeros((BATCH * LRU_SLABS, halo, LANES), F32)

    @pl.when(jnp.logical_not(segment_start))
    def _carry_halo():
        u_sc[0, :, 0:halo, :] = u_sc[0, :, tl:tl + halo, :]
        u_sc[1, :, halo + tl:, :] = u_sc[1, :, halo:2 * halo, :]

    @pl.when(step == 0)
    def _sequence_start():
        h_sc[...] = jnp.zeros_like(h_sc)

    for d, u_ref in ((0, uf_ref), (1, ub_ref)):
        for bi in range(BATCH):
            for s in range(LRU_SLABS):
                u_sc[d, bi * LRU_SLABS + s, halo:halo + tl, :] = u_ref[bi, :, s * LANES:(s + 1) * LANES]

    for d in range(2):
        nl = -lam_ref[d]
        half_decay = (0.5 * LRU_C) * (jnp.maximum(nl, 0.0) + jnp.log1p(jnp.exp(-jnp.abs(nl))))
        for bi in range(BATCH):
            cols = []
            for s in range(LRU_SLABS):
                lanes = slice(s * LANES, (s + 1) * LANES)
                half_x = 0.5 * cb_ref[d][:, lanes]
                for kk in range(LRU_CONV):
                    off = halo + (kk - (LRU_CONV - 1) if d == 0 else (LRU_CONV - 1) - kk)
                    rows = pl.ds(off, tl, stride=1)
                    tap = 0.5 * cw_ref[d][kk:kk + 1, lanes]
                    half_x = half_x + tap * u_sc[d, bi * LRU_SLABS + s, rows, :]
                cols.append(half_x)
            half_x = jnp.concatenate(cols, axis=1)
            xb = half_x.astype(BF16)
            tr = jnp.tanh(jnp.dot(xb, wa_ref[d, 0], preferred_element_type=F32) + ba_ref[d])
            ti = jnp.tanh(jnp.dot(xb, wx_ref[d, 0], preferred_element_type=F32) + bx_ref[d])
            neg_log_a = tr * half_decay + half_decay
            ix = ti * half_x + half_x
            a = jnp.exp(-neg_log_a)
            var = jnp.tanh(neg_log_a) * (a * a + 1.0)
            b = (var * lax.rsqrt(jnp.maximum(var, F32_TINY))) * ix
            for s in range(LRU_SLABS):
                lanes = slice(s * LANES, (s + 1) * LANES)
                rows = pl.ds(bi * pitch, tl, stride=1)
                a_sc[d, s, rows, :] = a[:, lanes]
                b_sc[d, s, rows, :] = b[:, lanes]

    def scan_step(t, carry):
        new = []
        for d in range(2):
            tt = t if d == 0 else tl - 1 - t
            for s in range(LRU_SLABS):
                rows = pl.ds(tt, BATCH, stride=pitch)
                h = a_sc[d, s, rows, :] * carry[d * LRU_SLABS + s] + b_sc[d, s, rows, :]
                y_sc[d, s, rows, :] = h
                new.append(h)
        return tuple(new)

    init = tuple(h_sc[d, s] for d in range(2) for s in range(LRU_SLABS))
    final = lax.fori_loop(0, tl, scan_step, init, unroll=8)
    for d in range(2):
        for s in range(LRU_SLABS):
            h_sc[d, s] = final[d * LRU_SLABS + s]

    for d, o_ref in ((0, hf_ref), (1, hb_ref)):
        for bi in range(BATCH):
            for s in range(LRU_SLABS):
                o_ref[bi, :, s * LANES:(s + 1) * LANES] = (
                    y_sc[d, s, pl.ds(bi * pitch, tl, stride=1), :].astype(o_ref.dtype))


def _block_diag_chunks(w):
    per = MXU_DIM // LRU_BLOCK
    rows = w.reshape(2, LRU_BLOCKS // per, MXU_DIM, LRU_BLOCK)
    tiled = jnp.concatenate([rows] * per, axis=-1)
    blocks = np.arange(MXU_DIM) // LRU_BLOCK
    on_diagonal = jnp.asarray(blocks[:, None] == blocks[None, :])
    return jnp.where(on_diagonal, tiled, 0.0).astype(BF16)


def _lru_scan(u, conv_w, conv_b, gate_a_w, gate_a_b, gate_x_w, gate_x_b, lam):
    def fwd_blk(w, s):
        return (0, s, w)

    def bwd_blk(w, s):
        blk = jnp.where(s < LRU_CTX_BLKS, LRU_CTX_BLKS - 1 - s, LRU_BLKS - 1 - (s - LRU_CTX_BLKS))
        return (0, blk, w)

    blk = (BATCH, LRU_TL, LRU_WC)
    vec = pl.BlockSpec((2, 1, LRU_WC), lambda w, s: (0, 0, w))
    gate = pl.BlockSpec((2, LRU_WC // MXU_DIM, MXU_DIM, MXU_DIM), lambda w, s: (0, w, 0, 0))
    width = u.shape[-1]
    out = jax.ShapeDtypeStruct((BATCH, TOK, width), BF16)
    scan_buf = pltpu.VMEM((2, LRU_SLABS, BATCH * LRU_PITCH, LANES), F32)
    return pl.pallas_call(
        _lru_kernel,
        grid=(width // LRU_WC, LRU_BLKS),
        in_specs=[pl.BlockSpec(blk, fwd_blk), pl.BlockSpec(blk, bwd_blk),
                  pl.BlockSpec((2, LRU_CONV, LRU_WC), lambda w, s: (0, 0, w)), vec,
                  gate, vec, gate, vec, vec],
        out_specs=[pl.BlockSpec(blk, fwd_blk), pl.BlockSpec(blk, bwd_blk)],
        out_shape=[out, out],
        scratch_shapes=[pltpu.VMEM((2, BATCH * LRU_SLABS, LRU_TL + 2 * SUBLANES, LANES), F32),
                        scan_buf, scan_buf, scan_buf,
                        pltpu.VMEM((2, LRU_SLABS, BATCH, LANES), F32)],
        compiler_params=_params("parallel", "arbitrary"),
        name="lru_scan",
    )(u, u, conv_w, conv_b.reshape(2, 1, width),
      _block_diag_chunks(gate_a_w), 0.5 * gate_a_b.reshape(2, 1, width),
      _block_diag_chunks(gate_x_w), 0.5 * gate_x_b.reshape(2, 1, width),
      lam.reshape(2, 1, width))


def _residual_rows(x_refs, mod_ref, acts, wb_sc, o_ref, post=None):
    y = jnp.dot(jnp.concatenate(acts, axis=0).astype(BF16), wb_sc[...], preferred_element_type=F32)
    for bb in range(len(acts)):
        out = _stream_tile(x_refs, bb) + mod_ref[bb, 0][2:3, :] * _tile_rows(y, bb)
        o_ref[bb] = out if post is None else post(out)


def _out_attn_final_kernel(ot_ref, g_ref, w_ref, x_ref, mod_ref, fg_ref, o_ref, wb_sc):
    _cast_once(w_ref, wb_sc)
    acts = [ot_ref[bb].astype(F32).T * _silu(g_ref[bb].astype(F32)) for bb in range(OUT_ROW_BATCH)]

    def final_norm(x):
        return x * lax.rsqrt(jnp.mean(x * x, axis=-1, keepdims=True) + EPS) * fg_ref[...]

    _residual_rows((x_ref,), mod_ref, acts, wb_sc, o_ref, final_norm)


def _out_attn_in_next_kernel(ot_ref, g_ref, wo_ref, ctx_ref, lat_ref, mod_ref, ng_ref, nmod_ref, wi_ref,
                             xs_ref, *refs):
    o_refs, (wo_sc, wi_sc) = refs[:-2], refs[-2:]
    _cast_once(wo_ref, wo_sc)
    _cast_once(wi_ref, wi_sc)
    acts = [ot_ref[bb].astype(F32).T * _silu(g_ref[bb].astype(F32)) for bb in range(ROW_BATCH)]
    y = jnp.dot(jnp.concatenate(acts, axis=0).astype(BF16), wo_sc[...], preferred_element_type=F32)
    tiles = []
    for bb in range(ROW_BATCH):
        out = _stream_tile((ctx_ref, lat_ref), bb) + mod_ref[bb, 0][2:3, :] * _tile_rows(y, bb)
        xs_ref[bb] = out
        tiles.append(_norm_mod(out, ng_ref[...], nmod_ref[bb, 0]))
    z = jnp.dot(jnp.concatenate(tiles, axis=0).astype(BF16), wi_sc[...], preferred_element_type=F32)
    width = z.shape[-1] // len(o_refs)
    for i, o_ref in enumerate(o_refs):
        for bb in range(ROW_BATCH):
            o_ref[bb] = _tile_rows(z, bb)[:, i * width:(i + 1) * width].astype(o_ref.dtype)


def _halo_rows(main_ref, prev_ref, next_ref, bb):
    return jnp.concatenate([prev_ref[bb], main_ref[bb], next_ref[bb]], axis=0).astype(F32)


def _out_lru_conv_kernel(hf_ref, hfp_ref, hfn_ref, hb_ref, hbp_ref, hbn_ref, gl_ref, glp_ref, gln_ref,
                         x_ref, xp_ref, xn_ref, lmod_ref, lwo_ref, mod_ref, g_ref, wu_ref, wb_ref, wc_ref,
                         wg_ref, cw_ref, cb_ref, wo_ref, o_ref, p_sc, lwo_sc, wo_sc):
    j = pl.program_id(1)
    halo, cc = SUBLANES, CONV_CHUNK
    rows = ROW_TILE + 2 * halo
    _cast_once(lwo_ref, lwo_sc)
    _cast_once(wo_ref, wo_sc)
    acts = [(_halo_rows(hf_ref, hfp_ref, hfn_ref, bb) + _halo_rows(hb_ref, hbp_ref, hbn_ref, bb))
            * _silu(_halo_rows(gl_ref, glp_ref, gln_ref, bb)) for bb in range(ROW_BATCH)]
    y = jnp.dot(jnp.concatenate(acts, axis=0).astype(BF16), lwo_sc[...], preferred_element_type=F32)
    x1 = [_halo_rows(x_ref, xp_ref, xn_ref, bb) + lmod_ref[bb, 0][2:3, :] * y[bb * rows:(bb + 1) * rows]
          for bb in range(ROW_BATCH)]
    tiles = [_norm_mod(x1[bb], g_ref[...], mod_ref[bb, 0]) for bb in range(ROW_BATCH)]
    h = jnp.concatenate(tiles, axis=0).astype(BF16)
    h_main = jnp.concatenate([h[bb * rows + halo:bb * rows + halo + ROW_TILE] for bb in range(ROW_BATCH)], axis=0)
    keep_prev = jnp.where((j == 0) | (j == CTX_ROW_TILES), 0.0, 1.0)
    keep_next = jnp.where((j == CTX_ROW_TILES - 1) | (j == ROW_TILES - 1), 0.0, 1.0)
    acc = jnp.zeros((ROW_BATCH * ROW_TILE, D_MODEL), F32)
    for c in range(D_MODEL // cc):
        chans = slice(c * cc, (c + 1) * cc)
        u, cg = (jnp.dot(h, w_ref[:, chans], preferred_element_type=F32) for w_ref in (wu_ref, wc_ref))
        bg, g = (jnp.dot(h_main, w_ref[:, chans], preferred_element_type=F32) for w_ref in (wb_ref, wg_ref))
        p = cg * u
        acts = []
        for bb in range(ROW_BATCH):
            top = bb * rows
            main = slice(top + halo, top + halo + ROW_TILE)
            tile = slice(bb * ROW_TILE, (bb + 1) * ROW_TILE)
            for s in range(cc // LANES):
                lanes = slice(s * LANES, (s + 1) * LANES)
                p_sc[bb, c, s, 0:halo, :] = p[top:top + halo, lanes] * keep_prev
                p_sc[bb, c, s, halo:halo + ROW_TILE, :] = p[main, lanes]
                p_sc[bb, c, s, halo + ROW_TILE:rows, :] = p[top + halo + ROW_TILE:top + rows, lanes] * keep_next
            before = jnp.concatenate([p_sc[bb, c, s, pl.ds(halo - 1, ROW_TILE, stride=1), :]
                                      for s in range(cc // LANES)], axis=1)
            after = jnp.concatenate([p_sc[bb, c, s, pl.ds(halo + 1, ROW_TILE, stride=1), :]
                                     for s in range(cc // LANES)], axis=1)
            conv = (cw_ref[0:1, chans] * before + cw_ref[1:2, chans] * p[main] + cw_ref[2:3, chans] * after
                    + cb_ref[:, chans])
            acts.append(bg[tile] * conv * _silu(g[tile]))
        acc = acc + jnp.dot(jnp.concatenate(acts, axis=0).astype(BF16), wo_sc[chans, :],
                            preferred_element_type=F32)
    for bb in range(ROW_BATCH):
        o_ref[bb] = x1[bb][halo:halo + ROW_TILE] + mod_ref[bb, 0][2:3, :] * _tile_rows(acc, bb)


_W_OUT_SPEC = _const_spec((D_MODEL, D_MODEL))


def _out_attn_final(ot, gate, w_out, xs, mod, final_g):
    off, rb = CTX_ROW_TILES, OUT_ROW_BATCH
    return pl.pallas_call(
        _out_attn_final_kernel,
        grid=(BATCH // rb, ROW_TILES - off),
        in_specs=[pl.BlockSpec((rb, Q_DIM, ROW_TILE), lambda b, j: (b, 0, j)), _row_spec(Q_DIM, off, rb),
                  _W_OUT_SPEC, _row_spec(D_MODEL, off, rb), _mod_spec(off, rb), _const_spec((1, D_MODEL))],
        out_specs=_row_spec(D_MODEL, rb=rb),
        out_shape=jax.ShapeDtypeStruct((BATCH, SEQ, D_MODEL), F32),
        scratch_shapes=[pltpu.VMEM((D_MODEL, D_MODEL), BF16)],
        compiler_params=_params("arbitrary", "arbitrary"),
        name="out_attn_final",
    )(ot, gate, w_out, xs, mod, final_g.reshape(1, D_MODEL))


def _out_attn_in_next(ot, gate, w_out, ctx, x, mod, next_norm_g, next_mod, next_w_in, dtypes):
    n = next_w_in.shape[1]
    width = n // len(dtypes)
    return pl.pallas_call(
        _out_attn_in_next_kernel,
        grid=_ROW_GRID,
        in_specs=[pl.BlockSpec((ROW_BATCH, Q_DIM, ROW_TILE), lambda b, j: (b, 0, j)), _row_spec(Q_DIM),
                  _W_OUT_SPEC] + _stream_specs(True)
                 + [_mod_spec(), _const_spec((1, D_MODEL)), _mod_spec(), _const_spec((D_MODEL, n))],
        out_specs=[_row_spec(D_MODEL)] + [_row_spec(width)] * len(dtypes),
        out_shape=[jax.ShapeDtypeStruct((BATCH, TOK, D_MODEL), F32)]
                  + [jax.ShapeDtypeStruct((BATCH, TOK, width), dt) for dt in dtypes],
        scratch_shapes=[pltpu.VMEM((D_MODEL, D_MODEL), BF16), pltpu.VMEM((D_MODEL, n), BF16)],
        compiler_params=_params("arbitrary", "arbitrary"),
        name="out_attn_in_lru",
    )(ot, gate, w_out, ctx, x, mod, next_norm_g.reshape(1, D_MODEL), next_mod, next_w_in)


def _out_lru_conv(hf, hb, gate, lru_w_out, xs, lru_mod, mod, norm_g, w_in, conv_w, conv_b, w_out):
    per = ROW_TILE // SUBLANES
    n8 = TOK // SUBLANES
    halo = lambda idx: pl.BlockSpec((ROW_BATCH, SUBLANES, D_MODEL), idx)
    prev = halo(lambda b, j: (b, jnp.maximum(j * per - 1, 0), 0))
    nxt = halo(lambda b, j: (b, jnp.minimum((j + 1) * per, n8 - 1), 0))
    with_halo = [_row_spec(D_MODEL), prev, nxt]
    w_blocks = [_const_spec((D_MODEL, D_MODEL), functools.partial(lambda i, b, j: (0, i), i)) for i in range(4)]
    return pl.pallas_call(
        _out_lru_conv_kernel,
        grid=_ROW_GRID,
        in_specs=with_halo * 4 + [_mod_spec(), _W_OUT_SPEC, _mod_spec(), _const_spec((1, D_MODEL))] + w_blocks
                 + [_const_spec((CONV_K, D_MODEL)), _const_spec((1, D_MODEL)), _W_OUT_SPEC],
        out_specs=_row_spec(D_MODEL),
        out_shape=jax.ShapeDtypeStruct((BATCH, TOK, D_MODEL), F32),
        scratch_shapes=[pltpu.VMEM((ROW_BATCH, D_MODEL // CONV_CHUNK, CONV_CHUNK // LANES,
                                    ROW_TILE + 2 * SUBLANES, LANES), F32),
                        pltpu.VMEM((D_MODEL, D_MODEL), BF16), pltpu.VMEM((D_MODEL, D_MODEL), BF16)],
        compiler_params=_params("arbitrary", "arbitrary"),
        name="out_lru_conv",
    )(hf, hf, hf, hb, hb, hb, gate, gate, gate, xs, xs, xs, lru_mod, lru_w_out, mod,
      norm_g.reshape(1, D_MODEL), w_in, w_in, w_in, w_in, conv_w, conv_b.reshape(1, D_MODEL), w_out)


def _rope_tables():
    quarter = HEAD_DIM // 4
    pos = np.arange(SEQ)
    half = HEAD_DIM // 2
    inv = (1.0 / (ROPE_BASE ** (np.arange(0, half, 2, dtype=np.float32) / half))).astype(np.float32)
    zero = np.zeros((SEQ, quarter), np.float32)
    parts_c, parts_a, parts_b = [], [], []
    for axis_pos in ((pos // GRID_W).astype(np.float32), (pos % GRID_W).astype(np.float32)):
        ang = axis_pos[:, None] * inv
        c, s = np.cos(ang), np.sin(ang)
        parts_c += [c, c]
        parts_a += [-s, zero]
        parts_b += [zero, s]

    def table(parts, ctx_value):
        head = np.concatenate(parts, axis=1)
        head = np.concatenate([np.full((CTX_LEN, HEAD_DIM), ctx_value, np.float32), head], axis=0)
        return jnp.asarray(np.tile(head, (1, LANES // HEAD_DIM)).astype(np.float32))

    return table(parts_c, 1.0), table(parts_a, 0.0), table(parts_b, 0.0)


def kernel(x, c, ctx, c_ctx, l0_norm_g, l0_mod_w, l0_mod_b, l0_w_in, l0_w_out, l0_sink, l1_norm_g, l1_mod_w, l1_mod_b, l1_w_in, l1_w_out, l1_conv_w, l1_conv_b, l1_gate_a_w, l1_gate_a_b, l1_gate_x_w, l1_gate_x_b, l1_lambda, l2_norm_g, l2_mod_w, l2_mod_b, l2_w_in, l2_w_out, l2_conv_w, l2_conv_b, l3_norm_g, l3_mod_w, l3_mod_b, l3_w_in, l3_w_out, l3_sink, final_norm_g):
    mods = _modulation(c, c_ctx, (l0_mod_w, l1_mod_w, l2_mod_w, l3_mod_w),
                       (l0_mod_b, l1_mod_b, l2_mod_b, l3_mod_b))
    rope_tabs = _rope_tables()

    qt, k, vt, gate = _in_attn((ctx, x), mods[0], l0_norm_g, l0_w_in, rope_tabs, ctx_queries=True)
    ot, l2_w_in_bf16 = _attention(qt, k, vt, l0_sink, need_ctx=True, later_weight=l2_w_in)
    xs, u, gate = _out_attn_in_next(ot, gate, l0_w_out, ctx, x, mods[0], l1_norm_g, mods[1], l1_w_in,
                                    (F32, BF16))

    hf, hb = _lru_scan(u, l1_conv_w, l1_conv_b, l1_gate_a_w, l1_gate_a_b, l1_gate_x_w, l1_gate_x_b,
                       l1_lambda)
    xs = _out_lru_conv(hf, hb, gate, l1_w_out, xs, mods[1], mods[2], l2_norm_g, l2_w_in_bf16, l2_conv_w,
                       l2_conv_b, l2_w_out)

    qt, k, vt, gate = _in_attn((xs,), mods[3], l3_norm_g, l3_w_in, rope_tabs, ctx_queries=False)
    ot = _attention(qt, k, vt, l3_sink, need_ctx=False)
    return _out_attn_final(ot, gate, l3_w_out, xs, mods[3], final_norm_g)
```
